```python
import jax
import jax.numpy as jnp
from jax import lax
import numpy as np

D_MODEL = 4096
BATCH = 2
SEQ = 4096
DEPTH = 2

GRID_W = 64
CTX_LEN = 256
POS_BASE = 10000.0
N_MOD = 6

ALPHA = (2 * DEPTH) ** 0.25
BETA = (8 * DEPTH) ** -0.25
LN_EPS = 1e-5

A_WIDTH = D_MODEL // 2
A_HEAD_DIM = 64
A_HEADS = A_WIDTH // A_HEAD_DIM
DECAY_LORA = 96
ICL_LORA = 96
GATE_LORA = 256
LNX_EPS = 64e-5
RWKV_SIZES = (A_WIDTH, A_WIDTH, A_WIDTH, DECAY_LORA, DECAY_LORA, ICL_LORA, ICL_LORA, GATE_LORA)
RWKV_SPLITS = tuple(int(s) for s in np.cumsum(RWKV_SIZES)[:-1])
RWKV_COLS = int(sum(RWKV_SIZES))

B_WIDTH = D_MODEL // 2
LRU_BLOCKS = 16
LRU_BLOCK_W = B_WIDTH // LRU_BLOCKS
CONV_W = 4
CONV_LO = 1
RGLRU_C = 8.0
EVEN_IN = RWKV_COLS + 2 * B_WIDTH
EVEN_MIX = A_WIDTH + B_WIDTH

C_HEADS = 8
C_QK_HEAD = 256
C_V_HEAD = 512
C_QK_W = C_HEADS * C_QK_HEAD
C_V_W = C_HEADS * C_V_HEAD
MLSTM_CHUNK = 64
GATE_CAP = 15.0
MLSTM_NORM_EPS = 1e-6
QO_COLS = C_QK_W + C_V_W
ODD_IN = QO_COLS + C_QK_W + C_V_W + 4 * C_HEADS

N_EXPERTS = 16
N_GROUPS = 4
EXPERTS_PER_GROUP = N_EXPERTS // N_GROUPS
GROUP_SCORE_TOPK = 2
TOP_K = 2
D_EXPERT = 1024
MOE_BLOCK = 256

kernel_name = "hybrid_rwkv7_rglru_mlstm_moe_dit"


def layer_norm(x, g, b, eps=LN_EPS):
    xf = x.astype(jnp.float32)
    xc = xf - xf.mean(-1, keepdims=True)
    var = jnp.mean(xc * xc, -1, keepdims=True)
    return (xc * lax.rsqrt(var + eps) * g + b).astype(x.dtype)


def head_norm(y, n_heads, g, b, eps):
    shp = y.shape
    yh = y.astype(jnp.float32).reshape(shp[:-1] + (n_heads, shp[-1] // n_heads))
    yc = yh - yh.mean(-1, keepdims=True)
    var = jnp.mean(yc * yc, -1, keepdims=True)
    return (yc * lax.rsqrt(var + eps)).reshape(shp) * g + b


def adaln(cvec, w, b):
    return jax.nn.silu(cvec) @ w + b


def modulate(x, shift, scale):
    return x * (1.0 + scale) + shift


def softcap(x):
    return GATE_CAP * jnp.tanh(x / GATE_CAP)


def sincos_2d(n_tokens, dim):
    rows = n_tokens // GRID_W
    row = jnp.repeat(jnp.arange(rows, dtype=jnp.float32), GRID_W)
    col = jnp.tile(jnp.arange(GRID_W, dtype=jnp.float32), rows)
    quarter = dim // 4
    omega = POS_BASE ** (-jnp.arange(quarter, dtype=jnp.float32) / quarter)
    ang_r = row[:, None] * omega[None, :]
    ang_c = col[:, None] * omega[None, :]
    return jnp.concatenate([jnp.sin(ang_r), jnp.cos(ang_r), jnp.sin(ang_c), jnp.cos(ang_c)], -1)


def token_shift(p, mu):
    pad = jnp.pad(p, ((0, 0), (1, 1), (0, 0)))
    return p + mu * (0.5 * (pad[:, :-2] + pad[:, 2:]) - p)


def dwconv_centred(x, w, b):
    s_len = x.shape[1]
    pad = jnp.pad(x, ((0, 0), (CONV_LO, CONV_W - 1 - CONV_LO), (0, 0)))
    out = b
    for j in range(CONV_W):
        out = out + pad[:, j:j + s_len] * w[j]
    return out


def rwkv_scan(r, w, k, v, a, b, s0, reverse):
    def step(s, inp):
        r_t, w_t, k_t, v_t, a_t, b_t = inp
        sa = jnp.einsum('bhvk,bhk->bhv', s, a_t)
        s = s * w_t[:, :, None, :] + sa[..., None] * b_t[:, :, None, :] + v_t[..., None] * k_t[:, :, None, :]
        return s, jnp.einsum('bhvk,bhk->bhv', s, r_t)
    xs = tuple(jnp.swapaxes(t, 0, 1) for t in (r, w, k, v, a, b))
    s_fin, ys = lax.scan(step, s0, xs, reverse=reverse)
    return jnp.swapaxes(ys, 0, 1), s_fin


def rglru_dir(xc, wa, ba, wx, bx, lam, h0, reverse):
    bsz, s_len, width = xc.shape
    xg = xc.reshape(bsz, s_len, LRU_BLOCKS, LRU_BLOCK_W)
    gate_r = jax.nn.sigmoid(jnp.einsum('bsnc,ncd->bsnd', xg, wa).reshape(bsz, s_len, width) + ba)
    gate_i = jax.nn.sigmoid(jnp.einsum('bsnc,ncd->bsnd', xg, wx).reshape(bsz, s_len, width) + bx)
    log_a = RGLRU_C * gate_r * jax.nn.log_sigmoid(lam)
    a = jnp.exp(log_a)
    u = xc * gate_i * jnp.sqrt(-jnp.expm1(2.0 * log_a))
    first = s_len - 1 if reverse else 0
    u = u.at[:, first].add(a[:, first] * h0)
    _, hs = lax.associative_scan(lambda l, r: (l[0] * r[0], r[0] * l[1] + r[1]), (a, u), axis=1, reverse=reverse)
    last = 0 if reverse else s_len - 1
    return hs, hs[:, last]


def to_chunks(t):
    bsz, s_len, n_h = t.shape[:3]
    t = t.reshape((bsz, s_len // MLSTM_CHUNK, MLSTM_CHUNK, n_h) + t.shape[3:])
    return jnp.moveaxis(jnp.moveaxis(t, 1, 0), 3, 2)


def from_chunks(t):
    t = jnp.moveaxis(jnp.moveaxis(t, 2, 3), 0, 1)
    return t.reshape((t.shape[0], t.shape[1] * t.shape[2]) + t.shape[3:])


def mlstm_chunkwise(q, k, v, ig, lf, state0, reverse):
    if reverse:
        k, v, ig, lf = (jnp.flip(t, 1) for t in (k, v, ig, lf))
        q = None if q is None else jnp.flip(q, 1)
    tri = jnp.tril(jnp.ones((MLSTM_CHUNK, MLSTM_CHUNK), dtype=bool))

    def step(carry, inp):
        c_st, n_st, m_st = carry
        kc, vc, igc, lfc = inp[:4]
        b = jnp.cumsum(lfc, axis=-1)
        carry_log = b[..., -1] + m_st
        w_in = b[..., -1:] - b + igc
        m_new = jnp.maximum(carry_log, w_in.max(-1))
        ew = jnp.exp(w_in - m_new[..., None])
        ec = jnp.exp(carry_log - m_new)
        c_new = ec[..., None, None] * c_st + jnp.einsum('bhl,bhlk,bhlv->bhkv', ew, kc, vc)
        n_new = ec[..., None] * n_st + jnp.einsum('bhl,bhlk->bhk', ew, kc)
        if len(inp) == 4:
            return (c_new, n_new, m_new), None
        qc = inp[4]
        dmat = jnp.where(tri, b[..., :, None] - b[..., None, :] + igc[..., None, :], -jnp.inf)
        inter = b + m_st[..., None]
        m_t = jnp.maximum(inter, dmat.max(-1))
        scores = jnp.einsum('bhtk,bhsk->bhts', qc, kc) * jnp.exp(dmat - m_t[..., None])
        e_inter = jnp.exp(inter - m_t)
        num = e_inter[..., None] * jnp.einsum('bhtk,bhkv->bhtv', qc, c_st) + jnp.einsum('bhts,bhsv->bhtv', scores, vc)
        den = e_inter * jnp.einsum('bhtk,bhk->bht', qc, n_st) + scores.sum(-1)
        h = num / jnp.maximum(jnp.abs(den), jnp.exp(-m_t))[..., None]
        return (c_new, n_new, m_new), h

    xs = tuple(to_chunks(t) for t in (k, v, ig, lf)) + (() if q is None else (to_chunks(q),))
    fin, hs = lax.scan(step, state0, xs)
    if q is None:
        return None, fin
    h = from_chunks(hs)
    return (jnp.flip(h, 1) if reverse else h), fin


def even_zero_state(bsz):
    s = jnp.zeros((bsz, A_HEADS, A_HEAD_DIM, A_HEAD_DIM), jnp.float32)
    h = jnp.zeros((bsz, B_WIDTH), jnp.float32)
    return (s, s, h, h)


def odd_zero_state(bsz):
    one = (jnp.zeros((bsz, C_HEADS, C_QK_HEAD, C_V_HEAD), jnp.float32),
           jnp.zeros((bsz, C_HEADS, C_QK_HEAD), jnp.float32),
           jnp.zeros((bsz, C_HEADS), jnp.float32))
    return (one, one)


def even_seq(h, state, prm):
    (w_in, w_out, mu, w0, w2, a0, a2, g2, k_k, k_a, r_k, lnx_w, lnx_b,
     conv_w, conv_b, wa, ba, wx, bx, lam) = prm
    bsz, s_len, _ = h.shape
    heads = lambda t: t.reshape(bsz, s_len, A_HEADS, A_HEAD_DIM)
    p = h @ w_in
    rw = token_shift(p[..., :RWKV_COLS], mu).astype(jnp.float32)
    r, k, v, wd_f, wd_b, ad_f, ad_b, gd = jnp.split(rw, RWKV_SPLITS, axis=-1)
    kk = heads(k * k_k)
    kk = kk * lax.rsqrt(jnp.maximum(jnp.sum(kk * kk, -1, keepdims=True), 1e-24))
    rh, vh = heads(r), heads(v)
    y_scan, bonus, finals = 0.0, 0.0, []
    for d, (wd, ad) in enumerate(((wd_f, ad_f), (wd_b, ad_b))):
        log_w = -jax.nn.softplus(-(w0[d] + jnp.tanh(wd) @ w2[d])) - 0.5
        icl = jax.nn.sigmoid(a0[d] + ad @ a2[d])
        kd = heads(k * (1.0 + (icl - 1.0) * k_a))
        y_d, s_fin = rwkv_scan(rh, heads(jnp.exp(-jnp.exp(log_w))), kd, vh, -kk, kk * heads(icl), state[d], d == 1)
        y_scan = y_scan + y_d
        bonus = bonus + jnp.sum(rh * kd * r_k, -1, keepdims=True) * vh
        finals.append(s_fin)
    g = jax.nn.sigmoid(gd) @ g2
    y_a = (head_norm(y_scan.reshape(bsz, s_len, A_WIDTH), A_HEADS, lnx_w, lnx_b, LNX_EPS)
           + bonus.reshape(bsz, s_len, A_WIDTH)) * g
    xb = p[..., RWKV_COLS:RWKV_COLS + B_WIDTH]
    gb = p[..., RWKV_COLS + B_WIDTH:]
    xc = dwconv_centred(xb, conv_w, conv_b).astype(jnp.float32)
    y_lru = 0.0
    for d in range(2):
        h_d, fin = rglru_dir(xc, wa[d], ba[d], wx[d], bx[d], lam[d], state[2 + d], d == 1)
        y_lru = y_lru + h_d
        finals.append(fin)
    y_b = y_lru * jax.nn.gelu(gb.astype(jnp.float32))
    y = jnp.concatenate([y_a, y_b], -1).astype(h.dtype) @ w_out
    return y, tuple(finals)


def odd_seq(h, state, prm, with_out):
    w_in, w_out, ig_b, fg_b, norm_w, norm_b = prm
    bsz, s_len, _ = h.shape
    if with_out:
        p = h @ w_in
        q = p[..., :C_QK_W].astype(jnp.float32).reshape(bsz, s_len, C_HEADS, C_QK_HEAD) * C_QK_HEAD ** -0.5
        o = p[..., C_QK_W:QO_COLS]
        p_kv = p[..., QO_COLS:]
    else:
        q = None
        p_kv = h @ w_in[:, QO_COLS:]
    p_kv = p_kv.astype(jnp.float32)
    k = p_kv[..., :C_QK_W].reshape(bsz, s_len, C_HEADS, C_QK_HEAD)
    v = p_kv[..., C_QK_W:C_QK_W + C_V_W].reshape(bsz, s_len, C_HEADS, C_V_HEAD)
    gp = p_kv[..., C_QK_W + C_V_W:].reshape(bsz, s_len, 2, 2, C_HEADS)
    outs, finals = [], []
    for d in range(2):
        ig = softcap(gp[:, :, d, 0] + ig_b[d])
        lf = jax.nn.log_sigmoid(softcap(gp[:, :, d, 1] + fg_b[d]))
        h_d, fin = mlstm_chunkwise(q, k, v, ig, lf, state[d], d == 1)
        outs.append(h_d)
        finals.append(fin)
    if not with_out:
        return None, tuple(finals)
    h_mix = (outs[0] + outs[1]).reshape(bsz, s_len, C_V_W)
    y = head_norm(h_mix, C_HEADS, norm_w, norm_b, MLSTM_NORM_EPS) * jax.nn.sigmoid(o.astype(jnp.float32))
    return y.astype(h.dtype) @ w_out, tuple(finals)


def moe_ffn(h, router_w, router_b, w_gate, w_up, w_down):
    n_tok, dim = h.shape
    aff = jax.nn.sigmoid(h.astype(jnp.float32) @ router_w.astype(jnp.float32))
    biased = aff + router_b.astype(jnp.float32)
    group_score = lax.top_k(biased.reshape(n_tok, N_GROUPS, EXPERTS_PER_GROUP), GROUP_SCORE_TOPK)[0].sum(-1)
    best_group = jnp.argmax(group_score, axis=-1)
    in_group = (jnp.arange(N_EXPERTS) // EXPERTS_PER_GROUP)[None, :] == best_group[:, None]
    _, expert_idx = lax.top_k(jnp.where(in_group, biased, -jnp.inf), TOP_K)
    sel = jnp.take_along_axis(aff, expert_idx, axis=1)
    gates = sel / jnp.sum(sel, -1, keepdims=True)
    n_slots = n_tok * TOP_K
    flat_e = expert_idx.reshape(n_slots)
    order = jnp.argsort(flat_e)
    e_sorted = flat_e[order]
    tok_sorted = order // TOP_K
    counts = jnp.bincount(flat_e, length=N_EXPERTS)
    padded = (counts + MOE_BLOCK - 1) // MOE_BLOCK * MOE_BLOCK
    pad_end = jnp.cumsum(padded)
    pad_start = pad_end - padded
    start = jnp.cumsum(counts) - counts
    dest = pad_start[e_sorted] + jnp.arange(n_slots) - start[e_sorted]
    n_blocks = -(-n_slots // MOE_BLOCK) + N_EXPERTS
    buf = jnp.zeros((n_blocks * MOE_BLOCK, dim), h.dtype).at[dest].set(h[tok_sorted])
    block_e = jnp.minimum(jnp.searchsorted(pad_end, jnp.arange(n_blocks) * MOE_BLOCK, side='right'), N_EXPERTS - 1)

    def expert_block(args):
        xb, e = args
        return (jax.nn.silu(xb @ w_gate[e]) * (xb @ w_up[e])) @ w_down[e]

    yb = lax.map(expert_block, (buf.reshape(n_blocks, MOE_BLOCK, dim), block_e))
    y_sorted = yb.reshape(n_blocks * MOE_BLOCK, dim)[dest].astype(jnp.float32)
    out = jnp.zeros((n_tok, dim), jnp.float32).at[tok_sorted].add(y_sorted * gates.reshape(n_slots)[order][:, None])
    return out.astype(h.dtype)


def setup_inputs(seed: int = 0) -> dict:
    key = jax.random.key(seed)
    keys = jax.random.split(key, 48)
    ki = iter(range(48))

    def nrm(shape, scale):
        return jax.random.normal(keys[next(ki)], shape, jnp.float32) * scale

    def uni(shape, lo, hi):
        return jax.random.uniform(keys[next(ki)], shape, jnp.float32, lo, hi)

    D = D_MODEL
    n_even = (DEPTH + 1) // 2
    n_odd = DEPTH // 2
    inp = {}
    inp["x"] = nrm((BATCH, SEQ, D), 1.0)
    inp["c"] = nrm((BATCH, D), 1.0)
    inp["ctx"] = nrm((BATCH, CTX_LEN, D), 1.0)
    inp["c_ctx"] = nrm((D,), 1.0)
    inp["ada_w"] = nrm((DEPTH, D, N_MOD * D), 0.5 * D ** -0.5)
    inp["ada_b"] = nrm((DEPTH, N_MOD * D), 0.02)
    inp["norm_g"] = 1.0 + nrm((DEPTH, 2, D), 0.02)
    inp["norm_b"] = nrm((DEPTH, 2, D), 0.02)
    inp["ev_w_in"] = nrm((n_even, D, EVEN_IN), D ** -0.5)
    inp["ev_w_out"] = nrm((n_even, EVEN_MIX, D), BETA * EVEN_MIX ** -0.5)
    inp["rwkv_mu"] = uni((n_even, RWKV_COLS), 0.0, 1.0)
    inp["rwkv_w0"] = uni((n_even, 2, A_WIDTH), -6.0, -1.0)
    inp["rwkv_w2"] = nrm((n_even, 2, DECAY_LORA, A_WIDTH), 0.5 * DECAY_LORA ** -0.5)
    inp["rwkv_a0"] = nrm((n_even, 2, A_WIDTH), 0.1)
    inp["rwkv_a2"] = nrm((n_even, 2, ICL_LORA, A_WIDTH), ICL_LORA ** -0.5)
    inp["rwkv_g2"] = nrm((n_even, GATE_LORA, A_WIDTH), GATE_LORA ** -0.5)
    inp["rwkv_k_k"] = 1.0 + nrm((n_even, A_WIDTH), 0.1)
    inp["rwkv_k_a"] = 1.0 + nrm((n_even, A_WIDTH), 0.1)
    inp["rwkv_r_k"] = nrm((n_even, A_HEADS, A_HEAD_DIM), 0.1)
    inp["rwkv_lnx_w"] = 1.0 + nrm((n_even, A_WIDTH), 0.02)
    inp["rwkv_lnx_b"] = nrm((n_even, A_WIDTH), 0.02)
    inp["lru_conv_w"] = nrm((n_even, CONV_W, B_WIDTH), CONV_W ** -0.5)
    inp["lru_conv_b"] = nrm((n_even, B_WIDTH), 0.02)
    inp["lru_wa"] = nrm((n_even, 2, LRU_BLOCKS, LRU_BLOCK_W, LRU_BLOCK_W), LRU_BLOCK_W ** -0.5)
    inp["lru_ba"] = nrm((n_even, 2, B_WIDTH), 0.02)
    inp["lru_wx"] = nrm((n_even, 2, LRU_BLOCKS, LRU_BLOCK_W, LRU_BLOCK_W), LRU_BLOCK_W ** -0.5)
    inp["lru_bx"] = nrm((n_even, 2, B_WIDTH), 0.02)
    sig_l = uni((n_even, 2, B_WIDTH), 0.9, 0.999) ** (1.0 / RGLRU_C)
    inp["lru_lam"] = jnp.log(sig_l) - jnp.log1p(-sig_l)
    inp["od_w_in"] = nrm((n_odd, D, ODD_IN), D ** -0.5)
    inp["od_w_out"] = nrm((n_odd, C_V_W, D), BETA * C_V_W ** -0.5)
    inp["mlstm_ig_b"] = nrm((n_odd, 2, C_HEADS), 0.1)
    inp["mlstm_fg_b"] = uni((n_odd, 2, C_HEADS), 3.0, 6.0)
    inp["mlstm_norm_w"] = 1.0 + nrm((n_odd, C_V_W), 0.02)
    inp["mlstm_norm_b"] = nrm((n_odd, C_V_W), 0.02)
    inp["router_w"] = nrm((D, N_EXPERTS), D ** -0.5)
    inp["router_b"] = nrm((N_EXPERTS,), 0.01)
    inp["moe_w_gate"] = nrm((DEPTH, N_EXPERTS, D, D_EXPERT), D ** -0.5)
    inp["moe_w_up"] = nrm((DEPTH, N_EXPERTS, D, D_EXPERT), D ** -0.5)
    inp["moe_w_down"] = nrm((DEPTH, N_EXPERTS, D_EXPERT, D), BETA * D_EXPERT ** -0.5)
    return inp


def reference(x, c, ctx, c_ctx, ada_w, ada_b, norm_g, norm_b,
              ev_w_in, ev_w_out, rwkv_mu, rwkv_w0, rwkv_w2, rwkv_a0, rwkv_a2, rwkv_g2,
              rwkv_k_k, rwkv_k_a, rwkv_r_k, rwkv_lnx_w, rwkv_lnx_b,
              lru_conv_w, lru_conv_b, lru_wa, lru_ba, lru_wx, lru_bx, lru_lam,
              od_w_in, od_w_out, mlstm_ig_b, mlstm_fg_b, mlstm_norm_w, mlstm_norm_b,
              router_w, router_b, moe_w_gate, moe_w_up, moe_w_down):
    bsz, s_len, dim = x.shape
    n_ctx = ctx.shape[1]
    lat = x + sincos_2d(s_len, dim).astype(x.dtype)
    cx = ctx
    for layer in range(DEPTH):
        last = layer == DEPTH - 1
        m_lat = jnp.split(adaln(c, ada_w[layer], ada_b[layer])[:, None, :], N_MOD, axis=-1)
        n_ctx_mod = 2 if last else N_MOD
        m_ctx = jnp.split(adaln(c_ctx, ada_w[layer, :, :n_ctx_mod * dim], ada_b[layer, :n_ctx_mod * dim]),
                          n_ctx_mod, axis=-1)
        h_lat = modulate(lat, m_lat[0], m_lat[1])
        h_ctx = modulate(cx, m_ctx[0], m_ctx[1])
        i = layer // 2
        if layer % 2 == 0:
            prm = (ev_w_in[i], ev_w_out[i], rwkv_mu[i], rwkv_w0[i], rwkv_w2[i], rwkv_a0[i], rwkv_a2[i],
                   rwkv_g2[i], rwkv_k_k[i], rwkv_k_a[i], rwkv_r_k[i], rwkv_lnx_w[i], rwkv_lnx_b[i],
                   lru_conv_w[i], lru_conv_b[i], lru_wa[i], lru_ba[i], lru_wx[i], lru_bx[i], lru_lam[i])
            y_ctx, ctx_state = even_seq(h_ctx, even_zero_state(bsz), prm)
            y_lat, _ = even_seq(h_lat, ctx_state, prm)
        else:
            prm = (od_w_in[i], od_w_out[i], mlstm_ig_b[i], mlstm_fg_b[i], mlstm_norm_w[i], mlstm_norm_b[i])
            y_ctx, ctx_state = odd_seq(h_ctx, odd_zero_state(bsz), prm, not last)
            y_lat, _ = odd_seq(h_lat, ctx_state, prm, True)
        lat = layer_norm(ALPHA * lat + m_lat[2] * y_lat, norm_g[layer, 0], norm_b[layer, 0])
        moe_args = (router_w, router_b, moe_w_gate[layer], moe_w_up[layer], moe_w_down[layer])
        if last:
            f_lat = moe_ffn(modulate(lat, m_lat[3], m_lat[4]).reshape(-1, dim), *moe_args).reshape(lat.shape)
        else:
            cx = layer_norm(ALPHA * cx + m_ctx[2] * y_ctx, norm_g[layer, 0], norm_b[layer, 0])
            h_all = jnp.concatenate([modulate(cx, m_ctx[3], m_ctx[4]), modulate(lat, m_lat[3], m_lat[4])], axis=1)
            f_all = moe_ffn(h_all.reshape(-1, dim), *moe_args).reshape(h_all.shape)
            f_lat = f_all[:, n_ctx:]
            cx = layer_norm(ALPHA * cx + m_ctx[5] * f_all[:, :n_ctx], norm_g[layer, 1], norm_b[layer, 1])
        lat = layer_norm(ALPHA * lat + m_lat[5] * f_lat, norm_g[layer, 1], norm_b[layer, 1])
    return lat
```

```python
import functools
import math

import jax
import jax.numpy as jnp
from jax import lax
from jax.experimental import pallas as pl
from jax.experimental.pallas import tpu as pltpu

F32 = jnp.float32
BF16 = jnp.bfloat16

DEPTH = 2
N_MOD = 6
ALPHA = (2 * DEPTH) ** 0.25
LN_EPS = 1e-5
HEAD_DIM = 64
DECAY_LORA = 96
ICL_LORA = 96
GATE_LORA = 256
LNX_EPS = 64e-5
LRU_BLOCK_W = 128
CONV_W = 4
RGLRU_C = 8.0
C_HEADS = 8
C_QK_HEAD = 256
C_V_HEAD = 512
GATE_CAP = 15.0
MLSTM_NORM_EPS = 1e-6
N_EXPERTS = 16
N_GROUPS = 4
EXPERTS_PER_GROUP = N_EXPERTS // N_GROUPS
TOP_K = 2
POS_BASE = 10000.0
GRID_W = 64

LANES = 128
SUBLANES = 8
SCAN_ROWS = 256
CHUNK = 64
PREP_ROWS = 128
LN_ROWS = 256
MM_ROWS = 512
MOE_ROWS = 512
MOE_FTILE = 256
MOE_NCHUNK = 1024
LORA_PAD = 1024
VMEM_LIMIT = 60 * 1024 * 1024


def _cparams(sem):
    return pltpu.CompilerParams(dimension_semantics=sem, vmem_limit_bytes=VMEM_LIMIT)


def _bdot(a, b):
    return jnp.dot(a.astype(BF16), b.astype(BF16), preferred_element_type=F32)


def _bdot_nt(a, b):
    return lax.dot_general(a.astype(BF16), b.astype(BF16), (((1,), (1,)), ((), ())),
                           preferred_element_type=F32)


def _bdot_tn(a, b):
    return lax.dot_general(a.astype(BF16), b.astype(BF16), (((0,), (0,)), ((), ())),
                           preferred_element_type=F32)


def _softplus(z):
    return jnp.maximum(z, 0.0) + jnp.log1p(jnp.exp(-jnp.abs(z)))


def _adaln_kernel(c_ref, w_ref, b_ref, o_ref):
    c = c_ref[...]
    s = c * jax.nn.sigmoid(c)
    o_ref[...] = _bdot(s, w_ref[...]) + b_ref[...]


def adaln_call(cvec, ada_w, ada_b, layer, tn=512):
    rows, d = cvec.shape
    n = ada_w.shape[2]
    return pl.pallas_call(
        _adaln_kernel,
        out_shape=jax.ShapeDtypeStruct((rows, n), F32),
        grid=(n // tn,),
        in_specs=[
            pl.BlockSpec((rows, d), lambda j: (0, 0)),
            pl.BlockSpec((None, d, tn), lambda j: (layer, 0, j)),
            pl.BlockSpec((None, 1, tn), lambda j: (layer, 0, j)),
        ],
        out_specs=pl.BlockSpec((rows, tn), lambda j: (0, j)),
        compiler_params=_cparams(("parallel",)),
        name="adaln",
    )(cvec, ada_w, ada_b)


def _mm_mod_kernel(x_ref, sh_ref, sc_ref, w_ref, o_ref, xb_ref):
    @pl.when(pl.program_id(1) == 0)
    def _():
        xb_ref[...] = (x_ref[...] * (1.0 + sc_ref[...]) + sh_ref[...]).astype(BF16)

    o_ref[...] = jnp.dot(xb_ref[...], w_ref[...], preferred_element_type=F32)


def matmul_mod_call(x, mods, w, layer_w, *, rows_per_mod, n_mod_rows, tn, n_rows=None):
    r, d = x.shape
    n = w.shape[2]
    r = r if n_rows is None else n_rows
    tm = MM_ROWS

    def midx(i):
        return jnp.minimum(i * tm // rows_per_mod, n_mod_rows - 1)

    return pl.pallas_call(
        _mm_mod_kernel,
        out_shape=jax.ShapeDtypeStruct((r, n), F32),
        grid=(r // tm, n // tn),
        in_specs=[
            pl.BlockSpec((tm, d), lambda i, j: (i, 0)),
            pl.BlockSpec((None, 1, d), lambda i, j: (midx(i), 0, 0)),
            pl.BlockSpec((None, 1, d), lambda i, j: (midx(i), 0, 1)),
            pl.BlockSpec((None, d, tn), lambda i, j: (layer_w, 0, j)),
        ],
        out_specs=pl.BlockSpec((tm, tn), lambda i, j: (i, j)),
        scratch_shapes=[pltpu.VMEM((tm, d), BF16)],
        compiler_params=_cparams(("parallel", "arbitrary")),
        name="proj_in",
    )(x, mods, mods, w)


def _mm_kernel(x_ref, w_ref, o_ref):
    o_ref[...] = jnp.dot(x_ref[...], w_ref[...], preferred_element_type=F32)


def matmul_call(x, w, layer_w, *, tn, n_rows=None):
    r, k = x.shape
    n = w.shape[2]
    r = r if n_rows is None else n_rows
    tm = MM_ROWS
    return pl.pallas_call(
        _mm_kernel,
        out_shape=jax.ShapeDtypeStruct((r, n), F32),
        grid=(r // tm, n // tn),
        in_specs=[
            pl.BlockSpec((tm, k), lambda i, j: (i, 0)),
            pl.BlockSpec((None, k, tn), lambda i, j: (layer_w, 0, j)),
        ],
        out_specs=pl.BlockSpec((tm, tn), lambda i, j: (i, j)),
        compiler_params=_cparams(("parallel", "arbitrary")),
        name="proj_out",
    )(x, w)


def _tile_flags(i, tm, n_lat_rows, lat_len, ctx_len):
    row = i * tm
    in_lat = row < n_lat_rows
    pos = jnp.where(in_lat, row % lat_len, (row - n_lat_rows) % ctx_len)
    seq = jnp.where(in_lat, lat_len, ctx_len)
    return pos == 0, pos + tm == seq


def _row_shift(c, prev_row, next_row):
    tm = c.shape[0]
    rows = lax.broadcasted_iota(jnp.int32, c.shape, 0)
    xp = jnp.where(rows == 0, prev_row, pltpu.roll(c, 1, axis=0))
    xn = jnp.where(rows == tm - 1, next_row, pltpu.roll(c, tm - 1, axis=0))
    return xp, xn


def _head_ones():
    r = lax.broadcasted_iota(jnp.int32, (LANES, LANES), 0) // HEAD_DIM
    c = lax.broadcasted_iota(jnp.int32, (LANES, LANES), 1) // HEAD_DIM
    return (r == c).astype(BF16)


def _head_sum(x, ones):
    w = x.shape[1]
    parts = [jnp.dot(x[:, j:j + LANES].astype(BF16), ones, preferred_element_type=F32)
             for j in range(0, w, LANES)]
    return jnp.concatenate(parts, axis=1)


def _halo_specs(tm, width, col, n_rows):
    per = tm // SUBLANES
    last = n_rows // SUBLANES - 1
    return [
        pl.BlockSpec((tm, width), lambda i: (i, col)),
        pl.BlockSpec((SUBLANES, width), lambda i: (jnp.maximum(i * per - 1, 0), col)),
        pl.BlockSpec((SUBLANES, width), lambda i: (jnp.minimum((i + 1) * per, last), col)),
    ]


def _rwkv_prep_kernel(rc, rp, rn, kc, kp, kn, vc, vp, vn, lc, lp, ln,
                      mu_r, mu_k, mu_v, mu_l, w0, w2, a0, a2, g2, kk_s, ka_s,
                      r_o, v_o, kk_o, lw_o, kd_o, bb_o, g_o, *, tm, n_lat_rows, lat_len, ctx_len):
    first, last = _tile_flags(pl.program_id(0), tm, n_lat_rows, lat_len, ctx_len)

    def shifted(c_ref, p_ref, n_ref, mu_ref):
        c = c_ref[...]
        prev_row = jnp.where(first, 0.0, p_ref[SUBLANES - 1:SUBLANES, :])
        next_row = jnp.where(last, 0.0, n_ref[0:1, :])
        xp, xn = _row_shift(c, prev_row, next_row)
        return c + mu_ref[...] * (0.5 * (xp + xn) - c)

    r = shifted(rc, rp, rn, mu_r)
    k = shifted(kc, kp, kn, mu_k)
    v = shifted(vc, vp, vn, mu_v)
    lo = shifted(lc, lp, ln, mu_l)
    r_o[...] = r
    v_o[...] = v

    ones = _head_ones()
    kk = k * kk_s[...]
    ss = _head_sum(kk * kk, ones)
    kk = kk * lax.rsqrt(jnp.maximum(ss, 1e-24))
    kk_o[...] = kk

    for d in range(2):
        wd = lo[:, d * LANES:(d + 1) * LANES]
        ad = lo[:, (2 + d) * LANES:(3 + d) * LANES]
        z = w0[d:d + 1, :] + _bdot(jnp.tanh(wd), w2[d])
        log_w = -_softplus(-z) - 0.5
        lw_o[d] = -jnp.exp(log_w)
        icl = jax.nn.sigmoid(a0[d:d + 1, :] + _bdot(ad, a2[d]))
        kd_o[d] = k * (1.0 + (icl - 1.0) * ka_s[...])
        bb_o[d] = kk * icl
    gd = lo[:, 4 * LANES:4 * LANES + GATE_LORA]
    g_o[...] = _bdot(jax.nn.sigmoid(gd), g2[...])


def rwkv_prep_call(p, prm, *, n_lat_rows, lat_len, ctx_len, col_r, col_k, col_v, col_l):
    r_rows = p.shape[0]
    tm = PREP_ROWS
    aw = prm["mu_r"].shape[1]
    full = lambda shape: pl.BlockSpec(shape, lambda i: (0,) * len(shape))
    in_specs = (_halo_specs(tm, aw, col_r, r_rows) + _halo_specs(tm, aw, col_k, r_rows)
                + _halo_specs(tm, aw, col_v, r_rows) + _halo_specs(tm, LORA_PAD, col_l, r_rows)
                + [full((1, aw)), full((1, aw)), full((1, aw)), full((1, LORA_PAD)),
                   full((2, aw)), full((2, LANES, aw)), full((2, aw)), full((2, LANES, aw)),
                   full((GATE_LORA, aw)), full((1, aw)), full((1, aw))])
    row_spec = pl.BlockSpec((tm, aw), lambda i: (i, 0))
    dir_spec = pl.BlockSpec((2, tm, aw), lambda i: (0, i, 0))
    one = jax.ShapeDtypeStruct((r_rows, aw), F32)
    two = jax.ShapeDtypeStruct((2, r_rows, aw), F32)
    kern = functools.partial(_rwkv_prep_kernel, tm=tm, n_lat_rows=n_lat_rows, lat_len=lat_len, ctx_len=ctx_len)
    return pl.pallas_call(
        kern,
        out_shape=(one, one, one, two, two, two, one),
        grid=(r_rows // tm,),
        in_specs=in_specs,
        out_specs=(row_spec, row_spec, row_spec, dir_spec, dir_spec, dir_spec, row_spec),
        compiler_params=_cparams(("parallel",)),
        name="rwkv_prep",
    )(p, p, p, p, p, p, p, p, p, p, p, p,
      prm["mu_r"], prm["mu_k"], prm["mu_v"], prm["mu_l"], prm["w0"], prm["w2"], prm["a0"], prm["a2"],
      prm["g2"], prm["k_k"], prm["k_a"])


def _lru_prep_kernel(xc_ref, xp_ref, xn_ref, cw, cb, wcat, bcat, lsl, a_o, u_o,
                     *, tm, n_lat_rows, lat_len, ctx_len):
    first, last = _tile_flags(pl.program_id(0), tm, n_lat_rows, lat_len, ctx_len)
    x = xc_ref[...]
    prev_row = jnp.where(first, 0.0, xp_ref[SUBLANES - 1:SUBLANES, :])
    next1 = jnp.where(last, 0.0, xn_ref[0:1, :])
    next2 = jnp.where(last, 0.0, xn_ref[1:2, :])
    rows = lax.broadcasted_iota(jnp.int32, x.shape, 0)
    xm1 = jnp.where(rows == 0, prev_row, pltpu.roll(x, 1, axis=0))
    xp1 = jnp.where(rows == tm - 1, next1, pltpu.roll(x, tm - 1, axis=0))
    xp2 = jnp.where(rows == tm - 1, next2, jnp.where(rows == tm - 2, next1, pltpu.roll(x, tm - 2, axis=0)))
    xc = cb[...] + xm1 * cw[0:1, :] + x * cw[1:2, :] + xp1 * cw[2:3, :] + xp2 * cw[3:4, :]
    nb = x.shape[1] // LRU_BLOCK_W
    gates = jnp.concatenate(
        [_bdot(xc[:, n * LRU_BLOCK_W:(n + 1) * LRU_BLOCK_W], wcat[n]) for n in range(nb)], axis=1)
    gates = gates + bcat[...]
    for d in range(2):
        gr = jnp.concatenate([gates[:, (4 * n + 2 * d) * LRU_BLOCK_W:(4 * n + 2 * d + 1) * LRU_BLOCK_W]
                              for n in range(nb)], axis=1)
        gi = jnp.concatenate([gates[:, (4 * n + 2 * d + 1) * LRU_BLOCK_W:(4 * n + 2 * d + 2) * LRU_BLOCK_W]
                              for n in range(nb)], axis=1)
        log_a = jax.nn.sigmoid(gr) * lsl[d:d + 1, :]
        a = jnp.exp(log_a)
        om = -jnp.tanh(log_a) * (a * a + 1.0)
        a_o[d] = a
        u_o[d] = xc * jax.nn.sigmoid(gi) * jnp.sqrt(om)


def lru_prep_call(p, prm, *, n_lat_rows, lat_len, ctx_len, col_x):
    r_rows = p.shape[0]
    tm = PREP_ROWS
    bw = prm["conv_b"].shape[1]
    nb = bw // LRU_BLOCK_W
    full = lambda shape: pl.BlockSpec(shape, lambda i: (0,) * len(shape))
    in_specs = _halo_specs(tm, bw, col_x, r_rows) + [
        full((CONV_W, bw)), full((1, bw)), full((nb, LRU_BLOCK_W, 4 * LRU_BLOCK_W)), full((1, 4 * bw)), full((2, bw))]
    dir_spec = pl.BlockSpec((2, tm, bw), lambda i: (0, i, 0))
    two = jax.ShapeDtypeStruct((2, r_rows, bw), F32)
    kern = functools.partial(_lru_prep_kernel, tm=tm, n_lat_rows=n_lat_rows, lat_len=lat_len, ctx_len=ctx_len)
    return pl.pallas_call(
        kern,
        out_shape=(two, two),
        grid=(r_rows // tm,),
        in_specs=in_specs,
        out_specs=(dir_spec, dir_spec),
        compiler_params=_cparams(("parallel",)),
        name="lru_prep",
    )(p, p, p, prm["conv_w"], prm["conv_b"], prm["wcat"], prm["bcat"], prm["lsl"])


def _scan_block(b, s, *, reverse, n_lat_blocks_total, lat_blocks):
    ctx_blk = n_lat_blocks_total + b
    lat_blk = b * lat_blocks + ((lat_blocks - s) if reverse else (s - 1))
    return jnp.where(s == 0, ctx_blk, lat_blk)


def _rwkv_scan_kernel(r_ref, lw_ref, k_ref, v_ref, kk_ref, bb_ref, y_ref, s_ref, *, reverse):
    @pl.when(pl.program_id(2) == 0)
    def _():
        s_ref[...] = jnp.zeros_like(s_ref)

    L = CHUNK
    ri = lax.broadcasted_iota(jnp.int32, (L, L), 0)
    ci = lax.broadcasted_iota(jnp.int32, (L, L), 1)
    tri = ((ci >= ri) if reverse else (ci <= ri)).astype(F32)
    r2 = lax.broadcasted_iota(jnp.int32, (2 * L, 2 * L), 0) % L
    c2 = lax.broadcasted_iota(jnp.int32, (2 * L, 2 * L), 1) % L
    strict = (c2 > r2) if reverse else (c2 < r2)
    incl = (c2 >= r2) if reverse else (c2 <= r2)
    rr = lax.broadcasted_iota(jnp.int32, (2 * L, 2 * L), 0)
    cc = lax.broadcasted_iota(jnp.int32, (2 * L, 2 * L), 1)
    eye = (rr == cc).astype(F32)
    off_masks = []
    sz = 2
    while sz <= L:
        late, early = (cc, rr) if reverse else (rr, cc)
        off_masks.append((rr // sz == cc // sz) & (late % sz >= sz // 2) & (early % sz < sz // 2))
        sz *= 2
    lane = lax.broadcasted_iota(jnp.int32, (L, LANES), 1)
    head0 = lane < HEAD_DIM

    def stack2(x):
        return jnp.concatenate([jnp.where(head0, x, 0.0), jnp.where(head0, 0.0, x)], axis=0)

    n_sub = SCAN_ROWS // L
    for j in range(n_sub):
        c = (n_sub - 1 - j) if reverse else j
        rows = pl.ds(c * L, L)
        lw = lw_ref[rows, :]
        cum = jnp.dot(tri, lw, preferred_element_type=F32, precision=lax.Precision.HIGHEST)
        tot = cum[0:1, :] if reverse else cum[L - 1:L, :]
        e_pos = jnp.exp(cum)
        e_neg = jnp.exp(-cum)
        e_rem = jnp.exp(tot - cum)
        kk = kk_ref[rows, :]
        kd = k_ref[rows, :]
        bb = bb_ref[rows, :]
        a2 = stack2(-kk * jnp.exp(cum - lw))
        b2 = stack2(bb * e_neg)
        k2 = stack2(kd * e_neg)
        q2 = stack2(r_ref[rows, :] * e_pos)
        v2 = stack2(v_ref[rows, :])
        bt2 = stack2(bb * e_rem)
        kt2 = stack2(kd * e_rem)

        mm = jnp.where(strict, _bdot_nt(a2, b2), 0.0)
        nn = jnp.where(strict, _bdot_nt(a2, k2), 0.0)
        qq = jnp.where(incl, _bdot_nt(q2, b2), 0.0)
        zz = jnp.where(incl, _bdot_nt(q2, k2), 0.0)
        tinv = eye + jnp.where(off_masks[0], mm, 0.0)
        for off in off_masks[1:]:
            tinv = tinv + _bdot(_bdot(tinv, jnp.where(off, mm, 0.0)), tinv)
        x = jnp.concatenate([a2, _bdot(nn, v2)], axis=1)
        px = _bdot(tinv, x)
        p1 = px[:, :LANES]
        p2 = px[:, LANES:]
        st = s_ref[...]
        u = _bdot_nt(p1, st) + p2
        y2 = _bdot_nt(q2, st) + _bdot(qq, u) + _bdot(zz, v2)
        y_ref[rows, :] = y2[:L, :] + y2[L:, :]
        s_ref[...] = st * jnp.exp(tot) + _bdot_tn(u, bt2) + _bdot_tn(v2, kt2)


def rwkv_scan_call(r, lw, kd, v, kk, bb, *, reverse, batch, lat_len, ctx_len):
    r_rows, aw = r.shape
    d = 1 if reverse else 0
    lat_blocks = lat_len // SCAN_ROWS
    blk = functools.partial(_scan_block, reverse=reverse, n_lat_blocks_total=batch * lat_blocks,
                            lat_blocks=lat_blocks)
    row_spec = pl.BlockSpec((SCAN_ROWS, LANES), lambda b, h, s: (blk(b, s), h))
    dir_spec = pl.BlockSpec((None, SCAN_ROWS, LANES), lambda b, h, s: (d, blk(b, s), h))
    return pl.pallas_call(
        functools.partial(_rwkv_scan_kernel, reverse=reverse),
        out_shape=jax.ShapeDtypeStruct((r_rows, aw), F32),
        grid=(batch, aw // LANES, 1 + lat_blocks),
        in_specs=[row_spec, dir_spec, dir_spec, row_spec, row_spec, dir_spec],
        out_specs=row_spec,
        scratch_shapes=[pltpu.VMEM((LANES, LANES), F32)],
        compiler_params=_cparams(("parallel", "parallel", "arbitrary")),
        name="rwkv_scan_bwd" if reverse else "rwkv_scan_fwd",
    )(r, lw, kd, v, kk, bb)


def _lru_scan_kernel(a_ref, u_ref, h_ref, carry_ref, *, reverse):
    @pl.when(pl.program_id(1) == 0)
    def _():
        carry_ref[...] = jnp.zeros_like(carry_ref)

    w = a_ref.shape[1]
    rows = lax.broadcasted_iota(jnp.int32, (SUBLANES, w), 0)
    n_groups = SCAN_ROWS // SUBLANES

    def body(gi, carry):
        g = (n_groups - 1 - gi) if reverse else gi
        sl = pl.ds(pl.multiple_of(g * SUBLANES, SUBLANES), SUBLANES)
        a = a_ref[sl, :]
        u = u_ref[sl, :]
        for sh in (1, 2, 4):
            if reverse:
                ok = rows < SUBLANES - sh
                a_s = pltpu.roll(a, SUBLANES - sh, axis=0)
                u_s = pltpu.roll(u, SUBLANES - sh, axis=0)
            else:
                ok = rows >= sh
                a_s = pltpu.roll(a, sh, axis=0)
                u_s = pltpu.roll(u, sh, axis=0)
            u = jnp.where(ok, a * u_s + u, u)
            a = jnp.where(ok, a * a_s, a)
        h = a * carry + u
        h_ref[sl, :] = h
        return h[0:1, :] if reverse else h[SUBLANES - 1:SUBLANES, :]

    carry_ref[...] = lax.fori_loop(0, n_groups, body, carry_ref[...])


def lru_scan_call(a, u, *, reverse, batch, lat_len, ctx_len):
    _, r_rows, bw = a.shape
    d = 1 if reverse else 0
    lat_blocks = lat_len // SCAN_ROWS
    blk = functools.partial(_scan_block, reverse=reverse, n_lat_blocks_total=batch * lat_blocks,
                            lat_blocks=lat_blocks)
    dir_spec = pl.BlockSpec((None, SCAN_ROWS, bw), lambda b, s: (d, blk(b, s), 0))
    return pl.pallas_call(
        functools.partial(_lru_scan_kernel, reverse=reverse),
        out_shape=jax.ShapeDtypeStruct((r_rows, bw), F32),
        grid=(batch, 1 + lat_blocks),
        in_specs=[dir_spec, dir_spec],
        out_specs=pl.BlockSpec((SCAN_ROWS, bw), lambda b, s: (blk(b, s), 0)),
        scratch_shapes=[pltpu.VMEM((1, bw), F32)],
        compiler_params=_cparams(("parallel", "arbitrary")),
        name="lru_scan_bwd" if reverse else "lru_scan_fwd",
    )(a, u)


def _even_post_kernel(yf, yb, r_ref, v_ref, kdf, kdb, g_ref, hf, hb, gb_ref, rk, lnw, lnb, o_ref):
    ones = _head_ones()
    y = yf[...] + yb[...]
    inv_n = 1.0 / HEAD_DIM
    yc = y - _head_sum(y, ones) * inv_n
    var = _head_sum(yc * yc, ones) * inv_n
    hn = yc * lax.rsqrt(var + LNX_EPS) * lnw[...] + lnb[...]
    r = r_ref[...]
    v = v_ref[...]
    bonus = _head_sum(r * kdf[...] * rk[...], ones) * v + _head_sum(r * kdb[...] * rk[...], ones) * v
    ya = (hn + bonus) * g_ref[...]
    yl = (hf[...] + hb[...]) * jax.nn.gelu(gb_ref[...])
    aw = ya.shape[1]
    o_ref[:, :aw] = ya.astype(BF16)
    o_ref[:, aw:] = yl.astype(BF16)


def even_post_call(yf, yb, r, v, kd, g, hf, hb, p, prm, *, col_gb):
    r_rows, aw = yf.shape
    bw = hf.shape[1]
    tm = PREP_ROWS
    row = lambda w: pl.BlockSpec((tm, w), lambda i: (i, 0))
    full = lambda shape: pl.BlockSpec(shape, lambda i: (0,) * len(shape))
    return pl.pallas_call(
        _even_post_kernel,
        out_shape=jax.ShapeDtypeStruct((r_rows, aw + bw), BF16),
        grid=(r_rows // tm,),
        in_specs=[row(aw), row(aw), row(aw), row(aw),
                  pl.BlockSpec((None, tm, aw), lambda i: (0, i, 0)),
                  pl.BlockSpec((None, tm, aw), lambda i: (1, i, 0)),
                  row(aw), row(bw), row(bw),
                  pl.BlockSpec((tm, bw), lambda i: (i, col_gb)),
                  full((1, aw)), full((1, aw)), full((1, aw))],
        out_specs=row(aw + bw),
        compiler_params=_cparams(("parallel",)),
        name="even_post",
    )(yf, yb, r, v, kd, kd, g, hf, hb, p, prm["r_k"], prm["lnx_w"], prm["lnx_b"])


def _mlstm_kernel(q_ref, k_ref, v_ref, br_ref, bc_ref, ir_ref, ic_ref, h_ref, c_ref, n_ref, m_ref, *, reverse):
    @pl.when(pl.program_id(2) == 0)
    def _():
        c_ref[...] = jnp.zeros_like(c_ref)
        n_ref[...] = jnp.zeros_like(n_ref)
        m_ref[...] = jnp.zeros_like(m_ref)

    L = CHUNK
    ri = lax.broadcasted_iota(jnp.int32, (L, L), 0)
    ci = lax.broadcasted_iota(jnp.int32, (L, L), 1)
    causal = (ci >= ri) if reverse else (ci <= ri)
    n_sub = SCAN_ROWS // L
    scale = C_QK_HEAD ** -0.5
    for j in range(n_sub):
        c = (n_sub - 1 - j) if reverse else j
        rows = pl.ds(c * L, L)
        b_r = br_ref[c:c + 1, :]
        ig_r = ir_ref[c:c + 1, :]
        b_c = bc_ref[:, c:c + 1]
        ig_c = ic_ref[:, c:c + 1]
        b_tot = b_r[:, 0:1] if reverse else b_r[:, L - 1:L]
        m_st = m_ref[...]
        c_st = c_ref[...]
        n_st = n_ref[...]
        q = q_ref[rows, :] * scale
        k = k_ref[rows, :]
        v = v_ref[rows, :]
        carry_log = b_tot + m_st
        w_in = b_tot - b_c + ig_c
        m_new = jnp.maximum(carry_log, jnp.max(w_in, axis=0, keepdims=True))
        ew = jnp.exp(w_in - m_new)
        ec = jnp.exp(carry_log - m_new)
        dmat = jnp.where(causal, b_c - b_r + ig_r, -jnp.inf)
        inter = b_c + m_st
        m_t = jnp.maximum(inter, jnp.max(dmat, axis=1, keepdims=True))
        scores = _bdot_nt(q, k) * jnp.exp(dmat - m_t)
        e_inter = jnp.exp(inter - m_t)
        num = e_inter * _bdot(q, c_st) + _bdot(scores, v)
        den = e_inter * jnp.sum(q * n_st, axis=1, keepdims=True) + jnp.sum(scores, axis=1, keepdims=True)
        h_ref[rows, :] = num / jnp.maximum(jnp.abs(den), jnp.exp(-m_t))
        kw = k * ew
        c_ref[...] = ec * c_st + _bdot_tn(kw, v)
        n_ref[...] = ec * n_st + jnp.sum(kw, axis=0, keepdims=True)
        m_ref[...] = m_new


def mlstm_call(p, g_rows, g_cols, *, reverse, batch, lat_len, ctx_len, col_q, col_k, col_v):
    r_rows = p.shape[0]
    d = 1 if reverse else 0
    n_sub = SCAN_ROWS // CHUNK
    lat_blocks = lat_len // SCAN_ROWS
    blk = functools.partial(_scan_block, reverse=reverse, n_lat_blocks_total=batch * lat_blocks,
                            lat_blocks=lat_blocks)
    grow = lambda which: pl.BlockSpec((None, None, None, None, n_sub, CHUNK),
                                      lambda b, h, s: (d, which, h, blk(b, s), 0, 0))
    gcol = lambda which: pl.BlockSpec((None, None, None, None, CHUNK, n_sub),
                                      lambda b, h, s: (d, which, h, blk(b, s), 0, 0))
    return pl.pallas_call(
        functools.partial(_mlstm_kernel, reverse=reverse),
        out_shape=jax.ShapeDtypeStruct((r_rows, C_HEADS * C_V_HEAD), F32),
        grid=(batch, C_HEADS, 1 + lat_blocks),
        in_specs=[
            pl.BlockSpec((SCAN_ROWS, C_QK_HEAD), lambda b, h, s: (blk(b, s), col_q + h)),
            pl.BlockSpec((SCAN_ROWS, C_QK_HEAD), lambda b, h, s: (blk(b, s), col_k + h)),
            pl.BlockSpec((SCAN_ROWS, C_V_HEAD), lambda b, h, s: (blk(b, s), col_v + h)),
            grow(0), gcol(0), grow(1), gcol(1),
        ],
        out_specs=pl.BlockSpec((SCAN_ROWS, C_V_HEAD), lambda b, h, s: (blk(b, s), h)),
        scratch_shapes=[pltpu.VMEM((C_QK_HEAD, C_V_HEAD), F32), pltpu.VMEM((1, C_QK_HEAD), F32),
                        pltpu.VMEM((1, 1), F32)],
        compiler_params=_cparams(("parallel", "parallel", "arbitrary")),
        name="mlstm_bwd" if reverse else "mlstm_fwd",
    )(p, p, p, g_rows, g_cols, g_rows, g_cols)


def _odd_post_kernel(hf, hb, o_ref_in, nw, nb, y_ref):
    x = hf[...] + hb[...]
    xc = x - jnp.mean(x, axis=1, keepdims=True)
    var = jnp.mean(xc * xc, axis=1, keepdims=True)
    hn = xc * lax.rsqrt(var + MLSTM_NORM_EPS) * nw[...] + nb[...]
    y_ref[...] = (hn * jax.nn.sigmoid(o_ref_in[...])).astype(BF16)


def odd_post_call(hf, hb, p, nw, nb, *, n_rows, col_o):
    vw = hf.shape[1]
    tm = LN_ROWS
    head = pl.BlockSpec((tm, C_V_HEAD), lambda i, h: (i, h))
    par = pl.BlockSpec((1, C_V_HEAD), lambda i, h: (0, h))
    return pl.pallas_call(
        _odd_post_kernel,
        out_shape=jax.ShapeDtypeStruct((n_rows, vw), BF16),
        grid=(n_rows // tm, C_HEADS),
        in_specs=[head, head, pl.BlockSpec((tm, C_V_HEAD), lambda i, h: (i, col_o + h)), par, par],
        out_specs=head,
        compiler_params=_cparams(("parallel", "parallel")),
        name="odd_post",
    )(hf, hb, p, nw, nb)


def _layer_norm(z, g, b):
    zc = z - jnp.mean(z, axis=1, keepdims=True)
    var = jnp.mean(zc * zc, axis=1, keepdims=True)
    return zc * lax.rsqrt(var + LN_EPS) * g + b


def _route(logits_t, bias):
    aff = [jax.nn.sigmoid(logits_t[e:e + 1, :]) for e in range(N_EXPERTS)]
    biased = [aff[e] + bias[e:e + 1, :] for e in range(N_EXPERTS)]
    best_g = best_v = None
    for g in range(N_GROUPS):
        m = biased[g * EXPERTS_PER_GROUP:(g + 1) * EXPERTS_PER_GROUP]
        pair = None
        for i in range(EXPERTS_PER_GROUP):
            for j in range(i + 1, EXPERTS_PER_GROUP):
                hi = jnp.maximum(m[i], m[j])
                lo_ = jnp.minimum(m[i], m[j])
                s = hi + lo_
                pair = s if pair is None else jnp.maximum(pair, s)
        if best_v is None:
            best_v, best_g = pair, jnp.zeros(pair.shape, jnp.int32)
        else:
            upd = pair > best_v
            best_g = jnp.where(upd, g, best_g)
            best_v = jnp.where(upd, pair, best_v)
    ids, sels = [], []
    taken = None
    for _ in range(TOP_K):
        cur_v = cur_i = cur_a = None
        for e in range(N_EXPERTS):
            ok = best_g == (e // EXPERTS_PER_GROUP)
            if taken is not None:
                ok = jnp.logical_and(ok, taken != e)
            val = jnp.where(ok, biased[e], -jnp.inf)
            if cur_v is None:
                cur_v, cur_i, cur_a = val, jnp.zeros(val.shape, jnp.int32), aff[e]
            else:
                upd = val > cur_v
                cur_i = jnp.where(upd, e, cur_i)
                cur_a = jnp.where(upd, aff[e], cur_a)
                cur_v = jnp.where(upd, val, cur_v)
        ids.append(cur_i)
        sels.append(cur_a)
        taken = cur_i
    tot = sels[0] + sels[1]
    return ids, [sels[0] / tot, sels[1] / tot]


def _ln_route_kernel(lat_ref, y_ref, gate_ref, sh_ref, sc_ref, g_ref, b_ref, rw_ref, rb_ref,
                     lat_o, h_o, id_o, gt_o):
    z = ALPHA * lat_ref[...] + gate_ref[...] * y_ref[...]
    ln = _layer_norm(z, g_ref[...], b_ref[...])
    lat_o[...] = ln
    h = ln * (1.0 + sc_ref[...]) + sh_ref[...]
    h_o[...] = h.astype(BF16)
    logits_t = lax.dot_general(rw_ref[...], h, (((1,), (1,)), ((), ())), preferred_element_type=F32,
                               precision=lax.Precision.HIGHEST)
    ids, gts = _route(logits_t, rb_ref[...])
    id_o[...] = jnp.concatenate(ids, axis=0)
    gt_o[...] = jnp.concatenate(gts, axis=0)


def ln_route_call(lat, y, mods, norm_g, norm_b, layer, router_wt, router_b, *, n_rows, rows_per_mod, n_mod_rows):
    d = lat.shape[1]
    tm = LN_ROWS

    def midx(i):
        return jnp.minimum(i * tm // rows_per_mod, n_mod_rows - 1)

    row = pl.BlockSpec((tm, d), lambda i: (i, 0))
    mod = lambda k: pl.BlockSpec((None, 1, d), lambda i: (midx(i), 0, k))
    nrm = pl.BlockSpec((None, None, 1, d), lambda i: (layer, 0, 0, 0))
    full = lambda shape: pl.BlockSpec(shape, lambda i: (0,) * len(shape))
    sel = pl.BlockSpec((TOP_K, tm), lambda i: (0, i))
    return pl.pallas_call(
        _ln_route_kernel,
        out_shape=(jax.ShapeDtypeStruct((n_rows, d), F32), jax.ShapeDtypeStruct((n_rows, d), BF16),
                   jax.ShapeDtypeStruct((TOP_K, n_rows), jnp.int32), jax.ShapeDtypeStruct((TOP_K, n_rows), F32)),
        grid=(n_rows // tm,),
        in_specs=[row, row, mod(2), mod(3), mod(4), nrm, nrm, full((N_EXPERTS, d)), full((N_EXPERTS, 1))],
        out_specs=(row, row, sel, sel),
        compiler_params=_cparams(("parallel",)),
        name="ln_route",
    )(lat, y, mods, mods, mods, norm_g, norm_b, router_wt, router_b)


def _ln_combine_kernel(lat_ref, ys_ref, gt_ref, gate_ref, g_ref, b_ref, lat_o):
    d = lat_ref.shape[1]
    gt = gt_ref[...]
    f = ys_ref[:, :d] * gt[:, 0:1] + ys_ref[:, d:] * gt[:, 1:2]
    z = ALPHA * lat_ref[...] + gate_ref[...] * f
    lat_o[...] = _layer_norm(z, g_ref[...], b_ref[...])


def ln_combine_call(lat, y_slots, gates_t, mods, norm_g, norm_b, layer, *, n_rows, rows_per_mod, n_mod_rows):
    d = lat.shape[1]
    tm = LN_ROWS

    def midx(i):
        return jnp.minimum(i * tm // rows_per_mod, n_mod_rows - 1)

    row = pl.BlockSpec((tm, d), lambda i: (i, 0))
    nrm = pl.BlockSpec((None, None, 1, d), lambda i: (layer, 1, 0, 0))
    return pl.pallas_call(
        _ln_combine_kernel,
        out_shape=jax.ShapeDtypeStruct((n_rows, d), F32),
        grid=(n_rows // tm,),
        in_specs=[row, pl.BlockSpec((tm, TOP_K * d), lambda i: (i, 0)), pl.BlockSpec((tm, TOP_K), lambda i: (i, 0)),
                  pl.BlockSpec((None, 1, d), lambda i: (midx(i), 0, 5)), nrm, nrm],
        out_specs=row,
        compiler_params=_cparams(("parallel",)),
        name="ln_combine",
    )(lat, y_slots, gates_t, mods, norm_g, norm_b)


def _moe_kernel(be_ref, x_ref, wg_ref, wu_ref, wd_ref, o_ref):
    j = pl.program_id(1)
    x = x_ref[...]
    g = _bdot(x, wg_ref[...])
    u = _bdot(x, wu_ref[...])
    hdn = (g * jax.nn.sigmoid(g) * u).astype(BF16)

    @pl.when(j == 0)
    def _():
        o_ref[...] = jnp.zeros_like(o_ref)

    d = o_ref.shape[1]
    for c0 in range(0, d, MOE_NCHUNK):
        o_ref[:, c0:c0 + MOE_NCHUNK] += _bdot(hdn, wd_ref[:, c0:c0 + MOE_NCHUNK])


def moe_call(xs, block_e, w_gate, w_up, w_down, layer):
    n_rows, d = xs.shape
    f = w_gate.shape[3]
    tm, tf = MOE_ROWS, MOE_FTILE
    grid_spec = pltpu.PrefetchScalarGridSpec(
        num_scalar_prefetch=1,
        grid=(n_rows // tm, f // tf),
        in_specs=[
            pl.BlockSpec((tm, d), lambda i, j, be: (i, 0)),
            pl.BlockSpec((None, None, d, tf), lambda i, j, be: (layer, be[i], 0, j)),
            pl.BlockSpec((None, None, d, tf), lambda i, j, be: (layer, be[i], 0, j)),
            pl.BlockSpec((None, None, tf, d), lambda i, j, be: (layer, be[i], j, 0)),
        ],
        out_specs=pl.BlockSpec((tm, d), lambda i, j, be: (i, 0)),
    )
    return pl.pallas_call(
        _moe_kernel,
        out_shape=jax.ShapeDtypeStruct((n_rows, d), F32),
        grid_spec=grid_spec,
        compiler_params=_cparams(("parallel", "arbitrary")),
        name="moe_experts",
    )(block_e, xs, w_gate, w_up, w_down)


def moe_ffn(h, ids, w_gate, w_up, w_down, layer):
    n_tok, d = h.shape
    n_slots = n_tok * TOP_K
    tm = MOE_ROWS
    flat_e = ids.T.reshape(n_slots)
    onehot = (flat_e[:, None] == jnp.arange(N_EXPERTS, dtype=jnp.int32)[None, :]).astype(jnp.int32)
    csum = jnp.cumsum(onehot, axis=0)
    rank = jnp.take_along_axis(csum, flat_e[:, None], axis=1)[:, 0] - 1
    counts = csum[-1]
    padded = (counts + tm - 1) // tm * tm
    pad_end = jnp.cumsum(padded)
    pad_start = pad_end - padded
    dest = pad_start[flat_e] + rank
    n_blocks = -(-n_slots // tm) + N_EXPERTS
    block_e = jnp.minimum(jnp.searchsorted(pad_end, jnp.arange(n_blocks, dtype=jnp.int32) * tm, side="right"),
                          N_EXPERTS - 1).astype(jnp.int32)
    tok = jnp.arange(n_slots, dtype=jnp.int32) // TOP_K
    xs = jnp.zeros((n_blocks * tm, d), h.dtype).at[dest].set(h[tok])
    ys = moe_call(xs, block_e, w_gate, w_up, w_down, layer)
    return ys[dest].reshape(n_tok, TOP_K * d)


def _sincos_2d(n_tokens, dim):
    rows = n_tokens // GRID_W
    row = jnp.repeat(jnp.arange(rows, dtype=F32), GRID_W)
    col = jnp.tile(jnp.arange(GRID_W, dtype=F32), rows)
    quarter = dim // 4
    omega = POS_BASE ** (-jnp.arange(quarter, dtype=F32) / quarter)
    ang_r = row[:, None] * omega[None, :]
    ang_c = col[:, None] * omega[None, :]
    return jnp.concatenate([jnp.sin(ang_r), jnp.cos(ang_r), jnp.sin(ang_c), jnp.cos(ang_c)], -1)


def _pad_cols(w, width):
    return jnp.pad(w, [(0, 0)] * (w.ndim - 1) + [(0, width - w.shape[-1])])


def _even_weights(ev_w_in, rwkv_mu, aw, bw):
    rk = 3 * aw
    lo0 = rk
    pieces = [DECAY_LORA, DECAY_LORA, ICL_LORA, ICL_LORA]

    def regroup(t):
        out = [t[..., rk + 2 * (DECAY_LORA + ICL_LORA) + GATE_LORA:],
               t[..., :rk]]
        off = lo0
        for wdt in pieces:
            out.append(_pad_cols(t[..., off:off + wdt], LANES))
            off += wdt
        out.append(t[..., off:off + GATE_LORA])
        used = 4 * LANES + GATE_LORA
        out.append(jnp.zeros(t.shape[:-1] + (LORA_PAD - used,), t.dtype))
        return jnp.concatenate(out, axis=-1)

    w = regroup(ev_w_in).astype(BF16)
    rwkv_cols = rk + 2 * (DECAY_LORA + ICL_LORA) + GATE_LORA
    mu_full = jnp.concatenate([rwkv_mu, jnp.zeros(rwkv_mu.shape[:-1] + (2 * bw,), rwkv_mu.dtype)], axis=-1)
    mu = regroup(mu_full)
    del rwkv_cols
    return w, mu


def kernel(x, c, ctx, c_ctx, ada_w, ada_b, norm_g, norm_b,
           ev_w_in, ev_w_out, rwkv_mu, rwkv_w0, rwkv_w2, rwkv_a0, rwkv_a2, rwkv_g2,
           rwkv_k_k, rwkv_k_a, rwkv_r_k, rwkv_lnx_w, rwkv_lnx_b,
           lru_conv_w, lru_conv_b, lru_wa, lru_ba, lru_wx, lru_bx, lru_lam,
           od_w_in, od_w_out, mlstm_ig_b, mlstm_fg_b, mlstm_norm_w, mlstm_norm_b,
           router_w, router_b, moe_w_gate, moe_w_up, moe_w_down):
    bsz, s_len, dim = x.shape
    n_ctx = ctx.shape[1]
    n_lat_rows = bsz * s_len
    n_all_rows = n_lat_rows + bsz * n_ctx
    aw = rwkv_k_k.shape[1]
    bw = lru_conv_b.shape[1]
    assert s_len % SCAN_ROWS == 0 and n_ctx == SCAN_ROWS and n_all_rows % MM_ROWS == 0
    assert (bsz * n_ctx) % MM_ROWS == 0 and s_len % MM_ROWS == 0 and bsz + 1 <= SUBLANES
    seq = dict(n_lat_rows=n_lat_rows, lat_len=s_len, ctx_len=n_ctx)
    scan = dict(batch=bsz, lat_len=s_len, ctx_len=n_ctx)
    modk = dict(rows_per_mod=s_len, n_mod_rows=bsz + 1)

    lat = (x + _sincos_2d(s_len, dim).astype(x.dtype)).reshape(n_lat_rows, dim)
    stream = jnp.concatenate([lat, ctx.reshape(bsz * n_ctx, dim)], axis=0)
    cvec = jnp.concatenate([c, c_ctx[None, :], jnp.zeros((SUBLANES - bsz - 1, dim), c.dtype)], axis=0)
    ada_b3 = ada_b[:, None, :]
    norm_g4 = norm_g[:, :, None, :]
    norm_b4 = norm_b[:, :, None, :]
    router_wt = router_w.T
    router_b2 = router_b[:, None]

    for layer in range(DEPTH):
        last = layer == DEPTH - 1
        i = layer // 2
        mods = adaln_call(cvec, ada_w, ada_b3, layer).reshape(SUBLANES, 1, N_MOD * dim)
        if layer % 2 == 0:
            w_in, mu = _even_weights(ev_w_in, rwkv_mu, aw, bw)
            n_in = w_in.shape[2]
            p = matmul_mod_call(stream, mods, w_in, i, tn=1024, **modk)
            col_x, col_gb = 0, bw // bw
            col_r, col_k, col_v = 2 * bw // aw, 2 * bw // aw + 1, 2 * bw // aw + 2
            col_l = (2 * bw + 3 * aw) // LORA_PAD
            assert (2 * bw + 3 * aw) % LORA_PAD == 0 and n_in == 2 * bw + 3 * aw + LORA_PAD
            mu_i = mu[i]
            o_r = 2 * bw
            prm = dict(
                mu_r=mu_i[None, o_r:o_r + aw], mu_k=mu_i[None, o_r + aw:o_r + 2 * aw],
                mu_v=mu_i[None, o_r + 2 * aw:o_r + 3 * aw], mu_l=mu_i[None, o_r + 3 * aw:],
                w0=rwkv_w0[i], w2=_pad_rows(rwkv_w2[i], LANES).astype(BF16),
                a0=rwkv_a0[i], a2=_pad_rows(rwkv_a2[i], LANES).astype(BF16),
                g2=rwkv_g2[i].astype(BF16), k_k=rwkv_k_k[i][None, :], k_a=rwkv_k_a[i][None, :])
            r_s, v_s, kk, lw, kd, bb, g = rwkv_prep_call(p, prm, col_r=col_r, col_k=col_k, col_v=col_v,
                                                          col_l=col_l, **seq)
            nb = bw // LRU_BLOCK_W
            wcat = jnp.concatenate([lru_wa[i, 0], lru_wx[i, 0], lru_wa[i, 1], lru_wx[i, 1]], axis=-1).astype(BF16)
            bcat = jnp.stack([lru_ba[i, 0].reshape(nb, LRU_BLOCK_W), lru_bx[i, 0].reshape(nb, LRU_BLOCK_W),
                              lru_ba[i, 1].reshape(nb, LRU_BLOCK_W), lru_bx[i, 1].reshape(nb, LRU_BLOCK_W)],
                             axis=1).reshape(1, 4 * bw)
            lprm = dict(conv_w=lru_conv_w[i], conv_b=lru_conv_b[i][None, :], wcat=wcat, bcat=bcat,
                        lsl=RGLRU_C * jax.nn.log_sigmoid(lru_lam[i]))
            la, lu = lru_prep_call(p, lprm, col_x=col_x, **seq)
            yf = rwkv_scan_call(r_s, lw, kd, v_s, kk, bb, reverse=False, **scan)
            yb = rwkv_scan_call(r_s, lw, kd, v_s, kk, bb, reverse=True, **scan)
            hf = lru_scan_call(la, lu, reverse=False, **scan)
            hb = lru_scan_call(la, lu, reverse=True, **scan)
            pprm = dict(r_k=rwkv_r_k[i].reshape(1, aw), lnx_w=rwkv_lnx_w[i][None, :], lnx_b=rwkv_lnx_b[i][None, :])
            ymix = even_post_call(yf, yb, r_s, v_s, kd, g, hf, hb, p, pprm, col_gb=col_gb)
            n_rows = n_all_rows if not last else n_lat_rows
            y = matmul_call(ymix, ev_w_out.astype(BF16), i, tn=1024, n_rows=n_rows)
        else:
            qk_w = C_HEADS * C_QK_HEAD
            v_w = C_HEADS * C_V_HEAD
            main = 2 * qk_w + 2 * v_w
            w_in = od_w_in[..., :main].astype(BF16)
            w_gate = _pad_cols(od_w_in[..., main:], LANES).astype(BF16)
            p = matmul_mod_call(stream, mods, w_in, i, tn=1024, **modk)
            gp = matmul_mod_call(stream, mods, w_gate, i, tn=LANES, **modk)[:, :4 * C_HEADS]
            gp = gp.reshape(n_all_rows, 2, 2, C_HEADS)
            g_rows, g_cols = _mlstm_gates(gp, mlstm_ig_b[i], mlstm_fg_b[i])
            cols = dict(col_q=0, col_k=(qk_w + v_w) // C_QK_HEAD, col_v=(2 * qk_w + v_w) // C_V_HEAD)
            hf = mlstm_call(p, g_rows, g_cols, reverse=False, **scan, **cols)
            hb = mlstm_call(p, g_rows, g_cols, reverse=True, **scan, **cols)
            n_rows = n_all_rows if not last else n_lat_rows
            ymix = odd_post_call(hf, hb, p, mlstm_norm_w[i][None, :], mlstm_norm_b[i][None, :],
                                 n_rows=n_rows, col_o=qk_w // C_V_HEAD)
            y = matmul_call(ymix, od_w_out.astype(BF16), i, tn=1024, n_rows=n_rows)
        n_rows = n_all_rows if not last else n_lat_rows
        stream1, h_moe, ids, gates = ln_route_call(stream, y, mods, norm_g4, norm_b4, layer, router_wt, router_b2,
                                                   n_rows=n_rows, **modk)
        y_slots = moe_ffn(h_moe, ids, moe_w_gate, moe_w_up, moe_w_down, layer)
        stream = ln_combine_call(stream1, y_slots, gates.T, mods, norm_g4, norm_b4, layer, n_rows=n_rows, **modk)
    return stream[:n_lat_rows].reshape(bsz, s_len, dim)


def _pad_rows(w, rows):
    return jnp.pad(w, ((0, 0), (0, rows - w.shape[1]), (0, 0)))


def _mlstm_gates(gp, ig_b, fg_b):
    n_rows = gp.shape[0]
    n_sub = SCAN_ROWS // CHUNK
    outs = []
    for d in range(2):
        ig = GATE_CAP * jnp.tanh((gp[:, d, 0] + ig_b[d]) / GATE_CAP)
        lf = jax.nn.log_sigmoid(GATE_CAP * jnp.tanh((gp[:, d, 1] + fg_b[d]) / GATE_CAP))
        lf = lf.reshape(n_rows // CHUNK, CHUNK, C_HEADS)
        b = jnp.cumsum(lf[:, ::-1], axis=1)[:, ::-1] if d == 1 else jnp.cumsum(lf, axis=1)
        both = jnp.stack([b, ig.reshape(n_rows // CHUNK, CHUNK, C_HEADS)], axis=0)
        outs.append(both)
    g = jnp.stack(outs, axis=0)
    g = g.reshape(2, 2, n_rows // SCAN_ROWS, n_sub, CHUNK, C_HEADS)
    g_rows = jnp.transpose(g, (0, 1, 5, 2, 3, 4))
    g_cols = jnp.transpose(g, (0, 1, 5, 2, 4, 3))
    return g_rows, g_cols
```

```python
import functools
import math

import jax
import jax.numpy as jnp
from jax import lax
from jax.experimental import pallas as pl
from jax.experimental.pallas import tpu as pltpu

F32 = jnp.float32
BF16 = jnp.bfloat16

DEPTH = 2
N_MOD = 6
ALPHA = (2 * DEPTH) ** 0.25
LN_EPS = 1e-5
HEAD_DIM = 64
DECAY_LORA = 96
ICL_LORA = 96
GATE_LORA = 256
LNX_EPS = 64e-5
LRU_BLOCK_W = 128
CONV_W = 4
RGLRU_C = 8.0
C_HEADS = 8
C_QK_HEAD = 256
C_V_HEAD = 512
GATE_CAP = 15.0
MLSTM_NORM_EPS = 1e-6
N_EXPERTS = 16
N_GROUPS = 4
EXPERTS_PER_GROUP = N_EXPERTS // N_GROUPS
TOP_K = 2
POS_BASE = 10000.0
GRID_W = 64

LANES = 128
SUBLANES = 8
SCAN_ROWS = 256
CHUNK = 64
SCAN_PAIRS = 4
PREP_ROWS = 128
LN_ROWS = 256
MM_ROWS = 512
MOE_ROWS = 512
MOE_FTILE = 256
MOE_NCHUNK = 1024
MOE_KCHUNK = 512
LORA_PAD = 1024
VMEM_LIMIT = 60 * 1024 * 1024


def _cparams(sem):
    return pltpu.CompilerParams(dimension_semantics=sem, vmem_limit_bytes=VMEM_LIMIT)


def _bdot(a, b):
    return jnp.dot(a.astype(BF16), b.astype(BF16), preferred_element_type=F32)


def _bdot_nt(a, b):
    return lax.dot_general(a.astype(BF16), b.astype(BF16), (((1,), (1,)), ((), ())),
                           preferred_element_type=F32)


def _bdot_tn(a, b):
    return lax.dot_general(a.astype(BF16), b.astype(BF16), (((0,), (0,)), ((), ())),
                           preferred_element_type=F32)


def _softplus(z):
    return jnp.maximum(z, 0.0) + jnp.log1p(jnp.exp(-jnp.abs(z)))


def _adaln_kernel(c_ref, w_ref, b_ref, o_ref):
    c = c_ref[...]
    s = c * jax.nn.sigmoid(c)
    o_ref[...] = _bdot(s, w_ref[...]) + b_ref[...]


def adaln_call(cvec, ada_w, ada_b, layer, tn=512):
    rows, d = cvec.shape
    n = ada_w.shape[2]
    return pl.pallas_call(
        _adaln_kernel,
        out_shape=jax.ShapeDtypeStruct((rows, n), F32),
        grid=(n // tn,),
        in_specs=[
            pl.BlockSpec((rows, d), lambda j: (0, 0)),
            pl.BlockSpec((None, d, tn), lambda j: (layer, 0, j)),
            pl.BlockSpec((None, 1, tn), lambda j: (layer, 0, j)),
        ],
        out_specs=pl.BlockSpec((rows, tn), lambda j: (0, j)),
        compiler_params=_cparams(("parallel",)),
        name="adaln",
    )(cvec, ada_w, ada_b)


def _mm_mod_kernel(x_ref, sh_ref, sc_ref, w_ref, o_ref, xb_ref):
    @pl.when(pl.program_id(1) == 0)
    def _():
        xb_ref[...] = (x_ref[...] * (1.0 + sc_ref[...]) + sh_ref[...]).astype(BF16)

    o_ref[...] = jnp.dot(xb_ref[...], w_ref[...], preferred_element_type=F32)


def matmul_mod_call(x, mods, w, layer_w, *, rows_per_mod, n_mod_rows, tn, n_rows=None):
    r, d = x.shape
    n = w.shape[2]
    r = r if n_rows is None else n_rows
    tm = MM_ROWS

    def midx(i):
        return jnp.minimum(i * tm // rows_per_mod, n_mod_rows - 1)

    return pl.pallas_call(
        _mm_mod_kernel,
        out_shape=jax.ShapeDtypeStruct((r, n), F32),
        grid=(r // tm, n // tn),
        in_specs=[
            pl.BlockSpec((tm, d), lambda i, j: (i, 0)),
            pl.BlockSpec((None, 1, d), lambda i, j: (midx(i), 0, 0)),
            pl.BlockSpec((None, 1, d), lambda i, j: (midx(i), 0, 1)),
            pl.BlockSpec((None, d, tn), lambda i, j: (layer_w, 0, j)),
        ],
        out_specs=pl.BlockSpec((tm, tn), lambda i, j: (i, j)),
        scratch_shapes=[pltpu.VMEM((tm, d), BF16)],
        compiler_params=_cparams(("parallel", "arbitrary")),
        name="proj_in",
    )(x, mods, mods, w)


def _mm_kernel(x_ref, w_ref, o_ref):
    o_ref[...] = jnp.dot(x_ref[...], w_ref[...], preferred_element_type=F32)


def matmul_call(x, w, layer_w, *, tn, n_rows=None):
    r, k = x.shape
    n = w.shape[2]
    r = r if n_rows is None else n_rows
    tm = MM_ROWS
    return pl.pallas_call(
        _mm_kernel,
        out_shape=jax.ShapeDtypeStruct((r, n), F32),
        grid=(r // tm, n // tn),
        in_specs=[
            pl.BlockSpec((tm, k), lambda i, j: (i, 0)),
            pl.BlockSpec((None, k, tn), lambda i, j: (layer_w, 0, j)),
        ],
        out_specs=pl.BlockSpec((tm, tn), lambda i, j: (i, j)),
        compiler_params=_cparams(("parallel", "arbitrary")),
        name="proj_out",
    )(x, w)


def _tile_flags(i, tm, n_lat_rows, lat_len, ctx_len):
    row = i * tm
    in_lat = row < n_lat_rows
    pos = jnp.where(in_lat, row % lat_len, (row - n_lat_rows) % ctx_len)
    seq = jnp.where(in_lat, lat_len, ctx_len)
    return pos == 0, pos + tm == seq


def _row_shift(c, prev_row, next_row):
    tm = c.shape[0]
    rows = lax.broadcasted_iota(jnp.int32, c.shape, 0)
    xp = jnp.where(rows == 0, prev_row, pltpu.roll(c, 1, axis=0))
    xn = jnp.where(rows == tm - 1, next_row, pltpu.roll(c, tm - 1, axis=0))
    return xp, xn


def _head_ones():
    r = lax.broadcasted_iota(jnp.int32, (LANES, LANES), 0) // HEAD_DIM
    c = lax.broadcasted_iota(jnp.int32, (LANES, LANES), 1) // HEAD_DIM
    return (r == c).astype(BF16)


def _head_sum(x, ones):
    w = x.shape[1]
    parts = [jnp.dot(x[:, j:j + LANES].astype(BF16), ones, preferred_element_type=F32)
             for j in range(0, w, LANES)]
    return jnp.concatenate(parts, axis=1)


def _halo_specs(tm, width, col, n_rows):
    per = tm // SUBLANES
    last = n_rows // SUBLANES - 1
    return [
        pl.BlockSpec((tm, width), lambda i: (i, col)),
        pl.BlockSpec((SUBLANES, width), lambda i: (jnp.maximum(i * per - 1, 0), col)),
        pl.BlockSpec((SUBLANES, width), lambda i: (jnp.minimum((i + 1) * per, last), col)),
    ]


def _rwkv_prep_kernel(rc, rp, rn, kc, kp, kn, vc, vp, vn, lc, lp, ln,
                      mu_r, mu_k, mu_v, mu_l, w0, w2, a0, a2, g2, kk_s, ka_s,
                      r_o, v_o, kk_o, lw_o, kd_o, bb_o, g_o, *, tm, n_lat_rows, lat_len, ctx_len):
    first, last = _tile_flags(pl.program_id(0), tm, n_lat_rows, lat_len, ctx_len)

    def shifted(c_ref, p_ref, n_ref, mu_ref):
        c = c_ref[...]
        prev_row = jnp.where(first, 0.0, p_ref[SUBLANES - 1:SUBLANES, :])
        next_row = jnp.where(last, 0.0, n_ref[0:1, :])
        xp, xn = _row_shift(c, prev_row, next_row)
        return c + mu_ref[...] * (0.5 * (xp + xn) - c)

    r = shifted(rc, rp, rn, mu_r)
    k = shifted(kc, kp, kn, mu_k)
    v = shifted(vc, vp, vn, mu_v)
    lo = shifted(lc, lp, ln, mu_l)
    r_o[...] = r
    v_o[...] = v

    ones = _head_ones()
    kk = k * kk_s[...]
    ss = _head_sum(kk * kk, ones)
    kk = kk * lax.rsqrt(jnp.maximum(ss, 1e-24))
    kk_o[...] = kk

    for d in range(2):
        wd = lo[:, d * LANES:(d + 1) * LANES]
        ad = lo[:, (2 + d) * LANES:(3 + d) * LANES]
        z = w0[d:d + 1, :] + _bdot(jnp.tanh(wd), w2[d])
        log_w = -_softplus(-z) - 0.5
        lw_o[d] = -jnp.exp(log_w)
        icl = jax.nn.sigmoid(a0[d:d + 1, :] + _bdot(ad, a2[d]))
        kd_o[d] = k * (1.0 + (icl - 1.0) * ka_s[...])
        bb_o[d] = kk * icl
    gd = lo[:, 4 * LANES:4 * LANES + GATE_LORA]
    g_o[...] = _bdot(jax.nn.sigmoid(gd), g2[...])


def rwkv_prep_call(p, prm, *, n_lat_rows, lat_len, ctx_len, col_r, col_k, col_v, col_l):
    r_rows = p.shape[0]
    tm = PREP_ROWS
    aw = prm["mu_r"].shape[1]
    full = lambda shape: pl.BlockSpec(shape, lambda i: (0,) * len(shape))
    in_specs = (_halo_specs(tm, aw, col_r, r_rows) + _halo_specs(tm, aw, col_k, r_rows)
                + _halo_specs(tm, aw, col_v, r_rows) + _halo_specs(tm, LORA_PAD, col_l, r_rows)
                + [full((1, aw)), full((1, aw)), full((1, aw)), full((1, LORA_PAD)),
                   full((2, aw)), full((2, LANES, aw)), full((2, aw)), full((2, LANES, aw)),
                   full((GATE_LORA, aw)), full((1, aw)), full((1, aw))])
    row_spec = pl.BlockSpec((tm, aw), lambda i: (i, 0))
    dir_spec = pl.BlockSpec((2, tm, aw), lambda i: (0, i, 0))
    one = jax.ShapeDtypeStruct((r_rows, aw), F32)
    two = jax.ShapeDtypeStruct((2, r_rows, aw), F32)
    kern = functools.partial(_rwkv_prep_kernel, tm=tm, n_lat_rows=n_lat_rows, lat_len=lat_len, ctx_len=ctx_len)
    return pl.pallas_call(
        kern,
        out_shape=(one, one, one, two, two, two, one),
        grid=(r_rows // tm,),
        in_specs=in_specs,
        out_specs=(row_spec, row_spec, row_spec, dir_spec, dir_spec, dir_spec, row_spec),
        compiler_params=_cparams(("parallel",)),
        name="rwkv_prep",
    )(p, p, p, p, p, p, p, p, p, p, p, p,
      prm["mu_r"], prm["mu_k"], prm["mu_v"], prm["mu_l"], prm["w0"], prm["w2"], prm["a0"], prm["a2"],
      prm["g2"], prm["k_k"], prm["k_a"])


def _lru_prep_kernel(xc_ref, xp_ref, xn_ref, cw, cb, wcat, bcat, lsl, a_o, u_o,
                     *, tm, n_lat_rows, lat_len, ctx_len):
    first, last = _tile_flags(pl.program_id(0), tm, n_lat_rows, lat_len, ctx_len)
    x = xc_ref[...]
    prev_row = jnp.where(first, 0.0, xp_ref[SUBLANES - 1:SUBLANES, :])
    next1 = jnp.where(last, 0.0, xn_ref[0:1, :])
    next2 = jnp.where(last, 0.0, xn_ref[1:2, :])
    rows = lax.broadcasted_iota(jnp.int32, x.shape, 0)
    xm1 = jnp.where(rows == 0, prev_row, pltpu.roll(x, 1, axis=0))
    xp1 = jnp.where(rows == tm - 1, next1, pltpu.roll(x, tm - 1, axis=0))
    xp2 = jnp.where(rows == tm - 1, next2, jnp.where(rows == tm - 2, next1, pltpu.roll(x, tm - 2, axis=0)))
    xc = cb[...] + xm1 * cw[0:1, :] + x * cw[1:2, :] + xp1 * cw[2:3, :] + xp2 * cw[3:4, :]
    nb = x.shape[1] // LRU_BLOCK_W
    gates = jnp.concatenate(
        [_bdot(xc[:, n * LRU_BLOCK_W:(n + 1) * LRU_BLOCK_W], wcat[n]) for n in range(nb)], axis=1)
    gates = gates + bcat[...]
    for d in range(2):
        gr = jnp.concatenate([gates[:, (4 * n + 2 * d) * LRU_BLOCK_W:(4 * n + 2 * d + 1) * LRU_BLOCK_W]
                              for n in range(nb)], axis=1)
        gi = jnp.concatenate([gates[:, (4 * n + 2 * d + 1) * LRU_BLOCK_W:(4 * n + 2 * d + 2) * LRU_BLOCK_W]
                              for n in range(nb)], axis=1)
        log_a = jax.nn.sigmoid(gr) * lsl[d:d + 1, :]
        a = jnp.exp(log_a)
        om = -jnp.tanh(log_a) * (a * a + 1.0)
        a_o[d] = a
        u_o[d] = xc * jax.nn.sigmoid(gi) * jnp.sqrt(om)


def lru_prep_call(p, prm, *, n_lat_rows, lat_len, ctx_len, col_x):
    r_rows = p.shape[0]
    tm = PREP_ROWS
    bw = prm["conv_b"].shape[1]
    nb = bw // LRU_BLOCK_W
    full = lambda shape: pl.BlockSpec(shape, lambda i: (0,) * len(shape))
    in_specs = _halo_specs(tm, bw, col_x, r_rows) + [
        full((CONV_W, bw)), full((1, bw)), full((nb, LRU_BLOCK_W, 4 * LRU_BLOCK_W)), full((1, 4 * bw)), full((2, bw))]
    dir_spec = pl.BlockSpec((2, tm, bw), lambda i: (0, i, 0))
    two = jax.ShapeDtypeStruct((2, r_rows, bw), F32)
    kern = functools.partial(_lru_prep_kernel, tm=tm, n_lat_rows=n_lat_rows, lat_len=lat_len, ctx_len=ctx_len)
    return pl.pallas_call(
        kern,
        out_shape=(two, two),
        grid=(r_rows // tm,),
        in_specs=in_specs,
        out_specs=(dir_spec, dir_spec),
        compiler_params=_cparams(("parallel",)),
        name="lru_prep",
    )(p, p, p, prm["conv_w"], prm["conv_b"], prm["wcat"], prm["bcat"], prm["lsl"])


def _scan_block(b, s, *, reverse, n_lat_blocks_total, lat_blocks):
    ctx_blk = n_lat_blocks_total + b
    lat_blk = b * lat_blocks + ((lat_blocks - s) if reverse else (s - 1))
    return jnp.where(s == 0, ctx_blk, lat_blk)


def _rwkv_scan_kernel(r_ref, lw_ref, k_ref, v_ref, kk_ref, bb_ref, y_ref, s_ref, *, reverse):
    @pl.when(pl.program_id(2) == 0)
    def _():
        s_ref[...] = jnp.zeros_like(s_ref)

    L = CHUNK
    ri = lax.broadcasted_iota(jnp.int32, (L, L), 0)
    ci = lax.broadcasted_iota(jnp.int32, (L, L), 1)
    tri = ((ci >= ri) if reverse else (ci <= ri)).astype(F32)
    r2 = lax.broadcasted_iota(jnp.int32, (2 * L, 2 * L), 0) % L
    c2 = lax.broadcasted_iota(jnp.int32, (2 * L, 2 * L), 1) % L
    strict = (c2 > r2) if reverse else (c2 < r2)
    incl = (c2 >= r2) if reverse else (c2 <= r2)
    rr = lax.broadcasted_iota(jnp.int32, (2 * L, 2 * L), 0)
    cc = lax.broadcasted_iota(jnp.int32, (2 * L, 2 * L), 1)
    eye = (rr == cc).astype(F32)
    off_masks = []
    sz = 2
    while sz <= L:
        late, early = (cc, rr) if reverse else (rr, cc)
        off_masks.append((rr // sz == cc // sz) & (late % sz >= sz // 2) & (early % sz < sz // 2))
        sz *= 2
    lane = lax.broadcasted_iota(jnp.int32, (L, LANES), 1)
    head0 = lane < HEAD_DIM

    def stack2(x):
        return jnp.concatenate([jnp.where(head0, x, 0.0), jnp.where(head0, 0.0, x)], axis=0)

    n_sub = SCAN_ROWS // L
    n_pair = r_ref.shape[1] // LANES
    order = [(n_sub - 1 - j) if reverse else j for j in range(n_sub)]

    keys = [(c, p) for c in order for p in range(n_pair)]
    a2, b2, k2, q2, v2, bt2, kt2, decay = ({} for _ in range(8))
    for c in order:
        rows = pl.ds(c * L, L)
        lw_all = lw_ref[rows, :]
        cum_all = jnp.dot(tri, lw_all, preferred_element_type=F32, precision=lax.Precision.HIGHEST)
        for p in range(n_pair):
            cols = slice(p * LANES, (p + 1) * LANES)
            lw = lw_all[:, cols]
            cum = cum_all[:, cols]
            tot = cum[0:1, :] if reverse else cum[L - 1:L, :]
            e_pos = jnp.exp(cum)
            e_neg = jnp.exp(-cum)
            e_rem = jnp.exp(tot - cum)
            kk = kk_ref[rows, cols]
            kd = k_ref[rows, cols]
            bb = bb_ref[rows, cols]
            key = (c, p)
            a2[key] = stack2(-kk * jnp.exp(cum - lw))
            b2[key] = stack2(bb * e_neg)
            k2[key] = stack2(kd * e_neg)
            q2[key] = stack2(r_ref[rows, cols] * e_pos)
            v2[key] = stack2(v_ref[rows, cols])
            bt2[key] = stack2(bb * e_rem)
            kt2[key] = stack2(kd * e_rem)
            decay[key] = jnp.exp(tot)

    mm = {k: jnp.where(strict, _bdot_nt(a2[k], b2[k]), 0.0) for k in keys}
    nn = {k: jnp.where(strict, _bdot_nt(a2[k], k2[k]), 0.0) for k in keys}
    qq = {k: jnp.where(incl, _bdot_nt(q2[k], b2[k]), 0.0) for k in keys}
    zz = {k: jnp.where(incl, _bdot_nt(q2[k], k2[k]), 0.0) for k in keys}
    tinv = {k: eye + jnp.where(off_masks[0], mm[k], 0.0) for k in keys}
    for off in off_masks[1:]:
        half_step = {k: _bdot(tinv[k], jnp.where(off, mm[k], 0.0)) for k in keys}
        tinv = {k: tinv[k] + _bdot(half_step[k], tinv[k]) for k in keys}
    nv = {k: _bdot(nn[k], v2[k]) for k in keys}
    px = {k: _bdot(tinv[k], jnp.concatenate([a2[k], nv[k]], axis=1)) for k in keys}
    p1 = {k: px[k][:, :LANES] for k in keys}
    p2 = {k: px[k][:, LANES:] for k in keys}
    g_mat = {k: q2[k] + _bdot(qq[k], p1[k]) for k in keys}
    y_loc = {k: _bdot(qq[k], p2[k]) + _bdot(zz[k], v2[k]) for k in keys}
    phi = {k: _bdot_tn(p1[k], bt2[k]) for k in keys}
    psi = {k: _bdot_tn(p2[k], bt2[k]) + _bdot_tn(v2[k], kt2[k]) for k in keys}
    items = {k: (g_mat[k], y_loc[k], decay[k], phi[k], psi[k]) for k in keys}

    st = [s_ref[p] for p in range(n_pair)]
    for c in order:
        for p in range(n_pair):
            g, yl, dec, ph, ps = items[(c, p)]
            y2 = _bdot_nt(g, st[p]) + yl
            y_ref[pl.ds(c * L, L), p * LANES:(p + 1) * LANES] = y2[:L, :] + y2[L:, :]
            st[p] = st[p] * dec + _bdot(st[p], ph) + ps
    for p in range(n_pair):
        s_ref[p] = st[p]


def rwkv_scan_call(r, lw, kd, v, kk, bb, *, reverse, batch, lat_len, ctx_len):
    r_rows, aw = r.shape
    d = 1 if reverse else 0
    lat_blocks = lat_len // SCAN_ROWS
    width = SCAN_PAIRS * LANES
    blk = functools.partial(_scan_block, reverse=reverse, n_lat_blocks_total=batch * lat_blocks,
                            lat_blocks=lat_blocks)
    row_spec = pl.BlockSpec((SCAN_ROWS, width), lambda b, h, s: (blk(b, s), h))
    dir_spec = pl.BlockSpec((None, SCAN_ROWS, width), lambda b, h, s: (d, blk(b, s), h))
    return pl.pallas_call(
        functools.partial(_rwkv_scan_kernel, reverse=reverse),
        out_shape=jax.ShapeDtypeStruct((r_rows, aw), F32),
        grid=(batch, aw // width, 1 + lat_blocks),
        in_specs=[row_spec, dir_spec, dir_spec, row_spec, row_spec, dir_spec],
        out_specs=row_spec,
        scratch_shapes=[pltpu.VMEM((SCAN_PAIRS, LANES, LANES), F32)],
        compiler_params=_cparams(("parallel", "parallel", "arbitrary")),
        name="rwkv_scan_bwd" if reverse else "rwkv_scan_fwd",
    )(r, lw, kd, v, kk, bb)


def _lru_scan_kernel(a_ref, u_ref, h_ref, carry_ref, *, reverse):
    @pl.when(pl.program_id(1) == 0)
    def _():
        carry_ref[...] = jnp.zeros_like(carry_ref)

    w = a_ref.shape[1]
    rows = lax.broadcasted_iota(jnp.int32, (SUBLANES, w), 0)
    n_groups = SCAN_ROWS // SUBLANES

    def body(gi, carry):
        g = (n_groups - 1 - gi) if reverse else gi
        sl = pl.ds(pl.multiple_of(g * SUBLANES, SUBLANES), SUBLANES)
        a = a_ref[sl, :]
        u = u_ref[sl, :]
        for sh in (1, 2, 4):
            if reverse:
                ok = rows < SUBLANES - sh
                a_s = pltpu.roll(a, SUBLANES - sh, axis=0)
                u_s = pltpu.roll(u, SUBLANES - sh, axis=0)
            else:
                ok = rows >= sh
                a_s = pltpu.roll(a, sh, axis=0)
                u_s = pltpu.roll(u, sh, axis=0)
            u = jnp.where(ok, a * u_s + u, u)
            a = jnp.where(ok, a * a_s, a)
        h = a * carry + u
        h_ref[sl, :] = h
        return h[0:1, :] if reverse else h[SUBLANES - 1:SUBLANES, :]

    carry_ref[...] = lax.fori_loop(0, n_groups, body, carry_ref[...])


def lru_scan_call(a, u, *, reverse, batch, lat_len, ctx_len):
    _, r_rows, bw = a.shape
    d = 1 if reverse else 0
    lat_blocks = lat_len // SCAN_ROWS
    blk = functools.partial(_scan_block, reverse=reverse, n_lat_blocks_total=batch * lat_blocks,
                            lat_blocks=lat_blocks)
    dir_spec = pl.BlockSpec((None, SCAN_ROWS, bw), lambda b, s: (d, blk(b, s), 0))
    return pl.pallas_call(
        functools.partial(_lru_scan_kernel, reverse=reverse),
        out_shape=jax.ShapeDtypeStruct((r_rows, bw), F32),
        grid=(batch, 1 + lat_blocks),
        in_specs=[dir_spec, dir_spec],
        out_specs=pl.BlockSpec((SCAN_ROWS, bw), lambda b, s: (blk(b, s), 0)),
        scratch_shapes=[pltpu.VMEM((1, bw), F32)],
        compiler_params=_cparams(("parallel", "arbitrary")),
        name="lru_scan_bwd" if reverse else "lru_scan_fwd",
    )(a, u)


def _even_post_kernel(yf, yb, r_ref, v_ref, kdf, kdb, g_ref, hf, hb, gb_ref, rk, lnw, lnb, o_ref):
    ones = _head_ones()
    y = yf[...] + yb[...]
    inv_n = 1.0 / HEAD_DIM
    yc = y - _head_sum(y, ones) * inv_n
    var = _head_sum(yc * yc, ones) * inv_n
    hn = yc * lax.rsqrt(var + LNX_EPS) * lnw[...] + lnb[...]
    r = r_ref[...]
    v = v_ref[...]
    bonus = _head_sum(r * kdf[...] * rk[...], ones) * v + _head_sum(r * kdb[...] * rk[...], ones) * v
    ya = (hn + bonus) * g_ref[...]
    yl = (hf[...] + hb[...]) * jax.nn.gelu(gb_ref[...])
    aw = ya.shape[1]
    o_ref[:, :aw] = ya.astype(BF16)
    o_ref[:, aw:] = yl.astype(BF16)


def even_post_call(yf, yb, r, v, kd, g, hf, hb, p, prm, *, col_gb):
    r_rows, aw = yf.shape
    bw = hf.shape[1]
    tm = PREP_ROWS
    row = lambda w: pl.BlockSpec((tm, w), lambda i: (i, 0))
    full = lambda shape: pl.BlockSpec(shape, lambda i: (0,) * len(shape))
    return pl.pallas_call(
        _even_post_kernel,
        out_shape=jax.ShapeDtypeStruct((r_rows, aw + bw), BF16),
        grid=(r_rows // tm,),
        in_specs=[row(aw), row(aw), row(aw), row(aw),
                  pl.BlockSpec((None, tm, aw), lambda i: (0, i, 0)),
                  pl.BlockSpec((None, tm, aw), lambda i: (1, i, 0)),
                  row(aw), row(bw), row(bw),
                  pl.BlockSpec((tm, bw), lambda i: (i, col_gb)),
                  full((1, aw)), full((1, aw)), full((1, aw))],
        out_specs=row(aw + bw),
        compiler_params=_cparams(("parallel",)),
        name="even_post",
    )(yf, yb, r, v, kd, kd, g, hf, hb, p, prm["r_k"], prm["lnx_w"], prm["lnx_b"])


def _mlstm_kernel(q_ref, k_ref, v_ref, br_ref, bc_ref, ir_ref, ic_ref, h_ref, c_ref, n_ref, m_ref, *, reverse):
    @pl.when(pl.program_id(2) == 0)
    def _():
        c_ref[...] = jnp.zeros_like(c_ref)
        n_ref[...] = jnp.zeros_like(n_ref)
        m_ref[...] = jnp.zeros_like(m_ref)

    L = CHUNK
    ri = lax.broadcasted_iota(jnp.int32, (L, L), 0)
    ci = lax.broadcasted_iota(jnp.int32, (L, L), 1)
    causal = (ci >= ri) if reverse else (ci <= ri)
    n_sub = SCAN_ROWS // L
    scale = C_QK_HEAD ** -0.5
    for j in range(n_sub):
        c = (n_sub - 1 - j) if reverse else j
        rows = pl.ds(c * L, L)
        b_r = br_ref[c:c + 1, :]
        ig_r = ir_ref[c:c + 1, :]
        b_c = bc_ref[:, c:c + 1]
        ig_c = ic_ref[:, c:c + 1]
        b_tot = b_r[:, 0:1] if reverse else b_r[:, L - 1:L]
        m_st = m_ref[...]
        c_st = c_ref[...]
        n_st = n_ref[...]
        q = q_ref[rows, :] * scale
        k = k_ref[rows, :]
        v = v_ref[rows, :]
        carry_log = b_tot + m_st
        w_in = b_tot - b_c + ig_c
        m_new = jnp.maximum(carry_log, jnp.max(w_in, axis=0, keepdims=True))
        ew = jnp.exp(w_in - m_new)
        ec = jnp.exp(carry_log - m_new)
        dmat = jnp.where(causal, b_c - b_r + ig_r, -jnp.inf)
        inter = b_c + m_st
        m_t = jnp.maximum(inter, jnp.max(dmat, axis=1, keepdims=True))
        scores = _bdot_nt(q, k) * jnp.exp(dmat - m_t)
        e_inter = jnp.exp(inter - m_t)
        num = e_inter * _bdot(q, c_st) + _bdot(scores, v)
        den = e_inter * jnp.sum(q * n_st, axis=1, keepdims=True) + jnp.sum(scores, axis=1, keepdims=True)
        h_ref[rows, :] = num / jnp.maximum(jnp.abs(den), jnp.exp(-m_t))
        kw = k * ew
        c_ref[...] = ec * c_st + _bdot_tn(kw, v)
        n_ref[...] = ec * n_st + jnp.sum(kw, axis=0, keepdims=True)
        m_ref[...] = m_new


def mlstm_call(p, g_rows, g_cols, *, reverse, batch, lat_len, ctx_len, col_q, col_k, col_v):
    r_rows = p.shape[0]
    d = 1 if reverse else 0
    n_sub = SCAN_ROWS // CHUNK
    lat_blocks = lat_len // SCAN_ROWS
    blk = functools.partial(_scan_block, reverse=reverse, n_lat_blocks_total=batch * lat_blocks,
                            lat_blocks=lat_blocks)
    grow = lambda which: pl.BlockSpec((None, None, None, None, n_sub, CHUNK),
                                      lambda b, h, s: (d, which, h, blk(b, s), 0, 0))
    gcol = lambda which: pl.BlockSpec((None, None, None, None, CHUNK, n_sub),
                                      lambda b, h, s: (d, which, h, blk(b, s), 0, 0))
    return pl.pallas_call(
        functools.partial(_mlstm_kernel, reverse=reverse),
        out_shape=jax.ShapeDtypeStruct((r_rows, C_HEADS * C_V_HEAD), F32),
        grid=(batch, C_HEADS, 1 + lat_blocks),
        in_specs=[
            pl.BlockSpec((SCAN_ROWS, C_QK_HEAD), lambda b, h, s: (blk(b, s), col_q + h)),
            pl.BlockSpec((SCAN_ROWS, C_QK_HEAD), lambda b, h, s: (blk(b, s), col_k + h)),
            pl.BlockSpec((SCAN_ROWS, C_V_HEAD), lambda b, h, s: (blk(b, s), col_v + h)),
            grow(0), gcol(0), grow(1), gcol(1),
        ],
        out_specs=pl.BlockSpec((SCAN_ROWS, C_V_HEAD), lambda b, h, s: (blk(b, s), h)),
        scratch_shapes=[pltpu.VMEM((C_QK_HEAD, C_V_HEAD), F32), pltpu.VMEM((1, C_QK_HEAD), F32),
                        pltpu.VMEM((1, 1), F32)],
        compiler_params=_cparams(("parallel", "parallel", "arbitrary")),
        name="mlstm_bwd" if reverse else "mlstm_fwd",
    )(p, p, p, g_rows, g_cols, g_rows, g_cols)


def _odd_post_kernel(hf, hb, o_ref_in, nw, nb, y_ref):
    x = hf[...] + hb[...]
    xc = x - jnp.mean(x, axis=1, keepdims=True)
    var = jnp.mean(xc * xc, axis=1, keepdims=True)
    hn = xc * lax.rsqrt(var + MLSTM_NORM_EPS) * nw[...] + nb[...]
    y_ref[...] = (hn * jax.nn.sigmoid(o_ref_in[...])).astype(BF16)


def odd_post_call(hf, hb, p, nw, nb, *, n_rows, col_o):
    vw = hf.shape[1]
    tm = LN_ROWS
    head = pl.BlockSpec((tm, C_V_HEAD), lambda i, h: (i, h))
    par = pl.BlockSpec((1, C_V_HEAD), lambda i, h: (0, h))
    return pl.pallas_call(
        _odd_post_kernel,
        out_shape=jax.ShapeDtypeStruct((n_rows, vw), BF16),
        grid=(n_rows // tm, C_HEADS),
        in_specs=[head, head, pl.BlockSpec((tm, C_V_HEAD), lambda i, h: (i, col_o + h)), par, par],
        out_specs=head,
        compiler_params=_cparams(("parallel", "parallel")),
        name="odd_post",
    )(hf, hb, p, nw, nb)


def _layer_norm(z, g, b):
    zc = z - jnp.mean(z, axis=1, keepdims=True)
    var = jnp.mean(zc * zc, axis=1, keepdims=True)
    return zc * lax.rsqrt(var + LN_EPS) * g + b


def _route(logits_t, bias):
    aff = [jax.nn.sigmoid(logits_t[e:e + 1, :]) for e in range(N_EXPERTS)]
    biased = [aff[e] + bias[e:e + 1, :] for e in range(N_EXPERTS)]
    best_g = best_v = None
    for g in range(N_GROUPS):
        m = biased[g * EXPERTS_PER_GROUP:(g + 1) * EXPERTS_PER_GROUP]
        pair = None
        for i in range(EXPERTS_PER_GROUP):
            for j in range(i + 1, EXPERTS_PER_GROUP):
                hi = jnp.maximum(m[i], m[j])
                lo_ = jnp.minimum(m[i], m[j])
                s = hi + lo_
                pair = s if pair is None else jnp.maximum(pair, s)
        if best_v is None:
            best_v, best_g = pair, jnp.zeros(pair.shape, jnp.int32)
        else:
            upd = pair > best_v
            best_g = jnp.where(upd, g, best_g)
            best_v = jnp.where(upd, pair, best_v)
    ids, sels = [], []
    taken = None
    for _ in range(TOP_K):
        cur_v = cur_i = cur_a = None
        for e in range(N_EXPERTS):
            ok = best_g == (e // EXPERTS_PER_GROUP)
            if taken is not None:
                ok = jnp.logical_and(ok, taken != e)
            val = jnp.where(ok, biased[e], -jnp.inf)
            if cur_v is None:
                cur_v, cur_i, cur_a = val, jnp.zeros(val.shape, jnp.int32), aff[e]
            else:
                upd = val > cur_v
                cur_i = jnp.where(upd, e, cur_i)
                cur_a = jnp.where(upd, aff[e], cur_a)
                cur_v = jnp.where(upd, val, cur_v)
        ids.append(cur_i)
        sels.append(cur_a)
        taken = cur_i
    tot = sels[0] + sels[1]
    return ids, [sels[0] / tot, sels[1] / tot]


def _ln_route_kernel(lat_ref, y_ref, gate_ref, sh_ref, sc_ref, g_ref, b_ref, rw_ref, rb_ref,
                     lat_o, h_o, id_o, gt_o):
    z = ALPHA * lat_ref[...] + gate_ref[...] * y_ref[...]
    ln = _layer_norm(z, g_ref[...], b_ref[...])
    lat_o[...] = ln
    h = ln * (1.0 + sc_ref[...]) + sh_ref[...]
    half = h.shape[1] // 2
    bits = lax.bitcast_convert_type(h.astype(BF16).astype(F32), jnp.uint32)
    h_o[...] = (bits[:, :half] >> 16) | bits[:, half:]
    logits_t = lax.dot_general(rw_ref[...], h, (((1,), (1,)), ((), ())), preferred_element_type=F32,
                               precision=lax.Precision.HIGHEST)
    ids, gts = _route(logits_t, rb_ref[...])
    id_o[...] = jnp.concatenate(ids, axis=0)
    gt_o[...] = jnp.concatenate(gts, axis=0)


def ln_route_call(lat, y, mods, norm_g, norm_b, layer, router_wt, router_b, *, n_rows, rows_per_mod, n_mod_rows):
    d = lat.shape[1]
    tm = LN_ROWS

    def midx(i):
        return jnp.minimum(i * tm // rows_per_mod, n_mod_rows - 1)

    row = pl.BlockSpec((tm, d), lambda i: (i, 0))
    packed = pl.BlockSpec((tm, d // 2), lambda i: (i, 0))
    mod = lambda k: pl.BlockSpec((None, 1, d), lambda i: (midx(i), 0, k))
    nrm = pl.BlockSpec((None, None, 1, d), lambda i: (layer, 0, 0, 0))
    full = lambda shape: pl.BlockSpec(shape, lambda i: (0,) * len(shape))
    sel = pl.BlockSpec((TOP_K, tm), lambda i: (0, i))
    return pl.pallas_call(
        _ln_route_kernel,
        out_shape=(jax.ShapeDtypeStruct((n_rows, d), F32), jax.ShapeDtypeStruct((n_rows, d // 2), jnp.uint32),
                   jax.ShapeDtypeStruct((TOP_K, n_rows), jnp.int32), jax.ShapeDtypeStruct((TOP_K, n_rows), F32)),
        grid=(n_rows // tm,),
        in_specs=[row, row, mod(2), mod(3), mod(4), nrm, nrm, full((N_EXPERTS, d)), full((N_EXPERTS, 1))],
        out_specs=(row, packed, sel, sel),
        compiler_params=_cparams(("parallel",)),
        name="ln_route",
    )(lat, y, mods, mods, mods, norm_g, norm_b, router_wt, router_b)


def _ln_combine_kernel(lat_ref, ys_ref, gt_ref, gate_ref, g_ref, b_ref, lat_o):
    d = lat_ref.shape[1]
    gt = gt_ref[...]
    f = ys_ref[:, :d] * gt[:, 0:1] + ys_ref[:, d:] * gt[:, 1:2]
    z = ALPHA * lat_ref[...] + gate_ref[...] * f
    lat_o[...] = _layer_norm(z, g_ref[...], b_ref[...])


def ln_combine_call(lat, y_slots, gates_t, mods, norm_g, norm_b, layer, *, n_rows, rows_per_mod, n_mod_rows):
    d = lat.shape[1]
    tm = LN_ROWS

    def midx(i):
        return jnp.minimum(i * tm // rows_per_mod, n_mod_rows - 1)

    row = pl.BlockSpec((tm, d), lambda i: (i, 0))
    nrm = pl.BlockSpec((None, None, 1, d), lambda i: (layer, 1, 0, 0))
    return pl.pallas_call(
        _ln_combine_kernel,
        out_shape=jax.ShapeDtypeStruct((n_rows, d), F32),
        grid=(n_rows // tm,),
        in_specs=[row, pl.BlockSpec((tm, TOP_K * d), lambda i: (i, 0)), pl.BlockSpec((tm, TOP_K), lambda i: (i, 0)),
                  pl.BlockSpec((None, 1, d), lambda i: (midx(i), 0, 5)), nrm, nrm],
        out_specs=row,
        compiler_params=_cparams(("parallel",)),
        name="ln_combine",
    )(lat, y_slots, gates_t, mods, norm_g, norm_b)


def _moe_kernel(nv_ref, be_ref, jw_ref, src_cur, src_nxt, dst_cur, h_hbm, wg_ref, wu_ref, wd_ref, y_hbm,
                xbuf, acc, gsem, ssem, *, n_blocks, n_j):
    del be_ref, jw_ref
    i = pl.program_id(0)
    j = pl.program_id(1)
    buf = i % 2
    nvi = nv_ref[i]
    tm, half = xbuf.shape[1], xbuf.shape[2]
    d = acc.shape[2]

    def row_in(tok, r, b):
        return pltpu.make_async_copy(h_hbm.at[pl.ds(tok, 1)], xbuf.at[b, pl.ds(r, 1)], gsem.at[b])

    def row_out(r, slot, b):
        return pltpu.make_async_copy(acc.at[b, pl.ds(r, 1)], y_hbm.at[pl.ds(slot, 1)], ssem.at[b])

    def gather_start(src_ref, n, b):
        def body(r, carry):
            row_in(src_ref[0, r], r, b).start()
            return carry
        lax.fori_loop(0, n, body, 0)

    def gather_wait(n, b):
        def body(r, carry):
            row_in(0, r, b).wait()
            return carry
        lax.fori_loop(0, n, body, 0)

    def scatter_start(n, b):
        def body(r, carry):
            row_out(r, dst_cur[0, r], b).start()
            return carry
        lax.fori_loop(0, n, body, 0)

    def scatter_wait(n, b):
        def body(r, carry):
            row_out(r, 0, b).wait()
            return carry
        lax.fori_loop(0, n, body, 0)

    @pl.when((i == 0) & (j == 0))
    def _():
        xbuf[...] = jnp.zeros_like(xbuf)
        gather_start(src_cur, nvi, 0)

    @pl.when(j == 0)
    def _():
        @pl.when(i + 1 < n_blocks)
        def _():
            gather_start(src_nxt, nv_ref[jnp.minimum(i + 1, n_blocks - 1)], 1 - buf)
        gather_wait(nvi, buf)
        acc[buf] = jnp.zeros((tm, d), F32)

    @pl.when(nvi > 0)
    def _():
        tf = wg_ref.shape[1]
        g = jnp.zeros((tm, tf), F32)
        u = jnp.zeros((tm, tf), F32)
        for c0 in range(0, half, MOE_KCHUNK):
            xp = xbuf[buf, :, c0:c0 + MOE_KCHUNK]
            lo = lax.bitcast_convert_type(xp << 16, F32)
            hi = lax.bitcast_convert_type(xp & jnp.uint32(0xFFFF0000), F32)
            g = g + _bdot(lo, wg_ref[c0:c0 + MOE_KCHUNK, :]) + _bdot(hi, wg_ref[half + c0:half + c0 + MOE_KCHUNK, :])
            u = u + _bdot(lo, wu_ref[c0:c0 + MOE_KCHUNK, :]) + _bdot(hi, wu_ref[half + c0:half + c0 + MOE_KCHUNK, :])
        hdn = (g * jax.nn.sigmoid(g) * u).astype(BF16)
        for c0 in range(0, d, MOE_NCHUNK):
            acc[buf, :, c0:c0 + MOE_NCHUNK] += _bdot(hdn, wd_ref[:, c0:c0 + MOE_NCHUNK])

    @pl.when(j == n_j - 1)
    def _():
        @pl.when(i > 0)
        def _():
            scatter_wait(nv_ref[jnp.maximum(i - 1, 0)], 1 - buf)
        scatter_start(nvi, buf)

        @pl.when(i == n_blocks - 1)
        def _():
            scatter_wait(nvi, buf)


def moe_call(h_packed, n_valid, block_e, j_of, src_tok, dst_slot, w_gate, w_up, w_down, layer, *, n_slots):
    d = 2 * h_packed.shape[1]
    f = w_gate.shape[3]
    tm, tf = MOE_ROWS, MOE_FTILE
    n_blocks = n_valid.shape[0]
    n_j = f // tf
    idx = lambda fn: pl.BlockSpec((None, 1, tm), fn, memory_space=pltpu.SMEM)
    grid_spec = pltpu.PrefetchScalarGridSpec(
        num_scalar_prefetch=3,
        grid=(n_blocks, n_j),
        in_specs=[
            idx(lambda i, j, nv, be, jw: (i, 0, 0)),
            idx(lambda i, j, nv, be, jw: (jnp.minimum(i + 1, n_blocks - 1), 0, 0)),
            idx(lambda i, j, nv, be, jw: (i, 0, 0)),
            pl.BlockSpec(memory_space=pl.ANY),
            pl.BlockSpec((None, None, d, tf), lambda i, j, nv, be, jw: (layer, be[i], 0, jw[i, j])),
            pl.BlockSpec((None, None, d, tf), lambda i, j, nv, be, jw: (layer, be[i], 0, jw[i, j])),
            pl.BlockSpec((None, None, tf, d), lambda i, j, nv, be, jw: (layer, be[i], jw[i, j], 0)),
        ],
        out_specs=pl.BlockSpec(memory_space=pl.ANY),
        scratch_shapes=[pltpu.VMEM((2, tm, d // 2), jnp.uint32), pltpu.VMEM((2, tm, d), F32),
                        pltpu.SemaphoreType.DMA((2,)), pltpu.SemaphoreType.DMA((2,))],
    )
    return pl.pallas_call(
        functools.partial(_moe_kernel, n_blocks=n_blocks, n_j=n_j),
        out_shape=jax.ShapeDtypeStruct((n_slots, d), F32),
        grid_spec=grid_spec,
        compiler_params=_cparams(("arbitrary", "arbitrary")),
        name="moe_experts",
    )(n_valid, block_e, j_of, src_tok, src_tok, dst_slot, h_packed, w_gate, w_up, w_down)


def moe_ffn(h_packed, ids, w_gate, w_up, w_down, layer):
    n_tok = h_packed.shape[0]
    d = 2 * h_packed.shape[1]
    n_slots = n_tok * TOP_K
    tm = MOE_ROWS
    n_j = w_gate.shape[3] // MOE_FTILE
    i32 = jnp.int32
    flat_e = ids.T.reshape(n_slots)
    order = jnp.argsort(flat_e, stable=True).astype(i32)
    counts = jnp.sum((flat_e[:, None] == jnp.arange(N_EXPERTS, dtype=i32)[None, :]).astype(i32), axis=0)
    start = jnp.cumsum(counts) - counts
    padded = (counts + tm - 1) // tm * tm
    pad_end = jnp.cumsum(padded)
    pad_start = pad_end - padded
    n_blocks = -(-n_slots // tm) + N_EXPERTS
    pos = jnp.arange(n_blocks * tm, dtype=i32)
    e_pos = jnp.minimum(jnp.searchsorted(pad_end, pos, side="right"), N_EXPERTS - 1).astype(i32)
    r_pos = pos - pad_start[e_pos]
    valid = r_pos < counts[e_pos]
    slot = order[jnp.clip(start[e_pos] + r_pos, 0, n_slots - 1)]
    src_tok = jnp.where(valid, slot // TOP_K, 0).reshape(n_blocks, 1, tm)
    dst_slot = jnp.where(valid, slot, 0).reshape(n_blocks, 1, tm)
    n_valid = jnp.sum(valid.reshape(n_blocks, tm).astype(i32), axis=1)
    n_used = pad_end[-1] // tm
    last_used = jnp.maximum(n_used - 1, 0)
    blk = jnp.arange(n_blocks, dtype=i32)
    block_e = e_pos.reshape(n_blocks, tm)[:, 0]
    block_e = jnp.where(blk < n_used, block_e, block_e[last_used]).astype(i32)
    j_of = jnp.where((blk < n_used)[:, None], jnp.arange(n_j, dtype=i32)[None, :], n_j - 1).astype(i32)
    ys = moe_call(h_packed, n_valid.astype(i32), block_e, j_of, src_tok.astype(i32), dst_slot.astype(i32),
                  w_gate, w_up, w_down, layer, n_slots=n_slots)
    return ys.reshape(n_tok, TOP_K * d)


def _sincos_2d(n_tokens, dim):
    rows = n_tokens // GRID_W
    row = jnp.repeat(jnp.arange(rows, dtype=F32), GRID_W)
    col = jnp.tile(jnp.arange(GRID_W, dtype=F32), rows)
    quarter = dim // 4
    omega = POS_BASE ** (-jnp.arange(quarter, dtype=F32) / quarter)
    ang_r = row[:, None] * omega[None, :]
    ang_c = col[:, None] * omega[None, :]
    return jnp.concatenate([jnp.sin(ang_r), jnp.cos(ang_r), jnp.sin(ang_c), jnp.cos(ang_c)], -1)


def _pad_cols(w, width):
    return jnp.pad(w, [(0, 0)] * (w.ndim - 1) + [(0, width - w.shape[-1])])


def _even_weights(ev_w_in, rwkv_mu, aw, bw):
    rk = 3 * aw
    lo0 = rk
    pieces = [DECAY_LORA, DECAY_LORA, ICL_LORA, ICL_LORA]

    def regroup(t):
        out = [t[..., rk + 2 * (DECAY_LORA + ICL_LORA) + GATE_LORA:],
               t[..., :rk]]
        off = lo0
        for wdt in pieces:
            out.append(_pad_cols(t[..., off:off + wdt], LANES))
            off += wdt
        out.append(t[..., off:off + GATE_LORA])
        used = 4 * LANES + GATE_LORA
        out.append(jnp.zeros(t.shape[:-1] + (LORA_PAD - used,), t.dtype))
        return jnp.concatenate(out, axis=-1)

    w = regroup(ev_w_in).astype(BF16)
    rwkv_cols = rk + 2 * (DECAY_LORA + ICL_LORA) + GATE_LORA
    mu_full = jnp.concatenate([rwkv_mu, jnp.zeros(rwkv_mu.shape[:-1] + (2 * bw,), rwkv_mu.dtype)], axis=-1)
    mu = regroup(mu_full)
    del rwkv_cols
    return w, mu


def kernel(x, c, ctx, c_ctx, ada_w, ada_b, norm_g, norm_b,
           ev_w_in, ev_w_out, rwkv_mu, rwkv_w0, rwkv_w2, rwkv_a0, rwkv_a2, rwkv_g2,
           rwkv_k_k, rwkv_k_a, rwkv_r_k, rwkv_lnx_w, rwkv_lnx_b,
           lru_conv_w, lru_conv_b, lru_wa, lru_ba, lru_wx, lru_bx, lru_lam,
           od_w_in, od_w_out, mlstm_ig_b, mlstm_fg_b, mlstm_norm_w, mlstm_norm_b,
           router_w, router_b, moe_w_gate, moe_w_up, moe_w_down):
    bsz, s_len, dim = x.shape
    n_ctx = ctx.shape[1]
    n_lat_rows = bsz * s_len
    n_all_rows = n_lat_rows + bsz * n_ctx
    aw = rwkv_k_k.shape[1]
    bw = lru_conv_b.shape[1]
    assert s_len % SCAN_ROWS == 0 and n_ctx == SCAN_ROWS and n_all_rows % MM_ROWS == 0
    assert (bsz * n_ctx) % MM_ROWS == 0 and s_len % MM_ROWS == 0 and bsz + 1 <= SUBLANES
    seq = dict(n_lat_rows=n_lat_rows, lat_len=s_len, ctx_len=n_ctx)
    scan = dict(batch=bsz, lat_len=s_len, ctx_len=n_ctx)
    modk = dict(rows_per_mod=s_len, n_mod_rows=bsz + 1)

    lat = (x + _sincos_2d(s_len, dim).astype(x.dtype)).reshape(n_lat_rows, dim)
    stream = jnp.concatenate([lat, ctx.reshape(bsz * n_ctx, dim)], axis=0)
    cvec = jnp.concatenate([c, c_ctx[None, :], jnp.zeros((SUBLANES - bsz - 1, dim), c.dtype)], axis=0)
    ada_b3 = ada_b[:, None, :]
    norm_g4 = norm_g[:, :, None, :]
    norm_b4 = norm_b[:, :, None, :]
    router_wt = router_w.T
    router_b2 = router_b[:, None]

    for layer in range(DEPTH):
        last = layer == DEPTH - 1
        i = layer // 2
        mods = adaln_call(cvec, ada_w, ada_b3, layer).reshape(SUBLANES, 1, N_MOD * dim)
        if layer % 2 == 0:
            w_in, mu = _even_weights(ev_w_in, rwkv_mu, aw, bw)
            n_in = w_in.shape[2]
            p = matmul_mod_call(stream, mods, w_in, i, tn=1024, **modk)
            col_x, col_gb = 0, bw // bw
            col_r, col_k, col_v = 2 * bw // aw, 2 * bw // aw + 1, 2 * bw // aw + 2
            col_l = (2 * bw + 3 * aw) // LORA_PAD
            assert (2 * bw + 3 * aw) % LORA_PAD == 0 and n_in == 2 * bw + 3 * aw + LORA_PAD
            mu_i = mu[i]
            o_r = 2 * bw
            prm = dict(
                mu_r=mu_i[None, o_r:o_r + aw], mu_k=mu_i[None, o_r + aw:o_r + 2 * aw],
                mu_v=mu_i[None, o_r + 2 * aw:o_r + 3 * aw], mu_l=mu_i[None, o_r + 3 * aw:],
                w0=rwkv_w0[i], w2=_pad_rows(rwkv_w2[i], LANES).astype(BF16),
                a0=rwkv_a0[i], a2=_pad_rows(rwkv_a2[i], LANES).astype(BF16),
                g2=rwkv_g2[i].astype(BF16), k_k=rwkv_k_k[i][None, :], k_a=rwkv_k_a[i][None, :])
            r_s, v_s, kk, lw, kd, bb, g = rwkv_prep_call(p, prm, col_r=col_r, col_k=col_k, col_v=col_v,
                                                          col_l=col_l, **seq)
            nb = bw // LRU_BLOCK_W
            wcat = jnp.concatenate([lru_wa[i, 0], lru_wx[i, 0], lru_wa[i, 1], lru_wx[i, 1]], axis=-1).astype(BF16)
            bcat = jnp.stack([lru_ba[i, 0].reshape(nb, LRU_BLOCK_W), lru_bx[i, 0].reshape(nb, LRU_BLOCK_W),
                              lru_ba[i, 1].reshape(nb, LRU_BLOCK_W), lru_bx[i, 1].reshape(nb, LRU_BLOCK_W)],
                             axis=1).reshape(1, 4 * bw)
            lprm = dict(conv_w=lru_conv_w[i], conv_b=lru_conv_b[i][None, :], wcat=wcat, bcat=bcat,
                        lsl=RGLRU_C * jax.nn.log_sigmoid(lru_lam[i]))
            la, lu = lru_prep_call(p, lprm, col_x=col_x, **seq)
            yf = rwkv_scan_call(r_s, lw, kd, v_s, kk, bb, reverse=False, **scan)
            yb = rwkv_scan_call(r_s, lw, kd, v_s, kk, bb, reverse=True, **scan)
            hf = lru_scan_call(la, lu, reverse=False, **scan)
            hb = lru_scan_call(la, lu, reverse=True, **scan)
            pprm = dict(r_k=rwkv_r_k[i].reshape(1, aw), lnx_w=rwkv_lnx_w[i][None, :], lnx_b=rwkv_lnx_b[i][None, :])
            ymix = even_post_call(yf, yb, r_s, v_s, kd, g, hf, hb, p, pprm, col_gb=col_gb)
            n_rows = n_all_rows if not last else n_lat_rows
            y = matmul_call(ymix, ev_w_out.astype(BF16), i, tn=1024, n_rows=n_rows)
        else:
            qk_w = C_HEADS * C_QK_HEAD
            v_w = C_HEADS * C_V_HEAD
            main = 2 * qk_w + 2 * v_w
            w_in = od_w_in[..., :main].astype(BF16)
            w_gate = _pad_cols(od_w_in[..., main:], LANES).astype(BF16)
            p = matmul_mod_call(stream, mods, w_in, i, tn=1024, **modk)
            gp = matmul_mod_call(stream, mods, w_gate, i, tn=LANES, **modk)[:, :4 * C_HEADS]
            gp = gp.reshape(n_all_rows, 2, 2, C_HEADS)
            g_rows, g_cols = _mlstm_gates(gp, mlstm_ig_b[i], mlstm_fg_b[i])
            cols = dict(col_q=0, col_k=(qk_w + v_w) // C_QK_HEAD, col_v=(2 * qk_w + v_w) // C_V_HEAD)
            hf = mlstm_call(p, g_rows, g_cols, reverse=False, **scan, **cols)
            hb = mlstm_call(p, g_rows, g_cols, reverse=True, **scan, **cols)
            n_rows = n_all_rows if not last else n_lat_rows
            ymix = odd_post_call(hf, hb, p, mlstm_norm_w[i][None, :], mlstm_norm_b[i][None, :],
                                 n_rows=n_rows, col_o=qk_w // C_V_HEAD)
            y = matmul_call(ymix, od_w_out.astype(BF16), i, tn=1024, n_rows=n_rows)
        n_rows = n_all_rows if not last else n_lat_rows
        stream1, h_moe, ids, gates = ln_route_call(stream, y, mods, norm_g4, norm_b4, layer, router_wt, router_b2,
                                                   n_rows=n_rows, **modk)
        y_slots = moe_ffn(h_moe, ids, moe_w_gate, moe_w_up, moe_w_down, layer)
        stream = ln_combine_call(stream1, y_slots, gates.T, mods, norm_g4, norm_b4, layer, n_rows=n_rows, **modk)
    return stream[:n_lat_rows].reshape(bsz, s_len, dim)


def _pad_rows(w, rows):
    return jnp.pad(w, ((0, 0), (0, rows - w.shape[1]), (0, 0)))


def _mlstm_gates(gp, ig_b, fg_b):
    n_rows = gp.shape[0]
    n_sub = SCAN_ROWS // CHUNK
    outs = []
    for d in range(2):
        ig = GATE_CAP * jnp.tanh((gp[:, d, 0] + ig_b[d]) / GATE_CAP)
        lf = jax.nn.log_sigmoid(GATE_CAP * jnp.tanh((gp[:, d, 1] + fg_b[d]) / GATE_CAP))
        lf = lf.reshape(n_rows // CHUNK, CHUNK, C_HEADS)
        b = jnp.cumsum(lf[:, ::-1], axis=1)[:, ::-1] if d == 1 else jnp.cumsum(lf, axis=1)
        both = jnp.stack([b, ig.reshape(n_rows // CHUNK, CHUNK, C_HEADS)], axis=0)
        outs.append(both)
    g = jnp.stack(outs, axis=0)
    g = g.reshape(2, 2, n_rows // SCAN_ROWS, n_sub, CHUNK, C_HEADS)
    g_rows = jnp.transpose(g, (0, 1, 5, 2, 3, 4))
    g_cols = jnp.transpose(g, (0, 1, 5, 2, 4, 3))
    return g_rows, g_cols
```

```python
import functools
import math

import jax
import jax.numpy as jnp
from jax import lax
from jax.experimental import pallas as pl
from jax.experimental.pallas import tpu as pltpu

F32 = jnp.float32
BF16 = jnp.bfloat16

DEPTH = 2
N_MOD = 6
ALPHA = (2 * DEPTH) ** 0.25
LN_EPS = 1e-5
HEAD_DIM = 64
DECAY_LORA = 96
ICL_LORA = 96
GATE_LORA = 256
LNX_EPS = 64e-5
LRU_BLOCK_W = 128
CONV_W = 4
RGLRU_C = 8.0
C_HEADS = 8
C_QK_HEAD = 256
C_V_HEAD = 512
GATE_CAP = 15.0
MLSTM_NORM_EPS = 1e-6
N_EXPERTS = 16
N_GROUPS = 4
EXPERTS_PER_GROUP = N_EXPERTS // N_GROUPS
TOP_K = 2
POS_BASE = 10000.0
GRID_W = 64

LANES = 128
SUBLANES = 8
SCAN_ROWS = 256
CHUNK = 64
SCAN_PAIRS = 4
PREP_ROWS = 128
LN_ROWS = 256
MM_ROWS = 512
MOE_ROWS = 512
MOE_FTILE = 256
MOE_NCHUNK = 1024
MOE_KCHUNK = 512
LORA_PAD = 1024
VMEM_LIMIT = 60 * 1024 * 1024


def _cparams(sem):
    return pltpu.CompilerParams(dimension_semantics=sem, vmem_limit_bytes=VMEM_LIMIT)


def _bdot(a, b):
    return jnp.dot(a.astype(BF16), b.astype(BF16), preferred_element_type=F32)


def _bdot_nt(a, b):
    return lax.dot_general(a.astype(BF16), b.astype(BF16), (((1,), (1,)), ((), ())),
                           preferred_element_type=F32)


def _bdot_tn(a, b):
    return lax.dot_general(a.astype(BF16), b.astype(BF16), (((0,), (0,)), ((), ())),
                           preferred_element_type=F32)


def _softplus(z):
    return jnp.maximum(z, 0.0) + jnp.log1p(jnp.exp(-jnp.abs(z)))


def _adaln_kernel(c_ref, w_ref, b_ref, o_ref):
    c = c_ref[...]
    s = c * jax.nn.sigmoid(c)
    o_ref[...] = _bdot(s, w_ref[...]) + b_ref[...]


def adaln_call(cvec, ada_w, ada_b, layer, tn=512):
    rows, d = cvec.shape
    n = ada_w.shape[2]
    return pl.pallas_call(
        _adaln_kernel,
        out_shape=jax.ShapeDtypeStruct((rows, n), F32),
        grid=(n // tn,),
        in_specs=[
            pl.BlockSpec((rows, d), lambda j: (0, 0)),
            pl.BlockSpec((None, d, tn), lambda j: (layer, 0, j)),
            pl.BlockSpec((None, 1, tn), lambda j: (layer, 0, j)),
        ],
        out_specs=pl.BlockSpec((rows, tn), lambda j: (0, j)),
        compiler_params=_cparams(("parallel",)),
        name="adaln",
    )(cvec, ada_w, ada_b)


def _mm_mod_kernel(x_ref, sh_ref, sc_ref, w_ref, o_ref, xb_ref):
    @pl.when(pl.program_id(1) == 0)
    def _():
        xb_ref[...] = (x_ref[...] * (1.0 + sc_ref[...]) + sh_ref[...]).astype(BF16)

    o_ref[...] = jnp.dot(xb_ref[...], w_ref[...], preferred_element_type=F32)


def matmul_mod_call(x, mods, w, layer_w, *, rows_per_mod, n_mod_rows, tn, n_rows=None):
    r, d = x.shape
    n = w.shape[2]
    r = r if n_rows is None else n_rows
    tm = MM_ROWS

    def midx(i):
        return jnp.minimum(i * tm // rows_per_mod, n_mod_rows - 1)

    return pl.pallas_call(
        _mm_mod_kernel,
        out_shape=jax.ShapeDtypeStruct((r, n), F32),
        grid=(r // tm, n // tn),
        in_specs=[
            pl.BlockSpec((tm, d), lambda i, j: (i, 0)),
            pl.BlockSpec((None, 1, d), lambda i, j: (midx(i), 0, 0)),
            pl.BlockSpec((None, 1, d), lambda i, j: (midx(i), 0, 1)),
            pl.BlockSpec((None, d, tn), lambda i, j: (layer_w, 0, j)),
        ],
        out_specs=pl.BlockSpec((tm, tn), lambda i, j: (i, j)),
        scratch_shapes=[pltpu.VMEM((tm, d), BF16)],
        compiler_params=_cparams(("parallel", "arbitrary")),
        name="proj_in",
    )(x, mods, mods, w)


def _mm_kernel(x_ref, w_ref, o_ref):
    o_ref[...] = jnp.dot(x_ref[...], w_ref[...], preferred_element_type=F32)


def matmul_call(x, w, layer_w, *, tn, n_rows=None):
    r, k = x.shape
    n = w.shape[2]
    r = r if n_rows is None else n_rows
    tm = MM_ROWS
    return pl.pallas_call(
        _mm_kernel,
        out_shape=jax.ShapeDtypeStruct((r, n), F32),
        grid=(r // tm, n // tn),
        in_specs=[
            pl.BlockSpec((tm, k), lambda i, j: (i, 0)),
            pl.BlockSpec((None, k, tn), lambda i, j: (layer_w, 0, j)),
        ],
        out_specs=pl.BlockSpec((tm, tn), lambda i, j: (i, j)),
        compiler_params=_cparams(("parallel", "arbitrary")),
        name="proj_out",
    )(x, w)


def _tile_flags(i, tm, n_lat_rows, lat_len, ctx_len):
    row = i * tm
    in_lat = row < n_lat_rows
    pos = jnp.where(in_lat, row % lat_len, (row - n_lat_rows) % ctx_len)
    seq = jnp.where(in_lat, lat_len, ctx_len)
    return pos == 0, pos + tm == seq


def _row_shift(c, prev_row, next_row):
    tm = c.shape[0]
    rows = lax.broadcasted_iota(jnp.int32, c.shape, 0)
    xp = jnp.where(rows == 0, prev_row, pltpu.roll(c, 1, axis=0))
    xn = jnp.where(rows == tm - 1, next_row, pltpu.roll(c, tm - 1, axis=0))
    return xp, xn


def _head_ones():
    r = lax.broadcasted_iota(jnp.int32, (LANES, LANES), 0) // HEAD_DIM
    c = lax.broadcasted_iota(jnp.int32, (LANES, LANES), 1) // HEAD_DIM
    return (r == c).astype(BF16)


def _head_sum(x, ones):
    w = x.shape[1]
    parts = [jnp.dot(x[:, j:j + LANES].astype(BF16), ones, preferred_element_type=F32)
             for j in range(0, w, LANES)]
    return jnp.concatenate(parts, axis=1)


def _halo_specs(tm, width, col, n_rows):
    per = tm // SUBLANES
    last = n_rows // SUBLANES - 1
    return [
        pl.BlockSpec((tm, width), lambda i: (i, col)),
        pl.BlockSpec((SUBLANES, width), lambda i: (jnp.maximum(i * per - 1, 0), col)),
        pl.BlockSpec((SUBLANES, width), lambda i: (jnp.minimum((i + 1) * per, last), col)),
    ]


def _rwkv_prep_kernel(rc, rp, rn, kc, kp, kn, vc, vp, vn, lc, lp, ln,
                      mu_r, mu_k, mu_v, mu_l, w0, w2, a0, a2, g2, kk_s, ka_s,
                      r_o, v_o, kk_o, lw_o, kd_o, bb_o, g_o, *, tm, n_lat_rows, lat_len, ctx_len):
    first, last = _tile_flags(pl.program_id(0), tm, n_lat_rows, lat_len, ctx_len)

    def shifted(c_ref, p_ref, n_ref, mu_ref):
        c = c_ref[...]
        prev_row = jnp.where(first, 0.0, p_ref[SUBLANES - 1:SUBLANES, :])
        next_row = jnp.where(last, 0.0, n_ref[0:1, :])
        xp, xn = _row_shift(c, prev_row, next_row)
        return c + mu_ref[...] * (0.5 * (xp + xn) - c)

    r = shifted(rc, rp, rn, mu_r)
    k = shifted(kc, kp, kn, mu_k)
    v = shifted(vc, vp, vn, mu_v)
    lo = shifted(lc, lp, ln, mu_l)
    r_o[...] = r
    v_o[...] = v

    ones = _head_ones()
    kk = k * kk_s[...]
    ss = _head_sum(kk * kk, ones)
    kk = kk * lax.rsqrt(jnp.maximum(ss, 1e-24))
    kk_o[...] = kk

    for d in range(2):
        wd = lo[:, d * LANES:(d + 1) * LANES]
        ad = lo[:, (2 + d) * LANES:(3 + d) * LANES]
        z = w0[d:d + 1, :] + _bdot(jnp.tanh(wd), w2[d])
        log_w = -_softplus(-z) - 0.5
        lw_o[d] = -jnp.exp(log_w)
        icl = jax.nn.sigmoid(a0[d:d + 1, :] + _bdot(ad, a2[d]))
        kd_o[d] = k * (1.0 + (icl - 1.0) * ka_s[...])
        bb_o[d] = kk * icl
    gd = lo[:, 4 * LANES:4 * LANES + GATE_LORA]
    g_o[...] = _bdot(jax.nn.sigmoid(gd), g2[...])


def rwkv_prep_call(p, prm, *, n_lat_rows, lat_len, ctx_len, col_r, col_k, col_v, col_l):
    r_rows = p.shape[0]
    tm = PREP_ROWS
    aw = prm["mu_r"].shape[1]
    full = lambda shape: pl.BlockSpec(shape, lambda i: (0,) * len(shape))
    in_specs = (_halo_specs(tm, aw, col_r, r_rows) + _halo_specs(tm, aw, col_k, r_rows)
                + _halo_specs(tm, aw, col_v, r_rows) + _halo_specs(tm, LORA_PAD, col_l, r_rows)
                + [full((1, aw)), full((1, aw)), full((1, aw)), full((1, LORA_PAD)),
                   full((2, aw)), full((2, LANES, aw)), full((2, aw)), full((2, LANES, aw)),
                   full((GATE_LORA, aw)), full((1, aw)), full((1, aw))])
    row_spec = pl.BlockSpec((tm, aw), lambda i: (i, 0))
    dir_spec = pl.BlockSpec((2, tm, aw), lambda i: (0, i, 0))
    one = jax.ShapeDtypeStruct((r_rows, aw), F32)
    two = jax.ShapeDtypeStruct((2, r_rows, aw), F32)
    kern = functools.partial(_rwkv_prep_kernel, tm=tm, n_lat_rows=n_lat_rows, lat_len=lat_len, ctx_len=ctx_len)
    return pl.pallas_call(
        kern,
        out_shape=(one, one, one, two, two, two, one),
        grid=(r_rows // tm,),
        in_specs=in_specs,
        out_specs=(row_spec, row_spec, row_spec, dir_spec, dir_spec, dir_spec, row_spec),
        compiler_params=_cparams(("parallel",)),
        name="rwkv_prep",
    )(p, p, p, p, p, p, p, p, p, p, p, p,
      prm["mu_r"], prm["mu_k"], prm["mu_v"], prm["mu_l"], prm["w0"], prm["w2"], prm["a0"], prm["a2"],
      prm["g2"], prm["k_k"], prm["k_a"])


def _lru_prep_kernel(xc_ref, xp_ref, xn_ref, cw, cb, wcat, bcat, lsl, a_o, u_o,
                     *, tm, n_lat_rows, lat_len, ctx_len):
    first, last = _tile_flags(pl.program_id(0), tm, n_lat_rows, lat_len, ctx_len)
    x = xc_ref[...]
    prev_row = jnp.where(first, 0.0, xp_ref[SUBLANES - 1:SUBLANES, :])
    next1 = jnp.where(last, 0.0, xn_ref[0:1, :])
    next2 = jnp.where(last, 0.0, xn_ref[1:2, :])
    rows = lax.broadcasted_iota(jnp.int32, x.shape, 0)
    xm1 = jnp.where(rows == 0, prev_row, pltpu.roll(x, 1, axis=0))
    xp1 = jnp.where(rows == tm - 1, next1, pltpu.roll(x, tm - 1, axis=0))
    xp2 = jnp.where(rows == tm - 1, next2, jnp.where(rows == tm - 2, next1, pltpu.roll(x, tm - 2, axis=0)))
    xc = cb[...] + xm1 * cw[0:1, :] + x * cw[1:2, :] + xp1 * cw[2:3, :] + xp2 * cw[3:4, :]
    nb = x.shape[1] // LRU_BLOCK_W
    gates = jnp.concatenate(
        [_bdot(xc[:, n * LRU_BLOCK_W:(n + 1) * LRU_BLOCK_W], wcat[n]) for n in range(nb)], axis=1)
    gates = gates + bcat[...]
    for d in range(2):
        gr = jnp.concatenate([gates[:, (4 * n + 2 * d) * LRU_BLOCK_W:(4 * n + 2 * d + 1) * LRU_BLOCK_W]
                              for n in range(nb)], axis=1)
        gi = jnp.concatenate([gates[:, (4 * n + 2 * d + 1) * LRU_BLOCK_W:(4 * n + 2 * d + 2) * LRU_BLOCK_W]
                              for n in range(nb)], axis=1)
        log_a = jax.nn.sigmoid(gr) * lsl[d:d + 1, :]
        a = jnp.exp(log_a)
        om = -jnp.tanh(log_a) * (a * a + 1.0)
        a_o[d] = a
        u_o[d] = xc * jax.nn.sigmoid(gi) * jnp.sqrt(om)


def lru_prep_call(p, prm, *, n_lat_rows, lat_len, ctx_len, col_x):
    r_rows = p.shape[0]
    tm = PREP_ROWS
    bw = prm["conv_b"].shape[1]
    nb = bw // LRU_BLOCK_W
    full = lambda shape: pl.BlockSpec(shape, lambda i: (0,) * len(shape))
    in_specs = _halo_specs(tm, bw, col_x, r_rows) + [
        full((CONV_W, bw)), full((1, bw)), full((nb, LRU_BLOCK_W, 4 * LRU_BLOCK_W)), full((1, 4 * bw)), full((2, bw))]
    dir_spec = pl.BlockSpec((2, tm, bw), lambda i: (0, i, 0))
    two = jax.ShapeDtypeStruct((2, r_rows, bw), F32)
    kern = functools.partial(_lru_prep_kernel, tm=tm, n_lat_rows=n_lat_rows, lat_len=lat_len, ctx_len=ctx_len)
    return pl.pallas_call(
        kern,
        out_shape=(two, two),
        grid=(r_rows // tm,),
        in_specs=in_specs,
        out_specs=(dir_spec, dir_spec),
        compiler_params=_cparams(("parallel",)),
        name="lru_prep",
    )(p, p, p, prm["conv_w"], prm["conv_b"], prm["wcat"], prm["bcat"], prm["lsl"])


def _scan_block(b, s, *, reverse, n_lat_blocks_total, lat_blocks):
    ctx_blk = n_lat_blocks_total + b
    lat_blk = b * lat_blocks + ((lat_blocks - s) if reverse else (s - 1))
    return jnp.where(s == 0, ctx_blk, lat_blk)


def _rwkv_scan_kernel(r_ref, lw_ref, k_ref, v_ref, kk_ref, bb_ref, y_ref, s_ref, *, reverse):
    @pl.when(pl.program_id(2) == 0)
    def _():
        s_ref[...] = jnp.zeros_like(s_ref)

    L = CHUNK
    ri = lax.broadcasted_iota(jnp.int32, (L, L), 0)
    ci = lax.broadcasted_iota(jnp.int32, (L, L), 1)
    tri = ((ci >= ri) if reverse else (ci <= ri)).astype(F32)
    r2 = lax.broadcasted_iota(jnp.int32, (2 * L, 2 * L), 0) % L
    c2 = lax.broadcasted_iota(jnp.int32, (2 * L, 2 * L), 1) % L
    strict = (c2 > r2) if reverse else (c2 < r2)
    incl = (c2 >= r2) if reverse else (c2 <= r2)
    rr = lax.broadcasted_iota(jnp.int32, (2 * L, 2 * L), 0)
    cc = lax.broadcasted_iota(jnp.int32, (2 * L, 2 * L), 1)
    eye = (rr == cc).astype(F32)
    off_masks = []
    sz = 2
    while sz <= L:
        late, early = (cc, rr) if reverse else (rr, cc)
        off_masks.append((rr // sz == cc // sz) & (late % sz >= sz // 2) & (early % sz < sz // 2))
        sz *= 2
    lane = lax.broadcasted_iota(jnp.int32, (L, LANES), 1)
    head0 = lane < HEAD_DIM

    def stack2(x):
        return jnp.concatenate([jnp.where(head0, x, 0.0), jnp.where(head0, 0.0, x)], axis=0)

    n_sub = SCAN_ROWS // L
    n_pair = r_ref.shape[1] // LANES
    order = [(n_sub - 1 - j) if reverse else j for j in range(n_sub)]

    keys = [(c, p) for c in order for p in range(n_pair)]
    a2, b2, k2, q2, v2, bt2, kt2, decay = ({} for _ in range(8))
    for c in order:
        rows = pl.ds(c * L, L)
        lw_all = lw_ref[rows, :]
        cum_all = jnp.dot(tri, lw_all, preferred_element_type=F32, precision=lax.Precision.HIGHEST)
        for p in range(n_pair):
            cols = slice(p * LANES, (p + 1) * LANES)
            lw = lw_all[:, cols]
            cum = cum_all[:, cols]
            tot = cum[0:1, :] if reverse else cum[L - 1:L, :]
            e_pos = jnp.exp(cum)
            e_neg = jnp.exp(-cum)
            e_rem = jnp.exp(tot - cum)
            kk = kk_ref[rows, cols]
            kd = k_ref[rows, cols]
            bb = bb_ref[rows, cols]
            key = (c, p)
            a2[key] = stack2(-kk * jnp.exp(cum - lw))
            b2[key] = stack2(bb * e_neg)
            k2[key] = stack2(kd * e_neg)
            q2[key] = stack2(r_ref[rows, cols] * e_pos)
            v2[key] = stack2(v_ref[rows, cols])
            bt2[key] = stack2(bb * e_rem)
            kt2[key] = stack2(kd * e_rem)
            decay[key] = jnp.exp(tot)

    mm = {k: jnp.where(strict, _bdot_nt(a2[k], b2[k]), 0.0) for k in keys}
    nn = {k: jnp.where(strict, _bdot_nt(a2[k], k2[k]), 0.0) for k in keys}
    qq = {k: jnp.where(incl, _bdot_nt(q2[k], b2[k]), 0.0) for k in keys}
    zz = {k: jnp.where(incl, _bdot_nt(q2[k], k2[k]), 0.0) for k in keys}
    tinv = {k: eye + jnp.where(off_masks[0], mm[k], 0.0) for k in keys}
    for off in off_masks[1:]:
        half_step = {k: _bdot(tinv[k], jnp.where(off, mm[k], 0.0)) for k in keys}
        tinv = {k: tinv[k] + _bdot(half_step[k], tinv[k]) for k in keys}
    nv = {k: _bdot(nn[k], v2[k]) for k in keys}
    px = {k: _bdot(tinv[k], jnp.concatenate([a2[k], nv[k]], axis=1)) for k in keys}
    p1 = {k: px[k][:, :LANES] for k in keys}
    p2 = {k: px[k][:, LANES:] for k in keys}
    g_mat = {k: q2[k] + _bdot(qq[k], p1[k]) for k in keys}
    y_loc = {k: _bdot(qq[k], p2[k]) + _bdot(zz[k], v2[k]) for k in keys}
    phi = {k: _bdot_tn(p1[k], bt2[k]) for k in keys}
    psi = {k: _bdot_tn(p2[k], bt2[k]) + _bdot_tn(v2[k], kt2[k]) for k in keys}
    items = {k: (g_mat[k], y_loc[k], decay[k], phi[k], psi[k]) for k in keys}

    st = [s_ref[p] for p in range(n_pair)]
    for c in order:
        for p in range(n_pair):
            g, yl, dec, ph, ps = items[(c, p)]
            y2 = _bdot_nt(g, st[p]) + yl
            y_ref[pl.ds(c * L, L), p * LANES:(p + 1) * LANES] = y2[:L, :] + y2[L:, :]
            st[p] = st[p] * dec + _bdot(st[p], ph) + ps
    for p in range(n_pair):
        s_ref[p] = st[p]


def rwkv_scan_call(r, lw, kd, v, kk, bb, *, reverse, batch, lat_len, ctx_len):
    r_rows, aw = r.shape
    d = 1 if reverse else 0
    lat_blocks = lat_len // SCAN_ROWS
    width = SCAN_PAIRS * LANES
    blk = functools.partial(_scan_block, reverse=reverse, n_lat_blocks_total=batch * lat_blocks,
                            lat_blocks=lat_blocks)
    row_spec = pl.BlockSpec((SCAN_ROWS, width), lambda b, h, s: (blk(b, s), h))
    dir_spec = pl.BlockSpec((None, SCAN_ROWS, width), lambda b, h, s: (d, blk(b, s), h))
    return pl.pallas_call(
        functools.partial(_rwkv_scan_kernel, reverse=reverse),
        out_shape=jax.ShapeDtypeStruct((r_rows, aw), F32),
        grid=(batch, aw // width, 1 + lat_blocks),
        in_specs=[row_spec, dir_spec, dir_spec, row_spec, row_spec, dir_spec],
        out_specs=row_spec,
        scratch_shapes=[pltpu.VMEM((SCAN_PAIRS, LANES, LANES), F32)],
        compiler_params=_cparams(("parallel", "parallel", "arbitrary")),
        name="rwkv_scan_bwd" if reverse else "rwkv_scan_fwd",
    )(r, lw, kd, v, kk, bb)


def _lru_scan_kernel(a_ref, u_ref, h_ref, carry_ref, *, reverse):
    @pl.when(pl.program_id(1) == 0)
    def _():
        carry_ref[...] = jnp.zeros_like(carry_ref)

    w = a_ref.shape[1]
    rows = lax.broadcasted_iota(jnp.int32, (SUBLANES, w), 0)
    n_groups = SCAN_ROWS // SUBLANES

    def body(gi, carry):
        g = (n_groups - 1 - gi) if reverse else gi
        sl = pl.ds(pl.multiple_of(g * SUBLANES, SUBLANES), SUBLANES)
        a = a_ref[sl, :]
        u = u_ref[sl, :]
        for sh in (1, 2, 4):
            if reverse:
                ok = rows < SUBLANES - sh
                a_s = pltpu.roll(a, SUBLANES - sh, axis=0)
                u_s = pltpu.roll(u, SUBLANES - sh, axis=0)
            else:
                ok = rows >= sh
                a_s = pltpu.roll(a, sh, axis=0)
                u_s = pltpu.roll(u, sh, axis=0)
            u = jnp.where(ok, a * u_s + u, u)
            a = jnp.where(ok, a * a_s, a)
        h = a * carry + u
        h_ref[sl, :] = h
        return h[0:1, :] if reverse else h[SUBLANES - 1:SUBLANES, :]

    carry_ref[...] = lax.fori_loop(0, n_groups, body, carry_ref[...])


def lru_scan_call(a, u, *, reverse, batch, lat_len, ctx_len):
    _, r_rows, bw = a.shape
    d = 1 if reverse else 0
    lat_blocks = lat_len // SCAN_ROWS
    blk = functools.partial(_scan_block, reverse=reverse, n_lat_blocks_total=batch * lat_blocks,
                            lat_blocks=lat_blocks)
    dir_spec = pl.BlockSpec((None, SCAN_ROWS, bw), lambda b, s: (d, blk(b, s), 0))
    return pl.pallas_call(
        functools.partial(_lru_scan_kernel, reverse=reverse),
        out_shape=jax.ShapeDtypeStruct((r_rows, bw), F32),
        grid=(batch, 1 + lat_blocks),
        in_specs=[dir_spec, dir_spec],
        out_specs=pl.BlockSpec((SCAN_ROWS, bw), lambda b, s: (blk(b, s), 0)),
        scratch_shapes=[pltpu.VMEM((1, bw), F32)],
        compiler_params=_cparams(("parallel", "arbitrary")),
        name="lru_scan_bwd" if reverse else "lru_scan_fwd",
    )(a, u)


def _even_post_kernel(yf, yb, r_ref, v_ref, kdf, kdb, g_ref, hf, hb, gb_ref, rk, lnw, lnb, o_ref):
    ones = _head_ones()
    y = yf[...] + yb[...]
    inv_n = 1.0 / HEAD_DIM
    yc = y - _head_sum(y, ones) * inv_n
    var = _head_sum(yc * yc, ones) * inv_n
    hn = yc * lax.rsqrt(var + LNX_EPS) * lnw[...] + lnb[...]
    r = r_ref[...]
    v = v_ref[...]
    bonus = _head_sum(r * kdf[...] * rk[...], ones) * v + _head_sum(r * kdb[...] * rk[...], ones) * v
    ya = (hn + bonus) * g_ref[...]
    yl = (hf[...] + hb[...]) * jax.nn.gelu(gb_ref[...])
    aw = ya.shape[1]
    o_ref[:, :aw] = ya.astype(BF16)
    o_ref[:, aw:] = yl.astype(BF16)


def even_post_call(yf, yb, r, v, kd, g, hf, hb, p, prm, *, col_gb):
    r_rows, aw = yf.shape
    bw = hf.shape[1]
    tm = PREP_ROWS
    row = lambda w: pl.BlockSpec((tm, w), lambda i: (i, 0))
    full = lambda shape: pl.BlockSpec(shape, lambda i: (0,) * len(shape))
    return pl.pallas_call(
        _even_post_kernel,
        out_shape=jax.ShapeDtypeStruct((r_rows, aw + bw), BF16),
        grid=(r_rows // tm,),
        in_specs=[row(aw), row(aw), row(aw), row(aw),
                  pl.BlockSpec((None, tm, aw), lambda i: (0, i, 0)),
                  pl.BlockSpec((None, tm, aw), lambda i: (1, i, 0)),
                  row(aw), row(bw), row(bw),
                  pl.BlockSpec((tm, bw), lambda i: (i, col_gb)),
                  full((1, aw)), full((1, aw)), full((1, aw))],
        out_specs=row(aw + bw),
        compiler_params=_cparams(("parallel",)),
        name="even_post",
    )(yf, yb, r, v, kd, kd, g, hf, hb, p, prm["r_k"], prm["lnx_w"], prm["lnx_b"])


def _mlstm_kernel(q_ref, k_ref, v_ref, br_ref, bc_ref, ir_ref, ic_ref, h_ref, c_ref, n_ref, m_ref, *, reverse):
    @pl.when(pl.program_id(2) == 0)
    def _():
        c_ref[...] = jnp.zeros_like(c_ref)
        n_ref[...] = jnp.zeros_like(n_ref)
        m_ref[...] = jnp.zeros_like(m_ref)

    L = CHUNK
    ri = lax.broadcasted_iota(jnp.int32, (L, L), 0)
    ci = lax.broadcasted_iota(jnp.int32, (L, L), 1)
    causal = (ci >= ri) if reverse else (ci <= ri)
    n_sub = SCAN_ROWS // L
    order = [(n_sub - 1 - j) if reverse else j for j in range(n_sub)]
    scale = C_QK_HEAD ** -0.5

    m_cur = m_ref[...]
    m_in, ew, ec = {}, {}, {}
    for c in order:
        b_r = br_ref[c:c + 1, :]
        b_tot = b_r[:, 0:1] if reverse else b_r[:, L - 1:L]
        w_in = b_tot - bc_ref[:, c:c + 1] + ic_ref[:, c:c + 1]
        carry_log = b_tot + m_cur
        m_new = jnp.maximum(carry_log, jnp.max(w_in, axis=0, keepdims=True))
        m_in[c] = m_cur
        ew[c] = jnp.exp(w_in - m_new)
        ec[c] = jnp.exp(carry_log - m_new)
        m_cur = m_new
    m_ref[...] = m_cur

    q = {c: (q_ref[pl.ds(c * L, L), :] * scale).astype(BF16) for c in order}
    k = {c: k_ref[pl.ds(c * L, L), :] for c in order}
    v = {c: v_ref[pl.ds(c * L, L), :].astype(BF16) for c in order}
    qk = {c: _bdot_nt(q[c], k[c]) for c in order}
    m_t, e_inter, scores = {}, {}, {}
    for c in order:
        b_c = bc_ref[:, c:c + 1]
        dmat = jnp.where(causal, b_c - br_ref[c:c + 1, :] + ir_ref[c:c + 1, :], -jnp.inf)
        inter = b_c + m_in[c]
        m_t[c] = jnp.maximum(inter, jnp.max(dmat, axis=1, keepdims=True))
        scores[c] = qk[c] * jnp.exp(dmat - m_t[c])
        e_inter[c] = jnp.exp(inter - m_t[c])
    intra = {c: _bdot(scores[c], v[c]) for c in order}
    kw = {c: k[c] * ew[c] for c in order}
    kv = {c: _bdot_tn(kw[c], v[c]) for c in order}

    c_cur = c_ref[...]
    n_cur = n_ref[...]
    c_in, n_in = {}, {}
    for c in order:
        c_in[c] = c_cur
        n_in[c] = n_cur
        c_cur = ec[c] * c_cur + kv[c]
        n_cur = ec[c] * n_cur + jnp.sum(kw[c], axis=0, keepdims=True)
    c_ref[...] = c_cur
    n_ref[...] = n_cur

    qc = {c: _bdot(q[c], c_in[c]) for c in order}
    for c in order:
        qf = q[c].astype(F32)
        num = e_inter[c] * qc[c] + intra[c]
        den = (e_inter[c] * jnp.sum(qf * n_in[c], axis=1, keepdims=True)
               + jnp.sum(scores[c], axis=1, keepdims=True))
        h_ref[pl.ds(c * L, L), :] = num / jnp.maximum(jnp.abs(den), jnp.exp(-m_t[c]))


def mlstm_call(p, g_rows, g_cols, *, reverse, batch, lat_len, ctx_len, col_q, col_k, col_v):
    r_rows = p.shape[0]
    d = 1 if reverse else 0
    n_sub = SCAN_ROWS // CHUNK
    lat_blocks = lat_len // SCAN_ROWS
    blk = functools.partial(_scan_block, reverse=reverse, n_lat_blocks_total=batch * lat_blocks,
                            lat_blocks=lat_blocks)
    grow = lambda which: pl.BlockSpec((None, None, None, None, n_sub, CHUNK),
                                      lambda b, h, s: (d, which, h, blk(b, s), 0, 0))
    gcol = lambda which: pl.BlockSpec((None, None, None, None, CHUNK, n_sub),
                                      lambda b, h, s: (d, which, h, blk(b, s), 0, 0))
    return pl.pallas_call(
        functools.partial(_mlstm_kernel, reverse=reverse),
        out_shape=jax.ShapeDtypeStruct((r_rows, C_HEADS * C_V_HEAD), F32),
        grid=(batch, C_HEADS, 1 + lat_blocks),
        in_specs=[
            pl.BlockSpec((SCAN_ROWS, C_QK_HEAD), lambda b, h, s: (blk(b, s), col_q + h)),
            pl.BlockSpec((SCAN_ROWS, C_QK_HEAD), lambda b, h, s: (blk(b, s), col_k + h)),
            pl.BlockSpec((SCAN_ROWS, C_V_HEAD), lambda b, h, s: (blk(b, s), col_v + h)),
            grow(0), gcol(0), grow(1), gcol(1),
        ],
        out_specs=pl.BlockSpec((SCAN_ROWS, C_V_HEAD), lambda b, h, s: (blk(b, s), h)),
        scratch_shapes=[pltpu.VMEM((C_QK_HEAD, C_V_HEAD), F32), pltpu.VMEM((1, C_QK_HEAD), F32),
                        pltpu.VMEM((1, 1), F32)],
        compiler_params=_cparams(("parallel", "parallel", "arbitrary")),
        name="mlstm_bwd" if reverse else "mlstm_fwd",
    )(p, p, p, g_rows, g_cols, g_rows, g_cols)


def _odd_post_kernel(hf, hb, o_ref_in, nw, nb, y_ref):
    x = hf[...] + hb[...]
    xc = x - jnp.mean(x, axis=1, keepdims=True)
    var = jnp.mean(xc * xc, axis=1, keepdims=True)
    hn = xc * lax.rsqrt(var + MLSTM_NORM_EPS) * nw[...] + nb[...]
    y_ref[...] = (hn * jax.nn.sigmoid(o_ref_in[...])).astype(BF16)


def odd_post_call(hf, hb, p, nw, nb, *, n_rows, col_o):
    vw = hf.shape[1]
    tm = LN_ROWS
    head = pl.BlockSpec((tm, C_V_HEAD), lambda i, h: (i, h))
    par = pl.BlockSpec((1, C_V_HEAD), lambda i, h: (0, h))
    return pl.pallas_call(
        _odd_post_kernel,
        out_shape=jax.ShapeDtypeStruct((n_rows, vw), BF16),
        grid=(n_rows // tm, C_HEADS),
        in_specs=[head, head, pl.BlockSpec((tm, C_V_HEAD), lambda i, h: (i, col_o + h)), par, par],
        out_specs=head,
        compiler_params=_cparams(("parallel", "parallel")),
        name="odd_post",
    )(hf, hb, p, nw, nb)


def _layer_norm(z, g, b):
    zc = z - jnp.mean(z, axis=1, keepdims=True)
    var = jnp.mean(zc * zc, axis=1, keepdims=True)
    return zc * lax.rsqrt(var + LN_EPS) * g + b


def _route(logits_t, bias):
    aff = [jax.nn.sigmoid(logits_t[e:e + 1, :]) for e in range(N_EXPERTS)]
    biased = [aff[e] + bias[e:e + 1, :] for e in range(N_EXPERTS)]
    best_g = best_v = None
    for g in range(N_GROUPS):
        m = biased[g * EXPERTS_PER_GROUP:(g + 1) * EXPERTS_PER_GROUP]
        pair = None
        for i in range(EXPERTS_PER_GROUP):
            for j in range(i + 1, EXPERTS_PER_GROUP):
                hi = jnp.maximum(m[i], m[j])
                lo_ = jnp.minimum(m[i], m[j])
                s = hi + lo_
                pair = s if pair is None else jnp.maximum(pair, s)
        if best_v is None:
            best_v, best_g = pair, jnp.zeros(pair.shape, jnp.int32)
        else:
            upd = pair > best_v
            best_g = jnp.where(upd, g, best_g)
            best_v = jnp.where(upd, pair, best_v)
    ids, sels = [], []
    taken = None
    for _ in range(TOP_K):
        cur_v = cur_i = cur_a = None
        for e in range(N_EXPERTS):
            ok = best_g == (e // EXPERTS_PER_GROUP)
            if taken is not None:
                ok = jnp.logical_and(ok, taken != e)
            val = jnp.where(ok, biased[e], -jnp.inf)
            if cur_v is None:
                cur_v, cur_i, cur_a = val, jnp.zeros(val.shape, jnp.int32), aff[e]
            else:
                upd = val > cur_v
                cur_i = jnp.where(upd, e, cur_i)
                cur_a = jnp.where(upd, aff[e], cur_a)
                cur_v = jnp.where(upd, val, cur_v)
        ids.append(cur_i)
        sels.append(cur_a)
        taken = cur_i
    tot = sels[0] + sels[1]
    return ids, [sels[0] / tot, sels[1] / tot]


def _ln_route_kernel(lat_ref, y_ref, gate_ref, sh_ref, sc_ref, g_ref, b_ref, rw_ref, rb_ref,
                     lat_o, h_o, id_o, gt_o):
    z = ALPHA * lat_ref[...] + gate_ref[...] * y_ref[...]
    ln = _layer_norm(z, g_ref[...], b_ref[...])
    lat_o[...] = ln
    h = ln * (1.0 + sc_ref[...]) + sh_ref[...]
    half = h.shape[1] // 2
    bits = lax.bitcast_convert_type(h.astype(BF16).astype(F32), jnp.uint32)
    h_o[...] = (bits[:, :half] >> 16) | bits[:, half:]
    logits_t = lax.dot_general(rw_ref[...], h, (((1,), (1,)), ((), ())), preferred_element_type=F32,
                               precision=lax.Precision.HIGHEST)
    ids, gts = _route(logits_t, rb_ref[...])
    id_o[...] = jnp.concatenate(ids, axis=0)
    gt_o[...] = jnp.concatenate(gts, axis=0)


def ln_route_call(lat, y, mods, norm_g, norm_b, layer, router_wt, router_b, *, n_rows, rows_per_mod, n_mod_rows):
    d = lat.shape[1]
    tm = LN_ROWS

    def midx(i):
        return jnp.minimum(i * tm // rows_per_mod, n_mod_rows - 1)

    row = pl.BlockSpec((tm, d), lambda i: (i, 0))
    packed = pl.BlockSpec((tm, d // 2), lambda i: (i, 0))
    mod = lambda k: pl.BlockSpec((None, 1, d), lambda i: (midx(i), 0, k))
    nrm = pl.BlockSpec((None, None, 1, d), lambda i: (layer, 0, 0, 0))
    full = lambda shape: pl.BlockSpec(shape, lambda i: (0,) * len(shape))
    sel = pl.BlockSpec((TOP_K, tm), lambda i: (0, i))
    return pl.pallas_call(
        _ln_route_kernel,
        out_shape=(jax.ShapeDtypeStruct((n_rows, d), F32), jax.ShapeDtypeStruct((n_rows, d // 2), jnp.uint32),
                   jax.ShapeDtypeStruct((TOP_K, n_rows), jnp.int32), jax.ShapeDtypeStruct((TOP_K, n_rows), F32)),
        grid=(n_rows // tm,),
        in_specs=[row, row, mod(2), mod(3), mod(4), nrm, nrm, full((N_EXPERTS, d)), full((N_EXPERTS, 1))],
        out_specs=(row, packed, sel, sel),
        compiler_params=_cparams(("parallel",)),
        name="ln_route",
    )(lat, y, mods, mods, mods, norm_g, norm_b, router_wt, router_b)


def _ln_combine_kernel(lat_ref, y0_ref, y1_ref, gt_ref, gate_ref, g_ref, b_ref, lat_o):
    gt = gt_ref[...]
    f = y0_ref[...] * gt[:, 0:1] + y1_ref[...] * gt[:, 1:2]
    z = ALPHA * lat_ref[...] + gate_ref[...] * f
    lat_o[...] = _layer_norm(z, g_ref[...], b_ref[...])


def ln_combine_call(lat, y_slots, gates_t, mods, norm_g, norm_b, layer, *, n_rows, rows_per_mod, n_mod_rows):
    d = lat.shape[1]
    tm = LN_ROWS
    per_k = n_rows // tm

    def midx(i):
        return jnp.minimum(i * tm // rows_per_mod, n_mod_rows - 1)

    row = pl.BlockSpec((tm, d), lambda i: (i, 0))
    nrm = pl.BlockSpec((None, None, 1, d), lambda i: (layer, 1, 0, 0))
    return pl.pallas_call(
        _ln_combine_kernel,
        out_shape=jax.ShapeDtypeStruct((n_rows, d), F32),
        grid=(per_k,),
        in_specs=[row, row, pl.BlockSpec((tm, d), lambda i: (per_k + i, 0)),
                  pl.BlockSpec((tm, TOP_K), lambda i: (i, 0)),
                  pl.BlockSpec((None, 1, d), lambda i: (midx(i), 0, 5)), nrm, nrm],
        out_specs=row,
        compiler_params=_cparams(("parallel",)),
        name="ln_combine",
    )(lat, y_slots, y_slots, gates_t, mods, norm_g, norm_b)


def _moe_kernel(nu_ref, be_ref, jw_ref, src_cur, src_nxt, dst_prv, h_hbm, wg_ref, wu_ref, wd_ref, y_hbm,
                xbuf, acc, gsem, ssem, *, n_j, n_slots):
    del be_ref, jw_ref
    i = pl.program_id(0)
    j = pl.program_id(1)
    buf = i % 2
    other = 1 - buf
    n_used = nu_ref[0]
    tm, half = xbuf.shape[1], xbuf.shape[2]
    d = acc.shape[2]
    per_step = tm // n_j

    def row_in(tok, r, b):
        return pltpu.make_async_copy(h_hbm.at[pl.ds(tok, 1)], xbuf.at[b, pl.ds(r, 1)], gsem.at[b])

    def row_out(r, slot, b):
        return pltpu.make_async_copy(acc.at[b, pl.ds(r, 1)], y_hbm.at[pl.ds(slot, 1)], ssem.at[b])

    def send_previous(r):
        slot = jnp.where(i > 0, dst_prv[0, r], n_slots + r)
        row_out(r, slot, other).start()

    @pl.when((i == 0) & (j == 0))
    def _():
        acc[1] = jnp.zeros((tm, d), F32)

        def body(r, carry):
            row_in(src_cur[0, r], r, 0).start()
            return carry
        lax.fori_loop(0, tm, body, 0, unroll=8)

    @pl.when((j == 0) & (i <= n_used))
    def _():
        pltpu.make_async_copy(xbuf.at[buf], xbuf.at[buf], gsem.at[buf]).wait()

    @pl.when((j == 0) & (i < n_used))
    def _():
        acc[buf] = jnp.zeros((tm, d), F32)

    @pl.when(i < n_used)
    def _():
        tf = wg_ref.shape[1]
        g = jnp.zeros((tm, tf), F32)
        u = jnp.zeros((tm, tf), F32)
        for c0 in range(0, half, MOE_KCHUNK):
            xp = xbuf[buf, :, c0:c0 + MOE_KCHUNK]
            lo = lax.bitcast_convert_type(xp << 16, F32)
            hi = lax.bitcast_convert_type(xp & jnp.uint32(0xFFFF0000), F32)
            g = g + _bdot(lo, wg_ref[c0:c0 + MOE_KCHUNK, :]) + _bdot(hi, wg_ref[half + c0:half + c0 + MOE_KCHUNK, :])
            u = u + _bdot(lo, wu_ref[c0:c0 + MOE_KCHUNK, :]) + _bdot(hi, wu_ref[half + c0:half + c0 + MOE_KCHUNK, :])
        hdn = (g * jax.nn.sigmoid(g) * u).astype(BF16)
        for c0 in range(0, d, MOE_NCHUNK):
            acc[buf, :, c0:c0 + MOE_NCHUNK] += _bdot(hdn, wd_ref[:, c0:c0 + MOE_NCHUNK])
        for rr in range(per_step):
            r = j * per_step + rr
            row_in(src_nxt[0, r], r, other).start()
            send_previous(r)

    @pl.when(i == n_used)
    def _():
        for rr in range(per_step):
            send_previous(j * per_step + rr)

    @pl.when((j == n_j - 1) & (i <= n_used))
    def _():
        pltpu.make_async_copy(acc.at[other], acc.at[other], ssem.at[other]).wait()


def moe_call(h_packed, n_used, block_e, j_of, src_tok, dst_slot, w_gate, w_up, w_down, layer, *, n_slots):
    d = 2 * h_packed.shape[1]
    f = w_gate.shape[3]
    tm, tf = MOE_ROWS, MOE_FTILE
    n_blocks = block_e.shape[0]
    n_j = f // tf
    assert tm % n_j == 0
    idx = lambda fn: pl.BlockSpec((None, 1, tm), fn, memory_space=pltpu.SMEM)
    grid_spec = pltpu.PrefetchScalarGridSpec(
        num_scalar_prefetch=3,
        grid=(n_blocks, n_j),
        in_specs=[
            idx(lambda i, j, nu, be, jw: (i, 0, 0)),
            idx(lambda i, j, nu, be, jw: (jnp.minimum(i + 1, n_blocks - 1), 0, 0)),
            idx(lambda i, j, nu, be, jw: (jnp.maximum(i - 1, 0), 0, 0)),
            pl.BlockSpec(memory_space=pl.ANY),
            pl.BlockSpec((None, None, d, tf), lambda i, j, nu, be, jw: (layer, be[i], 0, jw[i, j])),
            pl.BlockSpec((None, None, d, tf), lambda i, j, nu, be, jw: (layer, be[i], 0, jw[i, j])),
            pl.BlockSpec((None, None, tf, d), lambda i, j, nu, be, jw: (layer, be[i], jw[i, j], 0)),
        ],
        out_specs=pl.BlockSpec(memory_space=pl.ANY),
        scratch_shapes=[pltpu.VMEM((2, tm, d // 2), jnp.uint32), pltpu.VMEM((2, tm, d), F32),
                        pltpu.SemaphoreType.DMA((2,)), pltpu.SemaphoreType.DMA((2,))],
    )
    return pl.pallas_call(
        functools.partial(_moe_kernel, n_j=n_j, n_slots=n_slots),
        out_shape=jax.ShapeDtypeStruct((n_slots + tm, d), F32),
        grid_spec=grid_spec,
        compiler_params=_cparams(("arbitrary", "arbitrary")),
        name="moe_experts",
    )(n_used, block_e, j_of, src_tok, src_tok, dst_slot, h_packed, w_gate, w_up, w_down)


def moe_ffn(h_packed, ids, w_gate, w_up, w_down, layer):
    n_tok = h_packed.shape[0]
    n_slots = n_tok * TOP_K
    tm = MOE_ROWS
    n_j = w_gate.shape[3] // MOE_FTILE
    i32 = jnp.int32
    flat_e = ids.reshape(n_slots)
    order = jnp.argsort(flat_e, stable=True).astype(i32)
    counts = jnp.sum((flat_e[:, None] == jnp.arange(N_EXPERTS, dtype=i32)[None, :]).astype(i32), axis=0)
    start = jnp.cumsum(counts) - counts
    padded = (counts + tm - 1) // tm * tm
    pad_end = jnp.cumsum(padded)
    pad_start = pad_end - padded
    n_blocks = -(-n_slots // tm) + N_EXPERTS + 1
    pos = jnp.arange(n_blocks * tm, dtype=i32)
    e_pos = jnp.minimum(jnp.searchsorted(pad_end, pos, side="right"), N_EXPERTS - 1).astype(i32)
    r_pos = pos - pad_start[e_pos]
    valid = r_pos < counts[e_pos]
    slot = order[jnp.clip(start[e_pos] + r_pos, 0, n_slots - 1)]
    src_tok = jnp.where(valid, slot % n_tok, 0).reshape(n_blocks, 1, tm)
    dst_slot = jnp.where(valid, slot, n_slots + pos % tm).reshape(n_blocks, 1, tm)
    n_used = (pad_end[-1] // tm).astype(i32)
    blk = jnp.arange(n_blocks, dtype=i32)
    block_e = e_pos.reshape(n_blocks, tm)[:, 0]
    block_e = jnp.where(blk < n_used, block_e, block_e[jnp.maximum(n_used - 1, 0)]).astype(i32)
    j_of = jnp.where((blk < n_used)[:, None], jnp.arange(n_j, dtype=i32)[None, :], n_j - 1).astype(i32)
    return moe_call(h_packed, n_used.reshape(1), block_e, j_of, src_tok.astype(i32), dst_slot.astype(i32),
                    w_gate, w_up, w_down, layer, n_slots=n_slots)


def _sincos_2d(n_tokens, dim):
    rows = n_tokens // GRID_W
    row = jnp.repeat(jnp.arange(rows, dtype=F32), GRID_W)
    col = jnp.tile(jnp.arange(GRID_W, dtype=F32), rows)
    quarter = dim // 4
    omega = POS_BASE ** (-jnp.arange(quarter, dtype=F32) / quarter)
    ang_r = row[:, None] * omega[None, :]
    ang_c = col[:, None] * omega[None, :]
    return jnp.concatenate([jnp.sin(ang_r), jnp.cos(ang_r), jnp.sin(ang_c), jnp.cos(ang_c)], -1)


def _pad_cols(w, width):
    return jnp.pad(w, [(0, 0)] * (w.ndim - 1) + [(0, width - w.shape[-1])])


def _even_weights(ev_w_in, rwkv_mu, aw, bw):
    rk = 3 * aw
    lo0 = rk
    pieces = [DECAY_LORA, DECAY_LORA, ICL_LORA, ICL_LORA]

    def regroup(t):
        out = [t[..., rk + 2 * (DECAY_LORA + ICL_LORA) + GATE_LORA:],
               t[..., :rk]]
        off = lo0
        for wdt in pieces:
            out.append(_pad_cols(t[..., off:off + wdt], LANES))
            off += wdt
        out.append(t[..., off:off + GATE_LORA])
        used = 4 * LANES + GATE_LORA
        out.append(jnp.zeros(t.shape[:-1] + (LORA_PAD - used,), t.dtype))
        return jnp.concatenate(out, axis=-1)

    w = regroup(ev_w_in).astype(BF16)
    rwkv_cols = rk + 2 * (DECAY_LORA + ICL_LORA) + GATE_LORA
    mu_full = jnp.concatenate([rwkv_mu, jnp.zeros(rwkv_mu.shape[:-1] + (2 * bw,), rwkv_mu.dtype)], axis=-1)
    mu = regroup(mu_full)
    del rwkv_cols
    return w, mu


def kernel(x, c, ctx, c_ctx, ada_w, ada_b, norm_g, norm_b,
           ev_w_in, ev_w_out, rwkv_mu, rwkv_w0, rwkv_w2, rwkv_a0, rwkv_a2, rwkv_g2,
           rwkv_k_k, rwkv_k_a, rwkv_r_k, rwkv_lnx_w, rwkv_lnx_b,
           lru_conv_w, lru_conv_b, lru_wa, lru_ba, lru_wx, lru_bx, lru_lam,
           od_w_in, od_w_out, mlstm_ig_b, mlstm_fg_b, mlstm_norm_w, mlstm_norm_b,
           router_w, router_b, moe_w_gate, moe_w_up, moe_w_down):
    bsz, s_len, dim = x.shape
    n_ctx = ctx.shape[1]
    n_lat_rows = bsz * s_len
    n_all_rows = n_lat_rows + bsz * n_ctx
    aw = rwkv_k_k.shape[1]
    bw = lru_conv_b.shape[1]
    assert s_len % SCAN_ROWS == 0 and n_ctx == SCAN_ROWS and n_all_rows % MM_ROWS == 0
    assert (bsz * n_ctx) % MM_ROWS == 0 and s_len % MM_ROWS == 0 and bsz + 1 <= SUBLANES
    seq = dict(n_lat_rows=n_lat_rows, lat_len=s_len, ctx_len=n_ctx)
    scan = dict(batch=bsz, lat_len=s_len, ctx_len=n_ctx)
    modk = dict(rows_per_mod=s_len, n_mod_rows=bsz + 1)

    lat = (x + _sincos_2d(s_len, dim).astype(x.dtype)).reshape(n_lat_rows, dim)
    stream = jnp.concatenate([lat, ctx.reshape(bsz * n_ctx, dim)], axis=0)
    cvec = jnp.concatenate([c, c_ctx[None, :], jnp.zeros((SUBLANES - bsz - 1, dim), c.dtype)], axis=0)
    ada_b3 = ada_b[:, None, :]
    norm_g4 = norm_g[:, :, None, :]
    norm_b4 = norm_b[:, :, None, :]
    router_wt = router_w.T
    router_b2 = router_b[:, None]

    for layer in range(DEPTH):
        last = layer == DEPTH - 1
        i = layer // 2
        mods = adaln_call(cvec, ada_w, ada_b3, layer).reshape(SUBLANES, 1, N_MOD * dim)
        if layer % 2 == 0:
            w_in, mu = _even_weights(ev_w_in, rwkv_mu, aw, bw)
            n_in = w_in.shape[2]
            p = matmul_mod_call(stream, mods, w_in, i, tn=1024, **modk)
            col_x, col_gb = 0, bw // bw
            col_r, col_k, col_v = 2 * bw // aw, 2 * bw // aw + 1, 2 * bw // aw + 2
            col_l = (2 * bw + 3 * aw) // LORA_PAD
            assert (2 * bw + 3 * aw) % LORA_PAD == 0 and n_in == 2 * bw + 3 * aw + LORA_PAD
            mu_i = mu[i]
            o_r = 2 * bw
            prm = dict(
                mu_r=mu_i[None, o_r:o_r + aw], mu_k=mu_i[None, o_r + aw:o_r + 2 * aw],
                mu_v=mu_i[None, o_r + 2 * aw:o_r + 3 * aw], mu_l=mu_i[None, o_r + 3 * aw:],
                w0=rwkv_w0[i], w2=_pad_rows(rwkv_w2[i], LANES).astype(BF16),
                a0=rwkv_a0[i], a2=_pad_rows(rwkv_a2[i], LANES).astype(BF16),
                g2=rwkv_g2[i].astype(BF16), k_k=rwkv_k_k[i][None, :], k_a=rwkv_k_a[i][None, :])
            r_s, v_s, kk, lw, kd, bb, g = rwkv_prep_call(p, prm, col_r=col_r, col_k=col_k, col_v=col_v,
                                                          col_l=col_l, **seq)
            nb = bw // LRU_BLOCK_W
            wcat = jnp.concatenate([lru_wa[i, 0], lru_wx[i, 0], lru_wa[i, 1], lru_wx[i, 1]], axis=-1).astype(BF16)
            bcat = jnp.stack([lru_ba[i, 0].reshape(nb, LRU_BLOCK_W), lru_bx[i, 0].reshape(nb, LRU_BLOCK_W),
                              lru_ba[i, 1].reshape(nb, LRU_BLOCK_W), lru_bx[i, 1].reshape(nb, LRU_BLOCK_W)],
                             axis=1).reshape(1, 4 * bw)
            lprm = dict(conv_w=lru_conv_w[i], conv_b=lru_conv_b[i][None, :], wcat=wcat, bcat=bcat,
                        lsl=RGLRU_C * jax.nn.log_sigmoid(lru_lam[i]))
            la, lu = lru_prep_call(p, lprm, col_x=col_x, **seq)
            yf = rwkv_scan_call(r_s, lw, kd, v_s, kk, bb, reverse=False, **scan)
            yb = rwkv_scan_call(r_s, lw, kd, v_s, kk, bb, reverse=True, **scan)
            hf = lru_scan_call(la, lu, reverse=False, **scan)
            hb = lru_scan_call(la, lu, reverse=True, **scan)
            pprm = dict(r_k=rwkv_r_k[i].reshape(1, aw), lnx_w=rwkv_lnx_w[i][None, :], lnx_b=rwkv_lnx_b[i][None, :])
            ymix = even_post_call(yf, yb, r_s, v_s, kd, g, hf, hb, p, pprm, col_gb=col_gb)
            n_rows = n_all_rows if not last else n_lat_rows
            y = matmul_call(ymix, ev_w_out.astype(BF16), i, tn=1024, n_rows=n_rows)
        else:
            qk_w = C_HEADS * C_QK_HEAD
            v_w = C_HEADS * C_V_HEAD
            main = 2 * qk_w + 2 * v_w
            w_in = od_w_in[..., :main].astype(BF16)
            w_gate = _pad_cols(od_w_in[..., main:], LANES).astype(BF16)
            p = matmul_mod_call(stream, mods, w_in, i, tn=1024, **modk)
            gp = matmul_mod_call(stream, mods, w_gate, i, tn=LANES, **modk)[:, :4 * C_HEADS]
            gp = gp.reshape(n_all_rows, 2, 2, C_HEADS)
            g_rows, g_cols = _mlstm_gates(gp, mlstm_ig_b[i], mlstm_fg_b[i])
            cols = dict(col_q=0, col_k=(qk_w + v_w) // C_QK_HEAD, col_v=(2 * qk_w + v_w) // C_V_HEAD)
            hf = mlstm_call(p, g_rows, g_cols, reverse=False, **scan, **cols)
            hb = mlstm_call(p, g_rows, g_cols, reverse=True, **scan, **cols)
            n_rows = n_all_rows if not last else n_lat_rows
            ymix = odd_post_call(hf, hb, p, mlstm_norm_w[i][None, :], mlstm_norm_b[i][None, :],
                                 n_rows=n_rows, col_o=qk_w // C_V_HEAD)
            y = matmul_call(ymix, od_w_out.astype(BF16), i, tn=1024, n_rows=n_rows)
        n_rows = n_all_rows if not last else n_lat_rows
        stream1, h_moe, ids, gates = ln_route_call(stream, y, mods, norm_g4, norm_b4, layer, router_wt, router_b2,
                                                   n_rows=n_rows, **modk)
        y_slots = moe_ffn(h_moe, ids, moe_w_gate, moe_w_up, moe_w_down, layer)
        stream = ln_combine_call(stream1, y_slots, gates.T, mods, norm_g4, norm_b4, layer, n_rows=n_rows, **modk)
    return stream[:n_lat_rows].reshape(bsz, s_len, dim)


def _pad_rows(w, rows):
    return jnp.pad(w, ((0, 0), (0, rows - w.shape[1]), (0, 0)))


def _mlstm_gates(gp, ig_b, fg_b):
    n_rows = gp.shape[0]
    n_sub = SCAN_ROWS // CHUNK
    outs = []
    for d in range(2):
        ig = GATE_CAP * jnp.tanh((gp[:, d, 0] + ig_b[d]) / GATE_CAP)
        lf = jax.nn.log_sigmoid(GATE_CAP * jnp.tanh((gp[:, d, 1] + fg_b[d]) / GATE_CAP))
        lf = lf.reshape(n_rows // CHUNK, CHUNK, C_HEADS)
        b = jnp.cumsum(lf[:, ::-1], axis=1)[:, ::-1] if d == 1 else jnp.cumsum(lf, axis=1)
        both = jnp.stack([b, ig.reshape(n_rows // CHUNK, CHUNK, C_HEADS)], axis=0)
        outs.append(both)
    g = jnp.stack(outs, axis=0)
    g = g.reshape(2, 2, n_rows // SCAN_ROWS, n_sub, CHUNK, C_HEADS)
    g_rows = jnp.transpose(g, (0, 1, 5, 2, 3, 4))
    g_cols = jnp.transpose(g, (0, 1, 5, 2, 4, 3))
    return g_rows, g_cols
```

```python
import functools
import math

import jax
import jax.numpy as jnp
from jax import lax
from jax.experimental import pallas as pl
from jax.experimental.pallas import tpu as pltpu

F32 = jnp.float32
BF16 = jnp.bfloat16

DEPTH = 2
N_MOD = 6
ALPHA = (2 * DEPTH) ** 0.25
LN_EPS = 1e-5
HEAD_DIM = 64
DECAY_LORA = 96
ICL_LORA = 96
GATE_LORA = 256
LNX_EPS = 64e-5
LRU_BLOCK_W = 128
CONV_W = 4
RGLRU_C = 8.0
C_HEADS = 8
C_QK_HEAD = 256
C_V_HEAD = 512
GATE_CAP = 15.0
MLSTM_NORM_EPS = 1e-6
N_EXPERTS = 16
N_GROUPS = 4
EXPERTS_PER_GROUP = N_EXPERTS // N_GROUPS
TOP_K = 2
POS_BASE = 10000.0
GRID_W = 64

LANES = 128
SUBLANES = 8
SCAN_ROWS = 256
CHUNK = 64
SCAN_PAIRS = 4
PREP_ROWS = 128
LN_ROWS = 256
MM_ROWS = 512
MOE_ROWS = 512
MOE_FTILE = 256
MOE_NCHUNK = 1024
MOE_KCHUNK = 512
LORA_PAD = 1024
VMEM_LIMIT = 60 * 1024 * 1024


def _cparams(sem):
    return pltpu.CompilerParams(dimension_semantics=sem, vmem_limit_bytes=VMEM_LIMIT)


def _bdot(a, b):
    return jnp.dot(a.astype(BF16), b.astype(BF16), preferred_element_type=F32)


def _bdot_nt(a, b):
    return lax.dot_general(a.astype(BF16), b.astype(BF16), (((1,), (1,)), ((), ())),
                           preferred_element_type=F32)


def _bdot_tn(a, b):
    return lax.dot_general(a.astype(BF16), b.astype(BF16), (((0,), (0,)), ((), ())),
                           preferred_element_type=F32)


def _softplus(z):
    return jnp.maximum(z, 0.0) + jnp.log1p(jnp.exp(-jnp.abs(z)))


def _adaln_kernel(c_ref, w_ref, b_ref, o_ref):
    c = c_ref[...]
    s = c * jax.nn.sigmoid(c)
    o_ref[...] = _bdot(s, w_ref[...]) + b_ref[...]


def adaln_call(cvec, ada_w, ada_b, layer, tn=512):
    rows, d = cvec.shape
    n = ada_w.shape[2]
    return pl.pallas_call(
        _adaln_kernel,
        out_shape=jax.ShapeDtypeStruct((rows, n), F32),
        grid=(n // tn,),
        in_specs=[
            pl.BlockSpec((rows, d), lambda j: (0, 0)),
            pl.BlockSpec((None, d, tn), lambda j: (layer, 0, j)),
            pl.BlockSpec((None, 1, tn), lambda j: (layer, 0, j)),
        ],
        out_specs=pl.BlockSpec((rows, tn), lambda j: (0, j)),
        compiler_params=_cparams(("parallel",)),
        name="adaln",
    )(cvec, ada_w, ada_b)


def _mm_mod_kernel(x_ref, sh_ref, sc_ref, w_ref, o_ref, xb_ref):
    @pl.when(pl.program_id(1) == 0)
    def _():
        xb_ref[...] = (x_ref[...] * (1.0 + sc_ref[...]) + sh_ref[...]).astype(BF16)

    o_ref[...] = jnp.dot(xb_ref[...], w_ref[...], preferred_element_type=F32)


def matmul_mod_call(x, mods, w, layer_w, *, rows_per_mod, n_mod_rows, tn, n_rows=None, n_cols=None):
    r, d = x.shape
    n = w.shape[2] if n_cols is None else n_cols
    r = r if n_rows is None else n_rows
    tm = MM_ROWS

    def midx(i):
        return jnp.minimum(i * tm // rows_per_mod, n_mod_rows - 1)

    return pl.pallas_call(
        _mm_mod_kernel,
        out_shape=jax.ShapeDtypeStruct((r, n), F32),
        grid=(r // tm, n // tn),
        in_specs=[
            pl.BlockSpec((tm, d), lambda i, j: (i, 0)),
            pl.BlockSpec((None, 1, d), lambda i, j: (midx(i), 0, 0)),
            pl.BlockSpec((None, 1, d), lambda i, j: (midx(i), 0, 1)),
            pl.BlockSpec((None, d, tn), lambda i, j: (layer_w, 0, j)),
        ],
        out_specs=pl.BlockSpec((tm, tn), lambda i, j: (i, j)),
        scratch_shapes=[pltpu.VMEM((tm, d), BF16)],
        compiler_params=_cparams(("parallel", "arbitrary")),
        name="proj_in",
    )(x, mods, mods, w)


def _mm_kernel(x_ref, w_ref, o_ref):
    o_ref[...] = jnp.dot(x_ref[...], w_ref[...], preferred_element_type=F32)


def matmul_call(x, w, layer_w, *, tn, n_rows=None):
    r, k = x.shape
    n = w.shape[2]
    r = r if n_rows is None else n_rows
    tm = MM_ROWS
    return pl.pallas_call(
        _mm_kernel,
        out_shape=jax.ShapeDtypeStruct((r, n), F32),
        grid=(r // tm, n // tn),
        in_specs=[
            pl.BlockSpec((tm, k), lambda i, j: (i, 0)),
            pl.BlockSpec((None, k, tn), lambda i, j: (layer_w, 0, j)),
        ],
        out_specs=pl.BlockSpec((tm, tn), lambda i, j: (i, j)),
        compiler_params=_cparams(("parallel", "arbitrary")),
        name="proj_out",
    )(x, w)


def _tile_flags(i, tm, n_lat_rows, lat_len, ctx_len):
    row = i * tm
    in_lat = row < n_lat_rows
    pos = jnp.where(in_lat, row % lat_len, (row - n_lat_rows) % ctx_len)
    seq = jnp.where(in_lat, lat_len, ctx_len)
    return pos == 0, pos + tm == seq


def _row_shift(c, prev_row, next_row):
    tm = c.shape[0]
    rows = lax.broadcasted_iota(jnp.int32, c.shape, 0)
    xp = jnp.where(rows == 0, prev_row, pltpu.roll(c, 1, axis=0))
    xn = jnp.where(rows == tm - 1, next_row, pltpu.roll(c, tm - 1, axis=0))
    return xp, xn


def _head_ones():
    r = lax.broadcasted_iota(jnp.int32, (LANES, LANES), 0) // HEAD_DIM
    c = lax.broadcasted_iota(jnp.int32, (LANES, LANES), 1) // HEAD_DIM
    return (r == c).astype(BF16)


def _head_sum(x, ones):
    w = x.shape[1]
    parts = [jnp.dot(x[:, j:j + LANES].astype(BF16), ones, preferred_element_type=F32)
             for j in range(0, w, LANES)]
    return jnp.concatenate(parts, axis=1)


def _halo_specs(tm, width, col, n_rows):
    per = tm // SUBLANES
    last = n_rows // SUBLANES - 1
    return [
        pl.BlockSpec((tm, width), lambda i: (i, col)),
        pl.BlockSpec((SUBLANES, width), lambda i: (jnp.maximum(i * per - 1, 0), col)),
        pl.BlockSpec((SUBLANES, width), lambda i: (jnp.minimum((i + 1) * per, last), col)),
    ]


def _rwkv_prep_kernel(rc, rp, rn, kc, kp, kn, vc, vp, vn, lc, lp, ln,
                      mu_r, mu_k, mu_v, mu_l, w0, w2, a0, a2, g2, kk_s, ka_s,
                      r_o, v_o, kk_o, lw_o, kd_o, bb_o, g_o, *, tm, n_lat_rows, lat_len, ctx_len):
    first, last = _tile_flags(pl.program_id(0), tm, n_lat_rows, lat_len, ctx_len)

    def shifted(c_ref, p_ref, n_ref, mu_ref):
        c = c_ref[...]
        prev_row = jnp.where(first, 0.0, p_ref[SUBLANES - 1:SUBLANES, :])
        next_row = jnp.where(last, 0.0, n_ref[0:1, :])
        xp, xn = _row_shift(c, prev_row, next_row)
        return c + mu_ref[...] * (0.5 * (xp + xn) - c)

    r = shifted(rc, rp, rn, mu_r)
    k = shifted(kc, kp, kn, mu_k)
    v = shifted(vc, vp, vn, mu_v)
    lo = shifted(lc, lp, ln, mu_l)
    r_o[...] = r
    v_o[...] = v

    ones = _head_ones()
    kk = k * kk_s[...]
    ss = _head_sum(kk * kk, ones)
    kk = kk * lax.rsqrt(jnp.maximum(ss, 1e-24))
    kk_o[...] = kk

    for d in range(2):
        wd = lo[:, d * LANES:(d + 1) * LANES]
        ad = lo[:, (2 + d) * LANES:(3 + d) * LANES]
        z = w0[d:d + 1, :] + _bdot(jnp.tanh(wd), w2[d])
        log_w = -_softplus(-z) - 0.5
        lw_o[d] = -jnp.exp(log_w)
        icl = jax.nn.sigmoid(a0[d:d + 1, :] + _bdot(ad, a2[d]))
        kd_o[d] = k * (1.0 + (icl - 1.0) * ka_s[...])
        bb_o[d] = kk * icl
    gd = lo[:, 4 * LANES:4 * LANES + GATE_LORA]
    g_o[...] = _bdot(jax.nn.sigmoid(gd), g2[...])


def rwkv_prep_call(p, prm, *, n_lat_rows, lat_len, ctx_len, col_r, col_k, col_v, col_l):
    r_rows = p.shape[0]
    tm = PREP_ROWS
    aw = prm["mu_r"].shape[1]
    full = lambda shape: pl.BlockSpec(shape, lambda i: (0,) * len(shape))
    in_specs = (_halo_specs(tm, aw, col_r, r_rows) + _halo_specs(tm, aw, col_k, r_rows)
                + _halo_specs(tm, aw, col_v, r_rows) + _halo_specs(tm, LORA_PAD, col_l, r_rows)
                + [full((1, aw)), full((1, aw)), full((1, aw)), full((1, LORA_PAD)),
                   full((2, aw)), full((2, LANES, aw)), full((2, aw)), full((2, LANES, aw)),
                   full((GATE_LORA, aw)), full((1, aw)), full((1, aw))])
    row_spec = pl.BlockSpec((tm, aw), lambda i: (i, 0))
    dir_spec = pl.BlockSpec((2, tm, aw), lambda i: (0, i, 0))
    one = jax.ShapeDtypeStruct((r_rows, aw), F32)
    two = jax.ShapeDtypeStruct((2, r_rows, aw), F32)
    kern = functools.partial(_rwkv_prep_kernel, tm=tm, n_lat_rows=n_lat_rows, lat_len=lat_len, ctx_len=ctx_len)
    return pl.pallas_call(
        kern,
        out_shape=(one, one, one, two, two, two, one),
        grid=(r_rows // tm,),
        in_specs=in_specs,
        out_specs=(row_spec, row_spec, row_spec, dir_spec, dir_spec, dir_spec, row_spec),
        compiler_params=_cparams(("parallel",)),
        name="rwkv_prep",
    )(p, p, p, p, p, p, p, p, p, p, p, p,
      prm["mu_r"], prm["mu_k"], prm["mu_v"], prm["mu_l"], prm["w0"], prm["w2"], prm["a0"], prm["a2"],
      prm["g2"], prm["k_k"], prm["k_a"])


def _lru_prep_kernel(xc_ref, xp_ref, xn_ref, cw, cb, wcat, bcat, lsl, a_o, u_o,
                     *, tm, n_lat_rows, lat_len, ctx_len):
    first, last = _tile_flags(pl.program_id(0), tm, n_lat_rows, lat_len, ctx_len)
    x = xc_ref[...]
    prev_row = jnp.where(first, 0.0, xp_ref[SUBLANES - 1:SUBLANES, :])
    next1 = jnp.where(last, 0.0, xn_ref[0:1, :])
    next2 = jnp.where(last, 0.0, xn_ref[1:2, :])
    rows = lax.broadcasted_iota(jnp.int32, x.shape, 0)
    xm1 = jnp.where(rows == 0, prev_row, pltpu.roll(x, 1, axis=0))
    xp1 = jnp.where(rows == tm - 1, next1, pltpu.roll(x, tm - 1, axis=0))
    xp2 = jnp.where(rows == tm - 1, next2, jnp.where(rows == tm - 2, next1, pltpu.roll(x, tm - 2, axis=0)))
    xc = cb[...] + xm1 * cw[0:1, :] + x * cw[1:2, :] + xp1 * cw[2:3, :] + xp2 * cw[3:4, :]
    nb = x.shape[1] // LRU_BLOCK_W
    gates = jnp.concatenate(
        [_bdot(xc[:, n * LRU_BLOCK_W:(n + 1) * LRU_BLOCK_W], wcat[n]) for n in range(nb)], axis=1)
    gates = gates + bcat[...]
    for d in range(2):
        gr = jnp.concatenate([gates[:, (4 * n + 2 * d) * LRU_BLOCK_W:(4 * n + 2 * d + 1) * LRU_BLOCK_W]
                              for n in range(nb)], axis=1)
        gi = jnp.concatenate([gates[:, (4 * n + 2 * d + 1) * LRU_BLOCK_W:(4 * n + 2 * d + 2) * LRU_BLOCK_W]
                              for n in range(nb)], axis=1)
        log_a = jax.nn.sigmoid(gr) * lsl[d:d + 1, :]
        a = jnp.exp(log_a)
        om = -jnp.tanh(log_a) * (a * a + 1.0)
        a_o[d] = a
        u_o[d] = xc * jax.nn.sigmoid(gi) * jnp.sqrt(om)


def lru_prep_call(p, prm, *, n_lat_rows, lat_len, ctx_len, col_x):
    r_rows = p.shape[0]
    tm = PREP_ROWS
    bw = prm["conv_b"].shape[1]
    nb = bw // LRU_BLOCK_W
    full = lambda shape: pl.BlockSpec(shape, lambda i: (0,) * len(shape))
    in_specs = _halo_specs(tm, bw, col_x, r_rows) + [
        full((CONV_W, bw)), full((1, bw)), full((nb, LRU_BLOCK_W, 4 * LRU_BLOCK_W)), full((1, 4 * bw)), full((2, bw))]
    dir_spec = pl.BlockSpec((2, tm, bw), lambda i: (0, i, 0))
    two = jax.ShapeDtypeStruct((2, r_rows, bw), F32)
    kern = functools.partial(_lru_prep_kernel, tm=tm, n_lat_rows=n_lat_rows, lat_len=lat_len, ctx_len=ctx_len)
    return pl.pallas_call(
        kern,
        out_shape=(two, two),
        grid=(r_rows // tm,),
        in_specs=in_specs,
        out_specs=(dir_spec, dir_spec),
        compiler_params=_cparams(("parallel",)),
        name="lru_prep",
    )(p, p, p, prm["conv_w"], prm["conv_b"], prm["wcat"], prm["bcat"], prm["lsl"])


def _scan_block(b, s, *, reverse, n_lat_blocks_total, lat_blocks):
    ctx_blk = n_lat_blocks_total + b
    lat_blk = b * lat_blocks + ((lat_blocks - s) if reverse else (s - 1))
    return jnp.where(s == 0, ctx_blk, lat_blk)


def _rwkv_scan_kernel(r_ref, lw_ref, k_ref, v_ref, kk_ref, bb_ref, y_ref, s_ref, *, reverse):
    @pl.when(pl.program_id(2) == 0)
    def _():
        s_ref[...] = jnp.zeros_like(s_ref)

    L = CHUNK
    ri = lax.broadcasted_iota(jnp.int32, (L, L), 0)
    ci = lax.broadcasted_iota(jnp.int32, (L, L), 1)
    tri = ((ci >= ri) if reverse else (ci <= ri)).astype(F32)
    r2 = lax.broadcasted_iota(jnp.int32, (2 * L, 2 * L), 0) % L
    c2 = lax.broadcasted_iota(jnp.int32, (2 * L, 2 * L), 1) % L
    strict = (c2 > r2) if reverse else (c2 < r2)
    incl = (c2 >= r2) if reverse else (c2 <= r2)
    rr = lax.broadcasted_iota(jnp.int32, (2 * L, 2 * L), 0)
    cc = lax.broadcasted_iota(jnp.int32, (2 * L, 2 * L), 1)
    eye = (rr == cc).astype(F32)
    off_masks = []
    sz = 2
    while sz <= L:
        late, early = (cc, rr) if reverse else (rr, cc)
        off_masks.append((rr // sz == cc // sz) & (late % sz >= sz // 2) & (early % sz < sz // 2))
        sz *= 2
    lane = lax.broadcasted_iota(jnp.int32, (L, LANES), 1)
    head0 = lane < HEAD_DIM

    def stack2(x):
        return jnp.concatenate([jnp.where(head0, x, 0.0), jnp.where(head0, 0.0, x)], axis=0)

    n_sub = SCAN_ROWS // L
    n_pair = r_ref.shape[1] // LANES
    order = [(n_sub - 1 - j) if reverse else j for j in range(n_sub)]

    keys = [(c, p) for c in order for p in range(n_pair)]
    a2, b2, k2, q2, v2, bt2, kt2, decay = ({} for _ in range(8))
    for c in order:
        rows = pl.ds(c * L, L)
        lw_all = lw_ref[rows, :]
        cum_all = jnp.dot(tri, lw_all, preferred_element_type=F32, precision=lax.Precision.HIGHEST)
        for p in range(n_pair):
            cols = slice(p * LANES, (p + 1) * LANES)
            lw = lw_all[:, cols]
            cum = cum_all[:, cols]
            tot = cum[0:1, :] if reverse else cum[L - 1:L, :]
            e_pos = jnp.exp(cum)
            e_neg = jnp.exp(-cum)
            e_rem = jnp.exp(tot - cum)
            kk = kk_ref[rows, cols]
            kd = k_ref[rows, cols]
            bb = bb_ref[rows, cols]
            key = (c, p)
            a2[key] = stack2(-kk * jnp.exp(cum - lw))
            b2[key] = stack2(bb * e_neg)
            k2[key] = stack2(kd * e_neg)
            q2[key] = stack2(r_ref[rows, cols] * e_pos)
            v2[key] = stack2(v_ref[rows, cols])
            bt2[key] = stack2(bb * e_rem)
            kt2[key] = stack2(kd * e_rem)
            decay[key] = jnp.exp(tot)

    hh = 2 * L
    sc = {k: _bdot_nt(jnp.concatenate([a2[k], q2[k]], axis=0), jnp.concatenate([b2[k], k2[k]], axis=0))
          for k in keys}
    mm = {k: jnp.where(strict, sc[k][:hh, :hh], 0.0) for k in keys}
    nn = {k: jnp.where(strict, sc[k][:hh, hh:], 0.0) for k in keys}
    qq = {k: jnp.where(incl, sc[k][hh:, :hh], 0.0) for k in keys}
    zz = {k: jnp.where(incl, sc[k][hh:, hh:], 0.0) for k in keys}
    tinv = {k: eye + jnp.where(off_masks[0], mm[k], 0.0) for k in keys}
    for off in off_masks[1:]:
        half_step = {k: _bdot(tinv[k], jnp.where(off, mm[k], 0.0)) for k in keys}
        tinv = {k: tinv[k] + _bdot(half_step[k], tinv[k]) for k in keys}
    nzv = {k: _bdot(jnp.concatenate([nn[k], zz[k]], axis=0), v2[k]) for k in keys}
    px = {k: _bdot(tinv[k], jnp.concatenate([a2[k], nzv[k][:hh]], axis=1)) for k in keys}
    qpx = {k: _bdot(qq[k], px[k]) for k in keys}
    g_mat = {k: q2[k] + qpx[k][:, :LANES] for k in keys}
    y_loc = {k: qpx[k][:, LANES:] + nzv[k][hh:] for k in keys}
    pb = {k: _bdot_tn(px[k], bt2[k]) for k in keys}
    phi = {k: pb[k][:LANES] for k in keys}
    psi = {k: pb[k][LANES:] + _bdot_tn(v2[k], kt2[k]) for k in keys}
    items = {k: (g_mat[k], y_loc[k], decay[k], phi[k], psi[k]) for k in keys}

    st = [s_ref[p] for p in range(n_pair)]
    for c in order:
        for p in range(n_pair):
            g, yl, dec, ph, ps = items[(c, p)]
            y2 = _bdot_nt(g, st[p]) + yl
            y_ref[pl.ds(c * L, L), p * LANES:(p + 1) * LANES] = y2[:L, :] + y2[L:, :]
            st[p] = st[p] * dec + _bdot(st[p], ph) + ps
    for p in range(n_pair):
        s_ref[p] = st[p]


def rwkv_scan_call(r, lw, kd, v, kk, bb, *, reverse, batch, lat_len, ctx_len):
    r_rows, aw = r.shape
    d = 1 if reverse else 0
    lat_blocks = lat_len // SCAN_ROWS
    width = SCAN_PAIRS * LANES
    blk = functools.partial(_scan_block, reverse=reverse, n_lat_blocks_total=batch * lat_blocks,
                            lat_blocks=lat_blocks)
    row_spec = pl.BlockSpec((SCAN_ROWS, width), lambda b, h, s: (blk(b, s), h))
    dir_spec = pl.BlockSpec((None, SCAN_ROWS, width), lambda b, h, s: (d, blk(b, s), h))
    return pl.pallas_call(
        functools.partial(_rwkv_scan_kernel, reverse=reverse),
        out_shape=jax.ShapeDtypeStruct((r_rows, aw), F32),
        grid=(batch, aw // width, 1 + lat_blocks),
        in_specs=[row_spec, dir_spec, dir_spec, row_spec, row_spec, dir_spec],
        out_specs=row_spec,
        scratch_shapes=[pltpu.VMEM((SCAN_PAIRS, LANES, LANES), F32)],
        compiler_params=_cparams(("parallel", "parallel", "arbitrary")),
        name="rwkv_scan_bwd" if reverse else "rwkv_scan_fwd",
    )(r, lw, kd, v, kk, bb)


def _lru_scan_kernel(a_ref, u_ref, h_ref, carry_ref, *, reverse):
    @pl.when(pl.program_id(1) == 0)
    def _():
        carry_ref[...] = jnp.zeros_like(carry_ref)

    w = a_ref.shape[1]
    rows = lax.broadcasted_iota(jnp.int32, (SUBLANES, w), 0)
    n_groups = SCAN_ROWS // SUBLANES

    def body(gi, carry):
        g = (n_groups - 1 - gi) if reverse else gi
        sl = pl.ds(pl.multiple_of(g * SUBLANES, SUBLANES), SUBLANES)
        a = a_ref[sl, :]
        u = u_ref[sl, :]
        for sh in (1, 2, 4):
            if reverse:
                ok = rows < SUBLANES - sh
                a_s = pltpu.roll(a, SUBLANES - sh, axis=0)
                u_s = pltpu.roll(u, SUBLANES - sh, axis=0)
            else:
                ok = rows >= sh
                a_s = pltpu.roll(a, sh, axis=0)
                u_s = pltpu.roll(u, sh, axis=0)
            u = jnp.where(ok, a * u_s + u, u)
            a = jnp.where(ok, a * a_s, a)
        h = a * carry + u
        h_ref[sl, :] = h
        return h[0:1, :] if reverse else h[SUBLANES - 1:SUBLANES, :]

    carry_ref[...] = lax.fori_loop(0, n_groups, body, carry_ref[...])


def lru_scan_call(a, u, *, reverse, batch, lat_len, ctx_len):
    _, r_rows, bw = a.shape
    d = 1 if reverse else 0
    lat_blocks = lat_len // SCAN_ROWS
    blk = functools.partial(_scan_block, reverse=reverse, n_lat_blocks_total=batch * lat_blocks,
                            lat_blocks=lat_blocks)
    dir_spec = pl.BlockSpec((None, SCAN_ROWS, bw), lambda b, s: (d, blk(b, s), 0))
    return pl.pallas_call(
        functools.partial(_lru_scan_kernel, reverse=reverse),
        out_shape=jax.ShapeDtypeStruct((r_rows, bw), F32),
        grid=(batch, 1 + lat_blocks),
        in_specs=[dir_spec, dir_spec],
        out_specs=pl.BlockSpec((SCAN_ROWS, bw), lambda b, s: (blk(b, s), 0)),
        scratch_shapes=[pltpu.VMEM((1, bw), F32)],
        compiler_params=_cparams(("parallel", "arbitrary")),
        name="lru_scan_bwd" if reverse else "lru_scan_fwd",
    )(a, u)


def _even_post_kernel(yf, yb, r_ref, v_ref, kdf, kdb, g_ref, hf, hb, gb_ref, rk, lnw, lnb, o_ref):
    ones = _head_ones()
    y = yf[...] + yb[...]
    inv_n = 1.0 / HEAD_DIM
    yc = y - _head_sum(y, ones) * inv_n
    var = _head_sum(yc * yc, ones) * inv_n
    hn = yc * lax.rsqrt(var + LNX_EPS) * lnw[...] + lnb[...]
    r = r_ref[...]
    v = v_ref[...]
    bonus = _head_sum(r * kdf[...] * rk[...], ones) * v + _head_sum(r * kdb[...] * rk[...], ones) * v
    ya = (hn + bonus) * g_ref[...]
    yl = (hf[...] + hb[...]) * jax.nn.gelu(gb_ref[...])
    aw = ya.shape[1]
    o_ref[:, :aw] = ya.astype(BF16)
    o_ref[:, aw:] = yl.astype(BF16)


def even_post_call(yf, yb, r, v, kd, g, hf, hb, p, prm, *, col_gb):
    r_rows, aw = yf.shape
    bw = hf.shape[1]
    tm = PREP_ROWS
    row = lambda w: pl.BlockSpec((tm, w), lambda i: (i, 0))
    full = lambda shape: pl.BlockSpec(shape, lambda i: (0,) * len(shape))
    return pl.pallas_call(
        _even_post_kernel,
        out_shape=jax.ShapeDtypeStruct((r_rows, aw + bw), BF16),
        grid=(r_rows // tm,),
        in_specs=[row(aw), row(aw), row(aw), row(aw),
                  pl.BlockSpec((None, tm, aw), lambda i: (0, i, 0)),
                  pl.BlockSpec((None, tm, aw), lambda i: (1, i, 0)),
                  row(aw), row(bw), row(bw),
                  pl.BlockSpec((tm, bw), lambda i: (i, col_gb)),
                  full((1, aw)), full((1, aw)), full((1, aw))],
        out_specs=row(aw + bw),
        compiler_params=_cparams(("parallel",)),
        name="even_post",
    )(yf, yb, r, v, kd, kd, g, hf, hb, p, prm["r_k"], prm["lnx_w"], prm["lnx_b"])


def _mlstm_kernel(q_ref, k_ref, v_ref, br_ref, bc_ref, ir_ref, ic_ref, h_ref, c_ref, n_ref, m_ref, *, reverse):
    @pl.when(pl.program_id(2) == 0)
    def _():
        c_ref[...] = jnp.zeros_like(c_ref)
        n_ref[...] = jnp.zeros_like(n_ref)
        m_ref[...] = jnp.zeros_like(m_ref)

    L = CHUNK
    ri = lax.broadcasted_iota(jnp.int32, (L, L), 0)
    ci = lax.broadcasted_iota(jnp.int32, (L, L), 1)
    causal = (ci >= ri) if reverse else (ci <= ri)
    n_sub = SCAN_ROWS // L
    order = [(n_sub - 1 - j) if reverse else j for j in range(n_sub)]
    scale = C_QK_HEAD ** -0.5

    m_cur = m_ref[...]
    m_in, ew, ec = {}, {}, {}
    for c in order:
        b_r = br_ref[c:c + 1, :]
        b_tot = b_r[:, 0:1] if reverse else b_r[:, L - 1:L]
        w_in = b_tot - bc_ref[:, c:c + 1] + ic_ref[:, c:c + 1]
        carry_log = b_tot + m_cur
        m_new = jnp.maximum(carry_log, jnp.max(w_in, axis=0, keepdims=True))
        m_in[c] = m_cur
        ew[c] = jnp.exp(w_in - m_new)
        ec[c] = jnp.exp(carry_log - m_new)
        m_cur = m_new
    m_ref[...] = m_cur

    q = {c: (q_ref[pl.ds(c * L, L), :] * scale).astype(BF16) for c in order}
    k = {c: k_ref[pl.ds(c * L, L), :] for c in order}
    v = {c: v_ref[pl.ds(c * L, L), :].astype(BF16) for c in order}
    qk = {c: _bdot_nt(q[c], k[c]) for c in order}
    m_t, e_inter, scores = {}, {}, {}
    for c in order:
        b_c = bc_ref[:, c:c + 1]
        dmat = jnp.where(causal, b_c - br_ref[c:c + 1, :] + ir_ref[c:c + 1, :], -jnp.inf)
        inter = b_c + m_in[c]
        m_t[c] = jnp.maximum(inter, jnp.max(dmat, axis=1, keepdims=True))
        scores[c] = qk[c] * jnp.exp(dmat - m_t[c])
        e_inter[c] = jnp.exp(inter - m_t[c])
    intra = {c: _bdot(scores[c], v[c]) for c in order}
    kw = {c: k[c] * ew[c] for c in order}
    kv = {c: _bdot_tn(kw[c], v[c]) for c in order}

    c_cur = c_ref[...]
    n_cur = n_ref[...]
    c_in, n_in = {}, {}
    for c in order:
        c_in[c] = c_cur
        n_in[c] = n_cur
        c_cur = ec[c] * c_cur + kv[c]
        n_cur = ec[c] * n_cur + jnp.sum(kw[c], axis=0, keepdims=True)
    c_ref[...] = c_cur
    n_ref[...] = n_cur

    qc = {c: _bdot(q[c], c_in[c]) for c in order}
    for c in order:
        qf = q[c].astype(F32)
        num = e_inter[c] * qc[c] + intra[c]
        den = (e_inter[c] * jnp.sum(qf * n_in[c], axis=1, keepdims=True)
               + jnp.sum(scores[c], axis=1, keepdims=True))
        h_ref[pl.ds(c * L, L), :] = num / jnp.maximum(jnp.abs(den), jnp.exp(-m_t[c]))


def mlstm_call(p, g_rows, g_cols, *, reverse, batch, lat_len, ctx_len, col_q, col_k, col_v):
    r_rows = p.shape[0]
    d = 1 if reverse else 0
    n_sub = SCAN_ROWS // CHUNK
    lat_blocks = lat_len // SCAN_ROWS
    blk = functools.partial(_scan_block, reverse=reverse, n_lat_blocks_total=batch * lat_blocks,
                            lat_blocks=lat_blocks)
    grow = lambda which: pl.BlockSpec((None, None, None, None, n_sub, CHUNK),
                                      lambda b, h, s: (d, which, h, blk(b, s), 0, 0))
    gcol = lambda which: pl.BlockSpec((None, None, None, None, CHUNK, n_sub),
                                      lambda b, h, s: (d, which, h, blk(b, s), 0, 0))
    return pl.pallas_call(
        functools.partial(_mlstm_kernel, reverse=reverse),
        out_shape=jax.ShapeDtypeStruct((r_rows, C_HEADS * C_V_HEAD), F32),
        grid=(batch, C_HEADS, 1 + lat_blocks),
        in_specs=[
            pl.BlockSpec((SCAN_ROWS, C_QK_HEAD), lambda b, h, s: (blk(b, s), col_q + h)),
            pl.BlockSpec((SCAN_ROWS, C_QK_HEAD), lambda b, h, s: (blk(b, s), col_k + h)),
            pl.BlockSpec((SCAN_ROWS, C_V_HEAD), lambda b, h, s: (blk(b, s), col_v + h)),
            grow(0), gcol(0), grow(1), gcol(1),
        ],
        out_specs=pl.BlockSpec((SCAN_ROWS, C_V_HEAD), lambda b, h, s: (blk(b, s), h)),
        scratch_shapes=[pltpu.VMEM((C_QK_HEAD, C_V_HEAD), F32), pltpu.VMEM((1, C_QK_HEAD), F32),
                        pltpu.VMEM((1, 1), F32)],
        compiler_params=_cparams(("parallel", "parallel", "arbitrary")),
        name="mlstm_bwd" if reverse else "mlstm_fwd",
    )(p, p, p, g_rows, g_cols, g_rows, g_cols)


def _odd_post_kernel(hf, hb, o_ref_in, nw, nb, y_ref):
    x = hf[...] + hb[...]
    xc = x - jnp.mean(x, axis=1, keepdims=True)
    var = jnp.mean(xc * xc, axis=1, keepdims=True)
    hn = xc * lax.rsqrt(var + MLSTM_NORM_EPS) * nw[...] + nb[...]
    y_ref[...] = (hn * jax.nn.sigmoid(o_ref_in[...])).astype(BF16)


def odd_post_call(hf, hb, p, nw, nb, *, n_rows, col_o):
    vw = hf.shape[1]
    tm = LN_ROWS
    head = pl.BlockSpec((tm, C_V_HEAD), lambda i, h: (i, h))
    par = pl.BlockSpec((1, C_V_HEAD), lambda i, h: (0, h))
    return pl.pallas_call(
        _odd_post_kernel,
        out_shape=jax.ShapeDtypeStruct((n_rows, vw), BF16),
        grid=(n_rows // tm, C_HEADS),
        in_specs=[head, head, pl.BlockSpec((tm, C_V_HEAD), lambda i, h: (i, col_o + h)), par, par],
        out_specs=head,
        compiler_params=_cparams(("parallel", "parallel")),
        name="odd_post",
    )(hf, hb, p, nw, nb)


def _layer_norm(z, g, b):
    zc = z - jnp.mean(z, axis=1, keepdims=True)
    var = jnp.mean(zc * zc, axis=1, keepdims=True)
    return zc * lax.rsqrt(var + LN_EPS) * g + b


def _route(logits_t, bias):
    aff = [jax.nn.sigmoid(logits_t[e:e + 1, :]) for e in range(N_EXPERTS)]
    biased = [aff[e] + bias[e:e + 1, :] for e in range(N_EXPERTS)]
    best_g = best_v = None
    for g in range(N_GROUPS):
        m = biased[g * EXPERTS_PER_GROUP:(g + 1) * EXPERTS_PER_GROUP]
        pair = None
        for i in range(EXPERTS_PER_GROUP):
            for j in range(i + 1, EXPERTS_PER_GROUP):
                hi = jnp.maximum(m[i], m[j])
                lo_ = jnp.minimum(m[i], m[j])
                s = hi + lo_
                pair = s if pair is None else jnp.maximum(pair, s)
        if best_v is None:
            best_v, best_g = pair, jnp.zeros(pair.shape, jnp.int32)
        else:
            upd = pair > best_v
            best_g = jnp.where(upd, g, best_g)
            best_v = jnp.where(upd, pair, best_v)
    ids, sels = [], []
    taken = None
    for _ in range(TOP_K):
        cur_v = cur_i = cur_a = None
        for e in range(N_EXPERTS):
            ok = best_g == (e // EXPERTS_PER_GROUP)
            if taken is not None:
                ok = jnp.logical_and(ok, taken != e)
            val = jnp.where(ok, biased[e], -jnp.inf)
            if cur_v is None:
                cur_v, cur_i, cur_a = val, jnp.zeros(val.shape, jnp.int32), aff[e]
            else:
                upd = val > cur_v
                cur_i = jnp.where(upd, e, cur_i)
                cur_a = jnp.where(upd, aff[e], cur_a)
                cur_v = jnp.where(upd, val, cur_v)
        ids.append(cur_i)
        sels.append(cur_a)
        taken = cur_i
    tot = sels[0] + sels[1]
    return ids, [sels[0] / tot, sels[1] / tot]


def _ln_route_kernel(lat_ref, y_ref, gate_ref, sh_ref, sc_ref, g_ref, b_ref, rw_ref, rb_ref,
                     lat_o, h_o, id_o, gt_o):
    z = ALPHA * lat_ref[...] + gate_ref[...] * y_ref[...]
    ln = _layer_norm(z, g_ref[...], b_ref[...])
    lat_o[...] = ln
    h = ln * (1.0 + sc_ref[...]) + sh_ref[...]
    half = h.shape[1] // 2
    bits = lax.bitcast_convert_type(h.astype(BF16).astype(F32), jnp.uint32)
    h_o[...] = (bits[:, :half] >> 16) | bits[:, half:]
    logits_t = lax.dot_general(rw_ref[...], h, (((1,), (1,)), ((), ())), preferred_element_type=F32,
                               precision=lax.Precision.HIGHEST)
    ids, gts = _route(logits_t, rb_ref[...])
    id_o[...] = jnp.concatenate(ids, axis=0)
    gt_o[...] = jnp.concatenate(gts, axis=0)


def ln_route_call(lat, y, mods, norm_g, norm_b, layer, router_wt, router_b, *, n_rows, rows_per_mod, n_mod_rows):
    d = lat.shape[1]
    tm = LN_ROWS

    def midx(i):
        return jnp.minimum(i * tm // rows_per_mod, n_mod_rows - 1)

    row = pl.BlockSpec((tm, d), lambda i: (i, 0))
    packed = pl.BlockSpec((tm, d // 2), lambda i: (i, 0))
    mod = lambda k: pl.BlockSpec((None, 1, d), lambda i: (midx(i), 0, k))
    nrm = pl.BlockSpec((None, None, 1, d), lambda i: (layer, 0, 0, 0))
    full = lambda shape: pl.BlockSpec(shape, lambda i: (0,) * len(shape))
    sel = pl.BlockSpec((TOP_K, tm), lambda i: (0, i))
    return pl.pallas_call(
        _ln_route_kernel,
        out_shape=(jax.ShapeDtypeStruct((n_rows, d), F32), jax.ShapeDtypeStruct((n_rows, d // 2), jnp.uint32),
                   jax.ShapeDtypeStruct((TOP_K, n_rows), jnp.int32), jax.ShapeDtypeStruct((TOP_K, n_rows), F32)),
        grid=(n_rows // tm,),
        in_specs=[row, row, mod(2), mod(3), mod(4), nrm, nrm, full((N_EXPERTS, d)), full((N_EXPERTS, 1))],
        out_specs=(row, packed, sel, sel),
        compiler_params=_cparams(("parallel",)),
        name="ln_route",
    )(lat, y, mods, mods, mods, norm_g, norm_b, router_wt, router_b)


def _ln_combine_kernel(lat_ref, y0_ref, y1_ref, gt_ref, gate_ref, g_ref, b_ref, lat_o):
    gt = gt_ref[...]
    f = y0_ref[...] * gt[:, 0:1] + y1_ref[...] * gt[:, 1:2]
    z = ALPHA * lat_ref[...] + gate_ref[...] * f
    lat_o[...] = _layer_norm(z, g_ref[...], b_ref[...])


def ln_combine_call(lat, y_slots, gates_t, mods, norm_g, norm_b, layer, *, n_rows, rows_per_mod, n_mod_rows):
    d = lat.shape[1]
    tm = LN_ROWS
    per_k = n_rows // tm

    def midx(i):
        return jnp.minimum(i * tm // rows_per_mod, n_mod_rows - 1)

    row = pl.BlockSpec((tm, d), lambda i: (i, 0))
    nrm = pl.BlockSpec((None, None, 1, d), lambda i: (layer, 1, 0, 0))
    return pl.pallas_call(
        _ln_combine_kernel,
        out_shape=jax.ShapeDtypeStruct((n_rows, d), F32),
        grid=(per_k,),
        in_specs=[row, row, pl.BlockSpec((tm, d), lambda i: (per_k + i, 0)),
                  pl.BlockSpec((tm, TOP_K), lambda i: (i, 0)),
                  pl.BlockSpec((None, 1, d), lambda i: (midx(i), 0, 5)), nrm, nrm],
        out_specs=row,
        compiler_params=_cparams(("parallel",)),
        name="ln_combine",
    )(lat, y_slots, y_slots, gates_t, mods, norm_g, norm_b)


def _moe_kernel(nu_ref, be_ref, jw_ref, src_cur, src_nxt, dst_prv, h_hbm, wg_ref, wu_ref, wd_ref, y_hbm,
                xbuf, acc, gsem, ssem, *, n_j, n_slots):
    del be_ref, jw_ref
    i = pl.program_id(0)
    j = pl.program_id(1)
    buf = i % 2
    other = 1 - buf
    n_used = nu_ref[0]
    tm, half = xbuf.shape[1], xbuf.shape[2]
    d = acc.shape[2]
    per_step = tm // n_j

    def row_in(tok, r, b):
        return pltpu.make_async_copy(h_hbm.at[pl.ds(tok, 1)], xbuf.at[b, pl.ds(r, 1)], gsem.at[b])

    def row_out(r, slot, b):
        return pltpu.make_async_copy(acc.at[b, pl.ds(r, 1)], y_hbm.at[pl.ds(slot, 1)], ssem.at[b])

    def send_previous(r):
        slot = jnp.where(i > 0, dst_prv[0, r], n_slots + r)
        row_out(r, slot, other).start()

    @pl.when((i == 0) & (j == 0))
    def _():
        acc[1] = jnp.zeros((tm, d), F32)

        def body(r, carry):
            row_in(src_cur[0, r], r, 0).start()
            return carry
        lax.fori_loop(0, tm, body, 0, unroll=8)

    @pl.when((j == 0) & (i <= n_used))
    def _():
        pltpu.make_async_copy(xbuf.at[buf], xbuf.at[buf], gsem.at[buf]).wait()

    @pl.when((j == 0) & (i < n_used))
    def _():
        acc[buf] = jnp.zeros((tm, d), F32)

    @pl.when(i < n_used)
    def _():
        tf = wg_ref.shape[1]
        g = jnp.zeros((tm, tf), F32)
        u = jnp.zeros((tm, tf), F32)
        for c0 in range(0, half, MOE_KCHUNK):
            xp = xbuf[buf, :, c0:c0 + MOE_KCHUNK]
            lo = lax.bitcast_convert_type(xp << 16, F32)
            hi = lax.bitcast_convert_type(xp & jnp.uint32(0xFFFF0000), F32)
            g = g + _bdot(lo, wg_ref[c0:c0 + MOE_KCHUNK, :]) + _bdot(hi, wg_ref[half + c0:half + c0 + MOE_KCHUNK, :])
            u = u + _bdot(lo, wu_ref[c0:c0 + MOE_KCHUNK, :]) + _bdot(hi, wu_ref[half + c0:half + c0 + MOE_KCHUNK, :])
        hdn = (g * jax.nn.sigmoid(g) * u).astype(BF16)
        for c0 in range(0, d, MOE_NCHUNK):
            acc[buf, :, c0:c0 + MOE_NCHUNK] += _bdot(hdn, wd_ref[:, c0:c0 + MOE_NCHUNK])
        for rr in range(per_step):
            r = j * per_step + rr
            row_in(src_nxt[0, r], r, other).start()
            send_previous(r)

    @pl.when(i == n_used)
    def _():
        for rr in range(per_step):
            send_previous(j * per_step + rr)

    @pl.when((j == n_j - 1) & (i <= n_used))
    def _():
        pltpu.make_async_copy(acc.at[other], acc.at[other], ssem.at[other]).wait()


def moe_call(h_packed, n_used, block_e, j_of, src_tok, dst_slot, w_gate, w_up, w_down, layer, *, n_slots):
    d = 2 * h_packed.shape[1]
    f = w_gate.shape[3]
    tm, tf = MOE_ROWS, MOE_FTILE
    n_blocks = block_e.shape[0]
    n_j = f // tf
    assert tm % n_j == 0
    idx = lambda fn: pl.BlockSpec((None, 1, tm), fn, memory_space=pltpu.SMEM)
    grid_spec = pltpu.PrefetchScalarGridSpec(
        num_scalar_prefetch=3,
        grid=(n_blocks, n_j),
        in_specs=[
            idx(lambda i, j, nu, be, jw: (i, 0, 0)),
            idx(lambda i, j, nu, be, jw: (jnp.minimum(i + 1, n_blocks - 1), 0, 0)),
            idx(lambda i, j, nu, be, jw: (jnp.maximum(i - 1, 0), 0, 0)),
            pl.BlockSpec(memory_space=pl.ANY),
            pl.BlockSpec((None, None, d, tf), lambda i, j, nu, be, jw: (layer, be[i], 0, jw[i, j])),
            pl.BlockSpec((None, None, d, tf), lambda i, j, nu, be, jw: (layer, be[i], 0, jw[i, j])),
            pl.BlockSpec((None, None, tf, d), lambda i, j, nu, be, jw: (layer, be[i], jw[i, j], 0)),
        ],
        out_specs=pl.BlockSpec(memory_space=pl.ANY),
        scratch_shapes=[pltpu.VMEM((2, tm, d // 2), jnp.uint32), pltpu.VMEM((2, tm, d), F32),
                        pltpu.SemaphoreType.DMA((2,)), pltpu.SemaphoreType.DMA((2,))],
    )
    return pl.pallas_call(
        functools.partial(_moe_kernel, n_j=n_j, n_slots=n_slots),
        out_shape=jax.ShapeDtypeStruct((n_slots + tm, d), F32),
        grid_spec=grid_spec,
        compiler_params=_cparams(("arbitrary", "arbitrary")),
        name="moe_experts",
    )(n_used, block_e, j_of, src_tok, src_tok, dst_slot, h_packed, w_gate, w_up, w_down)


def moe_ffn(h_packed, ids, w_gate, w_up, w_down, layer):
    n_tok = h_packed.shape[0]
    n_slots = n_tok * TOP_K
    tm = MOE_ROWS
    n_j = w_gate.shape[3] // MOE_FTILE
    i32 = jnp.int32
    flat_e = ids.reshape(n_slots)
    order = jnp.argsort(flat_e, stable=True).astype(i32)
    counts = jnp.sum((flat_e[:, None] == jnp.arange(N_EXPERTS, dtype=i32)[None, :]).astype(i32), axis=0)
    start = jnp.cumsum(counts) - counts
    padded = (counts + tm - 1) // tm * tm
    pad_end = jnp.cumsum(padded)
    pad_start = pad_end - padded
    n_blocks = -(-n_slots // tm) + N_EXPERTS + 1
    pos = jnp.arange(n_blocks * tm, dtype=i32)
    e_pos = jnp.minimum(jnp.sum((pos[:, None] >= pad_end[None, :]).astype(i32), axis=1), N_EXPERTS - 1)
    r_pos = pos - pad_start[e_pos]
    valid = r_pos < counts[e_pos]
    slot = order[jnp.clip(start[e_pos] + r_pos, 0, n_slots - 1)]
    src_tok = jnp.where(valid, slot % n_tok, 0).reshape(n_blocks, 1, tm)
    dst_slot = jnp.where(valid, slot, n_slots + pos % tm).reshape(n_blocks, 1, tm)
    n_used = (pad_end[-1] // tm).astype(i32)
    blk = jnp.arange(n_blocks, dtype=i32)
    block_e = e_pos.reshape(n_blocks, tm)[:, 0]
    block_e = jnp.where(blk < n_used, block_e, block_e[jnp.maximum(n_used - 1, 0)]).astype(i32)
    tiles = jnp.arange(n_j, dtype=i32)[None, :]
    j_of = jnp.where((blk % 2 == 0)[:, None], tiles, n_j - 1 - tiles)
    j_of = jnp.where((blk < n_used)[:, None], j_of, j_of[jnp.maximum(n_used - 1, 0), n_j - 1]).astype(i32)
    return moe_call(h_packed, n_used.reshape(1), block_e, j_of, src_tok.astype(i32), dst_slot.astype(i32),
                    w_gate, w_up, w_down, layer, n_slots=n_slots)


def _sincos_2d(n_tokens, dim):
    rows = n_tokens // GRID_W
    quarter = dim // 4
    omega = POS_BASE ** (-jnp.arange(quarter, dtype=F32) / quarter)
    ang_r = jnp.arange(rows, dtype=F32)[:, None] * omega[None, :]
    ang_c = jnp.arange(GRID_W, dtype=F32)[:, None] * omega[None, :]
    by_row = lambda t: jnp.repeat(t, GRID_W, axis=0)
    by_col = lambda t: jnp.tile(t, (rows, 1))
    return jnp.concatenate([by_row(jnp.sin(ang_r)), by_row(jnp.cos(ang_r)),
                            by_col(jnp.sin(ang_c)), by_col(jnp.cos(ang_c))], -1)


def _pad_cols(w, width):
    return jnp.pad(w, [(0, 0)] * (w.ndim - 1) + [(0, width - w.shape[-1])])


def _even_weights(ev_w_in, rwkv_mu, aw, bw):
    rk = 3 * aw
    lo0 = rk
    pieces = [DECAY_LORA, DECAY_LORA, ICL_LORA, ICL_LORA]

    def regroup(t):
        out = [t[..., rk + 2 * (DECAY_LORA + ICL_LORA) + GATE_LORA:],
               t[..., :rk]]
        off = lo0
        for wdt in pieces:
            out.append(_pad_cols(t[..., off:off + wdt], LANES))
            off += wdt
        out.append(t[..., off:off + GATE_LORA])
        used = 4 * LANES + GATE_LORA
        out.append(jnp.zeros(t.shape[:-1] + (LORA_PAD - used,), t.dtype))
        return jnp.concatenate(out, axis=-1)

    w = regroup(ev_w_in).astype(BF16)
    rwkv_cols = rk + 2 * (DECAY_LORA + ICL_LORA) + GATE_LORA
    mu_full = jnp.concatenate([rwkv_mu, jnp.zeros(rwkv_mu.shape[:-1] + (2 * bw,), rwkv_mu.dtype)], axis=-1)
    mu = regroup(mu_full)
    del rwkv_cols
    return w, mu


def kernel(x, c, ctx, c_ctx, ada_w, ada_b, norm_g, norm_b,
           ev_w_in, ev_w_out, rwkv_mu, rwkv_w0, rwkv_w2, rwkv_a0, rwkv_a2, rwkv_g2,
           rwkv_k_k, rwkv_k_a, rwkv_r_k, rwkv_lnx_w, rwkv_lnx_b,
           lru_conv_w, lru_conv_b, lru_wa, lru_ba, lru_wx, lru_bx, lru_lam,
           od_w_in, od_w_out, mlstm_ig_b, mlstm_fg_b, mlstm_norm_w, mlstm_norm_b,
           router_w, router_b, moe_w_gate, moe_w_up, moe_w_down):
    bsz, s_len, dim = x.shape
    n_ctx = ctx.shape[1]
    n_lat_rows = bsz * s_len
    n_all_rows = n_lat_rows + bsz * n_ctx
    aw = rwkv_k_k.shape[1]
    bw = lru_conv_b.shape[1]
    assert s_len % SCAN_ROWS == 0 and n_ctx == SCAN_ROWS and n_all_rows % MM_ROWS == 0
    assert (bsz * n_ctx) % MM_ROWS == 0 and s_len % MM_ROWS == 0 and bsz + 1 <= SUBLANES
    seq = dict(n_lat_rows=n_lat_rows, lat_len=s_len, ctx_len=n_ctx)
    scan = dict(batch=bsz, lat_len=s_len, ctx_len=n_ctx)
    modk = dict(rows_per_mod=s_len, n_mod_rows=bsz + 1)

    lat = (x + _sincos_2d(s_len, dim).astype(x.dtype)).reshape(n_lat_rows, dim)
    stream = jnp.concatenate([lat, ctx.reshape(bsz * n_ctx, dim)], axis=0)
    cvec = jnp.concatenate([c, c_ctx[None, :], jnp.zeros((SUBLANES - bsz - 1, dim), c.dtype)], axis=0)
    ada_b3 = ada_b[:, None, :]
    norm_g4 = norm_g[:, :, None, :]
    norm_b4 = norm_b[:, :, None, :]
    router_wt = router_w.T
    router_b2 = router_b[:, None]

    for layer in range(DEPTH):
        last = layer == DEPTH - 1
        i = layer // 2
        mods = adaln_call(cvec, ada_w, ada_b3, layer).reshape(SUBLANES, 1, N_MOD * dim)
        if layer % 2 == 0:
            w_in, mu = _even_weights(ev_w_in, rwkv_mu, aw, bw)
            n_in = w_in.shape[2]
            p = matmul_mod_call(stream, mods, w_in, i, tn=1024, **modk)
            col_x, col_gb = 0, bw // bw
            col_r, col_k, col_v = 2 * bw // aw, 2 * bw // aw + 1, 2 * bw // aw + 2
            col_l = (2 * bw + 3 * aw) // LORA_PAD
            assert (2 * bw + 3 * aw) % LORA_PAD == 0 and n_in == 2 * bw + 3 * aw + LORA_PAD
            mu_i = mu[i]
            o_r = 2 * bw
            prm = dict(
                mu_r=mu_i[None, o_r:o_r + aw], mu_k=mu_i[None, o_r + aw:o_r + 2 * aw],
                mu_v=mu_i[None, o_r + 2 * aw:o_r + 3 * aw], mu_l=mu_i[None, o_r + 3 * aw:],
                w0=rwkv_w0[i], w2=_pad_rows(rwkv_w2[i], LANES).astype(BF16),
                a0=rwkv_a0[i], a2=_pad_rows(rwkv_a2[i], LANES).astype(BF16),
                g2=rwkv_g2[i].astype(BF16), k_k=rwkv_k_k[i][None, :], k_a=rwkv_k_a[i][None, :])
            r_s, v_s, kk, lw, kd, bb, g = rwkv_prep_call(p, prm, col_r=col_r, col_k=col_k, col_v=col_v,
                                                          col_l=col_l, **seq)
            nb = bw // LRU_BLOCK_W
            wcat = jnp.concatenate([lru_wa[i, 0], lru_wx[i, 0], lru_wa[i, 1], lru_wx[i, 1]], axis=-1).astype(BF16)
            bcat = jnp.stack([lru_ba[i, 0].reshape(nb, LRU_BLOCK_W), lru_bx[i, 0].reshape(nb, LRU_BLOCK_W),
                              lru_ba[i, 1].reshape(nb, LRU_BLOCK_W), lru_bx[i, 1].reshape(nb, LRU_BLOCK_W)],
                             axis=1).reshape(1, 4 * bw)
            lprm = dict(conv_w=lru_conv_w[i], conv_b=lru_conv_b[i][None, :], wcat=wcat, bcat=bcat,
                        lsl=RGLRU_C * jax.nn.log_sigmoid(lru_lam[i]))
            la, lu = lru_prep_call(p, lprm, col_x=col_x, **seq)
            yf = rwkv_scan_call(r_s, lw, kd, v_s, kk, bb, reverse=False, **scan)
            yb = rwkv_scan_call(r_s, lw, kd, v_s, kk, bb, reverse=True, **scan)
            hf = lru_scan_call(la, lu, reverse=False, **scan)
            hb = lru_scan_call(la, lu, reverse=True, **scan)
            pprm = dict(r_k=rwkv_r_k[i].reshape(1, aw), lnx_w=rwkv_lnx_w[i][None, :], lnx_b=rwkv_lnx_b[i][None, :])
            ymix = even_post_call(yf, yb, r_s, v_s, kd, g, hf, hb, p, pprm, col_gb=col_gb)
            n_rows = n_all_rows if not last else n_lat_rows
            y = matmul_call(ymix, ev_w_out.astype(BF16), i, tn=1024, n_rows=n_rows)
        else:
            qk_w = C_HEADS * C_QK_HEAD
            v_w = C_HEADS * C_V_HEAD
            main = 2 * qk_w + 2 * v_w
            w_gate = _pad_cols(od_w_in[..., main:], LANES).astype(BF16)
            p = matmul_mod_call(stream, mods, od_w_in.astype(BF16), i, tn=1024, n_cols=main, **modk)
            gp = matmul_mod_call(stream, mods, w_gate, i, tn=LANES, **modk)[:, :4 * C_HEADS]
            gp = gp.reshape(n_all_rows, 2, 2, C_HEADS)
            g_rows, g_cols = _mlstm_gates(gp, mlstm_ig_b[i], mlstm_fg_b[i])
            cols = dict(col_q=0, col_k=(qk_w + v_w) // C_QK_HEAD, col_v=(2 * qk_w + v_w) // C_V_HEAD)
            hf = mlstm_call(p, g_rows, g_cols, reverse=False, **scan, **cols)
            hb = mlstm_call(p, g_rows, g_cols, reverse=True, **scan, **cols)
            n_rows = n_all_rows if not last else n_lat_rows
            ymix = odd_post_call(hf, hb, p, mlstm_norm_w[i][None, :], mlstm_norm_b[i][None, :],
                                 n_rows=n_rows, col_o=qk_w // C_V_HEAD)
            y = matmul_call(ymix, od_w_out.astype(BF16), i, tn=1024, n_rows=n_rows)
        n_rows = n_all_rows if not last else n_lat_rows
        stream1, h_moe, ids, gates = ln_route_call(stream, y, mods, norm_g4, norm_b4, layer, router_wt, router_b2,
                                                   n_rows=n_rows, **modk)
        y_slots = moe_ffn(h_moe, ids, moe_w_gate, moe_w_up, moe_w_down, layer)
        stream = ln_combine_call(stream1, y_slots, gates.T, mods, norm_g4, norm_b4, layer, n_rows=n_rows, **modk)
    return stream[:n_lat_rows].reshape(bsz, s_len, dim)


def _pad_rows(w, rows):
    return jnp.pad(w, ((0, 0), (0, rows - w.shape[1]), (0, 0)))


def _mlstm_gates(gp, ig_b, fg_b):
    n_rows = gp.shape[0]
    n_sub = SCAN_ROWS // CHUNK
    outs = []
    for d in range(2):
        ig = GATE_CAP * jnp.tanh((gp[:, d, 0] + ig_b[d]) / GATE_CAP)
        lf = jax.nn.log_sigmoid(GATE_CAP * jnp.tanh((gp[:, d, 1] + fg_b[d]) / GATE_CAP))
        lf = lf.reshape(n_rows // CHUNK, CHUNK, C_HEADS)
        b = jnp.cumsum(lf[:, ::-1], axis=1)[:, ::-1] if d == 1 else jnp.cumsum(lf, axis=1)
        both = jnp.stack([b, ig.reshape(n_rows // CHUNK, CHUNK, C_HEADS)], axis=0)
        outs.append(both)
    g = jnp.stack(outs, axis=0)
    g = g.reshape(2, 2, n_rows // SCAN_ROWS, n_sub, CHUNK, C_HEADS)
    g_rows = jnp.transpose(g, (0, 1, 5, 2, 3, 4))
    g_cols = jnp.transpose(g, (0, 1, 5, 2, 4, 3))
    return g_rows, g_cols
```

```python
import functools
import math

import jax
import jax.numpy as jnp
from jax import lax
from jax.experimental import pallas as pl
from jax.experimental.pallas import tpu as pltpu

F32 = jnp.float32
BF16 = jnp.bfloat16

DEPTH = 2
N_MOD = 6
ALPHA = (2 * DEPTH) ** 0.25
LN_EPS = 1e-5
HEAD_DIM = 64
DECAY_LORA = 96
ICL_LORA = 96
GATE_LORA = 256
LNX_EPS = 64e-5
LRU_BLOCK_W = 128
CONV_W = 4
RGLRU_C = 8.0
C_HEADS = 8
C_QK_HEAD = 256
C_V_HEAD = 512
GATE_CAP = 15.0
MLSTM_NORM_EPS = 1e-6
N_EXPERTS = 16
N_GROUPS = 4
EXPERTS_PER_GROUP = N_EXPERTS // N_GROUPS
TOP_K = 2
POS_BASE = 10000.0
GRID_W = 64

LANES = 128
SUBLANES = 8
SCAN_ROWS = 256
CHUNK = 64
MLSTM_CHUNK = 128
SCAN_PAIRS = 4
PREP_ROWS = 128
LN_ROWS = 256
MM_ROWS = 512
MOE_ROWS = 512
MOE_KSTEPS = 4
MOE_CSTEPS = 8
LORA_PAD = 1024
VMEM_LIMIT = 60 * 1024 * 1024


def _cparams(sem):
    return pltpu.CompilerParams(dimension_semantics=sem, vmem_limit_bytes=VMEM_LIMIT)


def _bdot(a, b):
    return jnp.dot(a.astype(BF16), b.astype(BF16), preferred_element_type=F32)


def _bdot_nt(a, b):
    return lax.dot_general(a.astype(BF16), b.astype(BF16), (((1,), (1,)), ((), ())),
                           preferred_element_type=F32)


def _bdot_tn(a, b):
    return lax.dot_general(a.astype(BF16), b.astype(BF16), (((0,), (0,)), ((), ())),
                           preferred_element_type=F32)


def _softplus(z):
    return jnp.maximum(z, 0.0) + jnp.log1p(jnp.exp(-jnp.abs(z)))


def _adaln_kernel(c_ref, w_ref, b_ref, o_ref):
    c = c_ref[...]
    s = c * jax.nn.sigmoid(c)
    o_ref[...] = _bdot(s, w_ref[...]) + b_ref[...]


def adaln_call(cvec, ada_w, ada_b, layer, tn=512):
    rows, d = cvec.shape
    n = ada_w.shape[2]
    return pl.pallas_call(
        _adaln_kernel,
        out_shape=jax.ShapeDtypeStruct((rows, n), F32),
        grid=(n // tn,),
        in_specs=[
            pl.BlockSpec((rows, d), lambda j: (0, 0)),
            pl.BlockSpec((None, d, tn), lambda j: (layer, 0, j)),
            pl.BlockSpec((None, 1, tn), lambda j: (layer, 0, j)),
        ],
        out_specs=pl.BlockSpec((rows, tn), lambda j: (0, j)),
        compiler_params=_cparams(("parallel",)),
        name="adaln",
    )(cvec, ada_w, ada_b)


def _mm_mod_kernel(x_ref, sh_ref, sc_ref, w_ref, o_ref, xb_ref):
    @pl.when(pl.program_id(1) == 0)
    def _():
        xb_ref[...] = (x_ref[...] * (1.0 + sc_ref[...]) + sh_ref[...]).astype(BF16)

    o_ref[...] = jnp.dot(xb_ref[...], w_ref[...], preferred_element_type=F32)


def matmul_mod_call(x, mods, w, layer_w, *, rows_per_mod, n_mod_rows, tn, n_rows=None, n_cols=None):
    r, d = x.shape
    n = w.shape[2] if n_cols is None else n_cols
    r = r if n_rows is None else n_rows
    tm = MM_ROWS

    def midx(i):
        return jnp.minimum(i * tm // rows_per_mod, n_mod_rows - 1)

    return pl.pallas_call(
        _mm_mod_kernel,
        out_shape=jax.ShapeDtypeStruct((r, n), F32),
        grid=(r // tm, n // tn),
        in_specs=[
            pl.BlockSpec((tm, d), lambda i, j: (i, 0)),
            pl.BlockSpec((None, 1, d), lambda i, j: (midx(i), 0, 0)),
            pl.BlockSpec((None, 1, d), lambda i, j: (midx(i), 0, 1)),
            pl.BlockSpec((None, d, tn), lambda i, j: (layer_w, 0, j)),
        ],
        out_specs=pl.BlockSpec((tm, tn), lambda i, j: (i, j)),
        scratch_shapes=[pltpu.VMEM((tm, d), BF16)],
        compiler_params=_cparams(("parallel", "arbitrary")),
        name="proj_in",
    )(x, mods, mods, w)


def _mm_kernel(x_ref, w_ref, o_ref):
    o_ref[...] = jnp.dot(x_ref[...], w_ref[...], preferred_element_type=F32)


def matmul_call(x, w, layer_w, *, tn, n_rows=None):
    r, k = x.shape
    n = w.shape[2]
    r = r if n_rows is None else n_rows
    tm = MM_ROWS
    return pl.pallas_call(
        _mm_kernel,
        out_shape=jax.ShapeDtypeStruct((r, n), F32),
        grid=(r // tm, n // tn),
        in_specs=[
            pl.BlockSpec((tm, k), lambda i, j: (i, 0)),
            pl.BlockSpec((None, k, tn), lambda i, j: (layer_w, 0, j)),
        ],
        out_specs=pl.BlockSpec((tm, tn), lambda i, j: (i, j)),
        compiler_params=_cparams(("parallel", "arbitrary")),
        name="proj_out",
    )(x, w)


def _tile_flags(i, tm, n_lat_rows, lat_len, ctx_len):
    row = i * tm
    in_lat = row < n_lat_rows
    pos = jnp.where(in_lat, row % lat_len, (row - n_lat_rows) % ctx_len)
    seq = jnp.where(in_lat, lat_len, ctx_len)
    return pos == 0, pos + tm == seq


def _row_shift(c, prev_row, next_row):
    tm = c.shape[0]
    rows = lax.broadcasted_iota(jnp.int32, c.shape, 0)
    xp = jnp.where(rows == 0, prev_row, pltpu.roll(c, 1, axis=0))
    xn = jnp.where(rows == tm - 1, next_row, pltpu.roll(c, tm - 1, axis=0))
    return xp, xn


def _head_ones():
    r = lax.broadcasted_iota(jnp.int32, (LANES, LANES), 0) // HEAD_DIM
    c = lax.broadcasted_iota(jnp.int32, (LANES, LANES), 1) // HEAD_DIM
    return (r == c).astype(BF16)


def _head_sum(x, ones):
    w = x.shape[1]
    parts = [jnp.dot(x[:, j:j + LANES].astype(BF16), ones, preferred_element_type=F32)
             for j in range(0, w, LANES)]
    return jnp.concatenate(parts, axis=1)


def _halo_specs(tm, width, col, n_rows):
    per = tm // SUBLANES
    last = n_rows // SUBLANES - 1
    return [
        pl.BlockSpec((tm, width), lambda i: (i, col)),
        pl.BlockSpec((SUBLANES, width), lambda i: (jnp.maximum(i * per - 1, 0), col)),
        pl.BlockSpec((SUBLANES, width), lambda i: (jnp.minimum((i + 1) * per, last), col)),
    ]


def _rwkv_prep_kernel(rc, rp, rn, kc, kp, kn, vc, vp, vn, lc, lp, ln,
                      mu_r, mu_k, mu_v, mu_l, w0, w2, a0, a2, g2, kk_s, ka_s,
                      r_o, v_o, kk_o, lw_o, kd_o, bb_o, g_o, *, tm, n_lat_rows, lat_len, ctx_len):
    first, last = _tile_flags(pl.program_id(0), tm, n_lat_rows, lat_len, ctx_len)

    def shifted(c_ref, p_ref, n_ref, mu_ref):
        c = c_ref[...]
        prev_row = jnp.where(first, 0.0, p_ref[SUBLANES - 1:SUBLANES, :])
        next_row = jnp.where(last, 0.0, n_ref[0:1, :])
        xp, xn = _row_shift(c, prev_row, next_row)
        return c + mu_ref[...] * (0.5 * (xp + xn) - c)

    r = shifted(rc, rp, rn, mu_r)
    k = shifted(kc, kp, kn, mu_k)
    v = shifted(vc, vp, vn, mu_v)
    lo = shifted(lc, lp, ln, mu_l)
    r_o[...] = r
    v_o[...] = v

    ones = _head_ones()
    kk = k * kk_s[...]
    ss = _head_sum(kk * kk, ones)
    kk = kk * lax.rsqrt(jnp.maximum(ss, 1e-24))
    kk_o[...] = kk

    for d in range(2):
        wd = lo[:, d * LANES:(d + 1) * LANES]
        ad = lo[:, (2 + d) * LANES:(3 + d) * LANES]
        z = w0[d:d + 1, :] + _bdot(jnp.tanh(wd), w2[d])
        log_w = -_softplus(-z) - 0.5
        lw_o[d] = -jnp.exp(log_w)
        icl = jax.nn.sigmoid(a0[d:d + 1, :] + _bdot(ad, a2[d]))
        kd_o[d] = k * (1.0 + (icl - 1.0) * ka_s[...])
        bb_o[d] = kk * icl
    gd = lo[:, 4 * LANES:4 * LANES + GATE_LORA]
    g_o[...] = _bdot(jax.nn.sigmoid(gd), g2[...])


def rwkv_prep_call(p, prm, *, n_lat_rows, lat_len, ctx_len, col_r, col_k, col_v, col_l):
    r_rows = p.shape[0]
    tm = PREP_ROWS
    aw = prm["mu_r"].shape[1]
    full = lambda shape: pl.BlockSpec(shape, lambda i: (0,) * len(shape))
    in_specs = (_halo_specs(tm, aw, col_r, r_rows) + _halo_specs(tm, aw, col_k, r_rows)
                + _halo_specs(tm, aw, col_v, r_rows) + _halo_specs(tm, LORA_PAD, col_l, r_rows)
                + [full((1, aw)), full((1, aw)), full((1, aw)), full((1, LORA_PAD)),
                   full((2, aw)), full((2, LANES, aw)), full((2, aw)), full((2, LANES, aw)),
                   full((GATE_LORA, aw)), full((1, aw)), full((1, aw))])
    row_spec = pl.BlockSpec((tm, aw), lambda i: (i, 0))
    dir_spec = pl.BlockSpec((2, tm, aw), lambda i: (0, i, 0))
    one = jax.ShapeDtypeStruct((r_rows, aw), F32)
    two = jax.ShapeDtypeStruct((2, r_rows, aw), F32)
    kern = functools.partial(_rwkv_prep_kernel, tm=tm, n_lat_rows=n_lat_rows, lat_len=lat_len, ctx_len=ctx_len)
    return pl.pallas_call(
        kern,
        out_shape=(one, one, one, two, two, two, one),
        grid=(r_rows // tm,),
        in_specs=in_specs,
        out_specs=(row_spec, row_spec, row_spec, dir_spec, dir_spec, dir_spec, row_spec),
        compiler_params=_cparams(("parallel",)),
        name="rwkv_prep",
    )(p, p, p, p, p, p, p, p, p, p, p, p,
      prm["mu_r"], prm["mu_k"], prm["mu_v"], prm["mu_l"], prm["w0"], prm["w2"], prm["a0"], prm["a2"],
      prm["g2"], prm["k_k"], prm["k_a"])


def _scan_block(b, s, *, reverse, n_lat_blocks_total, lat_blocks):
    ctx_blk = n_lat_blocks_total + b
    lat_blk = b * lat_blocks + ((lat_blocks - s) if reverse else (s - 1))
    return jnp.where(s == 0, ctx_blk, lat_blk)


def _rwkv_scan_kernel(r_ref, lw_ref, k_ref, v_ref, kk_ref, bb_ref, y_ref, s_ref, *, reverse):
    @pl.when(pl.program_id(2) == 0)
    def _():
        s_ref[...] = jnp.zeros_like(s_ref)

    L = CHUNK
    ri = lax.broadcasted_iota(jnp.int32, (L, L), 0)
    ci = lax.broadcasted_iota(jnp.int32, (L, L), 1)
    tri = ((ci >= ri) if reverse else (ci <= ri)).astype(F32)
    r2 = lax.broadcasted_iota(jnp.int32, (2 * L, 2 * L), 0) % L
    c2 = lax.broadcasted_iota(jnp.int32, (2 * L, 2 * L), 1) % L
    strict = (c2 > r2) if reverse else (c2 < r2)
    incl = (c2 >= r2) if reverse else (c2 <= r2)
    rr = lax.broadcasted_iota(jnp.int32, (2 * L, 2 * L), 0)
    cc = lax.broadcasted_iota(jnp.int32, (2 * L, 2 * L), 1)
    eye = (rr == cc).astype(F32)
    off_masks = []
    sz = 2
    while sz <= L:
        late, early = (cc, rr) if reverse else (rr, cc)
        off_masks.append((rr // sz == cc // sz) & (late % sz >= sz // 2) & (early % sz < sz // 2))
        sz *= 2
    lane = lax.broadcasted_iota(jnp.int32, (L, LANES), 1)
    head0 = lane < HEAD_DIM

    def stack2(x):
        return jnp.concatenate([jnp.where(head0, x, 0.0), jnp.where(head0, 0.0, x)], axis=0)

    n_sub = SCAN_ROWS // L
    n_pair = r_ref.shape[1] // LANES
    order = [(n_sub - 1 - j) if reverse else j for j in range(n_sub)]

    keys = [(c, p) for c in order for p in range(n_pair)]
    a2, b2, k2, q2, v2, bt2, kt2, decay = ({} for _ in range(8))
    for c in order:
        rows = pl.ds(c * L, L)
        lw_all = lw_ref[rows, :]
        cum_all = jnp.dot(tri, lw_all, preferred_element_type=F32, precision=lax.Precision.HIGHEST)
        for p in range(n_pair):
            cols = slice(p * LANES, (p + 1) * LANES)
            lw = lw_all[:, cols]
            cum = cum_all[:, cols]
            tot = cum[0:1, :] if reverse else cum[L - 1:L, :]
            e_pos = jnp.exp(cum)
            e_neg = jnp.exp(-cum)
            e_rem = jnp.exp(tot - cum)
            kk = kk_ref[rows, cols]
            kd = k_ref[rows, cols]
            bb = bb_ref[rows, cols]
            key = (c, p)
            a2[key] = stack2(-kk * jnp.exp(cum - lw))
            b2[key] = stack2(bb * e_neg)
            k2[key] = stack2(kd * e_neg)
            q2[key] = stack2(r_ref[rows, cols] * e_pos)
            v2[key] = stack2(v_ref[rows, cols])
            bt2[key] = stack2(bb * e_rem)
            kt2[key] = stack2(kd * e_rem)
            decay[key] = jnp.exp(tot)

    hh = 2 * L
    sc = {k: _bdot_nt(jnp.concatenate([a2[k], q2[k]], axis=0), jnp.concatenate([b2[k], k2[k]], axis=0))
          for k in keys}
    mm = {k: jnp.where(strict, sc[k][:hh, :hh], 0.0) for k in keys}
    nn = {k: jnp.where(strict, sc[k][:hh, hh:], 0.0) for k in keys}
    qq = {k: jnp.where(incl, sc[k][hh:, :hh], 0.0) for k in keys}
    zz = {k: jnp.where(incl, sc[k][hh:, hh:], 0.0) for k in keys}
    tinv = {k: eye + jnp.where(off_masks[0], mm[k], 0.0) for k in keys}
    for off in off_masks[1:]:
        half_step = {k: _bdot(tinv[k], jnp.where(off, mm[k], 0.0)) for k in keys}
        tinv = {k: tinv[k] + _bdot(half_step[k], tinv[k]) for k in keys}
    nzv = {k: _bdot(jnp.concatenate([nn[k], zz[k]], axis=0), v2[k]) for k in keys}
    px = {k: _bdot(tinv[k], jnp.concatenate([a2[k], nzv[k][:hh]], axis=1)) for k in keys}
    qpx = {k: _bdot(qq[k], px[k]) for k in keys}
    g_mat = {k: q2[k] + qpx[k][:, :LANES] for k in keys}
    y_loc = {k: qpx[k][:, LANES:] + nzv[k][hh:] for k in keys}
    pb = {k: _bdot_tn(px[k], bt2[k]) for k in keys}
    phi = {k: pb[k][:LANES] for k in keys}
    psi = {k: pb[k][LANES:] + _bdot_tn(v2[k], kt2[k]) for k in keys}
    items = {k: (g_mat[k], y_loc[k], decay[k], phi[k], psi[k]) for k in keys}

    st = [s_ref[p] for p in range(n_pair)]
    for c in order:
        for p in range(n_pair):
            g, yl, dec, ph, ps = items[(c, p)]
            y2 = _bdot_nt(g, st[p]) + yl
            y_ref[pl.ds(c * L, L), p * LANES:(p + 1) * LANES] = y2[:L, :] + y2[L:, :]
            st[p] = st[p] * dec + _bdot(st[p], ph) + ps
    for p in range(n_pair):
        s_ref[p] = st[p]


def rwkv_scan_call(r, lw, kd, v, kk, bb, *, reverse, batch, lat_len, ctx_len):
    r_rows, aw = r.shape
    d = 1 if reverse else 0
    lat_blocks = lat_len // SCAN_ROWS
    width = SCAN_PAIRS * LANES
    blk = functools.partial(_scan_block, reverse=reverse, n_lat_blocks_total=batch * lat_blocks,
                            lat_blocks=lat_blocks)
    row_spec = pl.BlockSpec((SCAN_ROWS, width), lambda b, h, s: (blk(b, s), h))
    dir_spec = pl.BlockSpec((None, SCAN_ROWS, width), lambda b, h, s: (d, blk(b, s), h))
    return pl.pallas_call(
        functools.partial(_rwkv_scan_kernel, reverse=reverse),
        out_shape=jax.ShapeDtypeStruct((r_rows, aw), F32),
        grid=(batch, aw // width, 1 + lat_blocks),
        in_specs=[row_spec, dir_spec, dir_spec, row_spec, row_spec, dir_spec],
        out_specs=row_spec,
        scratch_shapes=[pltpu.VMEM((SCAN_PAIRS, LANES, LANES), F32)],
        compiler_params=_cparams(("parallel", "parallel", "arbitrary")),
        name="rwkv_scan_bwd" if reverse else "rwkv_scan_fwd",
    )(r, lw, kd, v, kk, bb)


def _lru_kernel(xc_ref, xp_ref, xn_ref, cw, cb, wcat, bcat, lsl, h_ref, a_ref, u_ref, carry_ref,
                *, reverse, lat_blocks):
    s = pl.program_id(1)

    @pl.when(s == 0)
    def _():
        carry_ref[...] = jnp.zeros_like(carry_ref)

    tm = SCAN_ROWS
    lat_idx = (lat_blocks - s) if reverse else (s - 1)
    first = (s == 0) | (lat_idx == 0)
    last = (s == 0) | (lat_idx == lat_blocks - 1)
    x = xc_ref[...]
    prev_row = jnp.where(first, 0.0, xp_ref[SUBLANES - 1:SUBLANES, :])
    next1 = jnp.where(last, 0.0, xn_ref[0:1, :])
    next2 = jnp.where(last, 0.0, xn_ref[1:2, :])
    row_id = lax.broadcasted_iota(jnp.int32, x.shape, 0)
    xm1 = jnp.where(row_id == 0, prev_row, pltpu.roll(x, 1, axis=0))
    xp1 = jnp.where(row_id == tm - 1, next1, pltpu.roll(x, tm - 1, axis=0))
    xp2 = jnp.where(row_id == tm - 1, next2, jnp.where(row_id == tm - 2, next1, pltpu.roll(x, tm - 2, axis=0)))
    xc = cb[...] + xm1 * cw[0:1, :] + x * cw[1:2, :] + xp1 * cw[2:3, :] + xp2 * cw[3:4, :]
    bwid = LRU_BLOCK_W
    for n in range(x.shape[1] // bwid):
        cols = slice(n * bwid, (n + 1) * bwid)
        xn = xc[:, cols]
        gates = _bdot(xn, wcat[n]) + bcat[:, 2 * n * bwid:2 * (n + 1) * bwid]
        log_a = jax.nn.sigmoid(gates[:, :bwid]) * lsl[:, cols]
        a = jnp.exp(log_a)
        om = -jnp.tanh(log_a) * (a * a + 1.0)
        a_ref[:, cols] = a
        u_ref[:, cols] = xn * jax.nn.sigmoid(gates[:, bwid:]) * jnp.sqrt(om)

    w = a_ref.shape[1]
    rows = lax.broadcasted_iota(jnp.int32, (SUBLANES, w), 0)
    n_groups = SCAN_ROWS // SUBLANES

    def body(gi, carry):
        g = (n_groups - 1 - gi) if reverse else gi
        sl = pl.ds(pl.multiple_of(g * SUBLANES, SUBLANES), SUBLANES)
        a = a_ref[sl, :]
        u = u_ref[sl, :]
        for sh in (1, 2, 4):
            if reverse:
                ok = rows < SUBLANES - sh
                a_s = pltpu.roll(a, SUBLANES - sh, axis=0)
                u_s = pltpu.roll(u, SUBLANES - sh, axis=0)
            else:
                ok = rows >= sh
                a_s = pltpu.roll(a, sh, axis=0)
                u_s = pltpu.roll(u, sh, axis=0)
            u = jnp.where(ok, a * u_s + u, u)
            a = jnp.where(ok, a * a_s, a)
        h = a * carry + u
        h_ref[sl, :] = h
        return h[0:1, :] if reverse else h[SUBLANES - 1:SUBLANES, :]

    carry_ref[...] = lax.fori_loop(0, n_groups, body, carry_ref[...])


def lru_call(p, prm, *, reverse, batch, lat_len, ctx_len, col_x):
    r_rows = p.shape[0]
    bw = prm["conv_b"].shape[1]
    nb = bw // LRU_BLOCK_W
    d = 1 if reverse else 0
    lat_blocks = lat_len // SCAN_ROWS
    per = SCAN_ROWS // SUBLANES
    last8 = r_rows // SUBLANES - 1
    blk = functools.partial(_scan_block, reverse=reverse, n_lat_blocks_total=batch * lat_blocks,
                            lat_blocks=lat_blocks)
    full = lambda shape: pl.BlockSpec(shape, lambda b, s: (0,) * len(shape))
    return pl.pallas_call(
        functools.partial(_lru_kernel, reverse=reverse, lat_blocks=lat_blocks),
        out_shape=jax.ShapeDtypeStruct((r_rows, bw), F32),
        grid=(batch, 1 + lat_blocks),
        in_specs=[
            pl.BlockSpec((SCAN_ROWS, bw), lambda b, s: (blk(b, s), col_x)),
            pl.BlockSpec((SUBLANES, bw), lambda b, s: (jnp.maximum(blk(b, s) * per - 1, 0), col_x)),
            pl.BlockSpec((SUBLANES, bw), lambda b, s: (jnp.minimum((blk(b, s) + 1) * per, last8), col_x)),
            full((CONV_W, bw)), full((1, bw)),
            pl.BlockSpec((None, nb, LRU_BLOCK_W, 2 * LRU_BLOCK_W), lambda b, s: (d, 0, 0, 0)),
            pl.BlockSpec((None, 1, 2 * bw), lambda b, s: (d, 0, 0)),
            pl.BlockSpec((None, 1, bw), lambda b, s: (d, 0, 0)),
        ],
        out_specs=pl.BlockSpec((SCAN_ROWS, bw), lambda b, s: (blk(b, s), 0)),
        scratch_shapes=[pltpu.VMEM((SCAN_ROWS, bw), F32), pltpu.VMEM((SCAN_ROWS, bw), F32),
                        pltpu.VMEM((1, bw), F32)],
        compiler_params=_cparams(("parallel", "arbitrary")),
        name="lru_bwd" if reverse else "lru_fwd",
    )(p, p, p, prm["conv_w"], prm["conv_b"], prm["wcat"], prm["bcat"], prm["lsl"])


def _even_post_kernel(yf, yb, r_ref, v_ref, kdf, kdb, g_ref, hf, hb, gb_ref, rk, lnw, lnb, o_ref):
    ones = _head_ones()
    y = yf[...] + yb[...]
    inv_n = 1.0 / HEAD_DIM
    yc = y - _head_sum(y, ones) * inv_n
    var = _head_sum(yc * yc, ones) * inv_n
    hn = yc * lax.rsqrt(var + LNX_EPS) * lnw[...] + lnb[...]
    r = r_ref[...]
    v = v_ref[...]
    bonus = _head_sum(r * kdf[...] * rk[...], ones) * v + _head_sum(r * kdb[...] * rk[...], ones) * v
    ya = (hn + bonus) * g_ref[...]
    yl = (hf[...] + hb[...]) * jax.nn.gelu(gb_ref[...])
    aw = ya.shape[1]
    o_ref[:, :aw] = ya.astype(BF16)
    o_ref[:, aw:] = yl.astype(BF16)


def even_post_call(yf, yb, r, v, kd, g, hf, hb, p, prm, *, col_gb):
    r_rows, aw = yf.shape
    bw = hf.shape[1]
    tm = PREP_ROWS
    row = lambda w: pl.BlockSpec((tm, w), lambda i: (i, 0))
    full = lambda shape: pl.BlockSpec(shape, lambda i: (0,) * len(shape))
    return pl.pallas_call(
        _even_post_kernel,
        out_shape=jax.ShapeDtypeStruct((r_rows, aw + bw), BF16),
        grid=(r_rows // tm,),
        in_specs=[row(aw), row(aw), row(aw), row(aw),
                  pl.BlockSpec((None, tm, aw), lambda i: (0, i, 0)),
                  pl.BlockSpec((None, tm, aw), lambda i: (1, i, 0)),
                  row(aw), row(bw), row(bw),
                  pl.BlockSpec((tm, bw), lambda i: (i, col_gb)),
                  full((1, aw)), full((1, aw)), full((1, aw))],
        out_specs=row(aw + bw),
        compiler_params=_cparams(("parallel",)),
        name="even_post",
    )(yf, yb, r, v, kd, kd, g, hf, hb, p, prm["r_k"], prm["lnx_w"], prm["lnx_b"])


def _mlstm_kernel(q_ref, k_ref, v_ref, br_ref, bc_ref, ir_ref, ic_ref, h_ref, c_ref, n_ref, m_ref, *, reverse):
    @pl.when(pl.program_id(2) == 0)
    def _():
        c_ref[...] = jnp.zeros_like(c_ref)
        n_ref[...] = jnp.zeros_like(n_ref)
        m_ref[...] = jnp.zeros_like(m_ref)

    L = MLSTM_CHUNK
    ri = lax.broadcasted_iota(jnp.int32, (L, L), 0)
    ci = lax.broadcasted_iota(jnp.int32, (L, L), 1)
    causal = (ci >= ri) if reverse else (ci <= ri)
    n_sub = SCAN_ROWS // L
    order = [(n_sub - 1 - j) if reverse else j for j in range(n_sub)]
    scale = C_QK_HEAD ** -0.5

    m_cur = m_ref[...]
    m_in, ew, ec = {}, {}, {}
    for c in order:
        b_r = br_ref[c:c + 1, :]
        b_tot = b_r[:, 0:1] if reverse else b_r[:, L - 1:L]
        w_in = b_tot - bc_ref[:, c:c + 1] + ic_ref[:, c:c + 1]
        carry_log = b_tot + m_cur
        m_new = jnp.maximum(carry_log, jnp.max(w_in, axis=0, keepdims=True))
        m_in[c] = m_cur
        ew[c] = jnp.exp(w_in - m_new)
        ec[c] = jnp.exp(carry_log - m_new)
        m_cur = m_new
    m_ref[...] = m_cur

    q = {c: (q_ref[pl.ds(c * L, L), :] * scale).astype(BF16) for c in order}
    k = {c: k_ref[pl.ds(c * L, L), :] for c in order}
    v = {c: v_ref[pl.ds(c * L, L), :].astype(BF16) for c in order}
    qk = {c: _bdot_nt(q[c], k[c]) for c in order}
    m_t, e_inter, scores = {}, {}, {}
    for c in order:
        b_c = bc_ref[:, c:c + 1]
        dmat = jnp.where(causal, b_c - br_ref[c:c + 1, :] + ir_ref[c:c + 1, :], -jnp.inf)
        inter = b_c + m_in[c]
        m_t[c] = jnp.maximum(inter, jnp.max(dmat, axis=1, keepdims=True))
        scores[c] = qk[c] * jnp.exp(dmat - m_t[c])
        e_inter[c] = jnp.exp(inter - m_t[c])
    intra = {c: _bdot(scores[c], v[c]) for c in order}
    kw = {c: k[c] * ew[c] for c in order}
    kv = {c: _bdot_tn(kw[c], v[c]) for c in order}

    c_cur = c_ref[...]
    n_cur = n_ref[...]
    c_in, n_in = {}, {}
    for c in order:
        c_in[c] = c_cur
        n_in[c] = n_cur
        c_cur = ec[c] * c_cur + kv[c]
        n_cur = ec[c] * n_cur + jnp.sum(kw[c], axis=0, keepdims=True)
    c_ref[...] = c_cur
    n_ref[...] = n_cur

    qc = {c: _bdot(q[c], c_in[c]) for c in order}
    for c in order:
        qf = q[c].astype(F32)
        num = e_inter[c] * qc[c] + intra[c]
        den = (e_inter[c] * jnp.sum(qf * n_in[c], axis=1, keepdims=True)
               + jnp.sum(scores[c], axis=1, keepdims=True))
        h_ref[pl.ds(c * L, L), :] = num / jnp.maximum(jnp.abs(den), jnp.exp(-m_t[c]))


def mlstm_call(p, g_rows, g_cols, *, reverse, batch, lat_len, ctx_len, col_q, col_k, col_v):
    r_rows = p.shape[0]
    d = 1 if reverse else 0
    n_sub = SCAN_ROWS // MLSTM_CHUNK
    lat_blocks = lat_len // SCAN_ROWS
    blk = functools.partial(_scan_block, reverse=reverse, n_lat_blocks_total=batch * lat_blocks,
                            lat_blocks=lat_blocks)
    grow = lambda which: pl.BlockSpec((None, None, None, None, n_sub, MLSTM_CHUNK),
                                      lambda b, h, s: (d, which, h, blk(b, s), 0, 0))
    gcol = lambda which: pl.BlockSpec((None, None, None, None, MLSTM_CHUNK, n_sub),
                                      lambda b, h, s: (d, which, h, blk(b, s), 0, 0))
    return pl.pallas_call(
        functools.partial(_mlstm_kernel, reverse=reverse),
        out_shape=jax.ShapeDtypeStruct((r_rows, C_HEADS * C_V_HEAD), F32),
        grid=(batch, C_HEADS, 1 + lat_blocks),
        in_specs=[
            pl.BlockSpec((SCAN_ROWS, C_QK_HEAD), lambda b, h, s: (blk(b, s), col_q + h)),
            pl.BlockSpec((SCAN_ROWS, C_QK_HEAD), lambda b, h, s: (blk(b, s), col_k + h)),
            pl.BlockSpec((SCAN_ROWS, C_V_HEAD), lambda b, h, s: (blk(b, s), col_v + h)),
            grow(0), gcol(0), grow(1), gcol(1),
        ],
        out_specs=pl.BlockSpec((SCAN_ROWS, C_V_HEAD), lambda b, h, s: (blk(b, s), h)),
        scratch_shapes=[pltpu.VMEM((C_QK_HEAD, C_V_HEAD), F32), pltpu.VMEM((1, C_QK_HEAD), F32),
                        pltpu.VMEM((1, 1), F32)],
        compiler_params=_cparams(("parallel", "parallel", "arbitrary")),
        name="mlstm_bwd" if reverse else "mlstm_fwd",
    )(p, p, p, g_rows, g_cols, g_rows, g_cols)


def _odd_post_kernel(hf, hb, o_lo, o_hi, nw, nb, y_ref):
    half_heads = C_HEADS // 2
    for hd in range(C_HEADS):
        cols = slice(hd * C_V_HEAD, (hd + 1) * C_V_HEAD)
        x = hf[:, cols] + hb[:, cols]
        xc = x - jnp.mean(x, axis=1, keepdims=True)
        var = jnp.mean(xc * xc, axis=1, keepdims=True)
        hn = xc * lax.rsqrt(var + MLSTM_NORM_EPS) * nw[:, cols] + nb[:, cols]
        o_ref = o_lo if hd < half_heads else o_hi
        oc = slice((hd % half_heads) * C_V_HEAD, (hd % half_heads + 1) * C_V_HEAD)
        y_ref[:, cols] = (hn * jax.nn.sigmoid(o_ref[:, oc])).astype(BF16)


def odd_post_call(hf, hb, p, nw, nb, *, n_rows, col_o):
    vw = hf.shape[1]
    tm = LN_ROWS
    row = pl.BlockSpec((tm, vw), lambda i: (i, 0))
    par = pl.BlockSpec((1, vw), lambda i: (0, 0))
    return pl.pallas_call(
        _odd_post_kernel,
        out_shape=jax.ShapeDtypeStruct((n_rows, vw), BF16),
        grid=(n_rows // tm,),
        in_specs=[row, row, pl.BlockSpec((tm, vw // 2), lambda i: (i, col_o)),
                  pl.BlockSpec((tm, vw // 2), lambda i: (i, col_o + 1)), par, par],
        out_specs=row,
        compiler_params=_cparams(("parallel",)),
        name="odd_post",
    )(hf, hb, p, p, nw, nb)


def _layer_norm(z, g, b):
    zc = z - jnp.mean(z, axis=1, keepdims=True)
    var = jnp.mean(zc * zc, axis=1, keepdims=True)
    return zc * lax.rsqrt(var + LN_EPS) * g + b


def _route(logits_t, bias):
    aff = [jax.nn.sigmoid(logits_t[e:e + 1, :]) for e in range(N_EXPERTS)]
    biased = [aff[e] + bias[e:e + 1, :] for e in range(N_EXPERTS)]
    best_g = best_v = None
    for g in range(N_GROUPS):
        m = biased[g * EXPERTS_PER_GROUP:(g + 1) * EXPERTS_PER_GROUP]
        pair = None
        for i in range(EXPERTS_PER_GROUP):
            for j in range(i + 1, EXPERTS_PER_GROUP):
                hi = jnp.maximum(m[i], m[j])
                lo_ = jnp.minimum(m[i], m[j])
                s = hi + lo_
                pair = s if pair is None else jnp.maximum(pair, s)
        if best_v is None:
            best_v, best_g = pair, jnp.zeros(pair.shape, jnp.int32)
        else:
            upd = pair > best_v
            best_g = jnp.where(upd, g, best_g)
            best_v = jnp.where(upd, pair, best_v)
    ids, sels = [], []
    taken = None
    for _ in range(TOP_K):
        cur_v = cur_i = cur_a = None
        for e in range(N_EXPERTS):
            ok = best_g == (e // EXPERTS_PER_GROUP)
            if taken is not None:
                ok = jnp.logical_and(ok, taken != e)
            val = jnp.where(ok, biased[e], -jnp.inf)
            if cur_v is None:
                cur_v, cur_i, cur_a = val, jnp.zeros(val.shape, jnp.int32), aff[e]
            else:
                upd = val > cur_v
                cur_i = jnp.where(upd, e, cur_i)
                cur_a = jnp.where(upd, aff[e], cur_a)
                cur_v = jnp.where(upd, val, cur_v)
        ids.append(cur_i)
        sels.append(cur_a)
        taken = cur_i
    tot = sels[0] + sels[1]
    return ids, [sels[0] / tot, sels[1] / tot]


def _ln_route_kernel(lat_ref, y_ref, gate_ref, sh_ref, sc_ref, g_ref, b_ref, rw_ref, rb_ref,
                     lat_o, h_o, id_o, gt_o):
    z = ALPHA * lat_ref[...] + gate_ref[...] * y_ref[...]
    ln = _layer_norm(z, g_ref[...], b_ref[...])
    lat_o[...] = ln
    h = ln * (1.0 + sc_ref[...]) + sh_ref[...]
    half = h.shape[1] // 2
    bits = lax.bitcast_convert_type(h.astype(BF16).astype(F32), jnp.uint32)
    h_o[...] = (bits[:, :half] >> 16) | bits[:, half:]
    logits_t = lax.dot_general(rw_ref[...], h, (((1,), (1,)), ((), ())), preferred_element_type=F32,
                               precision=lax.Precision.HIGHEST)
    ids, gts = _route(logits_t, rb_ref[...])
    id_o[...] = jnp.concatenate(ids, axis=0)
    gt_o[...] = jnp.concatenate(gts, axis=0)


def ln_route_call(lat, y, mods, norm_g, norm_b, layer, router_wt, router_b, *, n_rows, rows_per_mod, n_mod_rows):
    d = lat.shape[1]
    tm = LN_ROWS

    def midx(i):
        return jnp.minimum(i * tm // rows_per_mod, n_mod_rows - 1)

    row = pl.BlockSpec((tm, d), lambda i: (i, 0))
    packed = pl.BlockSpec((tm, d // 2), lambda i: (i, 0))
    mod = lambda k: pl.BlockSpec((None, 1, d), lambda i: (midx(i), 0, k))
    nrm = pl.BlockSpec((None, None, 1, d), lambda i: (layer, 0, 0, 0))
    full = lambda shape: pl.BlockSpec(shape, lambda i: (0,) * len(shape))
    sel = pl.BlockSpec((TOP_K, tm), lambda i: (0, i))
    return pl.pallas_call(
        _ln_route_kernel,
        out_shape=(jax.ShapeDtypeStruct((n_rows, d), F32), jax.ShapeDtypeStruct((n_rows, d // 2), jnp.uint32),
                   jax.ShapeDtypeStruct((TOP_K, n_rows), jnp.int32), jax.ShapeDtypeStruct((TOP_K, n_rows), F32)),
        grid=(n_rows // tm,),
        in_specs=[row, row, mod(2), mod(3), mod(4), nrm, nrm, full((N_EXPERTS, d)), full((N_EXPERTS, 1))],
        out_specs=(row, packed, sel, sel),
        compiler_params=_cparams(("parallel",)),
        name="ln_route",
    )(lat, y, mods, mods, mods, norm_g, norm_b, router_wt, router_b)


def _ln_combine_kernel(lat_ref, y0_ref, y1_ref, gt_ref, gate_ref, g_ref, b_ref, lat_o):
    gt = gt_ref[...]
    f = y0_ref[...] * gt[:, 0:1] + y1_ref[...] * gt[:, 1:2]
    z = ALPHA * lat_ref[...] + gate_ref[...] * f
    lat_o[...] = _layer_norm(z, g_ref[...], b_ref[...])


def ln_combine_call(lat, y_slots, gates_t, mods, norm_g, norm_b, layer, *, n_rows, rows_per_mod, n_mod_rows):
    d = lat.shape[1]
    tm = LN_ROWS
    per_k = n_rows // tm

    def midx(i):
        return jnp.minimum(i * tm // rows_per_mod, n_mod_rows - 1)

    row = pl.BlockSpec((tm, d), lambda i: (i, 0))
    nrm = pl.BlockSpec((None, None, 1, d), lambda i: (layer, 1, 0, 0))
    return pl.pallas_call(
        _ln_combine_kernel,
        out_shape=jax.ShapeDtypeStruct((n_rows, d), F32),
        grid=(per_k,),
        in_specs=[row, row, pl.BlockSpec((tm, d), lambda i: (per_k + i, 0)),
                  pl.BlockSpec((tm, TOP_K), lambda i: (i, 0)),
                  pl.BlockSpec((None, 1, d), lambda i: (midx(i), 0, 5)), nrm, nrm],
        out_specs=row,
        compiler_params=_cparams(("parallel",)),
        name="ln_combine",
    )(lat, y_slots, y_slots, gates_t, mods, norm_g, norm_b)


def _moe_kernel(nu_ref, be_ref, src_cur, src_nxt, dst_prv, h_hbm, wg_lo, wg_hi, wu_lo, wu_hi, wd_ref, y_hbm,
                xbuf, acc, g_acc, u_acc, hdn, gsem, ssem, *, n_k, n_c, n_slots):
    del be_ref
    i = pl.program_id(0)
    s = pl.program_id(1)
    buf = i % 2
    other = 1 - buf
    n_used = nu_ref[0]
    tm, half = xbuf.shape[1], xbuf.shape[2]
    d = acc.shape[2]
    kw = half // n_k
    cw = d // n_c
    per_step = tm // n_c

    def row_in(tok, r, b):
        return pltpu.make_async_copy(h_hbm.at[pl.ds(tok, 1)], xbuf.at[b, pl.ds(r, 1)], gsem.at[b])

    def row_out(r, slot, b):
        return pltpu.make_async_copy(acc.at[b, pl.ds(r, 1)], y_hbm.at[pl.ds(slot, 1)], ssem.at[b])

    def send_previous(r):
        slot = jnp.where(i > 0, dst_prv[0, r], n_slots + r)
        row_out(r, slot, other).start()

    @pl.when((i == 0) & (s == 0))
    def _():
        acc[1] = jnp.zeros((tm, d), F32)

        def body(r, carry):
            row_in(src_cur[0, r], r, 0).start()
            return carry
        lax.fori_loop(0, tm, body, 0, unroll=8)

    @pl.when((s == 0) & (i <= n_used))
    def _():
        pltpu.make_async_copy(xbuf.at[buf], xbuf.at[buf], gsem.at[buf]).wait()

    for t in range(n_k):
        @pl.when((s == t) & (i < n_used))
        def _(t=t):
            xp = xbuf[buf, :, t * kw:(t + 1) * kw]
            lo = lax.bitcast_convert_type(xp << 16, F32).astype(BF16)
            hi = lax.bitcast_convert_type(xp & jnp.uint32(0xFFFF0000), F32).astype(BF16)
            g = _bdot(lo, wg_lo[...]) + _bdot(hi, wg_hi[...])
            u = _bdot(lo, wu_lo[...]) + _bdot(hi, wu_hi[...])
            if t == 0:
                g_acc[...] = g
                u_acc[...] = u
            else:
                g_acc[...] += g
                u_acc[...] += u

    for c in range(n_c):
        @pl.when((s == n_k + c) & (i < n_used))
        def _(c=c):
            if c == 0:
                g = g_acc[...]
                hdn[...] = (g * jax.nn.sigmoid(g) * u_acc[...]).astype(BF16)
            acc[buf, :, c * cw:(c + 1) * cw] = _bdot(hdn[...], wd_ref[...])
            for rr in range(per_step):
                r = c * per_step + rr
                row_in(src_nxt[0, r], r, other).start()
                send_previous(r)

        @pl.when((s == n_k + c) & (i == n_used))
        def _(c=c):
            for rr in range(per_step):
                send_previous(c * per_step + rr)

    @pl.when((s == n_k + n_c - 1) & (i <= n_used))
    def _():
        pltpu.make_async_copy(acc.at[other], acc.at[other], ssem.at[other]).wait()


def moe_call(h_packed, n_used, block_e, src_tok, dst_slot, w_gate, w_up, w_down, layer, *, n_slots):
    d = 2 * h_packed.shape[1]
    f = w_gate.shape[3]
    tm = MOE_ROWS
    n_blocks = block_e.shape[0]
    n_k, n_c = MOE_KSTEPS, MOE_CSTEPS
    kw = d // 2 // n_k
    cw = d // n_c
    assert tm % n_c == 0 and (d // 2) % n_k == 0 and d % n_c == 0
    idx = lambda fn: pl.BlockSpec((None, 1, tm), fn, memory_space=pltpu.SMEM)

    def k_tile(i, s, nu):
        return jnp.where(i < nu[0], jnp.minimum(s, n_k - 1), n_k - 1)

    def c_tile(i, s, nu):
        return jnp.where(i < nu[0], jnp.clip(s - n_k, 0, n_c - 1), n_c - 1)

    w_rows = lambda high: pl.BlockSpec(
        (None, None, kw, f), lambda i, s, nu, be: (layer, be[i], k_tile(i, s, nu) + (n_k if high else 0), 0))
    grid_spec = pltpu.PrefetchScalarGridSpec(
        num_scalar_prefetch=2,
        grid=(n_blocks, n_k + n_c),
        in_specs=[
            idx(lambda i, s, nu, be: (i, 0, 0)),
            idx(lambda i, s, nu, be: (jnp.minimum(i + 1, n_blocks - 1), 0, 0)),
            idx(lambda i, s, nu, be: (jnp.maximum(i - 1, 0), 0, 0)),
            pl.BlockSpec(memory_space=pl.ANY),
            w_rows(False), w_rows(True), w_rows(False), w_rows(True),
            pl.BlockSpec((None, None, f, cw), lambda i, s, nu, be: (layer, be[i], 0, c_tile(i, s, nu))),
        ],
        out_specs=pl.BlockSpec(memory_space=pl.ANY),
        scratch_shapes=[pltpu.VMEM((2, tm, d // 2), jnp.uint32), pltpu.VMEM((2, tm, d), F32),
                        pltpu.VMEM((tm, f), F32), pltpu.VMEM((tm, f), F32), pltpu.VMEM((tm, f), BF16),
                        pltpu.SemaphoreType.DMA((2,)), pltpu.SemaphoreType.DMA((2,))],
    )
    return pl.pallas_call(
        functools.partial(_moe_kernel, n_k=n_k, n_c=n_c, n_slots=n_slots),
        out_shape=jax.ShapeDtypeStruct((n_slots + tm, d), F32),
        grid_spec=grid_spec,
        compiler_params=_cparams(("arbitrary", "arbitrary")),
        name="moe_experts",
    )(n_used, block_e, src_tok, src_tok, dst_slot, h_packed, w_gate, w_gate, w_up, w_up, w_down)


def moe_ffn(h_packed, ids, w_gate, w_up, w_down, layer):
    n_tok = h_packed.shape[0]
    n_slots = n_tok * TOP_K
    tm = MOE_ROWS
    i32 = jnp.int32
    flat_e = ids.reshape(n_slots)
    order = jnp.argsort(flat_e, stable=True).astype(i32)
    counts = jnp.sum((flat_e[:, None] == jnp.arange(N_EXPERTS, dtype=i32)[None, :]).astype(i32), axis=0)
    start = jnp.cumsum(counts) - counts
    padded = (counts + tm - 1) // tm * tm
    pad_end = jnp.cumsum(padded)
    pad_start = pad_end - padded
    n_blocks = -(-n_slots // tm) + N_EXPERTS + 1
    pos = jnp.arange(n_blocks * tm, dtype=i32)
    e_pos = jnp.minimum(jnp.sum((pos[:, None] >= pad_end[None, :]).astype(i32), axis=1), N_EXPERTS - 1)
    r_pos = pos - pad_start[e_pos]
    valid = r_pos < counts[e_pos]
    slot = order[jnp.clip(start[e_pos] + r_pos, 0, n_slots - 1)]
    src_tok = jnp.where(valid, slot % n_tok, 0).reshape(n_blocks, 1, tm)
    dst_slot = jnp.where(valid, slot, n_slots + pos % tm).reshape(n_blocks, 1, tm)
    n_used = (pad_end[-1] // tm).astype(i32)
    blk = jnp.arange(n_blocks, dtype=i32)
    block_e = e_pos.reshape(n_blocks, tm)[:, 0]
    block_e = jnp.where(blk < n_used, block_e, block_e[jnp.maximum(n_used - 1, 0)]).astype(i32)
    return moe_call(h_packed, n_used.reshape(1), block_e, src_tok.astype(i32), dst_slot.astype(i32),
                    w_gate, w_up, w_down, layer, n_slots=n_slots)


def _sincos_2d(n_tokens, dim):
    rows = n_tokens // GRID_W
    quarter = dim // 4
    omega = POS_BASE ** (-jnp.arange(quarter, dtype=F32) / quarter)
    ang_r = jnp.arange(rows, dtype=F32)[:, None] * omega[None, :]
    ang_c = jnp.arange(GRID_W, dtype=F32)[:, None] * omega[None, :]
    by_row = lambda t: jnp.repeat(t, GRID_W, axis=0)
    by_col = lambda t: jnp.tile(t, (rows, 1))
    return jnp.concatenate([by_row(jnp.sin(ang_r)), by_row(jnp.cos(ang_r)),
                            by_col(jnp.sin(ang_c)), by_col(jnp.cos(ang_c))], -1)


def _pad_cols(w, width):
    return jnp.pad(w, [(0, 0)] * (w.ndim - 1) + [(0, width - w.shape[-1])])


def _even_weights(ev_w_in, rwkv_mu, aw, bw):
    rk = 3 * aw
    lo0 = rk
    pieces = [DECAY_LORA, DECAY_LORA, ICL_LORA, ICL_LORA]

    def regroup(t):
        out = [t[..., rk + 2 * (DECAY_LORA + ICL_LORA) + GATE_LORA:],
               t[..., :rk]]
        off = lo0
        for wdt in pieces:
            out.append(_pad_cols(t[..., off:off + wdt], LANES))
            off += wdt
        out.append(t[..., off:off + GATE_LORA])
        used = 4 * LANES + GATE_LORA
        out.append(jnp.zeros(t.shape[:-1] + (LORA_PAD - used,), t.dtype))
        return jnp.concatenate(out, axis=-1)

    w = regroup(ev_w_in).astype(BF16)
    rwkv_cols = rk + 2 * (DECAY_LORA + ICL_LORA) + GATE_LORA
    mu_full = jnp.concatenate([rwkv_mu, jnp.zeros(rwkv_mu.shape[:-1] + (2 * bw,), rwkv_mu.dtype)], axis=-1)
    mu = regroup(mu_full)
    del rwkv_cols
    return w, mu


def kernel(x, c, ctx, c_ctx, ada_w, ada_b, norm_g, norm_b,
           ev_w_in, ev_w_out, rwkv_mu, rwkv_w0, rwkv_w2, rwkv_a0, rwkv_a2, rwkv_g2,
           rwkv_k_k, rwkv_k_a, rwkv_r_k, rwkv_lnx_w, rwkv_lnx_b,
           lru_conv_w, lru_conv_b, lru_wa, lru_ba, lru_wx, lru_bx, lru_lam,
           od_w_in, od_w_out, mlstm_ig_b, mlstm_fg_b, mlstm_norm_w, mlstm_norm_b,
           router_w, router_b, moe_w_gate, moe_w_up, moe_w_down):
    bsz, s_len, dim = x.shape
    n_ctx = ctx.shape[1]
    n_lat_rows = bsz * s_len
    n_all_rows = n_lat_rows + bsz * n_ctx
    aw = rwkv_k_k.shape[1]
    bw = lru_conv_b.shape[1]
    assert s_len % SCAN_ROWS == 0 and n_ctx == SCAN_ROWS and n_all_rows % MM_ROWS == 0
    assert (bsz * n_ctx) % MM_ROWS == 0 and s_len % MM_ROWS == 0 and bsz + 1 <= SUBLANES
    seq = dict(n_lat_rows=n_lat_rows, lat_len=s_len, ctx_len=n_ctx)
    scan = dict(batch=bsz, lat_len=s_len, ctx_len=n_ctx)
    modk = dict(rows_per_mod=s_len, n_mod_rows=bsz + 1)

    lat = (x + _sincos_2d(s_len, dim).astype(x.dtype)).reshape(n_lat_rows, dim)
    stream = jnp.concatenate([lat, ctx.reshape(bsz * n_ctx, dim)], axis=0)
    cvec = jnp.concatenate([c, c_ctx[None, :], jnp.zeros((SUBLANES - bsz - 1, dim), c.dtype)], axis=0)
    ada_b3 = ada_b[:, None, :]
    norm_g4 = norm_g[:, :, None, :]
    norm_b4 = norm_b[:, :, None, :]
    router_wt = router_w.T
    router_b2 = router_b[:, None]

    for layer in range(DEPTH):
        last = layer == DEPTH - 1
        i = layer // 2
        mods = adaln_call(cvec, ada_w, ada_b3, layer).reshape(SUBLANES, 1, N_MOD * dim)
        if layer % 2 == 0:
            w_in, mu = _even_weights(ev_w_in, rwkv_mu, aw, bw)
            n_in = w_in.shape[2]
            p = matmul_mod_call(stream, mods, w_in, i, tn=1024, **modk)
            col_x, col_gb = 0, bw // bw
            col_r, col_k, col_v = 2 * bw // aw, 2 * bw // aw + 1, 2 * bw // aw + 2
            col_l = (2 * bw + 3 * aw) // LORA_PAD
            assert (2 * bw + 3 * aw) % LORA_PAD == 0 and n_in == 2 * bw + 3 * aw + LORA_PAD
            mu_i = mu[i]
            o_r = 2 * bw
            prm = dict(
                mu_r=mu_i[None, o_r:o_r + aw], mu_k=mu_i[None, o_r + aw:o_r + 2 * aw],
                mu_v=mu_i[None, o_r + 2 * aw:o_r + 3 * aw], mu_l=mu_i[None, o_r + 3 * aw:],
                w0=rwkv_w0[i], w2=_pad_rows(rwkv_w2[i], LANES).astype(BF16),
                a0=rwkv_a0[i], a2=_pad_rows(rwkv_a2[i], LANES).astype(BF16),
                g2=rwkv_g2[i].astype(BF16), k_k=rwkv_k_k[i][None, :], k_a=rwkv_k_a[i][None, :])
            r_s, v_s, kk, lw, kd, bb, g = rwkv_prep_call(p, prm, col_r=col_r, col_k=col_k, col_v=col_v,
                                                          col_l=col_l, **seq)
            nb = bw // LRU_BLOCK_W
            wcat = jnp.concatenate([lru_wa[i], lru_wx[i]], axis=-1).astype(BF16)
            bcat = jnp.stack([lru_ba[i].reshape(2, nb, LRU_BLOCK_W), lru_bx[i].reshape(2, nb, LRU_BLOCK_W)],
                             axis=2).reshape(2, 1, 2 * bw)
            lprm = dict(conv_w=lru_conv_w[i], conv_b=lru_conv_b[i][None, :], wcat=wcat, bcat=bcat,
                        lsl=(RGLRU_C * jax.nn.log_sigmoid(lru_lam[i]))[:, None, :])
            yf = rwkv_scan_call(r_s, lw, kd, v_s, kk, bb, reverse=False, **scan)
            yb = rwkv_scan_call(r_s, lw, kd, v_s, kk, bb, reverse=True, **scan)
            hf = lru_call(p, lprm, reverse=False, col_x=col_x, **scan)
            hb = lru_call(p, lprm, reverse=True, col_x=col_x, **scan)
            pprm = dict(r_k=rwkv_r_k[i].reshape(1, aw), lnx_w=rwkv_lnx_w[i][None, :], lnx_b=rwkv_lnx_b[i][None, :])
            ymix = even_post_call(yf, yb, r_s, v_s, kd, g, hf, hb, p, pprm, col_gb=col_gb)
            n_rows = n_all_rows if not last else n_lat_rows
            y = matmul_call(ymix, ev_w_out.astype(BF16), i, tn=1024, n_rows=n_rows)
        else:
            qk_w = C_HEADS * C_QK_HEAD
            v_w = C_HEADS * C_V_HEAD
            main = 2 * qk_w + 2 * v_w
            w_gate = _pad_cols(od_w_in[..., main:], LANES).astype(BF16)
            p = matmul_mod_call(stream, mods, od_w_in.astype(BF16), i, tn=1024, n_cols=main, **modk)
            gp = matmul_mod_call(stream, mods, w_gate, i, tn=LANES, **modk)[:, :4 * C_HEADS]
            gp = gp.reshape(n_all_rows, 2, 2, C_HEADS)
            g_rows, g_cols = _mlstm_gates(gp, mlstm_ig_b[i], mlstm_fg_b[i])
            cols = dict(col_q=0, col_k=(qk_w + v_w) // C_QK_HEAD, col_v=(2 * qk_w + v_w) // C_V_HEAD)
            hf = mlstm_call(p, g_rows, g_cols, reverse=False, **scan, **cols)
            hb = mlstm_call(p, g_rows, g_cols, reverse=True, **scan, **cols)
            n_rows = n_all_rows if not last else n_lat_rows
            ymix = odd_post_call(hf, hb, p, mlstm_norm_w[i][None, :], mlstm_norm_b[i][None, :],
                                 n_rows=n_rows, col_o=qk_w // (v_w // 2))
            y = matmul_call(ymix, od_w_out.astype(BF16), i, tn=1024, n_rows=n_rows)
        n_rows = n_all_rows if not last else n_lat_rows
        stream1, h_moe, ids, gates = ln_route_call(stream, y, mods, norm_g4, norm_b4, layer, router_wt, router_b2,
                                                   n_rows=n_rows, **modk)
        y_slots = moe_ffn(h_moe, ids, moe_w_gate, moe_w_up, moe_w_down, layer)
        stream = ln_combine_call(stream1, y_slots, gates.T, mods, norm_g4, norm_b4, layer, n_rows=n_rows, **modk)
    return stream[:n_lat_rows].reshape(bsz, s_len, dim)


def _pad_rows(w, rows):
    return jnp.pad(w, ((0, 0), (0, rows - w.shape[1]), (0, 0)))


def _mlstm_gates(gp, ig_b, fg_b):
    n_rows = gp.shape[0]
    chunk = MLSTM_CHUNK
    n_sub = SCAN_ROWS // chunk
    outs = []
    for d in range(2):
        ig = GATE_CAP * jnp.tanh((gp[:, d, 0] + ig_b[d]) / GATE_CAP)
        lf = jax.nn.log_sigmoid(GATE_CAP * jnp.tanh((gp[:, d, 1] + fg_b[d]) / GATE_CAP))
        lf = lf.reshape(n_rows // chunk, chunk, C_HEADS)
        b = jnp.cumsum(lf[:, ::-1], axis=1)[:, ::-1] if d == 1 else jnp.cumsum(lf, axis=1)
        both = jnp.stack([b, ig.reshape(n_rows // chunk, chunk, C_HEADS)], axis=0)
        outs.append(both)
    g = jnp.stack(outs, axis=0)
    g = g.reshape(2, 2, n_rows // SCAN_ROWS, n_sub, chunk, C_HEADS)
    g_rows = jnp.transpose(g, (0, 1, 5, 2, 3, 4))
    g_cols = jnp.transpose(g, (0, 1, 5, 2, 4, 3))
    return g_rows, g_cols
```

```python
import functools
import math

import jax
import jax.numpy as jnp
from jax import lax
from jax.experimental import pallas as pl
from jax.experimental.pallas import tpu as pltpu

F32 = jnp.float32
BF16 = jnp.bfloat16

DEPTH = 2
N_MOD = 6
ALPHA = (2 * DEPTH) ** 0.25
LN_EPS = 1e-5
HEAD_DIM = 64
DECAY_LORA = 96
ICL_LORA = 96
GATE_LORA = 256
LNX_EPS = 64e-5
LRU_BLOCK_W = 128
CONV_W = 4
RGLRU_C = 8.0
C_HEADS = 8
C_QK_HEAD = 256
C_V_HEAD = 512
GATE_CAP = 15.0
MLSTM_NORM_EPS = 1e-6
N_EXPERTS = 16
N_GROUPS = 4
EXPERTS_PER_GROUP = N_EXPERTS // N_GROUPS
TOP_K = 2
POS_BASE = 10000.0
GRID_W = 64

LANES = 128
SUBLANES = 8
SCAN_ROWS = 256
CHUNK = 64
MLSTM_CHUNK = 128
SCAN_PAIRS = 4
PREP_ROWS = 128
LN_ROWS = 256
MM_ROWS = 512
MOE_ROWS = 512
MOE_FTILE = 256
MOE_NCHUNK = 1024
MOE_KCHUNK = 512
LORA_PAD = 1024
VMEM_LIMIT = 60 * 1024 * 1024


def _cparams(sem):
    return pltpu.CompilerParams(dimension_semantics=sem, vmem_limit_bytes=VMEM_LIMIT)


def _bdot(a, b):
    return jnp.dot(a.astype(BF16), b.astype(BF16), preferred_element_type=F32)


def _bdot_nt(a, b):
    return lax.dot_general(a.astype(BF16), b.astype(BF16), (((1,), (1,)), ((), ())),
                           preferred_element_type=F32)


def _bdot_tn(a, b):
    return lax.dot_general(a.astype(BF16), b.astype(BF16), (((0,), (0,)), ((), ())),
                           preferred_element_type=F32)


def _softplus(z):
    return jnp.maximum(z, 0.0) + jnp.log1p(jnp.exp(-jnp.abs(z)))


def _adaln_kernel(c_ref, w_ref, b_ref, o_ref):
    c = c_ref[...]
    s = c * jax.nn.sigmoid(c)
    o_ref[...] = _bdot(s, w_ref[...]) + b_ref[...]


def adaln_call(cvec, ada_w, ada_b, layer, tn=512):
    rows, d = cvec.shape
    n = ada_w.shape[2]
    return pl.pallas_call(
        _adaln_kernel,
        out_shape=jax.ShapeDtypeStruct((rows, n), F32),
        grid=(n // tn,),
        in_specs=[
            pl.BlockSpec((rows, d), lambda j: (0, 0)),
            pl.BlockSpec((None, d, tn), lambda j: (layer, 0, j)),
            pl.BlockSpec((None, 1, tn), lambda j: (layer, 0, j)),
        ],
        out_specs=pl.BlockSpec((rows, tn), lambda j: (0, j)),
        compiler_params=_cparams(("parallel",)),
        name="adaln",
    )(cvec, ada_w, ada_b)


def _embed_kernel(x_ref, ctx_ref, sr, cr, sc, cc, o_ref, *, n_lat_tiles, tiles_per_tab):
    i = pl.program_id(0)
    tm = o_ref.shape[0]
    g_rows = tm // GRID_W

    @pl.when(i < n_lat_tiles)
    def _():
        base = (i % tiles_per_tab) * g_rows

        def by_row(t_ref):
            return jnp.concatenate([jnp.broadcast_to(t_ref[pl.ds(base + g, 1), :], (GRID_W, t_ref.shape[1]))
                                    for g in range(g_rows)], axis=0)

        def by_col(t_ref):
            return jnp.concatenate([t_ref[...]] * g_rows, axis=0)

        o_ref[...] = x_ref[...] + jnp.concatenate([by_row(sr), by_row(cr), by_col(sc), by_col(cc)], axis=1)

    @pl.when(i >= n_lat_tiles)
    def _():
        o_ref[...] = ctx_ref[...]


def embed_call(x2d, ctx2d, seq_len):
    n_lat, d = x2d.shape
    n_ctx = ctx2d.shape[0]
    tm = LN_ROWS
    quarter = d // 4
    g_rows = tm // GRID_W
    assert tm % GRID_W == 0 and SUBLANES % g_rows == 0 and seq_len % (SUBLANES * GRID_W) == 0
    omega = POS_BASE ** (-jnp.arange(quarter, dtype=F32) / quarter)
    ang_r = jnp.arange(seq_len // GRID_W, dtype=F32)[:, None] * omega[None, :]
    ang_c = jnp.arange(GRID_W, dtype=F32)[:, None] * omega[None, :]
    n_lat_tiles = n_lat // tm
    tiles_per_tab = SUBLANES // g_rows
    tabs_per_seq = seq_len // GRID_W // SUBLANES
    tab = pl.BlockSpec((SUBLANES, quarter), lambda i: ((i // tiles_per_tab) % tabs_per_seq, 0))
    col = pl.BlockSpec((GRID_W, quarter), lambda i: (0, 0))
    return pl.pallas_call(
        functools.partial(_embed_kernel, n_lat_tiles=n_lat_tiles, tiles_per_tab=tiles_per_tab),
        out_shape=jax.ShapeDtypeStruct((n_lat + n_ctx, d), F32),
        grid=((n_lat + n_ctx) // tm,),
        in_specs=[pl.BlockSpec((tm, d), lambda i: (jnp.minimum(i, n_lat_tiles - 1), 0)),
                  pl.BlockSpec((tm, d), lambda i: (jnp.maximum(i - n_lat_tiles, 0), 0)),
                  tab, tab, col, col],
        out_specs=pl.BlockSpec((tm, d), lambda i: (i, 0)),
        compiler_params=_cparams(("parallel",)),
        name="embed",
    )(x2d, ctx2d, jnp.sin(ang_r), jnp.cos(ang_r), jnp.sin(ang_c), jnp.cos(ang_c))


def _mm_mod_kernel(x_ref, sh_ref, sc_ref, w_ref, o_ref, xb_ref):
    @pl.when(pl.program_id(1) == 0)
    def _():
        xb_ref[...] = (x_ref[...] * (1.0 + sc_ref[...]) + sh_ref[...]).astype(BF16)

    o_ref[...] = jnp.dot(xb_ref[...], w_ref[...], preferred_element_type=F32)


def matmul_mod_call(x, mods, w, layer_w, *, rows_per_mod, n_mod_rows, tn, n_rows=None, n_cols=None):
    r, d = x.shape
    n = w.shape[2] if n_cols is None else n_cols
    r = r if n_rows is None else n_rows
    tm = MM_ROWS

    def midx(i):
        return jnp.minimum(i * tm // rows_per_mod, n_mod_rows - 1)

    return pl.pallas_call(
        _mm_mod_kernel,
        out_shape=jax.ShapeDtypeStruct((r, n), F32),
        grid=(r // tm, n // tn),
        in_specs=[
            pl.BlockSpec((tm, d), lambda i, j: (i, 0)),
            pl.BlockSpec((None, 1, d), lambda i, j: (midx(i), 0, 0)),
            pl.BlockSpec((None, 1, d), lambda i, j: (midx(i), 0, 1)),
            pl.BlockSpec((None, d, tn), lambda i, j: (layer_w, 0, j)),
        ],
        out_specs=pl.BlockSpec((tm, tn), lambda i, j: (i, j)),
        scratch_shapes=[pltpu.VMEM((tm, d), BF16)],
        compiler_params=_cparams(("parallel", "arbitrary")),
        name="proj_in",
    )(x, mods, mods, w)


def _cast_kernel(w_ref, o_ref):
    o_ref[...] = w_ref[...].astype(o_ref.dtype)


def cast_cols_call(w, layer, n_cols, *, tr=512, tc=1024):
    k = w.shape[1]
    return pl.pallas_call(
        _cast_kernel,
        out_shape=jax.ShapeDtypeStruct((1, k, n_cols), BF16),
        grid=(k // tr, n_cols // tc),
        in_specs=[pl.BlockSpec((None, tr, tc), lambda i, j: (layer, i, j))],
        out_specs=pl.BlockSpec((None, tr, tc), lambda i, j: (0, i, j)),
        compiler_params=_cparams(("parallel", "parallel")),
        name="cast_cols",
    )(w)


def _mm_kernel(x_ref, w_ref, o_ref):
    o_ref[...] = jnp.dot(x_ref[...], w_ref[...], preferred_element_type=F32)


def matmul_call(x, w, layer_w, *, tn, n_rows=None):
    r, k = x.shape
    n = w.shape[2]
    r = r if n_rows is None else n_rows
    tm = MM_ROWS
    return pl.pallas_call(
        _mm_kernel,
        out_shape=jax.ShapeDtypeStruct((r, n), F32),
        grid=(r // tm, n // tn),
        in_specs=[
            pl.BlockSpec((tm, k), lambda i, j: (i, 0)),
            pl.BlockSpec((None, k, tn), lambda i, j: (layer_w, 0, j)),
        ],
        out_specs=pl.BlockSpec((tm, tn), lambda i, j: (i, j)),
        compiler_params=_cparams(("parallel", "arbitrary")),
        name="proj_out",
    )(x, w)


def _tile_flags(i, tm, n_lat_rows, lat_len, ctx_len):
    row = i * tm
    in_lat = row < n_lat_rows
    pos = jnp.where(in_lat, row % lat_len, (row - n_lat_rows) % ctx_len)
    seq = jnp.where(in_lat, lat_len, ctx_len)
    return pos == 0, pos + tm == seq


def _row_shift(c, prev_row, next_row):
    tm = c.shape[0]
    rows = lax.broadcasted_iota(jnp.int32, c.shape, 0)
    xp = jnp.where(rows == 0, prev_row, pltpu.roll(c, 1, axis=0))
    xn = jnp.where(rows == tm - 1, next_row, pltpu.roll(c, tm - 1, axis=0))
    return xp, xn


def _head_ones():
    r = lax.broadcasted_iota(jnp.int32, (LANES, LANES), 0) // HEAD_DIM
    c = lax.broadcasted_iota(jnp.int32, (LANES, LANES), 1) // HEAD_DIM
    return (r == c).astype(BF16)


def _head_sum(x, ones):
    w = x.shape[1]
    parts = [jnp.dot(x[:, j:j + LANES].astype(BF16), ones, preferred_element_type=F32)
             for j in range(0, w, LANES)]
    return jnp.concatenate(parts, axis=1)


def _halo_specs(tm, width, col, n_rows):
    per = tm // SUBLANES
    last = n_rows // SUBLANES - 1
    return [
        pl.BlockSpec((tm, width), lambda i: (i, col)),
        pl.BlockSpec((SUBLANES, width), lambda i: (jnp.maximum(i * per - 1, 0), col)),
        pl.BlockSpec((SUBLANES, width), lambda i: (jnp.minimum((i + 1) * per, last), col)),
    ]


def _rwkv_prep_kernel(rc, rp, rn, kc, kp, kn, vc, vp, vn, lc, lp, ln,
                      mu_r, mu_k, mu_v, mu_l, w0, w2, a0, a2, g2, kk_s, ka_s,
                      r_o, v_o, kk_o, lw_o, kd_o, bb_o, g_o, *, tm, n_lat_rows, lat_len, ctx_len):
    first, last = _tile_flags(pl.program_id(0), tm, n_lat_rows, lat_len, ctx_len)

    def shifted(c_ref, p_ref, n_ref, mu_ref):
        c = c_ref[...]
        prev_row = jnp.where(first, 0.0, p_ref[SUBLANES - 1:SUBLANES, :])
        next_row = jnp.where(last, 0.0, n_ref[0:1, :])
        xp, xn = _row_shift(c, prev_row, next_row)
        return c + mu_ref[...] * (0.5 * (xp + xn) - c)

    r = shifted(rc, rp, rn, mu_r)
    k = shifted(kc, kp, kn, mu_k)
    v = shifted(vc, vp, vn, mu_v)
    lo = shifted(lc, lp, ln, mu_l)
    r_o[...] = r
    v_o[...] = v

    ones = _head_ones()
    kk = k * kk_s[...]
    ss = _head_sum(kk * kk, ones)
    kk = kk * lax.rsqrt(jnp.maximum(ss, 1e-24))
    kk_o[...] = kk

    for d in range(2):
        wd = lo[:, d * LANES:(d + 1) * LANES]
        ad = lo[:, (2 + d) * LANES:(3 + d) * LANES]
        z = w0[d:d + 1, :] + _bdot(jnp.tanh(wd), w2[d])
        log_w = -_softplus(-z) - 0.5
        lw_o[d] = -jnp.exp(log_w)
        icl = jax.nn.sigmoid(a0[d:d + 1, :] + _bdot(ad, a2[d]))
        kd_o[d] = k * (1.0 + (icl - 1.0) * ka_s[...])
        bb_o[d] = kk * icl
    gd = lo[:, 4 * LANES:4 * LANES + GATE_LORA]
    g_o[...] = _bdot(jax.nn.sigmoid(gd), g2[...])


def rwkv_prep_call(p, prm, *, n_lat_rows, lat_len, ctx_len, col_r, col_k, col_v, col_l):
    r_rows = p.shape[0]
    tm = PREP_ROWS
    aw = prm["mu_r"].shape[1]
    full = lambda shape: pl.BlockSpec(shape, lambda i: (0,) * len(shape))
    in_specs = (_halo_specs(tm, aw, col_r, r_rows) + _halo_specs(tm, aw, col_k, r_rows)
                + _halo_specs(tm, aw, col_v, r_rows) + _halo_specs(tm, LORA_PAD, col_l, r_rows)
                + [full((1, aw)), full((1, aw)), full((1, aw)), full((1, LORA_PAD)),
                   full((2, aw)), full((2, LANES, aw)), full((2, aw)), full((2, LANES, aw)),
                   full((GATE_LORA, aw)), full((1, aw)), full((1, aw))])
    row_spec = pl.BlockSpec((tm, aw), lambda i: (i, 0))
    dir_spec = pl.BlockSpec((2, tm, aw), lambda i: (0, i, 0))
    one = jax.ShapeDtypeStruct((r_rows, aw), F32)
    two = jax.ShapeDtypeStruct((2, r_rows, aw), F32)
    kern = functools.partial(_rwkv_prep_kernel, tm=tm, n_lat_rows=n_lat_rows, lat_len=lat_len, ctx_len=ctx_len)
    return pl.pallas_call(
        kern,
        out_shape=(one, one, one, two, two, two, one),
        grid=(r_rows // tm,),
        in_specs=in_specs,
        out_specs=(row_spec, row_spec, row_spec, dir_spec, dir_spec, dir_spec, row_spec),
        compiler_params=_cparams(("parallel",)),
        name="rwkv_prep",
    )(p, p, p, p, p, p, p, p, p, p, p, p,
      prm["mu_r"], prm["mu_k"], prm["mu_v"], prm["mu_l"], prm["w0"], prm["w2"], prm["a0"], prm["a2"],
      prm["g2"], prm["k_k"], prm["k_a"])


def _scan_block(b, s, *, reverse, n_lat_blocks_total, lat_blocks):
    ctx_blk = n_lat_blocks_total + b
    lat_blk = b * lat_blocks + ((lat_blocks - s) if reverse else (s - 1))
    return jnp.where(s == 0, ctx_blk, lat_blk)


def _rwkv_scan_kernel(r_ref, lw_ref, k_ref, v_ref, kk_ref, bb_ref, y_ref, s_ref, *, reverse):
    @pl.when(pl.program_id(2) == 0)
    def _():
        s_ref[...] = jnp.zeros_like(s_ref)

    L = CHUNK
    ri = lax.broadcasted_iota(jnp.int32, (L, L), 0)
    ci = lax.broadcasted_iota(jnp.int32, (L, L), 1)
    tri = ((ci >= ri) if reverse else (ci <= ri)).astype(F32)
    r2 = lax.broadcasted_iota(jnp.int32, (2 * L, 2 * L), 0) % L
    c2 = lax.broadcasted_iota(jnp.int32, (2 * L, 2 * L), 1) % L
    strict = (c2 > r2) if reverse else (c2 < r2)
    incl = (c2 >= r2) if reverse else (c2 <= r2)
    rr = lax.broadcasted_iota(jnp.int32, (2 * L, 2 * L), 0)
    cc = lax.broadcasted_iota(jnp.int32, (2 * L, 2 * L), 1)
    eye = (rr == cc).astype(F32)
    off_masks = []
    sz = 2
    while sz <= L:
        late, early = (cc, rr) if reverse else (rr, cc)
        off_masks.append((rr // sz == cc // sz) & (late % sz >= sz // 2) & (early % sz < sz // 2))
        sz *= 2
    lane = lax.broadcasted_iota(jnp.int32, (L, LANES), 1)
    head0 = lane < HEAD_DIM

    def stack2(x):
        return jnp.concatenate([jnp.where(head0, x, 0.0), jnp.where(head0, 0.0, x)], axis=0)

    n_sub = SCAN_ROWS // L
    n_pair = r_ref.shape[1] // LANES
    order = [(n_sub - 1 - j) if reverse else j for j in range(n_sub)]

    keys = [(c, p) for c in order for p in range(n_pair)]
    a2, b2, k2, q2, v2, bt2, kt2, decay = ({} for _ in range(8))
    for c in order:
        rows = pl.ds(c * L, L)
        lw_all = lw_ref[rows, :]
        cum_all = jnp.dot(tri, lw_all, preferred_element_type=F32, precision=lax.Precision.HIGHEST)
        for p in range(n_pair):
            cols = slice(p * LANES, (p + 1) * LANES)
            lw = lw_all[:, cols]
            cum = cum_all[:, cols]
            tot = cum[0:1, :] if reverse else cum[L - 1:L, :]
            e_pos = jnp.exp(cum)
            e_neg = jnp.exp(-cum)
            e_rem = jnp.exp(tot - cum)
            kk = kk_ref[rows, cols]
            kd = k_ref[rows, cols]
            bb = bb_ref[rows, cols]
            key = (c, p)
            a2[key] = stack2(-kk * jnp.exp(cum - lw))
            b2[key] = stack2(bb * e_neg)
            k2[key] = stack2(kd * e_neg)
            q2[key] = stack2(r_ref[rows, cols] * e_pos)
            v2[key] = stack2(v_ref[rows, cols])
            bt2[key] = stack2(bb * e_rem)
            kt2[key] = stack2(kd * e_rem)
            decay[key] = jnp.exp(tot)

    hh = 2 * L
    sc = {k: _bdot_nt(jnp.concatenate([a2[k], q2[k]], axis=0), jnp.concatenate([b2[k], k2[k]], axis=0))
          for k in keys}
    mm = {k: jnp.where(strict, sc[k][:hh, :hh], 0.0) for k in keys}
    nn = {k: jnp.where(strict, sc[k][:hh, hh:], 0.0) for k in keys}
    qq = {k: jnp.where(incl, sc[k][hh:, :hh], 0.0) for k in keys}
    zz = {k: jnp.where(incl, sc[k][hh:, hh:], 0.0) for k in keys}
    tinv = {k: eye + jnp.where(off_masks[0], mm[k], 0.0) for k in keys}
    for off in off_masks[1:]:
        half_step = {k: _bdot(tinv[k], jnp.where(off, mm[k], 0.0)) for k in keys}
        tinv = {k: tinv[k] + _bdot(half_step[k], tinv[k]) for k in keys}
    nzv = {k: _bdot(jnp.concatenate([nn[k], zz[k]], axis=0), v2[k]) for k in keys}
    px = {k: _bdot(tinv[k], jnp.concatenate([a2[k], nzv[k][:hh]], axis=1)) for k in keys}
    qpx = {k: _bdot(qq[k], px[k]) for k in keys}
    g_mat = {k: q2[k] + qpx[k][:, :LANES] for k in keys}
    y_loc = {k: qpx[k][:, LANES:] + nzv[k][hh:] for k in keys}
    pb = {k: _bdot_tn(px[k], bt2[k]) for k in keys}
    phi = {k: pb[k][:LANES] for k in keys}
    psi = {k: pb[k][LANES:] + _bdot_tn(v2[k], kt2[k]) for k in keys}
    items = {k: (g_mat[k], y_loc[k], decay[k], phi[k], psi[k]) for k in keys}

    st = [s_ref[p] for p in range(n_pair)]
    for c in order:
        for p in range(n_pair):
            g, yl, dec, ph, ps = items[(c, p)]
            y2 = _bdot_nt(g, st[p]) + yl
            y_ref[pl.ds(c * L, L), p * LANES:(p + 1) * LANES] = y2[:L, :] + y2[L:, :]
            st[p] = st[p] * dec + _bdot(st[p], ph) + ps
    for p in range(n_pair):
        s_ref[p] = st[p]


def rwkv_scan_call(r, lw, kd, v, kk, bb, *, reverse, batch, lat_len, ctx_len):
    r_rows, aw = r.shape
    d = 1 if reverse else 0
    lat_blocks = lat_len // SCAN_ROWS
    width = SCAN_PAIRS * LANES
    blk = functools.partial(_scan_block, reverse=reverse, n_lat_blocks_total=batch * lat_blocks,
                            lat_blocks=lat_blocks)
    row_spec = pl.BlockSpec((SCAN_ROWS, width), lambda b, h, s: (blk(b, s), h))
    dir_spec = pl.BlockSpec((None, SCAN_ROWS, width), lambda b, h, s: (d, blk(b, s), h))
    return pl.pallas_call(
        functools.partial(_rwkv_scan_kernel, reverse=reverse),
        out_shape=jax.ShapeDtypeStruct((r_rows, aw), F32),
        grid=(batch, aw // width, 1 + lat_blocks),
        in_specs=[row_spec, dir_spec, dir_spec, row_spec, row_spec, dir_spec],
        out_specs=row_spec,
        scratch_shapes=[pltpu.VMEM((SCAN_PAIRS, LANES, LANES), F32)],
        compiler_params=_cparams(("parallel", "parallel", "arbitrary")),
        name="rwkv_scan_bwd" if reverse else "rwkv_scan_fwd",
    )(r, lw, kd, v, kk, bb)


def _lru_kernel(xc_ref, xp_ref, xn_ref, cw, cb, wcat, bcat, lsl, h_ref, a_ref, u_ref, carry_ref,
                *, reverse, lat_blocks):
    s = pl.program_id(1)

    @pl.when(s == 0)
    def _():
        carry_ref[...] = jnp.zeros_like(carry_ref)

    tm = SCAN_ROWS
    lat_idx = (lat_blocks - s) if reverse else (s - 1)
    first = (s == 0) | (lat_idx == 0)
    last = (s == 0) | (lat_idx == lat_blocks - 1)
    x = xc_ref[...]
    prev_row = jnp.where(first, 0.0, xp_ref[SUBLANES - 1:SUBLANES, :])
    next1 = jnp.where(last, 0.0, xn_ref[0:1, :])
    next2 = jnp.where(last, 0.0, xn_ref[1:2, :])
    row_id = lax.broadcasted_iota(jnp.int32, x.shape, 0)
    xm1 = jnp.where(row_id == 0, prev_row, pltpu.roll(x, 1, axis=0))
    xp1 = jnp.where(row_id == tm - 1, next1, pltpu.roll(x, tm - 1, axis=0))
    xp2 = jnp.where(row_id == tm - 1, next2, jnp.where(row_id == tm - 2, next1, pltpu.roll(x, tm - 2, axis=0)))
    xc = cb[...] + xm1 * cw[0:1, :] + x * cw[1:2, :] + xp1 * cw[2:3, :] + xp2 * cw[3:4, :]
    bwid = LRU_BLOCK_W
    for n in range(x.shape[1] // bwid):
        cols = slice(n * bwid, (n + 1) * bwid)
        xn = xc[:, cols]
        gates = _bdot(xn, wcat[n]) + bcat[:, 2 * n * bwid:2 * (n + 1) * bwid]
        log_a = jax.nn.sigmoid(gates[:, :bwid]) * lsl[:, cols]
        a = jnp.exp(log_a)
        om = -jnp.tanh(log_a) * (a * a + 1.0)
        a_ref[:, cols] = a
        u_ref[:, cols] = xn * jax.nn.sigmoid(gates[:, bwid:]) * jnp.sqrt(om)

    w = a_ref.shape[1]
    rows = lax.broadcasted_iota(jnp.int32, (SUBLANES, w), 0)
    n_groups = SCAN_ROWS // SUBLANES

    def body(gi, carry):
        g = (n_groups - 1 - gi) if reverse else gi
        sl = pl.ds(pl.multiple_of(g * SUBLANES, SUBLANES), SUBLANES)
        a = a_ref[sl, :]
        u = u_ref[sl, :]
        for sh in (1, 2, 4):
            if reverse:
                ok = rows < SUBLANES - sh
                a_s = pltpu.roll(a, SUBLANES - sh, axis=0)
                u_s = pltpu.roll(u, SUBLANES - sh, axis=0)
            else:
                ok = rows >= sh
                a_s = pltpu.roll(a, sh, axis=0)
                u_s = pltpu.roll(u, sh, axis=0)
            u = jnp.where(ok, a * u_s + u, u)
            a = jnp.where(ok, a * a_s, a)
        h = a * carry + u
        h_ref[sl, :] = h
        return h[0:1, :] if reverse else h[SUBLANES - 1:SUBLANES, :]

    carry_ref[...] = lax.fori_loop(0, n_groups, body, carry_ref[...])


def lru_call(p, prm, *, reverse, batch, lat_len, ctx_len, col_x):
    r_rows = p.shape[0]
    bw = prm["conv_b"].shape[1]
    nb = bw // LRU_BLOCK_W
    d = 1 if reverse else 0
    lat_blocks = lat_len // SCAN_ROWS
    per = SCAN_ROWS // SUBLANES
    last8 = r_rows // SUBLANES - 1
    blk = functools.partial(_scan_block, reverse=reverse, n_lat_blocks_total=batch * lat_blocks,
                            lat_blocks=lat_blocks)
    full = lambda shape: pl.BlockSpec(shape, lambda b, s: (0,) * len(shape))
    return pl.pallas_call(
        functools.partial(_lru_kernel, reverse=reverse, lat_blocks=lat_blocks),
        out_shape=jax.ShapeDtypeStruct((r_rows, bw), F32),
        grid=(batch, 1 + lat_blocks),
        in_specs=[
            pl.BlockSpec((SCAN_ROWS, bw), lambda b, s: (blk(b, s), col_x)),
            pl.BlockSpec((SUBLANES, bw), lambda b, s: (jnp.maximum(blk(b, s) * per - 1, 0), col_x)),
            pl.BlockSpec((SUBLANES, bw), lambda b, s: (jnp.minimum((blk(b, s) + 1) * per, last8), col_x)),
            full((CONV_W, bw)), full((1, bw)),
            pl.BlockSpec((None, nb, LRU_BLOCK_W, 2 * LRU_BLOCK_W), lambda b, s: (d, 0, 0, 0)),
            pl.BlockSpec((None, 1, 2 * bw), lambda b, s: (d, 0, 0)),
            pl.BlockSpec((None, 1, bw), lambda b, s: (d, 0, 0)),
        ],
        out_specs=pl.BlockSpec((SCAN_ROWS, bw), lambda b, s: (blk(b, s), 0)),
        scratch_shapes=[pltpu.VMEM((SCAN_ROWS, bw), F32), pltpu.VMEM((SCAN_ROWS, bw), F32),
                        pltpu.VMEM((1, bw), F32)],
        compiler_params=_cparams(("parallel", "arbitrary")),
        name="lru_bwd" if reverse else "lru_fwd",
    )(p, p, p, prm["conv_w"], prm["conv_b"], prm["wcat"], prm["bcat"], prm["lsl"])


def _even_post_kernel(yf, yb, r_ref, v_ref, kdf, kdb, g_ref, hf, hb, gb_ref, rk, lnw, lnb, o_ref):
    ones = _head_ones()
    y = yf[...] + yb[...]
    inv_n = 1.0 / HEAD_DIM
    yc = y - _head_sum(y, ones) * inv_n
    var = _head_sum(yc * yc, ones) * inv_n
    hn = yc * lax.rsqrt(var + LNX_EPS) * lnw[...] + lnb[...]
    r = r_ref[...]
    v = v_ref[...]
    bonus = _head_sum(r * kdf[...] * rk[...], ones) * v + _head_sum(r * kdb[...] * rk[...], ones) * v
    ya = (hn + bonus) * g_ref[...]
    yl = (hf[...] + hb[...]) * jax.nn.gelu(gb_ref[...])
    aw = ya.shape[1]
    o_ref[:, :aw] = ya.astype(BF16)
    o_ref[:, aw:] = yl.astype(BF16)


def even_post_call(yf, yb, r, v, kd, g, hf, hb, p, prm, *, col_gb):
    r_rows, aw = yf.shape
    bw = hf.shape[1]
    tm = PREP_ROWS
    row = lambda w: pl.BlockSpec((tm, w), lambda i: (i, 0))
    full = lambda shape: pl.BlockSpec(shape, lambda i: (0,) * len(shape))
    return pl.pallas_call(
        _even_post_kernel,
        out_shape=jax.ShapeDtypeStruct((r_rows, aw + bw), BF16),
        grid=(r_rows // tm,),
        in_specs=[row(aw), row(aw), row(aw), row(aw),
                  pl.BlockSpec((None, tm, aw), lambda i: (0, i, 0)),
                  pl.BlockSpec((None, tm, aw), lambda i: (1, i, 0)),
                  row(aw), row(bw), row(bw),
                  pl.BlockSpec((tm, bw), lambda i: (i, col_gb)),
                  full((1, aw)), full((1, aw)), full((1, aw))],
        out_specs=row(aw + bw),
        compiler_params=_cparams(("parallel",)),
        name="even_post",
    )(yf, yb, r, v, kd, kd, g, hf, hb, p, prm["r_k"], prm["lnx_w"], prm["lnx_b"])


def _mlstm_kernel(q_ref, k_ref, v_ref, br_ref, bc_ref, ir_ref, ic_ref, h_ref, c_ref, n_ref, m_ref, *, reverse):
    @pl.when(pl.program_id(2) == 0)
    def _():
        c_ref[...] = jnp.zeros_like(c_ref)
        n_ref[...] = jnp.zeros_like(n_ref)
        m_ref[...] = jnp.zeros_like(m_ref)

    L = MLSTM_CHUNK
    ri = lax.broadcasted_iota(jnp.int32, (L, L), 0)
    ci = lax.broadcasted_iota(jnp.int32, (L, L), 1)
    causal = (ci >= ri) if reverse else (ci <= ri)
    n_sub = SCAN_ROWS // L
    order = [(n_sub - 1 - j) if reverse else j for j in range(n_sub)]
    scale = C_QK_HEAD ** -0.5

    m_cur = m_ref[...]
    m_in, ew, ec = {}, {}, {}
    for c in order:
        b_r = br_ref[c:c + 1, :]
        b_tot = b_r[:, 0:1] if reverse else b_r[:, L - 1:L]
        w_in = b_tot - bc_ref[:, c:c + 1] + ic_ref[:, c:c + 1]
        carry_log = b_tot + m_cur
        m_new = jnp.maximum(carry_log, jnp.max(w_in, axis=0, keepdims=True))
        m_in[c] = m_cur
        ew[c] = jnp.exp(w_in - m_new)
        ec[c] = jnp.exp(carry_log - m_new)
        m_cur = m_new
    m_ref[...] = m_cur

    q = {c: (q_ref[pl.ds(c * L, L), :] * scale).astype(BF16) for c in order}
    k = {c: k_ref[pl.ds(c * L, L), :] for c in order}
    v = {c: v_ref[pl.ds(c * L, L), :].astype(BF16) for c in order}
    qk = {c: _bdot_nt(q[c], k[c]) for c in order}
    m_t, e_inter, scores = {}, {}, {}
    for c in order:
        b_c = bc_ref[:, c:c + 1]
        dmat = jnp.where(causal, b_c - br_ref[c:c + 1, :] + ir_ref[c:c + 1, :], -jnp.inf)
        inter = b_c + m_in[c]
        m_t[c] = jnp.maximum(inter, jnp.max(dmat, axis=1, keepdims=True))
        scores[c] = qk[c] * jnp.exp(dmat - m_t[c])
        e_inter[c] = jnp.exp(inter - m_t[c])
    intra = {c: _bdot(scores[c], v[c]) for c in order}
    kw = {c: k[c] * ew[c] for c in order}
    kv = {c: _bdot_tn(kw[c], v[c]) for c in order}

    c_cur = c_ref[...]
    n_cur = n_ref[...]
    c_in, n_in = {}, {}
    for c in order:
        c_in[c] = c_cur
        n_in[c] = n_cur
        c_cur = ec[c] * c_cur + kv[c]
        n_cur = ec[c] * n_cur + jnp.sum(kw[c], axis=0, keepdims=True)
    c_ref[...] = c_cur
    n_ref[...] = n_cur

    qc = {c: _bdot(q[c], c_in[c]) for c in order}
    for c in order:
        qf = q[c].astype(F32)
        num = e_inter[c] * qc[c] + intra[c]
        den = (e_inter[c] * jnp.sum(qf * n_in[c], axis=1, keepdims=True)
               + jnp.sum(scores[c], axis=1, keepdims=True))
        h_ref[pl.ds(c * L, L), :] = num / jnp.maximum(jnp.abs(den), jnp.exp(-m_t[c]))


def mlstm_call(p, g_rows, g_cols, *, reverse, batch, lat_len, ctx_len, col_q, col_k, col_v):
    r_rows = p.shape[0]
    d = 1 if reverse else 0
    n_sub = SCAN_ROWS // MLSTM_CHUNK
    lat_blocks = lat_len // SCAN_ROWS
    blk = functools.partial(_scan_block, reverse=reverse, n_lat_blocks_total=batch * lat_blocks,
                            lat_blocks=lat_blocks)
    grow = lambda which: pl.BlockSpec((None, None, None, None, n_sub, MLSTM_CHUNK),
                                      lambda b, h, s: (d, which, h, blk(b, s), 0, 0))
    gcol = lambda which: pl.BlockSpec((None, None, None, None, MLSTM_CHUNK, n_sub),
                                      lambda b, h, s: (d, which, h, blk(b, s), 0, 0))
    return pl.pallas_call(
        functools.partial(_mlstm_kernel, reverse=reverse),
        out_shape=jax.ShapeDtypeStruct((r_rows, C_HEADS * C_V_HEAD), F32),
        grid=(batch, C_HEADS, 1 + lat_blocks),
        in_specs=[
            pl.BlockSpec((SCAN_ROWS, C_QK_HEAD), lambda b, h, s: (blk(b, s), col_q + h)),
            pl.BlockSpec((SCAN_ROWS, C_QK_HEAD), lambda b, h, s: (blk(b, s), col_k + h)),
            pl.BlockSpec((SCAN_ROWS, C_V_HEAD), lambda b, h, s: (blk(b, s), col_v + h)),
            grow(0), gcol(0), grow(1), gcol(1),
        ],
        out_specs=pl.BlockSpec((SCAN_ROWS, C_V_HEAD), lambda b, h, s: (blk(b, s), h)),
        scratch_shapes=[pltpu.VMEM((C_QK_HEAD, C_V_HEAD), F32), pltpu.VMEM((1, C_QK_HEAD), F32),
                        pltpu.VMEM((1, 1), F32)],
        compiler_params=_cparams(("parallel", "parallel", "arbitrary")),
        name="mlstm_bwd" if reverse else "mlstm_fwd",
    )(p, p, p, g_rows, g_cols, g_rows, g_cols)


def _odd_post_kernel(hf, hb, o_lo, o_hi, nw, nb, y_ref):
    half_heads = C_HEADS // 2
    for hd in range(C_HEADS):
        cols = slice(hd * C_V_HEAD, (hd + 1) * C_V_HEAD)
        x = hf[:, cols] + hb[:, cols]
        xc = x - jnp.mean(x, axis=1, keepdims=True)
        var = jnp.mean(xc * xc, axis=1, keepdims=True)
        hn = xc * lax.rsqrt(var + MLSTM_NORM_EPS) * nw[:, cols] + nb[:, cols]
        o_ref = o_lo if hd < half_heads else o_hi
        oc = slice((hd % half_heads) * C_V_HEAD, (hd % half_heads + 1) * C_V_HEAD)
        y_ref[:, cols] = (hn * jax.nn.sigmoid(o_ref[:, oc])).astype(BF16)


def odd_post_call(hf, hb, p, nw, nb, *, n_rows, col_o):
    vw = hf.shape[1]
    tm = LN_ROWS
    row = pl.BlockSpec((tm, vw), lambda i: (i, 0))
    par = pl.BlockSpec((1, vw), lambda i: (0, 0))
    return pl.pallas_call(
        _odd_post_kernel,
        out_shape=jax.ShapeDtypeStruct((n_rows, vw), BF16),
        grid=(n_rows // tm,),
        in_specs=[row, row, pl.BlockSpec((tm, vw // 2), lambda i: (i, col_o)),
                  pl.BlockSpec((tm, vw // 2), lambda i: (i, col_o + 1)), par, par],
        out_specs=row,
        compiler_params=_cparams(("parallel",)),
        name="odd_post",
    )(hf, hb, p, p, nw, nb)


def _layer_norm(z, g, b):
    zc = z - jnp.mean(z, axis=1, keepdims=True)
    var = jnp.mean(zc * zc, axis=1, keepdims=True)
    return zc * lax.rsqrt(var + LN_EPS) * g + b


def _route(logits_t, bias):
    aff = [jax.nn.sigmoid(logits_t[e:e + 1, :]) for e in range(N_EXPERTS)]
    biased = [aff[e] + bias[e:e + 1, :] for e in range(N_EXPERTS)]
    best_g = best_v = None
    for g in range(N_GROUPS):
        m = biased[g * EXPERTS_PER_GROUP:(g + 1) * EXPERTS_PER_GROUP]
        pair = None
        for i in range(EXPERTS_PER_GROUP):
            for j in range(i + 1, EXPERTS_PER_GROUP):
                hi = jnp.maximum(m[i], m[j])
                lo_ = jnp.minimum(m[i], m[j])
                s = hi + lo_
                pair = s if pair is None else jnp.maximum(pair, s)
        if best_v is None:
            best_v, best_g = pair, jnp.zeros(pair.shape, jnp.int32)
        else:
            upd = pair > best_v
            best_g = jnp.where(upd, g, best_g)
            best_v = jnp.where(upd, pair, best_v)
    ids, sels = [], []
    taken = None
    for _ in range(TOP_K):
        cur_v = cur_i = cur_a = None
        for e in range(N_EXPERTS):
            ok = best_g == (e // EXPERTS_PER_GROUP)
            if taken is not None:
                ok = jnp.logical_and(ok, taken != e)
            val = jnp.where(ok, biased[e], -jnp.inf)
            if cur_v is None:
                cur_v, cur_i, cur_a = val, jnp.zeros(val.shape, jnp.int32), aff[e]
            else:
                upd = val > cur_v
                cur_i = jnp.where(upd, e, cur_i)
                cur_a = jnp.where(upd, aff[e], cur_a)
                cur_v = jnp.where(upd, val, cur_v)
        ids.append(cur_i)
        sels.append(cur_a)
        taken = cur_i
    tot = sels[0] + sels[1]
    return ids, [sels[0] / tot, sels[1] / tot]


def _ln_route_kernel(lat_ref, y_ref, gate_ref, sh_ref, sc_ref, g_ref, b_ref, rw_ref, rb_ref,
                     lat_o, h_o, id_o, gt_o):
    z = ALPHA * lat_ref[...] + gate_ref[...] * y_ref[...]
    ln = _layer_norm(z, g_ref[...], b_ref[...])
    lat_o[...] = ln
    h = ln * (1.0 + sc_ref[...]) + sh_ref[...]
    half = h.shape[1] // 2
    bits = lax.bitcast_convert_type(h.astype(BF16).astype(F32), jnp.uint32)
    h_o[...] = (bits[:, :half] >> 16) | bits[:, half:]
    logits_t = lax.dot_general(rw_ref[...], h, (((1,), (1,)), ((), ())), preferred_element_type=F32,
                               precision=lax.Precision.HIGHEST)
    ids, gts = _route(logits_t, rb_ref[...])
    id_o[...] = jnp.concatenate(ids, axis=0)
    gt_o[...] = jnp.concatenate(gts, axis=0)


def ln_route_call(lat, y, mods, norm_g, norm_b, layer, router_wt, router_b, *, n_rows, rows_per_mod, n_mod_rows):
    d = lat.shape[1]
    tm = LN_ROWS

    def midx(i):
        return jnp.minimum(i * tm // rows_per_mod, n_mod_rows - 1)

    row = pl.BlockSpec((tm, d), lambda i: (i, 0))
    packed = pl.BlockSpec((tm, d // 2), lambda i: (i, 0))
    mod = lambda k: pl.BlockSpec((None, 1, d), lambda i: (midx(i), 0, k))
    nrm = pl.BlockSpec((None, None, 1, d), lambda i: (layer, 0, 0, 0))
    full = lambda shape: pl.BlockSpec(shape, lambda i: (0,) * len(shape))
    sel = pl.BlockSpec((TOP_K, tm), lambda i: (0, i))
    return pl.pallas_call(
        _ln_route_kernel,
        out_shape=(jax.ShapeDtypeStruct((n_rows, d), F32), jax.ShapeDtypeStruct((n_rows, d // 2), jnp.uint32),
                   jax.ShapeDtypeStruct((TOP_K, n_rows), jnp.int32), jax.ShapeDtypeStruct((TOP_K, n_rows), F32)),
        grid=(n_rows // tm,),
        in_specs=[row, row, mod(2), mod(3), mod(4), nrm, nrm, full((N_EXPERTS, d)), full((N_EXPERTS, 1))],
        out_specs=(row, packed, sel, sel),
        compiler_params=_cparams(("parallel",)),
        name="ln_route",
    )(lat, y, mods, mods, mods, norm_g, norm_b, router_wt, router_b)


def _ln_combine_kernel(lat_ref, y0_ref, y1_ref, gt_ref, gate_ref, g_ref, b_ref, lat_o):
    gt = gt_ref[...]
    f = y0_ref[...] * gt[:, 0:1] + y1_ref[...] * gt[:, 1:2]
    z = ALPHA * lat_ref[...] + gate_ref[...] * f
    lat_o[...] = _layer_norm(z, g_ref[...], b_ref[...])


def ln_combine_call(lat, y_slots, gates_t, mods, norm_g, norm_b, layer, *, n_rows, rows_per_mod, n_mod_rows):
    d = lat.shape[1]
    tm = LN_ROWS
    per_k = n_rows // tm

    def midx(i):
        return jnp.minimum(i * tm // rows_per_mod, n_mod_rows - 1)

    row = pl.BlockSpec((tm, d), lambda i: (i, 0))
    nrm = pl.BlockSpec((None, None, 1, d), lambda i: (layer, 1, 0, 0))
    return pl.pallas_call(
        _ln_combine_kernel,
        out_shape=jax.ShapeDtypeStruct((n_rows, d), F32),
        grid=(per_k,),
        in_specs=[row, row, pl.BlockSpec((tm, d), lambda i: (per_k + i, 0)),
                  pl.BlockSpec((tm, TOP_K), lambda i: (i, 0)),
                  pl.BlockSpec((None, 1, d), lambda i: (midx(i), 0, 5)), nrm, nrm],
        out_specs=row,
        compiler_params=_cparams(("parallel",)),
        name="ln_combine",
    )(lat, y_slots, y_slots, gates_t, mods, norm_g, norm_b)


def _moe_kernel(nu_ref, be_ref, jw_ref, src_cur, src_nxt, dst_prv, h_hbm, wg_ref, wu_ref, wd_ref, y_hbm,
                xbuf, acc, gsem, ssem, *, n_j, n_slots):
    del be_ref, jw_ref
    i = pl.program_id(0)
    j = pl.program_id(1)
    buf = i % 2
    other = 1 - buf
    n_used = nu_ref[0]
    tm, half = xbuf.shape[1], xbuf.shape[2]
    d = acc.shape[2]
    per_step = tm // n_j

    def row_in(tok, r, b):
        return pltpu.make_async_copy(h_hbm.at[pl.ds(tok, 1)], xbuf.at[b, pl.ds(r, 1)], gsem.at[b])

    def row_out(r, slot, b):
        return pltpu.make_async_copy(acc.at[b, pl.ds(r, 1)], y_hbm.at[pl.ds(slot, 1)], ssem.at[b])

    def send_previous(r):
        slot = jnp.where(i > 0, dst_prv[0, r], n_slots + r)
        row_out(r, slot, other).start()

    @pl.when((i == 0) & (j == 0))
    def _():
        acc[1] = jnp.zeros((tm, d), F32)

        def body(r, carry):
            row_in(src_cur[0, r], r, 0).start()
            return carry
        lax.fori_loop(0, tm, body, 0, unroll=8)

    @pl.when((j == 0) & (i <= n_used))
    def _():
        pltpu.make_async_copy(xbuf.at[buf], xbuf.at[buf], gsem.at[buf]).wait()

    @pl.when((j == 0) & (i < n_used))
    def _():
        acc[buf] = jnp.zeros((tm, d), F32)

    @pl.when(i < n_used)
    def _():
        tf = wg_ref.shape[1]
        g = jnp.zeros((tm, tf), F32)
        u = jnp.zeros((tm, tf), F32)
        for c0 in range(0, half, MOE_KCHUNK):
            xp = xbuf[buf, :, c0:c0 + MOE_KCHUNK]
            lo = lax.bitcast_convert_type(xp << 16, F32)
            hi = lax.bitcast_convert_type(xp & jnp.uint32(0xFFFF0000), F32)
            g = g + _bdot(lo, wg_ref[c0:c0 + MOE_KCHUNK, :]) + _bdot(hi, wg_ref[half + c0:half + c0 + MOE_KCHUNK, :])
            u = u + _bdot(lo, wu_ref[c0:c0 + MOE_KCHUNK, :]) + _bdot(hi, wu_ref[half + c0:half + c0 + MOE_KCHUNK, :])
        hdn = (g * jax.nn.sigmoid(g) * u).astype(BF16)
        for c0 in range(0, d, MOE_NCHUNK):
            acc[buf, :, c0:c0 + MOE_NCHUNK] += _bdot(hdn, wd_ref[:, c0:c0 + MOE_NCHUNK])
        for rr in range(per_step):
            r = j * per_step + rr
            row_in(src_nxt[0, r], r, other).start()
            send_previous(r)

    @pl.when(i == n_used)
    def _():
        for rr in range(per_step):
            send_previous(j * per_step + rr)

    @pl.when((j == n_j - 1) & (i <= n_used))
    def _():
        pltpu.make_async_copy(acc.at[other], acc.at[other], ssem.at[other]).wait()


def moe_call(h_packed, n_used, block_e, j_of, src_tok, dst_slot, w_gate, w_up, w_down, layer, *, n_slots):
    d = 2 * h_packed.shape[1]
    f = w_gate.shape[3]
    tm, tf = MOE_ROWS, MOE_FTILE
    n_blocks = block_e.shape[0]
    n_j = f // tf
    assert tm % n_j == 0
    idx = lambda fn: pl.BlockSpec((None, 1, tm), fn, memory_space=pltpu.SMEM)
    grid_spec = pltpu.PrefetchScalarGridSpec(
        num_scalar_prefetch=3,
        grid=(n_blocks, n_j),
        in_specs=[
            idx(lambda i, j, nu, be, jw: (i, 0, 0)),
            idx(lambda i, j, nu, be, jw: (jnp.minimum(i + 1, n_blocks - 1), 0, 0)),
            idx(lambda i, j, nu, be, jw: (jnp.maximum(i - 1, 0), 0, 0)),
            pl.BlockSpec(memory_space=pl.ANY),
            pl.BlockSpec((None, None, d, tf), lambda i, j, nu, be, jw: (layer, be[i], 0, jw[i, j])),
            pl.BlockSpec((None, None, d, tf), lambda i, j, nu, be, jw: (layer, be[i], 0, jw[i, j])),
            pl.BlockSpec((None, None, tf, d), lambda i, j, nu, be, jw: (layer, be[i], jw[i, j], 0)),
        ],
        out_specs=pl.BlockSpec(memory_space=pl.ANY),
        scratch_shapes=[pltpu.VMEM((2, tm, d // 2), jnp.uint32), pltpu.VMEM((2, tm, d), F32),
                        pltpu.SemaphoreType.DMA((2,)), pltpu.SemaphoreType.DMA((2,))],
    )
    return pl.pallas_call(
        functools.partial(_moe_kernel, n_j=n_j, n_slots=n_slots),
        out_shape=jax.ShapeDtypeStruct((n_slots + tm, d), F32),
        grid_spec=grid_spec,
        compiler_params=_cparams(("arbitrary", "arbitrary")),
        name="moe_experts",
    )(n_used, block_e, j_of, src_tok, src_tok, dst_slot, h_packed, w_gate, w_up, w_down)


def moe_ffn(h_packed, ids, w_gate, w_up, w_down, layer):
    n_tok = h_packed.shape[0]
    n_slots = n_tok * TOP_K
    tm = MOE_ROWS
    i32 = jnp.int32
    flat_e = ids.reshape(n_slots)
    order = jnp.argsort(flat_e, stable=True).astype(i32)
    counts = jnp.sum((flat_e[:, None] == jnp.arange(N_EXPERTS, dtype=i32)[None, :]).astype(i32), axis=0)
    start = jnp.cumsum(counts) - counts
    padded = (counts + tm - 1) // tm * tm
    pad_end = jnp.cumsum(padded)
    pad_start = pad_end - padded
    n_blocks = -(-n_slots // tm) + N_EXPERTS + 1
    pos = jnp.arange(n_blocks * tm, dtype=i32)
    e_pos = jnp.minimum(jnp.sum((pos[:, None] >= pad_end[None, :]).astype(i32), axis=1), N_EXPERTS - 1)
    r_pos = pos - pad_start[e_pos]
    valid = r_pos < counts[e_pos]
    slot = order[jnp.clip(start[e_pos] + r_pos, 0, n_slots - 1)]
    src_tok = jnp.where(valid, slot % n_tok, 0).reshape(n_blocks, 1, tm)
    dst_slot = jnp.where(valid, slot, n_slots + pos % tm).reshape(n_blocks, 1, tm)
    n_used = (pad_end[-1] // tm).astype(i32)
    blk = jnp.arange(n_blocks, dtype=i32)
    block_e = e_pos.reshape(n_blocks, tm)[:, 0]
    block_e = jnp.where(blk < n_used, block_e, block_e[jnp.maximum(n_used - 1, 0)]).astype(i32)
    n_j = w_gate.shape[3] // MOE_FTILE
    tiles = jnp.arange(n_j, dtype=i32)[None, :]
    j_of = jnp.where((blk % 2 == 0)[:, None], tiles, n_j - 1 - tiles)
    j_of = jnp.where((blk < n_used)[:, None], j_of, j_of[jnp.maximum(n_used - 1, 0), n_j - 1]).astype(i32)
    return moe_call(h_packed, n_used.reshape(1), block_e, j_of, src_tok.astype(i32), dst_slot.astype(i32),
                    w_gate, w_up, w_down, layer, n_slots=n_slots)


def _pad_cols(w, width):
    return jnp.pad(w, [(0, 0)] * (w.ndim - 1) + [(0, width - w.shape[-1])])


def _even_weights(ev_w_in, rwkv_mu, aw, bw):
    rk = 3 * aw
    lo0 = rk
    pieces = [DECAY_LORA, DECAY_LORA, ICL_LORA, ICL_LORA]

    def regroup(t):
        out = [t[..., rk + 2 * (DECAY_LORA + ICL_LORA) + GATE_LORA:],
               t[..., :rk]]
        off = lo0
        for wdt in pieces:
            out.append(_pad_cols(t[..., off:off + wdt], LANES))
            off += wdt
        out.append(t[..., off:off + GATE_LORA])
        used = 4 * LANES + GATE_LORA
        out.append(jnp.zeros(t.shape[:-1] + (LORA_PAD - used,), t.dtype))
        return jnp.concatenate(out, axis=-1)

    w = regroup(ev_w_in).astype(BF16)
    rwkv_cols = rk + 2 * (DECAY_LORA + ICL_LORA) + GATE_LORA
    mu_full = jnp.concatenate([rwkv_mu, jnp.zeros(rwkv_mu.shape[:-1] + (2 * bw,), rwkv_mu.dtype)], axis=-1)
    mu = regroup(mu_full)
    del rwkv_cols
    return w, mu


def kernel(x, c, ctx, c_ctx, ada_w, ada_b, norm_g, norm_b,
           ev_w_in, ev_w_out, rwkv_mu, rwkv_w0, rwkv_w2, rwkv_a0, rwkv_a2, rwkv_g2,
           rwkv_k_k, rwkv_k_a, rwkv_r_k, rwkv_lnx_w, rwkv_lnx_b,
           lru_conv_w, lru_conv_b, lru_wa, lru_ba, lru_wx, lru_bx, lru_lam,
           od_w_in, od_w_out, mlstm_ig_b, mlstm_fg_b, mlstm_norm_w, mlstm_norm_b,
           router_w, router_b, moe_w_gate, moe_w_up, moe_w_down):
    bsz, s_len, dim = x.shape
    n_ctx = ctx.shape[1]
    n_lat_rows = bsz * s_len
    n_all_rows = n_lat_rows + bsz * n_ctx
    aw = rwkv_k_k.shape[1]
    bw = lru_conv_b.shape[1]
    assert s_len % SCAN_ROWS == 0 and n_ctx == SCAN_ROWS and n_all_rows % MM_ROWS == 0
    assert (bsz * n_ctx) % MM_ROWS == 0 and s_len % MM_ROWS == 0 and bsz + 1 <= SUBLANES
    seq = dict(n_lat_rows=n_lat_rows, lat_len=s_len, ctx_len=n_ctx)
    scan = dict(batch=bsz, lat_len=s_len, ctx_len=n_ctx)
    modk = dict(rows_per_mod=s_len, n_mod_rows=bsz + 1)

    stream = embed_call(x.reshape(n_lat_rows, dim), ctx.reshape(bsz * n_ctx, dim), s_len)
    cvec = jnp.concatenate([c, c_ctx[None, :], jnp.zeros((SUBLANES - bsz - 1, dim), c.dtype)], axis=0)
    ada_b3 = ada_b[:, None, :]
    norm_g4 = norm_g[:, :, None, :]
    norm_b4 = norm_b[:, :, None, :]
    router_wt = router_w.T
    router_b2 = router_b[:, None]

    for layer in range(DEPTH):
        last = layer == DEPTH - 1
        i = layer // 2
        mods = adaln_call(cvec, ada_w, ada_b3, layer).reshape(SUBLANES, 1, N_MOD * dim)
        if layer % 2 == 0:
            w_in, mu = _even_weights(ev_w_in, rwkv_mu, aw, bw)
            n_in = w_in.shape[2]
            p = matmul_mod_call(stream, mods, w_in, i, tn=1024, **modk)
            col_x, col_gb = 0, bw // bw
            col_r, col_k, col_v = 2 * bw // aw, 2 * bw // aw + 1, 2 * bw // aw + 2
            col_l = (2 * bw + 3 * aw) // LORA_PAD
            assert (2 * bw + 3 * aw) % LORA_PAD == 0 and n_in == 2 * bw + 3 * aw + LORA_PAD
            mu_i = mu[i]
            o_r = 2 * bw
            prm = dict(
                mu_r=mu_i[None, o_r:o_r + aw], mu_k=mu_i[None, o_r + aw:o_r + 2 * aw],
                mu_v=mu_i[None, o_r + 2 * aw:o_r + 3 * aw], mu_l=mu_i[None, o_r + 3 * aw:],
                w0=rwkv_w0[i], w2=_pad_rows(rwkv_w2[i], LANES).astype(BF16),
                a0=rwkv_a0[i], a2=_pad_rows(rwkv_a2[i], LANES).astype(BF16),
                g2=rwkv_g2[i].astype(BF16), k_k=rwkv_k_k[i][None, :], k_a=rwkv_k_a[i][None, :])
            r_s, v_s, kk, lw, kd, bb, g = rwkv_prep_call(p, prm, col_r=col_r, col_k=col_k, col_v=col_v,
                                                          col_l=col_l, **seq)
            nb = bw // LRU_BLOCK_W
            wcat = jnp.concatenate([lru_wa[i], lru_wx[i]], axis=-1).astype(BF16)
            bcat = jnp.stack([lru_ba[i].reshape(2, nb, LRU_BLOCK_W), lru_bx[i].reshape(2, nb, LRU_BLOCK_W)],
                             axis=2).reshape(2, 1, 2 * bw)
            lprm = dict(conv_w=lru_conv_w[i], conv_b=lru_conv_b[i][None, :], wcat=wcat, bcat=bcat,
                        lsl=(RGLRU_C * jax.nn.log_sigmoid(lru_lam[i]))[:, None, :])
            yf = rwkv_scan_call(r_s, lw, kd, v_s, kk, bb, reverse=False, **scan)
            yb = rwkv_scan_call(r_s, lw, kd, v_s, kk, bb, reverse=True, **scan)
            hf = lru_call(p, lprm, reverse=False, col_x=col_x, **scan)
            hb = lru_call(p, lprm, reverse=True, col_x=col_x, **scan)
            pprm = dict(r_k=rwkv_r_k[i].reshape(1, aw), lnx_w=rwkv_lnx_w[i][None, :], lnx_b=rwkv_lnx_b[i][None, :])
            ymix = even_post_call(yf, yb, r_s, v_s, kd, g, hf, hb, p, pprm, col_gb=col_gb)
            n_rows = n_all_rows if not last else n_lat_rows
            y = matmul_call(ymix, ev_w_out.astype(BF16), i, tn=1024, n_rows=n_rows)
        else:
            qk_w = C_HEADS * C_QK_HEAD
            v_w = C_HEADS * C_V_HEAD
            main = 2 * qk_w + 2 * v_w
            w_gate = _pad_cols(od_w_in[..., main:], LANES).astype(BF16)
            p = matmul_mod_call(stream, mods, cast_cols_call(od_w_in, i, main), 0, tn=1024, **modk)
            gp = matmul_mod_call(stream, mods, w_gate, i, tn=LANES, **modk)[:, :4 * C_HEADS]
            gp = gp.reshape(n_all_rows, 2, 2, C_HEADS)
            g_rows, g_cols = _mlstm_gates(gp, mlstm_ig_b[i], mlstm_fg_b[i])
            cols = dict(col_q=0, col_k=(qk_w + v_w) // C_QK_HEAD, col_v=(2 * qk_w + v_w) // C_V_HEAD)
            hf = mlstm_call(p, g_rows, g_cols, reverse=False, **scan, **cols)
            hb = mlstm_call(p, g_rows, g_cols, reverse=True, **scan, **cols)
            n_rows = n_all_rows if not last else n_lat_rows
            ymix = odd_post_call(hf, hb, p, mlstm_norm_w[i][None, :], mlstm_norm_b[i][None, :],
                                 n_rows=n_rows, col_o=qk_w // (v_w // 2))
            y = matmul_call(ymix, od_w_out.astype(BF16), i, tn=1024, n_rows=n_rows)
        n_rows = n_all_rows if not last else n_lat_rows
        stream1, h_moe, ids, gates = ln_route_call(stream, y, mods, norm_g4, norm_b4, layer, router_wt, router_b2,
                                                   n_rows=n_rows, **modk)
        y_slots = moe_ffn(h_moe, ids, moe_w_gate, moe_w_up, moe_w_down, layer)
        stream = ln_combine_call(stream1, y_slots, gates.T, mods, norm_g4, norm_b4, layer, n_rows=n_rows, **modk)
    return stream[:n_lat_rows].reshape(bsz, s_len, dim)


def _pad_rows(w, rows):
    return jnp.pad(w, ((0, 0), (0, rows - w.shape[1]), (0, 0)))


def _mlstm_gates(gp, ig_b, fg_b):
    n_rows = gp.shape[0]
    chunk = MLSTM_CHUNK
    n_sub = SCAN_ROWS // chunk
    outs = []
    for d in range(2):
        ig = GATE_CAP * jnp.tanh((gp[:, d, 0] + ig_b[d]) / GATE_CAP)
        lf = jax.nn.log_sigmoid(GATE_CAP * jnp.tanh((gp[:, d, 1] + fg_b[d]) / GATE_CAP))
        lf = lf.reshape(n_rows // chunk, chunk, C_HEADS)
        b = jnp.cumsum(lf[:, ::-1], axis=1)[:, ::-1] if d == 1 else jnp.cumsum(lf, axis=1)
        both = jnp.stack([b, ig.reshape(n_rows // chunk, chunk, C_HEADS)], axis=0)
        outs.append(both)
    g = jnp.stack(outs, axis=0)
    g = g.reshape(2, 2, n_rows // SCAN_ROWS, n_sub, chunk, C_HEADS)
    g_rows = jnp.transpose(g, (0, 1, 5, 2, 3, 4))
    g_cols = jnp.transpose(g, (0, 1, 5, 2, 4, 3))
    return g_rows, g_cols
```

```python
import functools
import math

import jax
import jax.numpy as jnp
from jax import lax
from jax.experimental import pallas as pl
from jax.experimental.pallas import tpu as pltpu

F32 = jnp.float32
BF16 = jnp.bfloat16

DEPTH = 2
N_MOD = 6
ALPHA = (2 * DEPTH) ** 0.25
LN_EPS = 1e-5
HEAD_DIM = 64
DECAY_LORA = 96
ICL_LORA = 96
GATE_LORA = 256
LNX_EPS = 64e-5
LRU_BLOCK_W = 128
CONV_W = 4
RGLRU_C = 8.0
C_HEADS = 8
C_QK_HEAD = 256
C_V_HEAD = 512
GATE_CAP = 15.0
MLSTM_NORM_EPS = 1e-6
N_EXPERTS = 16
N_GROUPS = 4
EXPERTS_PER_GROUP = N_EXPERTS // N_GROUPS
TOP_K = 2
POS_BASE = 10000.0
GRID_W = 64

LANES = 128
SUBLANES = 8
SCAN_ROWS = 256
CHUNK = 64
MLSTM_CHUNK = 128
SCAN_PAIRS = 4
PREP_ROWS = 128
LN_ROWS = 256
MM_ROWS = 512
MOE_ROWS = 512
MOE_FTILE = 256
MOE_NCHUNK = 1024
MOE_KCHUNK = 512
LORA_PAD = 1024
VMEM_LIMIT = 60 * 1024 * 1024


def _cparams(sem):
    return pltpu.CompilerParams(dimension_semantics=sem, vmem_limit_bytes=VMEM_LIMIT)


def _bdot(a, b):
    return jnp.dot(a.astype(BF16), b.astype(BF16), preferred_element_type=F32)


def _bdot_nt(a, b):
    return lax.dot_general(a.astype(BF16), b.astype(BF16), (((1,), (1,)), ((), ())),
                           preferred_element_type=F32)


def _bdot_tn(a, b):
    return lax.dot_general(a.astype(BF16), b.astype(BF16), (((0,), (0,)), ((), ())),
                           preferred_element_type=F32)


def _softplus(z):
    return jnp.maximum(z, 0.0) + jnp.log1p(jnp.exp(-jnp.abs(z)))


def _adaln_kernel(c_ref, w_ref, b_ref, o_ref):
    c = c_ref[...]
    s = c * jax.nn.sigmoid(c)
    o_ref[...] = _bdot(s, w_ref[...]) + b_ref[...]


def adaln_call(cvec, ada_w, ada_b, layer, tn=512):
    rows, d = cvec.shape
    n = ada_w.shape[2]
    return pl.pallas_call(
        _adaln_kernel,
        out_shape=jax.ShapeDtypeStruct((rows, n), F32),
        grid=(n // tn,),
        in_specs=[
            pl.BlockSpec((rows, d), lambda j: (0, 0)),
            pl.BlockSpec((None, d, tn), lambda j: (layer, 0, j)),
            pl.BlockSpec((None, 1, tn), lambda j: (layer, 0, j)),
        ],
        out_specs=pl.BlockSpec((rows, tn), lambda j: (0, j)),
        compiler_params=_cparams(("parallel",)),
        name="adaln",
    )(cvec, ada_w, ada_b)


def _embed_kernel(x_ref, ctx_ref, sr, cr, sc, cc, o_ref, *, n_lat_tiles, tiles_per_tab):
    i = pl.program_id(0)
    tm = o_ref.shape[0]
    g_rows = tm // GRID_W

    @pl.when(i < n_lat_tiles)
    def _():
        base = (i % tiles_per_tab) * g_rows

        def by_row(t_ref):
            return jnp.concatenate([jnp.broadcast_to(t_ref[pl.ds(base + g, 1), :], (GRID_W, t_ref.shape[1]))
                                    for g in range(g_rows)], axis=0)

        def by_col(t_ref):
            return jnp.concatenate([t_ref[...]] * g_rows, axis=0)

        o_ref[...] = x_ref[...] + jnp.concatenate([by_row(sr), by_row(cr), by_col(sc), by_col(cc)], axis=1)

    @pl.when(i >= n_lat_tiles)
    def _():
        o_ref[...] = ctx_ref[...]


def embed_call(x2d, ctx2d, seq_len):
    n_lat, d = x2d.shape
    n_ctx = ctx2d.shape[0]
    tm = LN_ROWS
    quarter = d // 4
    g_rows = tm // GRID_W
    assert tm % GRID_W == 0 and SUBLANES % g_rows == 0 and seq_len % (SUBLANES * GRID_W) == 0
    omega = POS_BASE ** (-jnp.arange(quarter, dtype=F32) / quarter)
    ang_r = jnp.arange(seq_len // GRID_W, dtype=F32)[:, None] * omega[None, :]
    ang_c = jnp.arange(GRID_W, dtype=F32)[:, None] * omega[None, :]
    n_lat_tiles = n_lat // tm
    tiles_per_tab = SUBLANES // g_rows
    tabs_per_seq = seq_len // GRID_W // SUBLANES
    tab = pl.BlockSpec((SUBLANES, quarter), lambda i: ((i // tiles_per_tab) % tabs_per_seq, 0))
    col = pl.BlockSpec((GRID_W, quarter), lambda i: (0, 0))
    return pl.pallas_call(
        functools.partial(_embed_kernel, n_lat_tiles=n_lat_tiles, tiles_per_tab=tiles_per_tab),
        out_shape=jax.ShapeDtypeStruct((n_lat + n_ctx, d), F32),
        grid=((n_lat + n_ctx) // tm,),
        in_specs=[pl.BlockSpec((tm, d), lambda i: (jnp.minimum(i, n_lat_tiles - 1), 0)),
                  pl.BlockSpec((tm, d), lambda i: (jnp.maximum(i - n_lat_tiles, 0), 0)),
                  tab, tab, col, col],
        out_specs=pl.BlockSpec((tm, d), lambda i: (i, 0)),
        compiler_params=_cparams(("parallel",)),
        name="embed",
    )(x2d, ctx2d, jnp.sin(ang_r), jnp.cos(ang_r), jnp.sin(ang_c), jnp.cos(ang_c))


def _mm_mod_kernel(x_ref, sh_ref, sc_ref, w_ref, o_ref, xb_ref):
    @pl.when(pl.program_id(1) == 0)
    def _():
        xb_ref[...] = (x_ref[...] * (1.0 + sc_ref[...]) + sh_ref[...]).astype(BF16)

    o_ref[...] = jnp.dot(xb_ref[...], w_ref[...], preferred_element_type=F32)


def matmul_mod_call(x, mods, w, layer_w, *, rows_per_mod, n_mod_rows, tn, n_rows=None, n_cols=None):
    r, d = x.shape
    n = w.shape[2] if n_cols is None else n_cols
    r = r if n_rows is None else n_rows
    tm = MM_ROWS

    def midx(i):
        return jnp.minimum(i * tm // rows_per_mod, n_mod_rows - 1)

    return pl.pallas_call(
        _mm_mod_kernel,
        out_shape=jax.ShapeDtypeStruct((r, n), F32),
        grid=(r // tm, n // tn),
        in_specs=[
            pl.BlockSpec((tm, d), lambda i, j: (i, 0)),
            pl.BlockSpec((None, 1, d), lambda i, j: (midx(i), 0, 0)),
            pl.BlockSpec((None, 1, d), lambda i, j: (midx(i), 0, 1)),
            pl.BlockSpec((None, d, tn), lambda i, j: (layer_w, 0, j)),
        ],
        out_specs=pl.BlockSpec((tm, tn), lambda i, j: (i, j)),
        scratch_shapes=[pltpu.VMEM((tm, d), BF16)],
        compiler_params=_cparams(("parallel", "arbitrary")),
        name="proj_in",
    )(x, mods, mods, w)


def _mm_kernel(x_ref, w_ref, o_ref):
    o_ref[...] = jnp.dot(x_ref[...], w_ref[...], preferred_element_type=F32)


def matmul_call(x, w, layer_w, *, tn, n_rows=None):
    r, k = x.shape
    n = w.shape[2]
    r = r if n_rows is None else n_rows
    tm = MM_ROWS
    return pl.pallas_call(
        _mm_kernel,
        out_shape=jax.ShapeDtypeStruct((r, n), F32),
        grid=(r // tm, n // tn),
        in_specs=[
            pl.BlockSpec((tm, k), lambda i, j: (i, 0)),
            pl.BlockSpec((None, k, tn), lambda i, j: (layer_w, 0, j)),
        ],
        out_specs=pl.BlockSpec((tm, tn), lambda i, j: (i, j)),
        compiler_params=_cparams(("parallel", "arbitrary")),
        name="proj_out",
    )(x, w)


def _tile_flags(i, tm, n_lat_rows, lat_len, ctx_len):
    row = i * tm
    in_lat = row < n_lat_rows
    pos = jnp.where(in_lat, row % lat_len, (row - n_lat_rows) % ctx_len)
    seq = jnp.where(in_lat, lat_len, ctx_len)
    return pos == 0, pos + tm == seq


def _row_shift(c, prev_row, next_row):
    tm = c.shape[0]
    rows = lax.broadcasted_iota(jnp.int32, c.shape, 0)
    xp = jnp.where(rows == 0, prev_row, pltpu.roll(c, 1, axis=0))
    xn = jnp.where(rows == tm - 1, next_row, pltpu.roll(c, tm - 1, axis=0))
    return xp, xn


def _head_ones():
    r = lax.broadcasted_iota(jnp.int32, (LANES, LANES), 0) // HEAD_DIM
    c = lax.broadcasted_iota(jnp.int32, (LANES, LANES), 1) // HEAD_DIM
    return (r == c).astype(BF16)


def _head_sum(x, ones):
    w = x.shape[1]
    parts = [jnp.dot(x[:, j:j + LANES].astype(BF16), ones, preferred_element_type=F32)
             for j in range(0, w, LANES)]
    return jnp.concatenate(parts, axis=1)


def _halo_specs(tm, width, col, n_rows):
    per = tm // SUBLANES
    last = n_rows // SUBLANES - 1
    return [
        pl.BlockSpec((tm, width), lambda i: (i, col)),
        pl.BlockSpec((SUBLANES, width), lambda i: (jnp.maximum(i * per - 1, 0), col)),
        pl.BlockSpec((SUBLANES, width), lambda i: (jnp.minimum((i + 1) * per, last), col)),
    ]


def _rwkv_prep_kernel(rc, rp, rn, kc, kp, kn, vc, vp, vn, lc, lp, ln,
                      mu_r, mu_k, mu_v, mu_l, w0, w2, a0, a2, g2, kk_s, ka_s,
                      r_o, v_o, kk_o, lw_o, kd_o, bb_o, g_o, *, tm, n_lat_rows, lat_len, ctx_len):
    first, last = _tile_flags(pl.program_id(0), tm, n_lat_rows, lat_len, ctx_len)

    def shifted(c_ref, p_ref, n_ref, mu_ref):
        c = c_ref[...]
        prev_row = jnp.where(first, 0.0, p_ref[SUBLANES - 1:SUBLANES, :])
        next_row = jnp.where(last, 0.0, n_ref[0:1, :])
        xp, xn = _row_shift(c, prev_row, next_row)
        return c + mu_ref[...] * (0.5 * (xp + xn) - c)

    r = shifted(rc, rp, rn, mu_r)
    k = shifted(kc, kp, kn, mu_k)
    v = shifted(vc, vp, vn, mu_v)
    lo = shifted(lc, lp, ln, mu_l)
    r_o[...] = r
    v_o[...] = v

    ones = _head_ones()
    kk = k * kk_s[...]
    ss = _head_sum(kk * kk, ones)
    kk = kk * lax.rsqrt(jnp.maximum(ss, 1e-24))
    kk_o[...] = kk

    for d in range(2):
        wd = lo[:, d * LANES:(d + 1) * LANES]
        ad = lo[:, (2 + d) * LANES:(3 + d) * LANES]
        z = w0[d:d + 1, :] + _bdot(jnp.tanh(wd), w2[d])
        log_w = -_softplus(-z) - 0.5
        lw_o[d] = -jnp.exp(log_w)
        icl = jax.nn.sigmoid(a0[d:d + 1, :] + _bdot(ad, a2[d]))
        kd_o[d] = k * (1.0 + (icl - 1.0) * ka_s[...])
        bb_o[d] = kk * icl
    gd = lo[:, 4 * LANES:4 * LANES + GATE_LORA]
    g_o[...] = _bdot(jax.nn.sigmoid(gd), g2[...])


def rwkv_prep_call(p, prm, *, n_lat_rows, lat_len, ctx_len, col_r, col_k, col_v, col_l):
    r_rows = p.shape[0]
    tm = PREP_ROWS
    aw = prm["mu_r"].shape[1]
    full = lambda shape: pl.BlockSpec(shape, lambda i: (0,) * len(shape))
    in_specs = (_halo_specs(tm, aw, col_r, r_rows) + _halo_specs(tm, aw, col_k, r_rows)
                + _halo_specs(tm, aw, col_v, r_rows) + _halo_specs(tm, LORA_PAD, col_l, r_rows)
                + [full((1, aw)), full((1, aw)), full((1, aw)), full((1, LORA_PAD)),
                   full((2, aw)), full((2, LANES, aw)), full((2, aw)), full((2, LANES, aw)),
                   full((GATE_LORA, aw)), full((1, aw)), full((1, aw))])
    row_spec = pl.BlockSpec((tm, aw), lambda i: (i, 0))
    dir_spec = pl.BlockSpec((2, tm, aw), lambda i: (0, i, 0))
    one = jax.ShapeDtypeStruct((r_rows, aw), F32)
    two = jax.ShapeDtypeStruct((2, r_rows, aw), F32)
    kern = functools.partial(_rwkv_prep_kernel, tm=tm, n_lat_rows=n_lat_rows, lat_len=lat_len, ctx_len=ctx_len)
    return pl.pallas_call(
        kern,
        out_shape=(one, one, one, two, two, two, one),
        grid=(r_rows // tm,),
        in_specs=in_specs,
        out_specs=(row_spec, row_spec, row_spec, dir_spec, dir_spec, dir_spec, row_spec),
        compiler_params=_cparams(("parallel",)),
        name="rwkv_prep",
    )(p, p, p, p, p, p, p, p, p, p, p, p,
      prm["mu_r"], prm["mu_k"], prm["mu_v"], prm["mu_l"], prm["w0"], prm["w2"], prm["a0"], prm["a2"],
      prm["g2"], prm["k_k"], prm["k_a"])


def _scan_block(b, s, *, reverse, n_lat_blocks_total, lat_blocks):
    ctx_blk = n_lat_blocks_total + b
    lat_blk = b * lat_blocks + ((lat_blocks - s) if reverse else (s - 1))
    return jnp.where(s == 0, ctx_blk, lat_blk)


def _rwkv_scan_kernel(r_ref, lw_ref, k_ref, v_ref, kk_ref, bb_ref, y_ref, s_ref, *, reverse):
    @pl.when(pl.program_id(2) == 0)
    def _():
        s_ref[...] = jnp.zeros_like(s_ref)

    L = CHUNK
    ri = lax.broadcasted_iota(jnp.int32, (L, L), 0)
    ci = lax.broadcasted_iota(jnp.int32, (L, L), 1)
    tri = ((ci >= ri) if reverse else (ci <= ri)).astype(F32)
    r2 = lax.broadcasted_iota(jnp.int32, (2 * L, 2 * L), 0) % L
    c2 = lax.broadcasted_iota(jnp.int32, (2 * L, 2 * L), 1) % L
    strict = (c2 > r2) if reverse else (c2 < r2)
    incl = (c2 >= r2) if reverse else (c2 <= r2)
    rr = lax.broadcasted_iota(jnp.int32, (2 * L, 2 * L), 0)
    cc = lax.broadcasted_iota(jnp.int32, (2 * L, 2 * L), 1)
    eye = (rr == cc).astype(F32)
    off_masks = []
    sz = 2
    while sz <= L:
        late, early = (cc, rr) if reverse else (rr, cc)
        off_masks.append((rr // sz == cc // sz) & (late % sz >= sz // 2) & (early % sz < sz // 2))
        sz *= 2
    lane = lax.broadcasted_iota(jnp.int32, (L, LANES), 1)
    head0 = lane < HEAD_DIM

    def stack2(x):
        return jnp.concatenate([jnp.where(head0, x, 0.0), jnp.where(head0, 0.0, x)], axis=0)

    n_sub = SCAN_ROWS // L
    n_pair = r_ref.shape[1] // LANES
    order = [(n_sub - 1 - j) if reverse else j for j in range(n_sub)]

    keys = [(c, p) for c in order for p in range(n_pair)]
    a2, b2, k2, q2, v2, bt2, kt2, decay = ({} for _ in range(8))
    for c in order:
        rows = pl.ds(c * L, L)
        lw_all = lw_ref[rows, :]
        cum_all = jnp.dot(tri, lw_all, preferred_element_type=F32, precision=lax.Precision.HIGHEST)
        for p in range(n_pair):
            cols = slice(p * LANES, (p + 1) * LANES)
            lw = lw_all[:, cols]
            cum = cum_all[:, cols]
            tot = cum[0:1, :] if reverse else cum[L - 1:L, :]
            e_pos = jnp.exp(cum)
            e_neg = jnp.exp(-cum)
            e_rem = jnp.exp(tot - cum)
            kk = kk_ref[rows, cols]
            kd = k_ref[rows, cols]
            bb = bb_ref[rows, cols]
            key = (c, p)
            a2[key] = stack2(-kk * jnp.exp(cum - lw))
            b2[key] = stack2(bb * e_neg)
            k2[key] = stack2(kd * e_neg)
            q2[key] = stack2(r_ref[rows, cols] * e_pos)
            v2[key] = stack2(v_ref[rows, cols])
            bt2[key] = stack2(bb * e_rem)
            kt2[key] = stack2(kd * e_rem)
            decay[key] = jnp.exp(tot)

    hh = 2 * L
    sc = {k: _bdot_nt(jnp.concatenate([a2[k], q2[k]], axis=0), jnp.concatenate([b2[k], k2[k]], axis=0))
          for k in keys}
    mm = {k: jnp.where(strict, sc[k][:hh, :hh], 0.0) for k in keys}
    nn = {k: jnp.where(strict, sc[k][:hh, hh:], 0.0) for k in keys}
    qq = {k: jnp.where(incl, sc[k][hh:, :hh], 0.0) for k in keys}
    zz = {k: jnp.where(incl, sc[k][hh:, hh:], 0.0) for k in keys}
    tinv = {k: eye + jnp.where(off_masks[0], mm[k], 0.0) for k in keys}
    for off in off_masks[1:]:
        half_step = {k: _bdot(tinv[k], jnp.where(off, mm[k], 0.0)) for k in keys}
        tinv = {k: tinv[k] + _bdot(half_step[k], tinv[k]) for k in keys}
    nzv = {k: _bdot(jnp.concatenate([nn[k], zz[k]], axis=0), v2[k]) for k in keys}
    px = {k: _bdot(tinv[k], jnp.concatenate([a2[k], nzv[k][:hh]], axis=1)) for k in keys}
    qpx = {k: _bdot(qq[k], px[k]) for k in keys}
    g_mat = {k: q2[k] + qpx[k][:, :LANES] for k in keys}
    y_loc = {k: qpx[k][:, LANES:] + nzv[k][hh:] for k in keys}
    pb = {k: _bdot_tn(px[k], bt2[k]) for k in keys}
    phi = {k: pb[k][:LANES] for k in keys}
    psi = {k: pb[k][LANES:] + _bdot_tn(v2[k], kt2[k]) for k in keys}
    items = {k: (g_mat[k], y_loc[k], decay[k], phi[k], psi[k]) for k in keys}

    st = [s_ref[p] for p in range(n_pair)]
    for c in order:
        for p in range(n_pair):
            g, yl, dec, ph, ps = items[(c, p)]
            y2 = _bdot_nt(g, st[p]) + yl
            y_ref[pl.ds(c * L, L), p * LANES:(p + 1) * LANES] = y2[:L, :] + y2[L:, :]
            st[p] = st[p] * dec + _bdot(st[p], ph) + ps
    for p in range(n_pair):
        s_ref[p] = st[p]


def rwkv_scan_call(r, lw, kd, v, kk, bb, *, reverse, batch, lat_len, ctx_len):
    r_rows, aw = r.shape
    d = 1 if reverse else 0
    lat_blocks = lat_len // SCAN_ROWS
    width = SCAN_PAIRS * LANES
    blk = functools.partial(_scan_block, reverse=reverse, n_lat_blocks_total=batch * lat_blocks,
                            lat_blocks=lat_blocks)
    row_spec = pl.BlockSpec((SCAN_ROWS, width), lambda b, h, s: (blk(b, s), h))
    dir_spec = pl.BlockSpec((None, SCAN_ROWS, width), lambda b, h, s: (d, blk(b, s), h))
    return pl.pallas_call(
        functools.partial(_rwkv_scan_kernel, reverse=reverse),
        out_shape=jax.ShapeDtypeStruct((r_rows, aw), F32),
        grid=(batch, aw // width, 1 + lat_blocks),
        in_specs=[row_spec, dir_spec, dir_spec, row_spec, row_spec, dir_spec],
        out_specs=row_spec,
        scratch_shapes=[pltpu.VMEM((SCAN_PAIRS, LANES, LANES), F32)],
        compiler_params=_cparams(("parallel", "parallel", "arbitrary")),
        name="rwkv_scan_bwd" if reverse else "rwkv_scan_fwd",
    )(r, lw, kd, v, kk, bb)


def _lru_kernel(xc_ref, xp_ref, xn_ref, cw, cb, wcat, bcat, lsl, h_ref, a_ref, u_ref, carry_ref,
                *, reverse, lat_blocks):
    s = pl.program_id(1)

    @pl.when(s == 0)
    def _():
        carry_ref[...] = jnp.zeros_like(carry_ref)

    tm = SCAN_ROWS
    lat_idx = (lat_blocks - s) if reverse else (s - 1)
    first = (s == 0) | (lat_idx == 0)
    last = (s == 0) | (lat_idx == lat_blocks - 1)
    x = xc_ref[...]
    prev_row = jnp.where(first, 0.0, xp_ref[SUBLANES - 1:SUBLANES, :])
    next1 = jnp.where(last, 0.0, xn_ref[0:1, :])
    next2 = jnp.where(last, 0.0, xn_ref[1:2, :])
    row_id = lax.broadcasted_iota(jnp.int32, x.shape, 0)
    xm1 = jnp.where(row_id == 0, prev_row, pltpu.roll(x, 1, axis=0))
    xp1 = jnp.where(row_id == tm - 1, next1, pltpu.roll(x, tm - 1, axis=0))
    xp2 = jnp.where(row_id == tm - 1, next2, jnp.where(row_id == tm - 2, next1, pltpu.roll(x, tm - 2, axis=0)))
    xc = cb[...] + xm1 * cw[0:1, :] + x * cw[1:2, :] + xp1 * cw[2:3, :] + xp2 * cw[3:4, :]
    bwid = LRU_BLOCK_W
    for n in range(x.shape[1] // bwid):
        cols = slice(n * bwid, (n + 1) * bwid)
        xn = xc[:, cols]
        gates = _bdot(xn, wcat[n]) + bcat[:, 2 * n * bwid:2 * (n + 1) * bwid]
        log_a = jax.nn.sigmoid(gates[:, :bwid]) * lsl[:, cols]
        a = jnp.exp(log_a)
        om = -jnp.tanh(log_a) * (a * a + 1.0)
        a_ref[:, cols] = a
        u_ref[:, cols] = xn * jax.nn.sigmoid(gates[:, bwid:]) * jnp.sqrt(om)

    w = a_ref.shape[1]
    rows = lax.broadcasted_iota(jnp.int32, (SUBLANES, w), 0)
    n_groups = SCAN_ROWS // SUBLANES

    def body(gi, carry):
        g = (n_groups - 1 - gi) if reverse else gi
        sl = pl.ds(pl.multiple_of(g * SUBLANES, SUBLANES), SUBLANES)
        a = a_ref[sl, :]
        u = u_ref[sl, :]
        for sh in (1, 2, 4):
            if reverse:
                ok = rows < SUBLANES - sh
                a_s = pltpu.roll(a, SUBLANES - sh, axis=0)
                u_s = pltpu.roll(u, SUBLANES - sh, axis=0)
            else:
                ok = rows >= sh
                a_s = pltpu.roll(a, sh, axis=0)
                u_s = pltpu.roll(u, sh, axis=0)
            u = jnp.where(ok, a * u_s + u, u)
            a = jnp.where(ok, a * a_s, a)
        h = a * carry + u
        h_ref[sl, :] = h
        return h[0:1, :] if reverse else h[SUBLANES - 1:SUBLANES, :]

    carry_ref[...] = lax.fori_loop(0, n_groups, body, carry_ref[...])


def lru_call(p, prm, *, reverse, batch, lat_len, ctx_len, col_x):
    r_rows = p.shape[0]
    bw = prm["conv_b"].shape[1]
    nb = bw // LRU_BLOCK_W
    d = 1 if reverse else 0
    lat_blocks = lat_len // SCAN_ROWS
    per = SCAN_ROWS // SUBLANES
    last8 = r_rows // SUBLANES - 1
    blk = functools.partial(_scan_block, reverse=reverse, n_lat_blocks_total=batch * lat_blocks,
                            lat_blocks=lat_blocks)
    full = lambda shape: pl.BlockSpec(shape, lambda b, s: (0,) * len(shape))
    return pl.pallas_call(
        functools.partial(_lru_kernel, reverse=reverse, lat_blocks=lat_blocks),
        out_shape=jax.ShapeDtypeStruct((r_rows, bw), F32),
        grid=(batch, 1 + lat_blocks),
        in_specs=[
            pl.BlockSpec((SCAN_ROWS, bw), lambda b, s: (blk(b, s), col_x)),
            pl.BlockSpec((SUBLANES, bw), lambda b, s: (jnp.maximum(blk(b, s) * per - 1, 0), col_x)),
            pl.BlockSpec((SUBLANES, bw), lambda b, s: (jnp.minimum((blk(b, s) + 1) * per, last8), col_x)),
            full((CONV_W, bw)), full((1, bw)),
            pl.BlockSpec((None, nb, LRU_BLOCK_W, 2 * LRU_BLOCK_W), lambda b, s: (d, 0, 0, 0)),
            pl.BlockSpec((None, 1, 2 * bw), lambda b, s: (d, 0, 0)),
            pl.BlockSpec((None, 1, bw), lambda b, s: (d, 0, 0)),
        ],
        out_specs=pl.BlockSpec((SCAN_ROWS, bw), lambda b, s: (blk(b, s), 0)),
        scratch_shapes=[pltpu.VMEM((SCAN_ROWS, bw), F32), pltpu.VMEM((SCAN_ROWS, bw), F32),
                        pltpu.VMEM((1, bw), F32)],
        compiler_params=_cparams(("parallel", "arbitrary")),
        name="lru_bwd" if reverse else "lru_fwd",
    )(p, p, p, prm["conv_w"], prm["conv_b"], prm["wcat"], prm["bcat"], prm["lsl"])


def _even_post_kernel(yf, yb, r_ref, v_ref, kdf, kdb, g_ref, hf, hb, gb_ref, rk, lnw, lnb, o_ref):
    ones = _head_ones()
    y = yf[...] + yb[...]
    inv_n = 1.0 / HEAD_DIM
    yc = y - _head_sum(y, ones) * inv_n
    var = _head_sum(yc * yc, ones) * inv_n
    hn = yc * lax.rsqrt(var + LNX_EPS) * lnw[...] + lnb[...]
    r = r_ref[...]
    v = v_ref[...]
    bonus = _head_sum(r * kdf[...] * rk[...], ones) * v + _head_sum(r * kdb[...] * rk[...], ones) * v
    ya = (hn + bonus) * g_ref[...]
    yl = (hf[...] + hb[...]) * jax.nn.gelu(gb_ref[...])
    aw = ya.shape[1]
    o_ref[:, :aw] = ya.astype(BF16)
    o_ref[:, aw:] = yl.astype(BF16)


def even_post_call(yf, yb, r, v, kd, g, hf, hb, p, prm, *, col_gb):
    r_rows, aw = yf.shape
    bw = hf.shape[1]
    tm = PREP_ROWS
    row = lambda w: pl.BlockSpec((tm, w), lambda i: (i, 0))
    full = lambda shape: pl.BlockSpec(shape, lambda i: (0,) * len(shape))
    return pl.pallas_call(
        _even_post_kernel,
        out_shape=jax.ShapeDtypeStruct((r_rows, aw + bw), BF16),
        grid=(r_rows // tm,),
        in_specs=[row(aw), row(aw), row(aw), row(aw),
                  pl.BlockSpec((None, tm, aw), lambda i: (0, i, 0)),
                  pl.BlockSpec((None, tm, aw), lambda i: (1, i, 0)),
                  row(aw), row(bw), row(bw),
                  pl.BlockSpec((tm, bw), lambda i: (i, col_gb)),
                  full((1, aw)), full((1, aw)), full((1, aw))],
        out_specs=row(aw + bw),
        compiler_params=_cparams(("parallel",)),
        name="even_post",
    )(yf, yb, r, v, kd, kd, g, hf, hb, p, prm["r_k"], prm["lnx_w"], prm["lnx_b"])


def _mlstm_kernel(qf, kf, vf, brf, bcf, irf, icf, qb, kb, vb, brb, bcb, irb, icb, hf_ref, hb_ref,
                  cf, nf, mf, cb, nb, mb):
    @pl.when(pl.program_id(2) == 0)
    def _():
        for ref in (cf, nf, mf, cb, nb, mb):
            ref[...] = jnp.zeros_like(ref)

    _mlstm_block(qf, kf, vf, brf, bcf, irf, icf, hf_ref, cf, nf, mf, reverse=False)
    _mlstm_block(qb, kb, vb, brb, bcb, irb, icb, hb_ref, cb, nb, mb, reverse=True)


def _mlstm_block(q_ref, k_ref, v_ref, br_ref, bc_ref, ir_ref, ic_ref, h_ref, c_ref, n_ref, m_ref, *, reverse):
    L = MLSTM_CHUNK
    ri = lax.broadcasted_iota(jnp.int32, (L, L), 0)
    ci = lax.broadcasted_iota(jnp.int32, (L, L), 1)
    causal = (ci >= ri) if reverse else (ci <= ri)
    n_sub = SCAN_ROWS // L
    order = [(n_sub - 1 - j) if reverse else j for j in range(n_sub)]
    scale = C_QK_HEAD ** -0.5

    m_cur = m_ref[...]
    m_in, ew, ec = {}, {}, {}
    for c in order:
        b_r = br_ref[c:c + 1, :]
        b_tot = b_r[:, 0:1] if reverse else b_r[:, L - 1:L]
        w_in = b_tot - bc_ref[:, c:c + 1] + ic_ref[:, c:c + 1]
        carry_log = b_tot + m_cur
        m_new = jnp.maximum(carry_log, jnp.max(w_in, axis=0, keepdims=True))
        m_in[c] = m_cur
        ew[c] = jnp.exp(w_in - m_new)
        ec[c] = jnp.exp(carry_log - m_new)
        m_cur = m_new
    m_ref[...] = m_cur

    q = {c: (q_ref[pl.ds(c * L, L), :] * scale).astype(BF16) for c in order}
    k = {c: k_ref[pl.ds(c * L, L), :] for c in order}
    v = {c: v_ref[pl.ds(c * L, L), :].astype(BF16) for c in order}
    qk = {c: _bdot_nt(q[c], k[c]) for c in order}
    m_t, e_inter, scores = {}, {}, {}
    for c in order:
        b_c = bc_ref[:, c:c + 1]
        dmat = jnp.where(causal, b_c - br_ref[c:c + 1, :] + ir_ref[c:c + 1, :], -jnp.inf)
        inter = b_c + m_in[c]
        m_t[c] = jnp.maximum(inter, jnp.max(dmat, axis=1, keepdims=True))
        scores[c] = qk[c] * jnp.exp(dmat - m_t[c])
        e_inter[c] = jnp.exp(inter - m_t[c])
    intra = {c: _bdot(scores[c], v[c]) for c in order}
    kw = {c: k[c] * ew[c] for c in order}
    kv = {c: _bdot_tn(kw[c], v[c]) for c in order}

    c_cur = c_ref[...]
    n_cur = n_ref[...]
    c_in, n_in = {}, {}
    for c in order:
        c_in[c] = c_cur
        n_in[c] = n_cur
        c_cur = ec[c] * c_cur + kv[c]
        n_cur = ec[c] * n_cur + jnp.sum(kw[c], axis=0, keepdims=True)
    c_ref[...] = c_cur
    n_ref[...] = n_cur

    qc = {c: _bdot(q[c], c_in[c]) for c in order}
    for c in order:
        qf = q[c].astype(F32)
        num = e_inter[c] * qc[c] + intra[c]
        den = (e_inter[c] * jnp.sum(qf * n_in[c], axis=1, keepdims=True)
               + jnp.sum(scores[c], axis=1, keepdims=True))
        h_ref[pl.ds(c * L, L), :] = num / jnp.maximum(jnp.abs(den), jnp.exp(-m_t[c]))


def mlstm_call(p, g_rows, g_cols, *, batch, lat_len, ctx_len, col_q, col_k, col_v):
    r_rows = p.shape[0]
    n_sub = SCAN_ROWS // MLSTM_CHUNK
    lat_blocks = lat_len // SCAN_ROWS
    in_specs, out_specs = [], []
    for d, reverse in enumerate((False, True)):
        blk = functools.partial(_scan_block, reverse=reverse, n_lat_blocks_total=batch * lat_blocks,
                                lat_blocks=lat_blocks)
        grow = lambda which, d=d, blk=blk: pl.BlockSpec(
            (None, None, None, None, n_sub, MLSTM_CHUNK), lambda b, h, s: (d, which, h, blk(b, s), 0, 0))
        gcol = lambda which, d=d, blk=blk: pl.BlockSpec(
            (None, None, None, None, MLSTM_CHUNK, n_sub), lambda b, h, s: (d, which, h, blk(b, s), 0, 0))
        in_specs += [
            pl.BlockSpec((SCAN_ROWS, C_QK_HEAD), lambda b, h, s, blk=blk: (blk(b, s), col_q + h)),
            pl.BlockSpec((SCAN_ROWS, C_QK_HEAD), lambda b, h, s, blk=blk: (blk(b, s), col_k + h)),
            pl.BlockSpec((SCAN_ROWS, C_V_HEAD), lambda b, h, s, blk=blk: (blk(b, s), col_v + h)),
            grow(0), gcol(0), grow(1), gcol(1),
        ]
        out_specs.append(pl.BlockSpec((SCAN_ROWS, C_V_HEAD), lambda b, h, s, blk=blk: (blk(b, s), h)))
    out = jax.ShapeDtypeStruct((r_rows, C_HEADS * C_V_HEAD), F32)
    state = [pltpu.VMEM((C_QK_HEAD, C_V_HEAD), F32), pltpu.VMEM((1, C_QK_HEAD), F32), pltpu.VMEM((1, 1), F32)]
    one_dir = (p, p, p, g_rows, g_cols, g_rows, g_cols)
    return pl.pallas_call(
        _mlstm_kernel,
        out_shape=(out, out),
        grid=(batch, C_HEADS, 1 + lat_blocks),
        in_specs=in_specs,
        out_specs=tuple(out_specs),
        scratch_shapes=state + state,
        compiler_params=_cparams(("parallel", "parallel", "arbitrary")),
        name="mlstm",
    )(*one_dir, *one_dir)


def _odd_post_kernel(hf, hb, o_lo, o_hi, nw, nb, y_ref):
    half_heads = C_HEADS // 2
    for hd in range(C_HEADS):
        cols = slice(hd * C_V_HEAD, (hd + 1) * C_V_HEAD)
        x = hf[:, cols] + hb[:, cols]
        xc = x - jnp.mean(x, axis=1, keepdims=True)
        var = jnp.mean(xc * xc, axis=1, keepdims=True)
        hn = xc * lax.rsqrt(var + MLSTM_NORM_EPS) * nw[:, cols] + nb[:, cols]
        o_ref = o_lo if hd < half_heads else o_hi
        oc = slice((hd % half_heads) * C_V_HEAD, (hd % half_heads + 1) * C_V_HEAD)
        y_ref[:, cols] = (hn * jax.nn.sigmoid(o_ref[:, oc])).astype(BF16)


def odd_post_call(hf, hb, p, nw, nb, *, n_rows, col_o):
    vw = hf.shape[1]
    tm = LN_ROWS
    row = pl.BlockSpec((tm, vw), lambda i: (i, 0))
    par = pl.BlockSpec((1, vw), lambda i: (0, 0))
    return pl.pallas_call(
        _odd_post_kernel,
        out_shape=jax.ShapeDtypeStruct((n_rows, vw), BF16),
        grid=(n_rows // tm,),
        in_specs=[row, row, pl.BlockSpec((tm, vw // 2), lambda i: (i, col_o)),
                  pl.BlockSpec((tm, vw // 2), lambda i: (i, col_o + 1)), par, par],
        out_specs=row,
        compiler_params=_cparams(("parallel",)),
        name="odd_post",
    )(hf, hb, p, p, nw, nb)


def _layer_norm(z, g, b):
    zc = z - jnp.mean(z, axis=1, keepdims=True)
    var = jnp.mean(zc * zc, axis=1, keepdims=True)
    return zc * lax.rsqrt(var + LN_EPS) * g + b


def _route(logits_t, bias):
    aff = [jax.nn.sigmoid(logits_t[e:e + 1, :]) for e in range(N_EXPERTS)]
    biased = [aff[e] + bias[e:e + 1, :] for e in range(N_EXPERTS)]
    best_g = best_v = None
    for g in range(N_GROUPS):
        m = biased[g * EXPERTS_PER_GROUP:(g + 1) * EXPERTS_PER_GROUP]
        pair = None
        for i in range(EXPERTS_PER_GROUP):
            for j in range(i + 1, EXPERTS_PER_GROUP):
                hi = jnp.maximum(m[i], m[j])
                lo_ = jnp.minimum(m[i], m[j])
                s = hi + lo_
                pair = s if pair is None else jnp.maximum(pair, s)
        if best_v is None:
            best_v, best_g = pair, jnp.zeros(pair.shape, jnp.int32)
        else:
            upd = pair > best_v
            best_g = jnp.where(upd, g, best_g)
            best_v = jnp.where(upd, pair, best_v)
    ids, sels = [], []
    taken = None
    for _ in range(TOP_K):
        cur_v = cur_i = cur_a = None
        for e in range(N_EXPERTS):
            ok = best_g == (e // EXPERTS_PER_GROUP)
            if taken is not None:
                ok = jnp.logical_and(ok, taken != e)
            val = jnp.where(ok, biased[e], -jnp.inf)
            if cur_v is None:
                cur_v, cur_i, cur_a = val, jnp.zeros(val.shape, jnp.int32), aff[e]
            else:
                upd = val > cur_v
                cur_i = jnp.where(upd, e, cur_i)
                cur_a = jnp.where(upd, aff[e], cur_a)
                cur_v = jnp.where(upd, val, cur_v)
        ids.append(cur_i)
        sels.append(cur_a)
        taken = cur_i
    tot = sels[0] + sels[1]
    return ids, [sels[0] / tot, sels[1] / tot]


def _ln_route_kernel(lat_ref, y_ref, gate_ref, sh_ref, sc_ref, g_ref, b_ref, rw_ref, rb_ref,
                     lat_o, h_o, id_o, gt_o):
    z = ALPHA * lat_ref[...] + gate_ref[...] * y_ref[...]
    ln = _layer_norm(z, g_ref[...], b_ref[...])
    lat_o[...] = ln
    h = ln * (1.0 + sc_ref[...]) + sh_ref[...]
    half = h.shape[1] // 2
    bits = lax.bitcast_convert_type(h.astype(BF16).astype(F32), jnp.uint32)
    h_o[...] = (bits[:, :half] >> 16) | bits[:, half:]
    logits_t = lax.dot_general(rw_ref[...], h, (((1,), (1,)), ((), ())), preferred_element_type=F32,
                               precision=lax.Precision.HIGHEST)
    ids, gts = _route(logits_t, rb_ref[...])
    id_o[...] = jnp.concatenate(ids, axis=0)
    gt_o[...] = jnp.concatenate(gts, axis=0)


def ln_route_call(lat, y, mods, norm_g, norm_b, layer, router_wt, router_b, *, n_rows, rows_per_mod, n_mod_rows):
    d = lat.shape[1]
    tm = LN_ROWS

    def midx(i):
        return jnp.minimum(i * tm // rows_per_mod, n_mod_rows - 1)

    row = pl.BlockSpec((tm, d), lambda i: (i, 0))
    packed = pl.BlockSpec((tm, d // 2), lambda i: (i, 0))
    mod = lambda k: pl.BlockSpec((None, 1, d), lambda i: (midx(i), 0, k))
    nrm = pl.BlockSpec((None, None, 1, d), lambda i: (layer, 0, 0, 0))
    full = lambda shape: pl.BlockSpec(shape, lambda i: (0,) * len(shape))
    sel = pl.BlockSpec((TOP_K, tm), lambda i: (0, i))
    return pl.pallas_call(
        _ln_route_kernel,
        out_shape=(jax.ShapeDtypeStruct((n_rows, d), F32), jax.ShapeDtypeStruct((n_rows, d // 2), jnp.uint32),
                   jax.ShapeDtypeStruct((TOP_K, n_rows), jnp.int32), jax.ShapeDtypeStruct((TOP_K, n_rows), F32)),
        grid=(n_rows // tm,),
        in_specs=[row, row, mod(2), mod(3), mod(4), nrm, nrm, full((N_EXPERTS, d)), full((N_EXPERTS, 1))],
        out_specs=(row, packed, sel, sel),
        compiler_params=_cparams(("parallel",)),
        name="ln_route",
    )(lat, y, mods, mods, mods, norm_g, norm_b, router_wt, router_b)


def _ln_combine_kernel(lat_ref, y0_ref, y1_ref, gt_ref, gate_ref, g_ref, b_ref, lat_o):
    gt = gt_ref[...]
    f = y0_ref[...] * gt[:, 0:1] + y1_ref[...] * gt[:, 1:2]
    z = ALPHA * lat_ref[...] + gate_ref[...] * f
    lat_o[...] = _layer_norm(z, g_ref[...], b_ref[...])


def ln_combine_call(lat, y_slots, gates_t, mods, norm_g, norm_b, layer, *, n_rows, rows_per_mod, n_mod_rows):
    d = lat.shape[1]
    tm = LN_ROWS
    per_k = n_rows // tm

    def midx(i):
        return jnp.minimum(i * tm // rows_per_mod, n_mod_rows - 1)

    row = pl.BlockSpec((tm, d), lambda i: (i, 0))
    nrm = pl.BlockSpec((None, None, 1, d), lambda i: (layer, 1, 0, 0))
    return pl.pallas_call(
        _ln_combine_kernel,
        out_shape=jax.ShapeDtypeStruct((n_rows, d), F32),
        grid=(per_k,),
        in_specs=[row, row, pl.BlockSpec((tm, d), lambda i: (per_k + i, 0)),
                  pl.BlockSpec((tm, TOP_K), lambda i: (i, 0)),
                  pl.BlockSpec((None, 1, d), lambda i: (midx(i), 0, 5)), nrm, nrm],
        out_specs=row,
        compiler_params=_cparams(("parallel",)),
        name="ln_combine",
    )(lat, y_slots, y_slots, gates_t, mods, norm_g, norm_b)


def _moe_kernel(nu_ref, be_ref, jw_ref, src_cur, src_nxt, dst_prv, h_hbm, wg_ref, wu_ref, wd_ref, y_hbm,
                xbuf, acc, gsem, ssem, *, n_j, n_slots):
    del be_ref, jw_ref
    i = pl.program_id(0)
    j = pl.program_id(1)
    buf = i % 2
    other = 1 - buf
    n_used = nu_ref[0]
    tm, half = xbuf.shape[1], xbuf.shape[2]
    d = acc.shape[2]
    per_step = tm // n_j

    def row_in(tok, r, b):
        return pltpu.make_async_copy(h_hbm.at[pl.ds(tok, 1)], xbuf.at[b, pl.ds(r, 1)], gsem.at[b])

    def row_out(r, slot, b):
        return pltpu.make_async_copy(acc.at[b, pl.ds(r, 1)], y_hbm.at[pl.ds(slot, 1)], ssem.at[b])

    def send_previous(r):
        slot = jnp.where(i > 0, dst_prv[0, r], n_slots + r)
        row_out(r, slot, other).start()

    @pl.when((i == 0) & (j == 0))
    def _():
        acc[1] = jnp.zeros((tm, d), F32)

        def body(r, carry):
            row_in(src_cur[0, r], r, 0).start()
            return carry
        lax.fori_loop(0, tm, body, 0, unroll=8)

    @pl.when((j == 0) & (i <= n_used))
    def _():
        pltpu.make_async_copy(xbuf.at[buf], xbuf.at[buf], gsem.at[buf]).wait()

    @pl.when((j == 0) & (i < n_used))
    def _():
        acc[buf] = jnp.zeros((tm, d), F32)

    @pl.when(i < n_used)
    def _():
        tf = wg_ref.shape[1]
        g = jnp.zeros((tm, tf), F32)
        u = jnp.zeros((tm, tf), F32)
        for c0 in range(0, half, MOE_KCHUNK):
            xp = xbuf[buf, :, c0:c0 + MOE_KCHUNK]
            lo = lax.bitcast_convert_type(xp << 16, F32)
            hi = lax.bitcast_convert_type(xp & jnp.uint32(0xFFFF0000), F32)
            g = g + _bdot(lo, wg_ref[c0:c0 + MOE_KCHUNK, :]) + _bdot(hi, wg_ref[half + c0:half + c0 + MOE_KCHUNK, :])
            u = u + _bdot(lo, wu_ref[c0:c0 + MOE_KCHUNK, :]) + _bdot(hi, wu_ref[half + c0:half + c0 + MOE_KCHUNK, :])
        hdn = (g * jax.nn.sigmoid(g) * u).astype(BF16)
        for c0 in range(0, d, MOE_NCHUNK):
            acc[buf, :, c0:c0 + MOE_NCHUNK] += _bdot(hdn, wd_ref[:, c0:c0 + MOE_NCHUNK])
        for rr in range(per_step):
            r = j * per_step + rr
            row_in(src_nxt[0, r], r, other).start()
            send_previous(r)

    @pl.when(i == n_used)
    def _():
        for rr in range(per_step):
            send_previous(j * per_step + rr)

    @pl.when((j == n_j - 1) & (i <= n_used))
    def _():
        pltpu.make_async_copy(acc.at[other], acc.at[other], ssem.at[other]).wait()


def moe_call(h_packed, n_used, block_e, j_of, src_tok, dst_slot, w_gate, w_up, w_down, layer, *, n_slots):
    d = 2 * h_packed.shape[1]
    f = w_gate.shape[3]
    tm, tf = MOE_ROWS, MOE_FTILE
    n_blocks = block_e.shape[0]
    n_j = f // tf
    assert tm % n_j == 0
    idx = lambda fn: pl.BlockSpec((None, 1, tm), fn, memory_space=pltpu.SMEM)
    grid_spec = pltpu.PrefetchScalarGridSpec(
        num_scalar_prefetch=3,
        grid=(n_blocks, n_j),
        in_specs=[
            idx(lambda i, j, nu, be, jw: (i, 0, 0)),
            idx(lambda i, j, nu, be, jw: (jnp.minimum(i + 1, n_blocks - 1), 0, 0)),
            idx(lambda i, j, nu, be, jw: (jnp.maximum(i - 1, 0), 0, 0)),
            pl.BlockSpec(memory_space=pl.ANY),
            pl.BlockSpec((None, None, d, tf), lambda i, j, nu, be, jw: (layer, be[i], 0, jw[i, j])),
            pl.BlockSpec((None, None, d, tf), lambda i, j, nu, be, jw: (layer, be[i], 0, jw[i, j])),
            pl.BlockSpec((None, None, tf, d), lambda i, j, nu, be, jw: (layer, be[i], jw[i, j], 0)),
        ],
        out_specs=pl.BlockSpec(memory_space=pl.ANY),
        scratch_shapes=[pltpu.VMEM((2, tm, d // 2), jnp.uint32), pltpu.VMEM((2, tm, d), F32),
                        pltpu.SemaphoreType.DMA((2,)), pltpu.SemaphoreType.DMA((2,))],
    )
    return pl.pallas_call(
        functools.partial(_moe_kernel, n_j=n_j, n_slots=n_slots),
        out_shape=jax.ShapeDtypeStruct((n_slots + tm, d), F32),
        grid_spec=grid_spec,
        compiler_params=_cparams(("arbitrary", "arbitrary")),
        name="moe_experts",
    )(n_used, block_e, j_of, src_tok, src_tok, dst_slot, h_packed, w_gate, w_up, w_down)


def moe_ffn(h_packed, ids, w_gate, w_up, w_down, layer):
    n_tok = h_packed.shape[0]
    n_slots = n_tok * TOP_K
    tm = MOE_ROWS
    i32 = jnp.int32
    flat_e = ids.reshape(n_slots)
    order = jnp.argsort(flat_e, stable=True).astype(i32)
    counts = jnp.sum((flat_e[:, None] == jnp.arange(N_EXPERTS, dtype=i32)[None, :]).astype(i32), axis=0)
    start = jnp.cumsum(counts) - counts
    padded = (counts + tm - 1) // tm * tm
    pad_end = jnp.cumsum(padded)
    pad_start = pad_end - padded
    n_blocks = -(-n_slots // tm) + N_EXPERTS + 1
    pos = jnp.arange(n_blocks * tm, dtype=i32)
    e_pos = jnp.minimum(jnp.sum((pos[:, None] >= pad_end[None, :]).astype(i32), axis=1), N_EXPERTS - 1)
    r_pos = pos - pad_start[e_pos]
    valid = r_pos < counts[e_pos]
    slot = order[jnp.clip(start[e_pos] + r_pos, 0, n_slots - 1)]
    src_tok = jnp.where(valid, slot % n_tok, 0).reshape(n_blocks, 1, tm)
    dst_slot = jnp.where(valid, slot, n_slots + pos % tm).reshape(n_blocks, 1, tm)
    n_used = (pad_end[-1] // tm).astype(i32)
    blk = jnp.arange(n_blocks, dtype=i32)
    block_e = e_pos.reshape(n_blocks, tm)[:, 0]
    block_e = jnp.where(blk < n_used, block_e, block_e[jnp.maximum(n_used - 1, 0)]).astype(i32)
    n_j = w_gate.shape[3] // MOE_FTILE
    tiles = jnp.arange(n_j, dtype=i32)[None, :]
    j_of = jnp.where((blk % 2 == 0)[:, None], tiles, n_j - 1 - tiles)
    j_of = jnp.where((blk < n_used)[:, None], j_of, j_of[jnp.maximum(n_used - 1, 0), n_j - 1]).astype(i32)
    return moe_call(h_packed, n_used.reshape(1), block_e, j_of, src_tok.astype(i32), dst_slot.astype(i32),
                    w_gate, w_up, w_down, layer, n_slots=n_slots)


def _pad_cols(w, width):
    return jnp.pad(w, [(0, 0)] * (w.ndim - 1) + [(0, width - w.shape[-1])])


def _even_weights(ev_w_in, rwkv_mu, aw, bw):
    rk = 3 * aw
    lo0 = rk
    pieces = [DECAY_LORA, DECAY_LORA, ICL_LORA, ICL_LORA]

    def regroup(t):
        out = [t[..., rk + 2 * (DECAY_LORA + ICL_LORA) + GATE_LORA:],
               t[..., :rk]]
        off = lo0
        for wdt in pieces:
            out.append(_pad_cols(t[..., off:off + wdt], LANES))
            off += wdt
        out.append(t[..., off:off + GATE_LORA])
        used = 4 * LANES + GATE_LORA
        out.append(jnp.zeros(t.shape[:-1] + (LORA_PAD - used,), t.dtype))
        return jnp.concatenate(out, axis=-1)

    w = regroup(ev_w_in).astype(BF16)
    rwkv_cols = rk + 2 * (DECAY_LORA + ICL_LORA) + GATE_LORA
    mu_full = jnp.concatenate([rwkv_mu, jnp.zeros(rwkv_mu.shape[:-1] + (2 * bw,), rwkv_mu.dtype)], axis=-1)
    mu = regroup(mu_full)
    del rwkv_cols
    return w, mu


def kernel(x, c, ctx, c_ctx, ada_w, ada_b, norm_g, norm_b,
           ev_w_in, ev_w_out, rwkv_mu, rwkv_w0, rwkv_w2, rwkv_a0, rwkv_a2, rwkv_g2,
           rwkv_k_k, rwkv_k_a, rwkv_r_k, rwkv_lnx_w, rwkv_lnx_b,
           lru_conv_w, lru_conv_b, lru_wa, lru_ba, lru_wx, lru_bx, lru_lam,
           od_w_in, od_w_out, mlstm_ig_b, mlstm_fg_b, mlstm_norm_w, mlstm_norm_b,
           router_w, router_b, moe_w_gate, moe_w_up, moe_w_down):
    bsz, s_len, dim = x.shape
    n_ctx = ctx.shape[1]
    n_lat_rows = bsz * s_len
    n_all_rows = n_lat_rows + bsz * n_ctx
    aw = rwkv_k_k.shape[1]
    bw = lru_conv_b.shape[1]
    assert s_len % SCAN_ROWS == 0 and n_ctx == SCAN_ROWS and n_all_rows % MM_ROWS == 0
    assert (bsz * n_ctx) % MM_ROWS == 0 and s_len % MM_ROWS == 0 and bsz + 1 <= SUBLANES
    seq = dict(n_lat_rows=n_lat_rows, lat_len=s_len, ctx_len=n_ctx)
    scan = dict(batch=bsz, lat_len=s_len, ctx_len=n_ctx)
    modk = dict(rows_per_mod=s_len, n_mod_rows=bsz + 1)

    stream = embed_call(x.reshape(n_lat_rows, dim), ctx.reshape(bsz * n_ctx, dim), s_len)
    cvec = jnp.concatenate([c, c_ctx[None, :], jnp.zeros((SUBLANES - bsz - 1, dim), c.dtype)], axis=0)
    ada_b3 = ada_b[:, None, :]
    norm_g4 = norm_g[:, :, None, :]
    norm_b4 = norm_b[:, :, None, :]
    router_wt = router_w.T
    router_b2 = router_b[:, None]

    for layer in range(DEPTH):
        last = layer == DEPTH - 1
        i = layer // 2
        mods = adaln_call(cvec, ada_w, ada_b3, layer).reshape(SUBLANES, 1, N_MOD * dim)
        if layer % 2 == 0:
            w_in, mu = _even_weights(ev_w_in, rwkv_mu, aw, bw)
            n_in = w_in.shape[2]
            p = matmul_mod_call(stream, mods, w_in, i, tn=1024, **modk)
            col_x, col_gb = 0, bw // bw
            col_r, col_k, col_v = 2 * bw // aw, 2 * bw // aw + 1, 2 * bw // aw + 2
            col_l = (2 * bw + 3 * aw) // LORA_PAD
            assert (2 * bw + 3 * aw) % LORA_PAD == 0 and n_in == 2 * bw + 3 * aw + LORA_PAD
            mu_i = mu[i]
            o_r = 2 * bw
            prm = dict(
                mu_r=mu_i[None, o_r:o_r + aw], mu_k=mu_i[None, o_r + aw:o_r + 2 * aw],
                mu_v=mu_i[None, o_r + 2 * aw:o_r + 3 * aw], mu_l=mu_i[None, o_r + 3 * aw:],
                w0=rwkv_w0[i], w2=_pad_rows(rwkv_w2[i], LANES).astype(BF16),
                a0=rwkv_a0[i], a2=_pad_rows(rwkv_a2[i], LANES).astype(BF16),
                g2=rwkv_g2[i].astype(BF16), k_k=rwkv_k_k[i][None, :], k_a=rwkv_k_a[i][None, :])
            r_s, v_s, kk, lw, kd, bb, g = rwkv_prep_call(p, prm, col_r=col_r, col_k=col_k, col_v=col_v,
                                                          col_l=col_l, **seq)
            nb = bw // LRU_BLOCK_W
            wcat = jnp.concatenate([lru_wa[i], lru_wx[i]], axis=-1).astype(BF16)
            bcat = jnp.stack([lru_ba[i].reshape(2, nb, LRU_BLOCK_W), lru_bx[i].reshape(2, nb, LRU_BLOCK_W)],
                             axis=2).reshape(2, 1, 2 * bw)
            lprm = dict(conv_w=lru_conv_w[i], conv_b=lru_conv_b[i][None, :], wcat=wcat, bcat=bcat,
                        lsl=(RGLRU_C * jax.nn.log_sigmoid(lru_lam[i]))[:, None, :])
            yf = rwkv_scan_call(r_s, lw, kd, v_s, kk, bb, reverse=False, **scan)
            yb = rwkv_scan_call(r_s, lw, kd, v_s, kk, bb, reverse=True, **scan)
            hf = lru_call(p, lprm, reverse=False, col_x=col_x, **scan)
            hb = lru_call(p, lprm, reverse=True, col_x=col_x, **scan)
            pprm = dict(r_k=rwkv_r_k[i].reshape(1, aw), lnx_w=rwkv_lnx_w[i][None, :], lnx_b=rwkv_lnx_b[i][None, :])
            ymix = even_post_call(yf, yb, r_s, v_s, kd, g, hf, hb, p, pprm, col_gb=col_gb)
            n_rows = n_all_rows if not last else n_lat_rows
            y = matmul_call(ymix, ev_w_out.astype(BF16), i, tn=1024, n_rows=n_rows)
        else:
            qk_w = C_HEADS * C_QK_HEAD
            v_w = C_HEADS * C_V_HEAD
            main = 2 * qk_w + 2 * v_w
            w_gate = _pad_cols(od_w_in[..., main:], LANES).astype(BF16)
            p = matmul_mod_call(stream, mods, od_w_in.astype(BF16), i, tn=1024, n_cols=main, **modk)
            gp = matmul_mod_call(stream, mods, w_gate, i, tn=LANES, **modk)[:, :4 * C_HEADS]
            gp = gp.reshape(n_all_rows, 2, 2, C_HEADS)
            g_rows, g_cols = _mlstm_gates(gp, mlstm_ig_b[i], mlstm_fg_b[i])
            cols = dict(col_q=0, col_k=(qk_w + v_w) // C_QK_HEAD, col_v=(2 * qk_w + v_w) // C_V_HEAD)
            hf, hb = mlstm_call(p, g_rows, g_cols, **scan, **cols)
            n_rows = n_all_rows if not last else n_lat_rows
            ymix = odd_post_call(hf, hb, p, mlstm_norm_w[i][None, :], mlstm_norm_b[i][None, :],
                                 n_rows=n_rows, col_o=qk_w // (v_w // 2))
            y = matmul_call(ymix, od_w_out.astype(BF16), i, tn=1024, n_rows=n_rows)
        n_rows = n_all_rows if not last else n_lat_rows
        stream1, h_moe, ids, gates = ln_route_call(stream, y, mods, norm_g4, norm_b4, layer, router_wt, router_b2,
                                                   n_rows=n_rows, **modk)
        y_slots = moe_ffn(h_moe, ids, moe_w_gate, moe_w_up, moe_w_down, layer)
        stream = ln_combine_call(stream1, y_slots, gates.T, mods, norm_g4, norm_b4, layer, n_rows=n_rows, **modk)
    return stream[:n_lat_rows].reshape(bsz, s_len, dim)


def _pad_rows(w, rows):
    return jnp.pad(w, ((0, 0), (0, rows - w.shape[1]), (0, 0)))


def _mlstm_gates(gp, ig_b, fg_b):
    n_rows = gp.shape[0]
    chunk = MLSTM_CHUNK
    n_sub = SCAN_ROWS // chunk
    outs = []
    for d in range(2):
        ig = GATE_CAP * jnp.tanh((gp[:, d, 0] + ig_b[d]) / GATE_CAP)
        lf = jax.nn.log_sigmoid(GATE_CAP * jnp.tanh((gp[:, d, 1] + fg_b[d]) / GATE_CAP))
        lf = lf.reshape(n_rows // chunk, chunk, C_HEADS)
        b = jnp.cumsum(lf[:, ::-1], axis=1)[:, ::-1] if d == 1 else jnp.cumsum(lf, axis=1)
        both = jnp.stack([b, ig.reshape(n_rows // chunk, chunk, C_HEADS)], axis=0)
        outs.append(both)
    g = jnp.stack(outs, axis=0)
    g = g.reshape(2, 2, n_rows // SCAN_ROWS, n_sub, chunk, C_HEADS)
    g_rows = jnp.transpose(g, (0, 1, 5, 2, 3, 4))
    g_cols = jnp.transpose(g, (0, 1, 5, 2, 4, 3))
    return g_rows, g_cols
```

```python
import functools
import math

import jax
import jax.numpy as jnp
from jax import lax
from jax.experimental import pallas as pl
from jax.experimental.pallas import tpu as pltpu

F32 = jnp.float32
BF16 = jnp.bfloat16

DEPTH = 2
N_MOD = 6
ALPHA = (2 * DEPTH) ** 0.25
LN_EPS = 1e-5
HEAD_DIM = 64
DECAY_LORA = 96
ICL_LORA = 96
GATE_LORA = 256
LNX_EPS = 64e-5
LRU_BLOCK_W = 128
CONV_W = 4
RGLRU_C = 8.0
C_HEADS = 8
C_QK_HEAD = 256
C_V_HEAD = 512
GATE_CAP = 15.0
MLSTM_NORM_EPS = 1e-6
N_EXPERTS = 16
N_GROUPS = 4
EXPERTS_PER_GROUP = N_EXPERTS // N_GROUPS
TOP_K = 2
POS_BASE = 10000.0
GRID_W = 64

LANES = 128
SUBLANES = 8
SCAN_ROWS = 256
CHUNK = 64
MLSTM_CHUNK = 128
SCAN_PAIRS = 4
PREP_ROWS = 128
LN_ROWS = 256
MM_ROWS = 512
MOE_ROWS = 512
MOE_FTILE = 256
MOE_NCHUNK = 1024
MOE_KCHUNK = 512
LORA_PAD = 1024
VMEM_LIMIT = 60 * 1024 * 1024


def _cparams(sem):
    return pltpu.CompilerParams(dimension_semantics=sem, vmem_limit_bytes=VMEM_LIMIT)


def _bdot(a, b):
    return jnp.dot(a.astype(BF16), b.astype(BF16), preferred_element_type=F32)


def _bdot_nt(a, b):
    return lax.dot_general(a.astype(BF16), b.astype(BF16), (((1,), (1,)), ((), ())),
                           preferred_element_type=F32)


def _bdot_tn(a, b):
    return lax.dot_general(a.astype(BF16), b.astype(BF16), (((0,), (0,)), ((), ())),
                           preferred_element_type=F32)


def _softplus(z):
    return jnp.maximum(z, 0.0) + jnp.log1p(jnp.exp(-jnp.abs(z)))


def _adaln_kernel(c_ref, w_ref, b_ref, o_ref):
    c = c_ref[...]
    s = c * jax.nn.sigmoid(c)
    o_ref[...] = _bdot(s, w_ref[...]) + b_ref[...]


def adaln_call(cvec, ada_w, ada_b, layer, tn=512):
    rows, d = cvec.shape
    n = ada_w.shape[2]
    return pl.pallas_call(
        _adaln_kernel,
        out_shape=jax.ShapeDtypeStruct((rows, n), F32),
        grid=(n // tn,),
        in_specs=[
            pl.BlockSpec((rows, d), lambda j: (0, 0)),
            pl.BlockSpec((None, d, tn), lambda j: (layer, 0, j)),
            pl.BlockSpec((None, 1, tn), lambda j: (layer, 0, j)),
        ],
        out_specs=pl.BlockSpec((rows, tn), lambda j: (0, j)),
        compiler_params=_cparams(("parallel",)),
        name="adaln",
    )(cvec, ada_w, ada_b)


def _embed_kernel(x_ref, ctx_ref, sr, cr, sc, cc, o_ref, *, n_lat_tiles, tiles_per_tab):
    i = pl.program_id(0)
    tm = o_ref.shape[0]
    g_rows = tm // GRID_W

    @pl.when(i < n_lat_tiles)
    def _():
        base = (i % tiles_per_tab) * g_rows

        def by_row(t_ref):
            return jnp.concatenate([jnp.broadcast_to(t_ref[pl.ds(base + g, 1), :], (GRID_W, t_ref.shape[1]))
                                    for g in range(g_rows)], axis=0)

        def by_col(t_ref):
            return jnp.concatenate([t_ref[...]] * g_rows, axis=0)

        o_ref[...] = x_ref[...] + jnp.concatenate([by_row(sr), by_row(cr), by_col(sc), by_col(cc)], axis=1)

    @pl.when(i >= n_lat_tiles)
    def _():
        o_ref[...] = ctx_ref[...]


def embed_call(x2d, ctx2d, seq_len):
    n_lat, d = x2d.shape
    n_ctx = ctx2d.shape[0]
    tm = LN_ROWS
    quarter = d // 4
    g_rows = tm // GRID_W
    assert tm % GRID_W == 0 and SUBLANES % g_rows == 0 and seq_len % (SUBLANES * GRID_W) == 0
    omega = POS_BASE ** (-jnp.arange(quarter, dtype=F32) / quarter)
    ang_r = jnp.arange(seq_len // GRID_W, dtype=F32)[:, None] * omega[None, :]
    ang_c = jnp.arange(GRID_W, dtype=F32)[:, None] * omega[None, :]
    n_lat_tiles = n_lat // tm
    tiles_per_tab = SUBLANES // g_rows
    tabs_per_seq = seq_len // GRID_W // SUBLANES
    tab = pl.BlockSpec((SUBLANES, quarter), lambda i: ((i // tiles_per_tab) % tabs_per_seq, 0))
    col = pl.BlockSpec((GRID_W, quarter), lambda i: (0, 0))
    return pl.pallas_call(
        functools.partial(_embed_kernel, n_lat_tiles=n_lat_tiles, tiles_per_tab=tiles_per_tab),
        out_shape=jax.ShapeDtypeStruct((n_lat + n_ctx, d), F32),
        grid=((n_lat + n_ctx) // tm,),
        in_specs=[pl.BlockSpec((tm, d), lambda i: (jnp.minimum(i, n_lat_tiles - 1), 0)),
                  pl.BlockSpec((tm, d), lambda i: (jnp.maximum(i - n_lat_tiles, 0), 0)),
                  tab, tab, col, col],
        out_specs=pl.BlockSpec((tm, d), lambda i: (i, 0)),
        compiler_params=_cparams(("parallel",)),
        name="embed",
    )(x2d, ctx2d, jnp.sin(ang_r), jnp.cos(ang_r), jnp.sin(ang_c), jnp.cos(ang_c))


def _mm_mod_kernel(x_ref, sh_ref, sc_ref, w_ref, o_ref, xb_ref):
    @pl.when(pl.program_id(1) == 0)
    def _():
        xb_ref[...] = (x_ref[...] * (1.0 + sc_ref[...]) + sh_ref[...]).astype(BF16)

    o_ref[...] = jnp.dot(xb_ref[...], w_ref[...], preferred_element_type=F32)


def matmul_mod_call(x, mods, w, layer_w, *, rows_per_mod, n_mod_rows, tn, n_rows=None, n_cols=None):
    r, d = x.shape
    n = w.shape[2] if n_cols is None else n_cols
    r = r if n_rows is None else n_rows
    tm = MM_ROWS

    def midx(i):
        return jnp.minimum(i * tm // rows_per_mod, n_mod_rows - 1)

    return pl.pallas_call(
        _mm_mod_kernel,
        out_shape=jax.ShapeDtypeStruct((r, n), F32),
        grid=(r // tm, n // tn),
        in_specs=[
            pl.BlockSpec((tm, d), lambda i, j: (i, 0)),
            pl.BlockSpec((None, 1, d), lambda i, j: (midx(i), 0, 0)),
            pl.BlockSpec((None, 1, d), lambda i, j: (midx(i), 0, 1)),
            pl.BlockSpec((None, d, tn), lambda i, j: (layer_w, 0, j)),
        ],
        out_specs=pl.BlockSpec((tm, tn), lambda i, j: (i, j)),
        scratch_shapes=[pltpu.VMEM((tm, d), BF16)],
        compiler_params=_cparams(("parallel", "arbitrary")),
        name="proj_in",
    )(x, mods, mods, w)


def _mm_kernel(x_ref, w_ref, o_ref):
    o_ref[...] = jnp.dot(x_ref[...], w_ref[...], preferred_element_type=F32)


def matmul_call(x, w, layer_w, *, tn, n_rows=None):
    r, k = x.shape
    n = w.shape[2]
    r = r if n_rows is None else n_rows
    tm = MM_ROWS
    return pl.pallas_call(
        _mm_kernel,
        out_shape=jax.ShapeDtypeStruct((r, n), F32),
        grid=(r // tm, n // tn),
        in_specs=[
            pl.BlockSpec((tm, k), lambda i, j: (i, 0)),
            pl.BlockSpec((None, k, tn), lambda i, j: (layer_w, 0, j)),
        ],
        out_specs=pl.BlockSpec((tm, tn), lambda i, j: (i, j)),
        compiler_params=_cparams(("parallel", "arbitrary")),
        name="proj_out",
    )(x, w)


def _tile_flags(i, tm, n_lat_rows, lat_len, ctx_len):
    row = i * tm
    in_lat = row < n_lat_rows
    pos = jnp.where(in_lat, row % lat_len, (row - n_lat_rows) % ctx_len)
    seq = jnp.where(in_lat, lat_len, ctx_len)
    return pos == 0, pos + tm == seq


def _row_shift(c, prev_row, next_row):
    tm = c.shape[0]
    rows = lax.broadcasted_iota(jnp.int32, c.shape, 0)
    xp = jnp.where(rows == 0, prev_row, pltpu.roll(c, 1, axis=0))
    xn = jnp.where(rows == tm - 1, next_row, pltpu.roll(c, tm - 1, axis=0))
    return xp, xn


def _head_ones():
    r = lax.broadcasted_iota(jnp.int32, (LANES, LANES), 0) // HEAD_DIM
    c = lax.broadcasted_iota(jnp.int32, (LANES, LANES), 1) // HEAD_DIM
    return (r == c).astype(BF16)


def _head_sum(x, ones):
    w = x.shape[1]
    parts = [jnp.dot(x[:, j:j + LANES].astype(BF16), ones, preferred_element_type=F32)
             for j in range(0, w, LANES)]
    return jnp.concatenate(parts, axis=1)


def _halo_specs(tm, width, col, n_rows):
    per = tm // SUBLANES
    last = n_rows // SUBLANES - 1
    return [
        pl.BlockSpec((tm, width), lambda i: (i, col)),
        pl.BlockSpec((SUBLANES, width), lambda i: (jnp.maximum(i * per - 1, 0), col)),
        pl.BlockSpec((SUBLANES, width), lambda i: (jnp.minimum((i + 1) * per, last), col)),
    ]


def _rwkv_prep_kernel(rc, rp, rn, kc, kp, kn, vc, vp, vn, lc, lp, ln,
                      mu_r, mu_k, mu_v, mu_l, w0, w2, a0, a2, g2, kk_s, ka_s,
                      r_o, v_o, kk_o, lw_o, kd_o, bb_o, g_o, *, tm, n_lat_rows, lat_len, ctx_len):
    first, last = _tile_flags(pl.program_id(0), tm, n_lat_rows, lat_len, ctx_len)

    def shifted(c_ref, p_ref, n_ref, mu_ref):
        c = c_ref[...]
        prev_row = jnp.where(first, 0.0, p_ref[SUBLANES - 1:SUBLANES, :])
        next_row = jnp.where(last, 0.0, n_ref[0:1, :])
        xp, xn = _row_shift(c, prev_row, next_row)
        return c + mu_ref[...] * (0.5 * (xp + xn) - c)

    r = shifted(rc, rp, rn, mu_r)
    k = shifted(kc, kp, kn, mu_k)
    v = shifted(vc, vp, vn, mu_v)
    lo = shifted(lc, lp, ln, mu_l)
    r_o[...] = r
    v_o[...] = v

    ones = _head_ones()
    kk = k * kk_s[...]
    ss = _head_sum(kk * kk, ones)
    kk = kk * lax.rsqrt(jnp.maximum(ss, 1e-24))
    kk_o[...] = kk

    for d in range(2):
        wd = lo[:, d * LANES:(d + 1) * LANES]
        ad = lo[:, (2 + d) * LANES:(3 + d) * LANES]
        z = w0[d:d + 1, :] + _bdot(jnp.tanh(wd), w2[d])
        log_w = -_softplus(-z) - 0.5
        lw_o[d] = -jnp.exp(log_w)
        icl = jax.nn.sigmoid(a0[d:d + 1, :] + _bdot(ad, a2[d]))
        kd_o[d] = k * (1.0 + (icl - 1.0) * ka_s[...])
        bb_o[d] = kk * icl
    gd = lo[:, 4 * LANES:4 * LANES + GATE_LORA]
    g_o[...] = _bdot(jax.nn.sigmoid(gd), g2[...])


def rwkv_prep_call(p, prm, *, n_lat_rows, lat_len, ctx_len, col_r, col_k, col_v, col_l):
    r_rows = p.shape[0]
    tm = PREP_ROWS
    aw = prm["mu_r"].shape[1]
    full = lambda shape: pl.BlockSpec(shape, lambda i: (0,) * len(shape))
    in_specs = (_halo_specs(tm, aw, col_r, r_rows) + _halo_specs(tm, aw, col_k, r_rows)
                + _halo_specs(tm, aw, col_v, r_rows) + _halo_specs(tm, LORA_PAD, col_l, r_rows)
                + [full((1, aw)), full((1, aw)), full((1, aw)), full((1, LORA_PAD)),
                   full((2, aw)), full((2, LANES, aw)), full((2, aw)), full((2, LANES, aw)),
                   full((GATE_LORA, aw)), full((1, aw)), full((1, aw))])
    row_spec = pl.BlockSpec((tm, aw), lambda i: (i, 0))
    dir_spec = pl.BlockSpec((2, tm, aw), lambda i: (0, i, 0))
    one = jax.ShapeDtypeStruct((r_rows, aw), F32)
    two = jax.ShapeDtypeStruct((2, r_rows, aw), F32)
    kern = functools.partial(_rwkv_prep_kernel, tm=tm, n_lat_rows=n_lat_rows, lat_len=lat_len, ctx_len=ctx_len)
    return pl.pallas_call(
        kern,
        out_shape=(one, one, one, two, two, two, one),
        grid=(r_rows // tm,),
        in_specs=in_specs,
        out_specs=(row_spec, row_spec, row_spec, dir_spec, dir_spec, dir_spec, row_spec),
        compiler_params=_cparams(("parallel",)),
        name="rwkv_prep",
    )(p, p, p, p, p, p, p, p, p, p, p, p,
      prm["mu_r"], prm["mu_k"], prm["mu_v"], prm["mu_l"], prm["w0"], prm["w2"], prm["a0"], prm["a2"],
      prm["g2"], prm["k_k"], prm["k_a"])


def _scan_block(b, s, *, reverse, n_lat_blocks_total, lat_blocks):
    ctx_blk = n_lat_blocks_total + b
    lat_blk = b * lat_blocks + ((lat_blocks - s) if reverse else (s - 1))
    return jnp.where(s == 0, ctx_blk, lat_blk)


def _rwkv_scan_kernel(r_ref, lw_ref, k_ref, v_ref, kk_ref, bb_ref, y_ref, s_ref, *, reverse):
    @pl.when(pl.program_id(2) == 0)
    def _():
        s_ref[...] = jnp.zeros_like(s_ref)

    L = CHUNK
    ri = lax.broadcasted_iota(jnp.int32, (L, L), 0)
    ci = lax.broadcasted_iota(jnp.int32, (L, L), 1)
    tri = ((ci >= ri) if reverse else (ci <= ri)).astype(F32)
    r2 = lax.broadcasted_iota(jnp.int32, (2 * L, 2 * L), 0) % L
    c2 = lax.broadcasted_iota(jnp.int32, (2 * L, 2 * L), 1) % L
    strict = (c2 > r2) if reverse else (c2 < r2)
    incl = (c2 >= r2) if reverse else (c2 <= r2)
    rr = lax.broadcasted_iota(jnp.int32, (2 * L, 2 * L), 0)
    cc = lax.broadcasted_iota(jnp.int32, (2 * L, 2 * L), 1)
    eye = (rr == cc).astype(F32)
    off_masks = []
    sz = 2
    while sz <= L:
        late, early = (cc, rr) if reverse else (rr, cc)
        off_masks.append((rr // sz == cc // sz) & (late % sz >= sz // 2) & (early % sz < sz // 2))
        sz *= 2
    lane = lax.broadcasted_iota(jnp.int32, (L, LANES), 1)
    head0 = lane < HEAD_DIM

    def stack2(x):
        return jnp.concatenate([jnp.where(head0, x, 0.0), jnp.where(head0, 0.0, x)], axis=0)

    n_sub = SCAN_ROWS // L
    n_pair = r_ref.shape[1] // LANES
    order = [(n_sub - 1 - j) if reverse else j for j in range(n_sub)]

    keys = [(c, p) for c in order for p in range(n_pair)]
    a2, b2, k2, q2, v2, bt2, kt2, decay = ({} for _ in range(8))
    for c in order:
        rows = pl.ds(c * L, L)
        lw_all = lw_ref[rows, :]
        cum_all = jnp.dot(tri, lw_all, preferred_element_type=F32, precision=lax.Precision.HIGHEST)
        for p in range(n_pair):
            cols = slice(p * LANES, (p + 1) * LANES)
            lw = lw_all[:, cols]
            cum = cum_all[:, cols]
            tot = cum[0:1, :] if reverse else cum[L - 1:L, :]
            e_pos = jnp.exp(cum)
            e_neg = jnp.exp(-cum)
            e_rem = jnp.exp(tot - cum)
            kk = kk_ref[rows, cols]
            kd = k_ref[rows, cols]
            bb = bb_ref[rows, cols]
            key = (c, p)
            a2[key] = stack2(-kk * jnp.exp(cum - lw))
            b2[key] = stack2(bb * e_neg)
            k2[key] = stack2(kd * e_neg)
            q2[key] = stack2(r_ref[rows, cols] * e_pos)
            v2[key] = stack2(v_ref[rows, cols])
            bt2[key] = stack2(bb * e_rem)
            kt2[key] = stack2(kd * e_rem)
            decay[key] = jnp.exp(tot)

    hh = 2 * L
    sc = {k: _bdot_nt(jnp.concatenate([a2[k], q2[k]], axis=0), jnp.concatenate([b2[k], k2[k]], axis=0))
          for k in keys}
    mm = {k: jnp.where(strict, sc[k][:hh, :hh], 0.0) for k in keys}
    nn = {k: jnp.where(strict, sc[k][:hh, hh:], 0.0) for k in keys}
    qq = {k: jnp.where(incl, sc[k][hh:, :hh], 0.0) for k in keys}
    zz = {k: jnp.where(incl, sc[k][hh:, hh:], 0.0) for k in keys}
    tinv = {k: eye + jnp.where(off_masks[0], mm[k], 0.0) for k in keys}
    for off in off_masks[1:]:
        half_step = {k: _bdot(tinv[k], jnp.where(off, mm[k], 0.0)) for k in keys}
        tinv = {k: tinv[k] + _bdot(half_step[k], tinv[k]) for k in keys}
    nzv = {k: _bdot(jnp.concatenate([nn[k], zz[k]], axis=0), v2[k]) for k in keys}
    px = {k: _bdot(tinv[k], jnp.concatenate([a2[k], nzv[k][:hh]], axis=1)) for k in keys}
    qpx = {k: _bdot(qq[k], px[k]) for k in keys}
    g_mat = {k: q2[k] + qpx[k][:, :LANES] for k in keys}
    y_loc = {k: qpx[k][:, LANES:] + nzv[k][hh:] for k in keys}
    pb = {k: _bdot_tn(px[k], bt2[k]) for k in keys}
    phi = {k: pb[k][:LANES] for k in keys}
    psi = {k: pb[k][LANES:] + _bdot_tn(v2[k], kt2[k]) for k in keys}
    items = {k: (g_mat[k], y_loc[k], decay[k], phi[k], psi[k]) for k in keys}

    st = [s_ref[p] for p in range(n_pair)]
    for c in order:
        for p in range(n_pair):
            g, yl, dec, ph, ps = items[(c, p)]
            y2 = _bdot_nt(g, st[p]) + yl
            y_ref[pl.ds(c * L, L), p * LANES:(p + 1) * LANES] = y2[:L, :] + y2[L:, :]
            st[p] = st[p] * dec + _bdot(st[p], ph) + ps
    for p in range(n_pair):
        s_ref[p] = st[p]


def rwkv_scan_call(r, lw, kd, v, kk, bb, *, reverse, batch, lat_len, ctx_len):
    r_rows, aw = r.shape
    d = 1 if reverse else 0
    lat_blocks = lat_len // SCAN_ROWS
    width = SCAN_PAIRS * LANES
    blk = functools.partial(_scan_block, reverse=reverse, n_lat_blocks_total=batch * lat_blocks,
                            lat_blocks=lat_blocks)
    row_spec = pl.BlockSpec((SCAN_ROWS, width), lambda b, h, s: (blk(b, s), h))
    dir_spec = pl.BlockSpec((None, SCAN_ROWS, width), lambda b, h, s: (d, blk(b, s), h))
    return pl.pallas_call(
        functools.partial(_rwkv_scan_kernel, reverse=reverse),
        out_shape=jax.ShapeDtypeStruct((r_rows, aw), F32),
        grid=(batch, aw // width, 1 + lat_blocks),
        in_specs=[row_spec, dir_spec, dir_spec, row_spec, row_spec, dir_spec],
        out_specs=row_spec,
        scratch_shapes=[pltpu.VMEM((SCAN_PAIRS, LANES, LANES), F32)],
        compiler_params=_cparams(("parallel", "parallel", "arbitrary")),
        name="rwkv_scan_bwd" if reverse else "rwkv_scan_fwd",
    )(r, lw, kd, v, kk, bb)


def _lru_kernel(xcf, xpf, xnf, xcb, xpb, xnb, cw, cb, wcat, bcat, lsl, hf_ref, hb_ref,
                af_ref, uf_ref, ab_ref, ub_ref, carry_f, carry_b, *, lat_blocks):
    s = pl.program_id(1)

    @pl.when(s == 0)
    def _():
        carry_f[...] = jnp.zeros_like(carry_f)
        carry_b[...] = jnp.zeros_like(carry_b)

    for d, (xc_ref, xp_ref, xn_ref, a_ref, u_ref) in enumerate(((xcf, xpf, xnf, af_ref, uf_ref),
                                                                (xcb, xpb, xnb, ab_ref, ub_ref))):
        lat_idx = (lat_blocks - s) if d == 1 else (s - 1)
        first = (s == 0) | (lat_idx == 0)
        last = (s == 0) | (lat_idx == lat_blocks - 1)
        _lru_gates(xc_ref, xp_ref, xn_ref, cw, cb, wcat.at[d], bcat.at[d], lsl.at[d], a_ref, u_ref, first, last)

    w = af_ref.shape[1]
    rows = lax.broadcasted_iota(jnp.int32, (SUBLANES, w), 0)
    n_groups = SCAN_ROWS // SUBLANES

    def group_scan(a_ref, u_ref, h_ref, g, carry, reverse):
        sl = pl.ds(pl.multiple_of(g * SUBLANES, SUBLANES), SUBLANES)
        a = a_ref[sl, :]
        u = u_ref[sl, :]
        for sh in (1, 2, 4):
            if reverse:
                ok = rows < SUBLANES - sh
                a_s = pltpu.roll(a, SUBLANES - sh, axis=0)
                u_s = pltpu.roll(u, SUBLANES - sh, axis=0)
            else:
                ok = rows >= sh
                a_s = pltpu.roll(a, sh, axis=0)
                u_s = pltpu.roll(u, sh, axis=0)
            u = jnp.where(ok, a * u_s + u, u)
            a = jnp.where(ok, a * a_s, a)
        h = a * carry + u
        h_ref[sl, :] = h
        return h[0:1, :] if reverse else h[SUBLANES - 1:SUBLANES, :]

    def body(gi, carry):
        cf, cb_ = carry
        return (group_scan(af_ref, uf_ref, hf_ref, gi, cf, False),
                group_scan(ab_ref, ub_ref, hb_ref, n_groups - 1 - gi, cb_, True))

    cf, cb_ = lax.fori_loop(0, n_groups, body, (carry_f[...], carry_b[...]))
    carry_f[...] = cf
    carry_b[...] = cb_


def _lru_gates(xc_ref, xp_ref, xn_ref, cw, cb, wcat, bcat, lsl, a_ref, u_ref, first, last):
    tm = SCAN_ROWS
    x = xc_ref[...]
    prev_row = jnp.where(first, 0.0, xp_ref[SUBLANES - 1:SUBLANES, :])
    next1 = jnp.where(last, 0.0, xn_ref[0:1, :])
    next2 = jnp.where(last, 0.0, xn_ref[1:2, :])
    row_id = lax.broadcasted_iota(jnp.int32, x.shape, 0)
    xm1 = jnp.where(row_id == 0, prev_row, pltpu.roll(x, 1, axis=0))
    xp1 = jnp.where(row_id == tm - 1, next1, pltpu.roll(x, tm - 1, axis=0))
    xp2 = jnp.where(row_id == tm - 1, next2, jnp.where(row_id == tm - 2, next1, pltpu.roll(x, tm - 2, axis=0)))
    xc = cb[...] + xm1 * cw[0:1, :] + x * cw[1:2, :] + xp1 * cw[2:3, :] + xp2 * cw[3:4, :]
    bwid = LRU_BLOCK_W
    for n in range(x.shape[1] // bwid):
        cols = slice(n * bwid, (n + 1) * bwid)
        xn = xc[:, cols]
        gates = _bdot(xn, wcat[n]) + bcat[:, 2 * n * bwid:2 * (n + 1) * bwid]
        log_a = jax.nn.sigmoid(gates[:, :bwid]) * lsl[:, cols]
        a = jnp.exp(log_a)
        om = -jnp.tanh(log_a) * (a * a + 1.0)
        a_ref[:, cols] = a
        u_ref[:, cols] = xn * jax.nn.sigmoid(gates[:, bwid:]) * jnp.sqrt(om)


def lru_call(p, prm, *, batch, lat_len, ctx_len, col_x):
    r_rows = p.shape[0]
    bw = prm["conv_b"].shape[1]
    nb = bw // LRU_BLOCK_W
    lat_blocks = lat_len // SCAN_ROWS
    per = SCAN_ROWS // SUBLANES
    last8 = r_rows // SUBLANES - 1
    full = lambda shape: pl.BlockSpec(shape, lambda b, s: (0,) * len(shape))
    x_specs, out_specs = [], []
    for reverse in (False, True):
        blk = functools.partial(_scan_block, reverse=reverse, n_lat_blocks_total=batch * lat_blocks,
                                lat_blocks=lat_blocks)
        x_specs += [
            pl.BlockSpec((SCAN_ROWS, bw), lambda b, s, blk=blk: (blk(b, s), col_x)),
            pl.BlockSpec((SUBLANES, bw), lambda b, s, blk=blk: (jnp.maximum(blk(b, s) * per - 1, 0), col_x)),
            pl.BlockSpec((SUBLANES, bw), lambda b, s, blk=blk: (jnp.minimum((blk(b, s) + 1) * per, last8), col_x)),
        ]
        out_specs.append(pl.BlockSpec((SCAN_ROWS, bw), lambda b, s, blk=blk: (blk(b, s), 0)))
    out = jax.ShapeDtypeStruct((r_rows, bw), F32)
    tile = pltpu.VMEM((SCAN_ROWS, bw), F32)
    return pl.pallas_call(
        functools.partial(_lru_kernel, lat_blocks=lat_blocks),
        out_shape=(out, out),
        grid=(batch, 1 + lat_blocks),
        in_specs=x_specs + [full((CONV_W, bw)), full((1, bw)), full((2, nb, LRU_BLOCK_W, 2 * LRU_BLOCK_W)),
                            full((2, 1, 2 * bw)), full((2, 1, bw))],
        out_specs=tuple(out_specs),
        scratch_shapes=[tile, tile, tile, tile, pltpu.VMEM((1, bw), F32), pltpu.VMEM((1, bw), F32)],
        compiler_params=_cparams(("parallel", "arbitrary")),
        name="lru",
    )(p, p, p, p, p, p, prm["conv_w"], prm["conv_b"], prm["wcat"], prm["bcat"], prm["lsl"])


def _even_post_kernel(yf, yb, r_ref, v_ref, kdf, kdb, g_ref, hf, hb, gb_ref, rk, lnw, lnb, o_ref):
    ones = _head_ones()
    y = yf[...] + yb[...]
    inv_n = 1.0 / HEAD_DIM
    yc = y - _head_sum(y, ones) * inv_n
    var = _head_sum(yc * yc, ones) * inv_n
    hn = yc * lax.rsqrt(var + LNX_EPS) * lnw[...] + lnb[...]
    r = r_ref[...]
    v = v_ref[...]
    bonus = _head_sum(r * kdf[...] * rk[...], ones) * v + _head_sum(r * kdb[...] * rk[...], ones) * v
    ya = (hn + bonus) * g_ref[...]
    yl = (hf[...] + hb[...]) * jax.nn.gelu(gb_ref[...])
    aw = ya.shape[1]
    o_ref[:, :aw] = ya.astype(BF16)
    o_ref[:, aw:] = yl.astype(BF16)


def even_post_call(yf, yb, r, v, kd, g, hf, hb, p, prm, *, col_gb):
    r_rows, aw = yf.shape
    bw = hf.shape[1]
    tm = PREP_ROWS
    row = lambda w: pl.BlockSpec((tm, w), lambda i: (i, 0))
    full = lambda shape: pl.BlockSpec(shape, lambda i: (0,) * len(shape))
    return pl.pallas_call(
        _even_post_kernel,
        out_shape=jax.ShapeDtypeStruct((r_rows, aw + bw), BF16),
        grid=(r_rows // tm,),
        in_specs=[row(aw), row(aw), row(aw), row(aw),
                  pl.BlockSpec((None, tm, aw), lambda i: (0, i, 0)),
                  pl.BlockSpec((None, tm, aw), lambda i: (1, i, 0)),
                  row(aw), row(bw), row(bw),
                  pl.BlockSpec((tm, bw), lambda i: (i, col_gb)),
                  full((1, aw)), full((1, aw)), full((1, aw))],
        out_specs=row(aw + bw),
        compiler_params=_cparams(("parallel",)),
        name="even_post",
    )(yf, yb, r, v, kd, kd, g, hf, hb, p, prm["r_k"], prm["lnx_w"], prm["lnx_b"])


def _mlstm_kernel(qf, kf, vf, brf, bcf, irf, icf, qb, kb, vb, brb, bcb, irb, icb, hf_ref, hb_ref,
                  cf, nf, mf, cb, nb, mb):
    @pl.when(pl.program_id(2) == 0)
    def _():
        for ref in (cf, nf, mf, cb, nb, mb):
            ref[...] = jnp.zeros_like(ref)

    _mlstm_block(qf, kf, vf, brf, bcf, irf, icf, hf_ref, cf, nf, mf, reverse=False)
    _mlstm_block(qb, kb, vb, brb, bcb, irb, icb, hb_ref, cb, nb, mb, reverse=True)


def _mlstm_block(q_ref, k_ref, v_ref, br_ref, bc_ref, ir_ref, ic_ref, h_ref, c_ref, n_ref, m_ref, *, reverse):
    L = MLSTM_CHUNK
    ri = lax.broadcasted_iota(jnp.int32, (L, L), 0)
    ci = lax.broadcasted_iota(jnp.int32, (L, L), 1)
    causal = (ci >= ri) if reverse else (ci <= ri)
    n_sub = SCAN_ROWS // L
    order = [(n_sub - 1 - j) if reverse else j for j in range(n_sub)]
    scale = C_QK_HEAD ** -0.5

    m_cur = m_ref[...]
    m_in, ew, ec = {}, {}, {}
    for c in order:
        b_r = br_ref[c:c + 1, :]
        b_tot = b_r[:, 0:1] if reverse else b_r[:, L - 1:L]
        w_in = b_tot - bc_ref[:, c:c + 1] + ic_ref[:, c:c + 1]
        carry_log = b_tot + m_cur
        m_new = jnp.maximum(carry_log, jnp.max(w_in, axis=0, keepdims=True))
        m_in[c] = m_cur
        ew[c] = jnp.exp(w_in - m_new)
        ec[c] = jnp.exp(carry_log - m_new)
        m_cur = m_new
    m_ref[...] = m_cur

    q = {c: (q_ref[pl.ds(c * L, L), :] * scale).astype(BF16) for c in order}
    k = {c: k_ref[pl.ds(c * L, L), :] for c in order}
    v = {c: v_ref[pl.ds(c * L, L), :].astype(BF16) for c in order}
    qk = {c: _bdot_nt(q[c], k[c]) for c in order}
    m_t, e_inter, scores = {}, {}, {}
    for c in order:
        b_c = bc_ref[:, c:c + 1]
        dmat = jnp.where(causal, b_c - br_ref[c:c + 1, :] + ir_ref[c:c + 1, :], -jnp.inf)
        inter = b_c + m_in[c]
        m_t[c] = jnp.maximum(inter, jnp.max(dmat, axis=1, keepdims=True))
        scores[c] = qk[c] * jnp.exp(dmat - m_t[c])
        e_inter[c] = jnp.exp(inter - m_t[c])
    intra = {c: _bdot(scores[c], v[c]) for c in order}
    kw = {c: k[c] * ew[c] for c in order}
    kv = {c: _bdot_tn(kw[c], v[c]) for c in order}

    c_cur = c_ref[...]
    n_cur = n_ref[...]
    c_in, n_in = {}, {}
    for c in order:
        c_in[c] = c_cur
        n_in[c] = n_cur
        c_cur = ec[c] * c_cur + kv[c]
        n_cur = ec[c] * n_cur + jnp.sum(kw[c], axis=0, keepdims=True)
    c_ref[...] = c_cur
    n_ref[...] = n_cur

    qc = {c: _bdot(q[c], c_in[c]) for c in order}
    for c in order:
        qf = q[c].astype(F32)
        num = e_inter[c] * qc[c] + intra[c]
        den = (e_inter[c] * jnp.sum(qf * n_in[c], axis=1, keepdims=True)
               + jnp.sum(scores[c], axis=1, keepdims=True))
        h_ref[pl.ds(c * L, L), :] = num / jnp.maximum(jnp.abs(den), jnp.exp(-m_t[c]))


def mlstm_call(p, g_rows, g_cols, *, batch, lat_len, ctx_len, col_q, col_k, col_v):
    r_rows = p.shape[0]
    n_sub = SCAN_ROWS // MLSTM_CHUNK
    lat_blocks = lat_len // SCAN_ROWS
    in_specs, out_specs = [], []
    for d, reverse in enumerate((False, True)):
        blk = functools.partial(_scan_block, reverse=reverse, n_lat_blocks_total=batch * lat_blocks,
                                lat_blocks=lat_blocks)
        grow = lambda which, d=d, blk=blk: pl.BlockSpec(
            (None, None, None, None, n_sub, MLSTM_CHUNK), lambda b, h, s: (d, which, h, blk(b, s), 0, 0))
        gcol = lambda which, d=d, blk=blk: pl.BlockSpec(
            (None, None, None, None, MLSTM_CHUNK, n_sub), lambda b, h, s: (d, which, h, blk(b, s), 0, 0))
        in_specs += [
            pl.BlockSpec((SCAN_ROWS, C_QK_HEAD), lambda b, h, s, blk=blk: (blk(b, s), col_q + h)),
            pl.BlockSpec((SCAN_ROWS, C_QK_HEAD), lambda b, h, s, blk=blk: (blk(b, s), col_k + h)),
            pl.BlockSpec((SCAN_ROWS, C_V_HEAD), lambda b, h, s, blk=blk: (blk(b, s), col_v + h)),
            grow(0), gcol(0), grow(1), gcol(1),
        ]
        out_specs.append(pl.BlockSpec((SCAN_ROWS, C_V_HEAD), lambda b, h, s, blk=blk: (blk(b, s), h)))
    out = jax.ShapeDtypeStruct((r_rows, C_HEADS * C_V_HEAD), F32)
    state = [pltpu.VMEM((C_QK_HEAD, C_V_HEAD), F32), pltpu.VMEM((1, C_QK_HEAD), F32), pltpu.VMEM((1, 1), F32)]
    one_dir = (p, p, p, g_rows, g_cols, g_rows, g_cols)
    return pl.pallas_call(
        _mlstm_kernel,
        out_shape=(out, out),
        grid=(batch, C_HEADS, 1 + lat_blocks),
        in_specs=in_specs,
        out_specs=tuple(out_specs),
        scratch_shapes=state + state,
        compiler_params=_cparams(("parallel", "parallel", "arbitrary")),
        name="mlstm",
    )(*one_dir, *one_dir)


def _odd_post_kernel(hf, hb, o_lo, o_hi, nw, nb, y_ref):
    half_heads = C_HEADS // 2
    for hd in range(C_HEADS):
        cols = slice(hd * C_V_HEAD, (hd + 1) * C_V_HEAD)
        x = hf[:, cols] + hb[:, cols]
        xc = x - jnp.mean(x, axis=1, keepdims=True)
        var = jnp.mean(xc * xc, axis=1, keepdims=True)
        hn = xc * lax.rsqrt(var + MLSTM_NORM_EPS) * nw[:, cols] + nb[:, cols]
        o_ref = o_lo if hd < half_heads else o_hi
        oc = slice((hd % half_heads) * C_V_HEAD, (hd % half_heads + 1) * C_V_HEAD)
        y_ref[:, cols] = (hn * jax.nn.sigmoid(o_ref[:, oc])).astype(BF16)


def odd_post_call(hf, hb, p, nw, nb, *, n_rows, col_o):
    vw = hf.shape[1]
    tm = LN_ROWS
    row = pl.BlockSpec((tm, vw), lambda i: (i, 0))
    par = pl.BlockSpec((1, vw), lambda i: (0, 0))
    return pl.pallas_call(
        _odd_post_kernel,
        out_shape=jax.ShapeDtypeStruct((n_rows, vw), BF16),
        grid=(n_rows // tm,),
        in_specs=[row, row, pl.BlockSpec((tm, vw // 2), lambda i: (i, col_o)),
                  pl.BlockSpec((tm, vw // 2), lambda i: (i, col_o + 1)), par, par],
        out_specs=row,
        compiler_params=_cparams(("parallel",)),
        name="odd_post",
    )(hf, hb, p, p, nw, nb)


def _layer_norm(z, g, b):
    zc = z - jnp.mean(z, axis=1, keepdims=True)
    var = jnp.mean(zc * zc, axis=1, keepdims=True)
    return zc * lax.rsqrt(var + LN_EPS) * g + b


def _route(logits_t, bias):
    aff = [jax.nn.sigmoid(logits_t[e:e + 1, :]) for e in range(N_EXPERTS)]
    biased = [aff[e] + bias[e:e + 1, :] for e in range(N_EXPERTS)]
    best_g = best_v = None
    for g in range(N_GROUPS):
        m = biased[g * EXPERTS_PER_GROUP:(g + 1) * EXPERTS_PER_GROUP]
        pair = None
        for i in range(EXPERTS_PER_GROUP):
            for j in range(i + 1, EXPERTS_PER_GROUP):
                hi = jnp.maximum(m[i], m[j])
                lo_ = jnp.minimum(m[i], m[j])
                s = hi + lo_
                pair = s if pair is None else jnp.maximum(pair, s)
        if best_v is None:
            best_v, best_g = pair, jnp.zeros(pair.shape, jnp.int32)
        else:
            upd = pair > best_v
            best_g = jnp.where(upd, g, best_g)
            best_v = jnp.where(upd, pair, best_v)
    ids, sels = [], []
    taken = None
    for _ in range(TOP_K):
        cur_v = cur_i = cur_a = None
        for e in range(N_EXPERTS):
            ok = best_g == (e // EXPERTS_PER_GROUP)
            if taken is not None:
                ok = jnp.logical_and(ok, taken != e)
            val = jnp.where(ok, biased[e], -jnp.inf)
            if cur_v is None:
                cur_v, cur_i, cur_a = val, jnp.zeros(val.shape, jnp.int32), aff[e]
            else:
                upd = val > cur_v
                cur_i = jnp.where(upd, e, cur_i)
                cur_a = jnp.where(upd, aff[e], cur_a)
                cur_v = jnp.where(upd, val, cur_v)
        ids.append(cur_i)
        sels.append(cur_a)
        taken = cur_i
    tot = sels[0] + sels[1]
    return ids, [sels[0] / tot, sels[1] / tot]


def _ln_route_kernel(lat_ref, y_ref, gate_ref, sh_ref, sc_ref, g_ref, b_ref, rw_ref, rb_ref,
                     lat_o, h_o, id_o, gt_o):
    z = ALPHA * lat_ref[...] + gate_ref[...] * y_ref[...]
    ln = _layer_norm(z, g_ref[...], b_ref[...])
    lat_o[...] = ln
    h = ln * (1.0 + sc_ref[...]) + sh_ref[...]
    half = h.shape[1] // 2
    bits = lax.bitcast_convert_type(h.astype(BF16).astype(F32), jnp.uint32)
    h_o[...] = (bits[:, :half] >> 16) | bits[:, half:]
    logits_t = lax.dot_general(rw_ref[...], h, (((1,), (1,)), ((), ())), preferred_element_type=F32,
                               precision=lax.Precision.HIGHEST)
    ids, gts = _route(logits_t, rb_ref[...])
    id_o[...] = jnp.concatenate(ids, axis=0)
    gt_o[...] = jnp.concatenate(gts, axis=0)


def ln_route_call(lat, y, mods, norm_g, norm_b, layer, router_wt, router_b, *, n_rows, rows_per_mod, n_mod_rows):
    d = lat.shape[1]
    tm = LN_ROWS

    def midx(i):
        return jnp.minimum(i * tm // rows_per_mod, n_mod_rows - 1)

    row = pl.BlockSpec((tm, d), lambda i: (i, 0))
    packed = pl.BlockSpec((tm, d // 2), lambda i: (i, 0))
    mod = lambda k: pl.BlockSpec((None, 1, d), lambda i: (midx(i), 0, k))
    nrm = pl.BlockSpec((None, None, 1, d), lambda i: (layer, 0, 0, 0))
    full = lambda shape: pl.BlockSpec(shape, lambda i: (0,) * len(shape))
    sel = pl.BlockSpec((TOP_K, tm), lambda i: (0, i))
    return pl.pallas_call(
        _ln_route_kernel,
        out_shape=(jax.ShapeDtypeStruct((n_rows, d), F32), jax.ShapeDtypeStruct((n_rows, d // 2), jnp.uint32),
                   jax.ShapeDtypeStruct((TOP_K, n_rows), jnp.int32), jax.ShapeDtypeStruct((TOP_K, n_rows), F32)),
        grid=(n_rows // tm,),
        in_specs=[row, row, mod(2), mod(3), mod(4), nrm, nrm, full((N_EXPERTS, d)), full((N_EXPERTS, 1))],
        out_specs=(row, packed, sel, sel),
        compiler_params=_cparams(("parallel",)),
        name="ln_route",
    )(lat, y, mods, mods, mods, norm_g, norm_b, router_wt, router_b)


def _ln_combine_kernel(lat_ref, y0_ref, y1_ref, gt_ref, gate_ref, g_ref, b_ref, lat_o):
    gt = gt_ref[...]
    f = y0_ref[...] * gt[:, 0:1] + y1_ref[...] * gt[:, 1:2]
    z = ALPHA * lat_ref[...] + gate_ref[...] * f
    lat_o[...] = _layer_norm(z, g_ref[...], b_ref[...])


def ln_combine_call(lat, y_slots, gates_t, mods, norm_g, norm_b, layer, *, n_rows, rows_per_mod, n_mod_rows):
    d = lat.shape[1]
    tm = LN_ROWS
    per_k = n_rows // tm

    def midx(i):
        return jnp.minimum(i * tm // rows_per_mod, n_mod_rows - 1)

    row = pl.BlockSpec((tm, d), lambda i: (i, 0))
    nrm = pl.BlockSpec((None, None, 1, d), lambda i: (layer, 1, 0, 0))
    return pl.pallas_call(
        _ln_combine_kernel,
        out_shape=jax.ShapeDtypeStruct((n_rows, d), F32),
        grid=(per_k,),
        in_specs=[row, row, pl.BlockSpec((tm, d), lambda i: (per_k + i, 0)),
                  pl.BlockSpec((tm, TOP_K), lambda i: (i, 0)),
                  pl.BlockSpec((None, 1, d), lambda i: (midx(i), 0, 5)), nrm, nrm],
        out_specs=row,
        compiler_params=_cparams(("parallel",)),
        name="ln_combine",
    )(lat, y_slots, y_slots, gates_t, mods, norm_g, norm_b)


def _moe_kernel(nu_ref, be_ref, jw_ref, src_cur, src_nxt, dst_prv, h_hbm, wg_ref, wu_ref, wd_ref, y_hbm,
                xbuf, acc, gsem, ssem, *, n_j, n_slots):
    del be_ref, jw_ref
    i = pl.program_id(0)
    j = pl.program_id(1)
    buf = i % 2
    other = 1 - buf
    n_used = nu_ref[0]
    per_step, half = xbuf.shape[2], xbuf.shape[3]
    tm = n_j * per_step
    d = acc.shape[3]

    def row_in(tok, q, rr, b):
        return pltpu.make_async_copy(h_hbm.at[pl.ds(tok, 1)], xbuf.at[b, q, pl.ds(rr, 1)], gsem.at[b])

    def row_out(q, rr, slot, b):
        return pltpu.make_async_copy(acc.at[b, q, pl.ds(rr, 1)], y_hbm.at[pl.ds(slot, 1)], ssem.at[b])

    def send_previous(q, rr):
        r = q * per_step + rr
        slot = jnp.where(i > 0, dst_prv[0, r], n_slots + r)
        row_out(q, rr, slot, other).start()

    @pl.when((i == 0) & (j == 0))
    def _():
        acc[1] = jnp.zeros((n_j, per_step, d), F32)
        for q in range(n_j):
            def body(rr, carry, q=q):
                row_in(src_cur[0, q * per_step + rr], q, rr, 0).start()
                return carry
            lax.fori_loop(0, per_step, body, 0, unroll=8)

    @pl.when((j == 0) & (i <= n_used))
    def _():
        pltpu.make_async_copy(xbuf.at[buf], xbuf.at[buf], gsem.at[buf]).wait()

    @pl.when((j == 0) & (i < n_used))
    def _():
        acc[buf] = jnp.zeros((n_j, per_step, d), F32)

    @pl.when(i < n_used)
    def _():
        tf = wg_ref.shape[1]
        g = jnp.zeros((tm, tf), F32)
        u = jnp.zeros((tm, tf), F32)
        for c0 in range(0, half, MOE_KCHUNK):
            xp = xbuf[buf, :, :, c0:c0 + MOE_KCHUNK].reshape(tm, MOE_KCHUNK)
            lo = lax.bitcast_convert_type(xp << 16, F32)
            hi = lax.bitcast_convert_type(xp & jnp.uint32(0xFFFF0000), F32)
            g = g + _bdot(lo, wg_ref[c0:c0 + MOE_KCHUNK, :]) + _bdot(hi, wg_ref[half + c0:half + c0 + MOE_KCHUNK, :])
            u = u + _bdot(lo, wu_ref[c0:c0 + MOE_KCHUNK, :]) + _bdot(hi, wu_ref[half + c0:half + c0 + MOE_KCHUNK, :])
        hdn = (g * jax.nn.sigmoid(g) * u).astype(BF16)
        for c0 in range(0, d, MOE_NCHUNK):
            part = _bdot(hdn, wd_ref[:, c0:c0 + MOE_NCHUNK])
            acc[buf, :, :, c0:c0 + MOE_NCHUNK] += part.reshape(n_j, per_step, MOE_NCHUNK)
        for rr in range(per_step):
            row_in(src_nxt[0, j * per_step + rr], j, rr, other).start()
            send_previous(j, rr)

    @pl.when(i == n_used)
    def _():
        for rr in range(per_step):
            send_previous(j, rr)

    @pl.when((j == n_j - 1) & (i <= n_used))
    def _():
        pltpu.make_async_copy(acc.at[other], acc.at[other], ssem.at[other]).wait()


def moe_call(h_packed, n_used, block_e, j_of, src_tok, dst_slot, w_gate, w_up, w_down, layer, *, n_slots):
    d = 2 * h_packed.shape[1]
    f = w_gate.shape[3]
    tm, tf = MOE_ROWS, MOE_FTILE
    n_blocks = block_e.shape[0]
    n_j = f // tf
    assert tm % n_j == 0
    idx = lambda fn: pl.BlockSpec((None, 1, tm), fn, memory_space=pltpu.SMEM)
    grid_spec = pltpu.PrefetchScalarGridSpec(
        num_scalar_prefetch=3,
        grid=(n_blocks, n_j),
        in_specs=[
            idx(lambda i, j, nu, be, jw: (i, 0, 0)),
            idx(lambda i, j, nu, be, jw: (jnp.minimum(i + 1, n_blocks - 1), 0, 0)),
            idx(lambda i, j, nu, be, jw: (jnp.maximum(i - 1, 0), 0, 0)),
            pl.BlockSpec(memory_space=pl.ANY),
            pl.BlockSpec((None, None, d, tf), lambda i, j, nu, be, jw: (layer, be[i], 0, jw[i, j])),
            pl.BlockSpec((None, None, d, tf), lambda i, j, nu, be, jw: (layer, be[i], 0, jw[i, j])),
            pl.BlockSpec((None, None, tf, d), lambda i, j, nu, be, jw: (layer, be[i], jw[i, j], 0)),
        ],
        out_specs=pl.BlockSpec(memory_space=pl.ANY),
        scratch_shapes=[pltpu.VMEM((2, n_j, tm // n_j, d // 2), jnp.uint32),
                        pltpu.VMEM((2, n_j, tm // n_j, d), F32),
                        pltpu.SemaphoreType.DMA((2,)), pltpu.SemaphoreType.DMA((2,))],
    )
    return pl.pallas_call(
        functools.partial(_moe_kernel, n_j=n_j, n_slots=n_slots),
        out_shape=jax.ShapeDtypeStruct((n_slots + tm, d), F32),
        grid_spec=grid_spec,
        compiler_params=_cparams(("arbitrary", "arbitrary")),
        name="moe_experts",
    )(n_used, block_e, j_of, src_tok, src_tok, dst_slot, h_packed, w_gate, w_up, w_down)


def moe_ffn(h_packed, ids, w_gate, w_up, w_down, layer):
    n_tok = h_packed.shape[0]
    n_slots = n_tok * TOP_K
    tm = MOE_ROWS
    i32 = jnp.int32
    flat_e = ids.reshape(n_slots)
    order = jnp.argsort(flat_e, stable=True).astype(i32)
    counts = jnp.sum((flat_e[:, None] == jnp.arange(N_EXPERTS, dtype=i32)[None, :]).astype(i32), axis=0)
    start = jnp.cumsum(counts) - counts
    padded = (counts + tm - 1) // tm * tm
    pad_end = jnp.cumsum(padded)
    pad_start = pad_end - padded
    n_blocks = -(-n_slots // tm) + N_EXPERTS + 1
    pos = jnp.arange(n_blocks * tm, dtype=i32)
    e_pos = jnp.minimum(jnp.sum((pos[:, None] >= pad_end[None, :]).astype(i32), axis=1), N_EXPERTS - 1)
    r_pos = pos - pad_start[e_pos]
    valid = r_pos < counts[e_pos]
    slot = order[jnp.clip(start[e_pos] + r_pos, 0, n_slots - 1)]
    src_tok = jnp.where(valid, slot % n_tok, 0).reshape(n_blocks, 1, tm)
    dst_slot = jnp.where(valid, slot, n_slots + pos % tm).reshape(n_blocks, 1, tm)
    n_used = (pad_end[-1] // tm).astype(i32)
    blk = jnp.arange(n_blocks, dtype=i32)
    block_e = e_pos.reshape(n_blocks, tm)[:, 0]
    block_e = jnp.where(blk < n_used, block_e, block_e[jnp.maximum(n_used - 1, 0)]).astype(i32)
    n_j = w_gate.shape[3] // MOE_FTILE
    tiles = jnp.arange(n_j, dtype=i32)[None, :]
    j_of = jnp.where((blk % 2 == 0)[:, None], tiles, n_j - 1 - tiles)
    j_of = jnp.where((blk < n_used)[:, None], j_of, j_of[jnp.maximum(n_used - 1, 0), n_j - 1]).astype(i32)
    return moe_call(h_packed, n_used.reshape(1), block_e, j_of, src_tok.astype(i32), dst_slot.astype(i32),
                    w_gate, w_up, w_down, layer, n_slots=n_slots)


def _pad_cols(w, width):
    return jnp.pad(w, [(0, 0)] * (w.ndim - 1) + [(0, width - w.shape[-1])])


def _even_weights(ev_w_in, rwkv_mu, aw, bw):
    rk = 3 * aw
    lo0 = rk
    pieces = [DECAY_LORA, DECAY_LORA, ICL_LORA, ICL_LORA]

    def regroup(t):
        out = [t[..., rk + 2 * (DECAY_LORA + ICL_LORA) + GATE_LORA:],
               t[..., :rk]]
        off = lo0
        for wdt in pieces:
            out.append(_pad_cols(t[..., off:off + wdt], LANES))
            off += wdt
        out.append(t[..., off:off + GATE_LORA])
        used = 4 * LANES + GATE_LORA
        out.append(jnp.zeros(t.shape[:-1] + (LORA_PAD - used,), t.dtype))
        return jnp.concatenate(out, axis=-1)

    w = regroup(ev_w_in).astype(BF16)
    rwkv_cols = rk + 2 * (DECAY_LORA + ICL_LORA) + GATE_LORA
    mu_full = jnp.concatenate([rwkv_mu, jnp.zeros(rwkv_mu.shape[:-1] + (2 * bw,), rwkv_mu.dtype)], axis=-1)
    mu = regroup(mu_full)
    del rwkv_cols
    return w, mu


def kernel(x, c, ctx, c_ctx, ada_w, ada_b, norm_g, norm_b,
           ev_w_in, ev_w_out, rwkv_mu, rwkv_w0, rwkv_w2, rwkv_a0, rwkv_a2, rwkv_g2,
           rwkv_k_k, rwkv_k_a, rwkv_r_k, rwkv_lnx_w, rwkv_lnx_b,
           lru_conv_w, lru_conv_b, lru_wa, lru_ba, lru_wx, lru_bx, lru_lam,
           od_w_in, od_w_out, mlstm_ig_b, mlstm_fg_b, mlstm_norm_w, mlstm_norm_b,
           router_w, router_b, moe_w_gate, moe_w_up, moe_w_down):
    bsz, s_len, dim = x.shape
    n_ctx = ctx.shape[1]
    n_lat_rows = bsz * s_len
    n_all_rows = n_lat_rows + bsz * n_ctx
    aw = rwkv_k_k.shape[1]
    bw = lru_conv_b.shape[1]
    assert s_len % SCAN_ROWS == 0 and n_ctx == SCAN_ROWS and n_all_rows % MM_ROWS == 0
    assert (bsz * n_ctx) % MM_ROWS == 0 and s_len % MM_ROWS == 0 and bsz + 1 <= SUBLANES
    seq = dict(n_lat_rows=n_lat_rows, lat_len=s_len, ctx_len=n_ctx)
    scan = dict(batch=bsz, lat_len=s_len, ctx_len=n_ctx)
    modk = dict(rows_per_mod=s_len, n_mod_rows=bsz + 1)

    stream = embed_call(x.reshape(n_lat_rows, dim), ctx.reshape(bsz * n_ctx, dim), s_len)
    cvec = jnp.concatenate([c, c_ctx[None, :], jnp.zeros((SUBLANES - bsz - 1, dim), c.dtype)], axis=0)
    ada_b3 = ada_b[:, None, :]
    norm_g4 = norm_g[:, :, None, :]
    norm_b4 = norm_b[:, :, None, :]
    router_wt = router_w.T
    router_b2 = router_b[:, None]

    for layer in range(DEPTH):
        last = layer == DEPTH - 1
        i = layer // 2
        mods = adaln_call(cvec, ada_w, ada_b3, layer).reshape(SUBLANES, 1, N_MOD * dim)
        if layer % 2 == 0:
            w_in, mu = _even_weights(ev_w_in, rwkv_mu, aw, bw)
            n_in = w_in.shape[2]
            p = matmul_mod_call(stream, mods, w_in, i, tn=1024, **modk)
            col_x, col_gb = 0, bw // bw
            col_r, col_k, col_v = 2 * bw // aw, 2 * bw // aw + 1, 2 * bw // aw + 2
            col_l = (2 * bw + 3 * aw) // LORA_PAD
            assert (2 * bw + 3 * aw) % LORA_PAD == 0 and n_in == 2 * bw + 3 * aw + LORA_PAD
            mu_i = mu[i]
            o_r = 2 * bw
            prm = dict(
                mu_r=mu_i[None, o_r:o_r + aw], mu_k=mu_i[None, o_r + aw:o_r + 2 * aw],
                mu_v=mu_i[None, o_r + 2 * aw:o_r + 3 * aw], mu_l=mu_i[None, o_r + 3 * aw:],
                w0=rwkv_w0[i], w2=_pad_rows(rwkv_w2[i], LANES).astype(BF16),
                a0=rwkv_a0[i], a2=_pad_rows(rwkv_a2[i], LANES).astype(BF16),
                g2=rwkv_g2[i].astype(BF16), k_k=rwkv_k_k[i][None, :], k_a=rwkv_k_a[i][None, :])
            r_s, v_s, kk, lw, kd, bb, g = rwkv_prep_call(p, prm, col_r=col_r, col_k=col_k, col_v=col_v,
                                                          col_l=col_l, **seq)
            nb = bw // LRU_BLOCK_W
            wcat = jnp.concatenate([lru_wa[i], lru_wx[i]], axis=-1).astype(BF16)
            bcat = jnp.stack([lru_ba[i].reshape(2, nb, LRU_BLOCK_W), lru_bx[i].reshape(2, nb, LRU_BLOCK_W)],
                             axis=2).reshape(2, 1, 2 * bw)
            lprm = dict(conv_w=lru_conv_w[i], conv_b=lru_conv_b[i][None, :], wcat=wcat, bcat=bcat,
                        lsl=(RGLRU_C * jax.nn.log_sigmoid(lru_lam[i]))[:, None, :])
            yf = rwkv_scan_call(r_s, lw, kd, v_s, kk, bb, reverse=False, **scan)
            yb = rwkv_scan_call(r_s, lw, kd, v_s, kk, bb, reverse=True, **scan)
            hf, hb = lru_call(p, lprm, col_x=col_x, **scan)
            pprm = dict(r_k=rwkv_r_k[i].reshape(1, aw), lnx_w=rwkv_lnx_w[i][None, :], lnx_b=rwkv_lnx_b[i][None, :])
            ymix = even_post_call(yf, yb, r_s, v_s, kd, g, hf, hb, p, pprm, col_gb=col_gb)
            n_rows = n_all_rows if not last else n_lat_rows
            y = matmul_call(ymix, ev_w_out.astype(BF16), i, tn=1024, n_rows=n_rows)
        else:
            qk_w = C_HEADS * C_QK_HEAD
            v_w = C_HEADS * C_V_HEAD
            main = 2 * qk_w + 2 * v_w
            w_gate = _pad_cols(od_w_in[..., main:], LANES).astype(BF16)
            p = matmul_mod_call(stream, mods, od_w_in.astype(BF16), i, tn=1024, n_cols=main, **modk)
            gp = matmul_mod_call(stream, mods, w_gate, i, tn=LANES, **modk)[:, :4 * C_HEADS]
            gp = gp.reshape(n_all_rows, 2, 2, C_HEADS)
            g_rows, g_cols = _mlstm_gates(gp, mlstm_ig_b[i], mlstm_fg_b[i])
            cols = dict(col_q=0, col_k=(qk_w + v_w) // C_QK_HEAD, col_v=(2 * qk_w + v_w) // C_V_HEAD)
            hf, hb = mlstm_call(p, g_rows, g_cols, **scan, **cols)
            n_rows = n_all_rows if not last else n_lat_rows
            ymix = odd_post_call(hf, hb, p, mlstm_norm_w[i][None, :], mlstm_norm_b[i][None, :],
                                 n_rows=n_rows, col_o=qk_w // (v_w // 2))
            y = matmul_call(ymix, od_w_out.astype(BF16), i, tn=1024, n_rows=n_rows)
        n_rows = n_all_rows if not last else n_lat_rows
        stream1, h_moe, ids, gates = ln_route_call(stream, y, mods, norm_g4, norm_b4, layer, router_wt, router_b2,
                                                   n_rows=n_rows, **modk)
        y_slots = moe_ffn(h_moe, ids, moe_w_gate, moe_w_up, moe_w_down, layer)
        stream = ln_combine_call(stream1, y_slots, gates.T, mods, norm_g4, norm_b4, layer, n_rows=n_rows, **modk)
    return stream[:n_lat_rows].reshape(bsz, s_len, dim)


def _pad_rows(w, rows):
    return jnp.pad(w, ((0, 0), (0, rows - w.shape[1]), (0, 0)))


def _mlstm_gates(gp, ig_b, fg_b):
    n_rows = gp.shape[0]
    chunk = MLSTM_CHUNK
    n_sub = SCAN_ROWS // chunk
    outs = []
    for d in range(2):
        ig = GATE_CAP * jnp.tanh((gp[:, d, 0] + ig_b[d]) / GATE_CAP)
        lf = jax.nn.log_sigmoid(GATE_CAP * jnp.tanh((gp[:, d, 1] + fg_b[d]) / GATE_CAP))
        lf = lf.reshape(n_rows // chunk, chunk, C_HEADS)
        b = jnp.cumsum(lf[:, ::-1], axis=1)[:, ::-1] if d == 1 else jnp.cumsum(lf, axis=1)
        both = jnp.stack([b, ig.reshape(n_rows // chunk, chunk, C_HEADS)], axis=0)
        outs.append(both)
    g = jnp.stack(outs, axis=0)
    g = g.reshape(2, 2, n_rows // SCAN_ROWS, n_sub, chunk, C_HEADS)
    g_rows = jnp.transpose(g, (0, 1, 5, 2, 3, 4))
    g_cols = jnp.transpose(g, (0, 1, 5, 2, 4, 3))
    return g_rows, g_cols
```

```python
import functools
import math

import jax
import jax.numpy as jnp
from jax import lax
from jax.experimental import pallas as pl
from jax.experimental.pallas import tpu as pltpu

F32 = jnp.float32
BF16 = jnp.bfloat16

DEPTH = 2
N_MOD = 6
ALPHA = (2 * DEPTH) ** 0.25
LN_EPS = 1e-5
HEAD_DIM = 64
DECAY_LORA = 96
ICL_LORA = 96
GATE_LORA = 256
LNX_EPS = 64e-5
LRU_BLOCK_W = 128
CONV_W = 4
RGLRU_C = 8.0
C_HEADS = 8
C_QK_HEAD = 256
C_V_HEAD = 512
GATE_CAP = 15.0
MLSTM_NORM_EPS = 1e-6
N_EXPERTS = 16
N_GROUPS = 4
EXPERTS_PER_GROUP = N_EXPERTS // N_GROUPS
TOP_K = 2
POS_BASE = 10000.0
GRID_W = 64

LANES = 128
SUBLANES = 8
SCAN_ROWS = 256
CHUNK = 64
MLSTM_CHUNK = 128
SCAN_PAIRS = 8
PREP_ROWS = 128
LN_ROWS = 256
MM_ROWS = 512
MOE_ROWS = 512
MOE_FTILE = 256
MOE_NCHUNK = 1024
MOE_KCHUNK = 512
LORA_PAD = 1024
VMEM_LIMIT = 60 * 1024 * 1024


def _cparams(sem):
    return pltpu.CompilerParams(dimension_semantics=sem, vmem_limit_bytes=VMEM_LIMIT)


def _bdot(a, b):
    return jnp.dot(a.astype(BF16), b.astype(BF16), preferred_element_type=F32)


def _bdot_nt(a, b):
    return lax.dot_general(a.astype(BF16), b.astype(BF16), (((1,), (1,)), ((), ())),
                           preferred_element_type=F32)


def _bdot_tn(a, b):
    return lax.dot_general(a.astype(BF16), b.astype(BF16), (((0,), (0,)), ((), ())),
                           preferred_element_type=F32)


def _softplus(z):
    return jnp.maximum(z, 0.0) + jnp.log1p(jnp.exp(-jnp.abs(z)))


def _adaln_kernel(c_ref, w_ref, b_ref, o_ref):
    c = c_ref[...]
    s = c * jax.nn.sigmoid(c)
    o_ref[...] = _bdot(s, w_ref[...]) + b_ref[...]


def adaln_call(cvec, ada_w, ada_b, layer, tn=512):
    rows, d = cvec.shape
    n = ada_w.shape[2]
    return pl.pallas_call(
        _adaln_kernel,
        out_shape=jax.ShapeDtypeStruct((rows, n), F32),
        grid=(n // tn,),
        in_specs=[
            pl.BlockSpec((rows, d), lambda j: (0, 0)),
            pl.BlockSpec((None, d, tn), lambda j: (layer, 0, j)),
            pl.BlockSpec((None, 1, tn), lambda j: (layer, 0, j)),
        ],
        out_specs=pl.BlockSpec((rows, tn), lambda j: (0, j)),
        compiler_params=_cparams(("parallel",)),
        name="adaln",
    )(cvec, ada_w, ada_b)


def _embed_kernel(x_ref, ctx_ref, sr, cr, sc, cc, o_ref, *, n_lat_tiles, tiles_per_tab):
    i = pl.program_id(0)
    tm = o_ref.shape[0]
    g_rows = tm // GRID_W

    @pl.when(i < n_lat_tiles)
    def _():
        base = (i % tiles_per_tab) * g_rows

        def by_row(t_ref):
            return jnp.concatenate([jnp.broadcast_to(t_ref[pl.ds(base + g, 1), :], (GRID_W, t_ref.shape[1]))
                                    for g in range(g_rows)], axis=0)

        def by_col(t_ref):
            return jnp.concatenate([t_ref[...]] * g_rows, axis=0)

        o_ref[...] = x_ref[...] + jnp.concatenate([by_row(sr), by_row(cr), by_col(sc), by_col(cc)], axis=1)

    @pl.when(i >= n_lat_tiles)
    def _():
        o_ref[...] = ctx_ref[...]


def embed_call(x2d, ctx2d, seq_len):
    n_lat, d = x2d.shape
    n_ctx = ctx2d.shape[0]
    tm = LN_ROWS
    quarter = d // 4
    g_rows = tm // GRID_W
    assert tm % GRID_W == 0 and SUBLANES % g_rows == 0 and seq_len % (SUBLANES * GRID_W) == 0
    omega = POS_BASE ** (-jnp.arange(quarter, dtype=F32) / quarter)
    ang_r = jnp.arange(seq_len // GRID_W, dtype=F32)[:, None] * omega[None, :]
    ang_c = jnp.arange(GRID_W, dtype=F32)[:, None] * omega[None, :]
    n_lat_tiles = n_lat // tm
    tiles_per_tab = SUBLANES // g_rows
    tabs_per_seq = seq_len // GRID_W // SUBLANES
    tab = pl.BlockSpec((SUBLANES, quarter), lambda i: ((i // tiles_per_tab) % tabs_per_seq, 0))
    col = pl.BlockSpec((GRID_W, quarter), lambda i: (0, 0))
    return pl.pallas_call(
        functools.partial(_embed_kernel, n_lat_tiles=n_lat_tiles, tiles_per_tab=tiles_per_tab),
        out_shape=jax.ShapeDtypeStruct((n_lat + n_ctx, d), F32),
        grid=((n_lat + n_ctx) // tm,),
        in_specs=[pl.BlockSpec((tm, d), lambda i: (jnp.minimum(i, n_lat_tiles - 1), 0)),
                  pl.BlockSpec((tm, d), lambda i: (jnp.maximum(i - n_lat_tiles, 0), 0)),
                  tab, tab, col, col],
        out_specs=pl.BlockSpec((tm, d), lambda i: (i, 0)),
        compiler_params=_cparams(("parallel",)),
        name="embed",
    )(x2d, ctx2d, jnp.sin(ang_r), jnp.cos(ang_r), jnp.sin(ang_c), jnp.cos(ang_c))


def _mm_mod_kernel(x_ref, sh_ref, sc_ref, w_ref, o_ref, xb_ref):
    @pl.when(pl.program_id(1) == 0)
    def _():
        xb_ref[...] = (x_ref[...] * (1.0 + sc_ref[...]) + sh_ref[...]).astype(BF16)

    o_ref[...] = jnp.dot(xb_ref[...], w_ref[...], preferred_element_type=F32)


def matmul_mod_call(x, mods, w, layer_w, *, rows_per_mod, n_mod_rows, tn, n_rows=None, n_cols=None):
    r, d = x.shape
    n = w.shape[2] if n_cols is None else n_cols
    r = r if n_rows is None else n_rows
    tm = MM_ROWS

    def midx(i):
        return jnp.minimum(i * tm // rows_per_mod, n_mod_rows - 1)

    return pl.pallas_call(
        _mm_mod_kernel,
        out_shape=jax.ShapeDtypeStruct((r, n), F32),
        grid=(r // tm, n // tn),
        in_specs=[
            pl.BlockSpec((tm, d), lambda i, j: (i, 0)),
            pl.BlockSpec((None, 1, d), lambda i, j: (midx(i), 0, 0)),
            pl.BlockSpec((None, 1, d), lambda i, j: (midx(i), 0, 1)),
            pl.BlockSpec((None, d, tn), lambda i, j: (layer_w, 0, j)),
        ],
        out_specs=pl.BlockSpec((tm, tn), lambda i, j: (i, j)),
        scratch_shapes=[pltpu.VMEM((tm, d), BF16)],
        compiler_params=_cparams(("parallel", "arbitrary")),
        name="proj_in",
    )(x, mods, mods, w)


def _mm_kernel(x_ref, w_ref, o_ref):
    o_ref[...] = jnp.dot(x_ref[...], w_ref[...], preferred_element_type=F32)


def matmul_call(x, w, layer_w, *, tn, n_rows=None):
    r, k = x.shape
    n = w.shape[2]
    r = r if n_rows is None else n_rows
    tm = MM_ROWS
    return pl.pallas_call(
        _mm_kernel,
        out_shape=jax.ShapeDtypeStruct((r, n), F32),
        grid=(r // tm, n // tn),
        in_specs=[
            pl.BlockSpec((tm, k), lambda i, j: (i, 0)),
            pl.BlockSpec((None, k, tn), lambda i, j: (layer_w, 0, j)),
        ],
        out_specs=pl.BlockSpec((tm, tn), lambda i, j: (i, j)),
        compiler_params=_cparams(("parallel", "arbitrary")),
        name="proj_out",
    )(x, w)


def _tile_flags(i, tm, n_lat_rows, lat_len, ctx_len):
    row = i * tm
    in_lat = row < n_lat_rows
    pos = jnp.where(in_lat, row % lat_len, (row - n_lat_rows) % ctx_len)
    seq = jnp.where(in_lat, lat_len, ctx_len)
    return pos == 0, pos + tm == seq


def _row_shift(c, prev_row, next_row):
    tm = c.shape[0]
    rows = lax.broadcasted_iota(jnp.int32, c.shape, 0)
    xp = jnp.where(rows == 0, prev_row, pltpu.roll(c, 1, axis=0))
    xn = jnp.where(rows == tm - 1, next_row, pltpu.roll(c, tm - 1, axis=0))
    return xp, xn


def _head_ones():
    r = lax.broadcasted_iota(jnp.int32, (LANES, LANES), 0) // HEAD_DIM
    c = lax.broadcasted_iota(jnp.int32, (LANES, LANES), 1) // HEAD_DIM
    return (r == c).astype(BF16)


def _head_sum(x, ones):
    w = x.shape[1]
    parts = [jnp.dot(x[:, j:j + LANES].astype(BF16), ones, preferred_element_type=F32)
             for j in range(0, w, LANES)]
    return jnp.concatenate(parts, axis=1)


def _halo_specs(tm, width, col, n_rows):
    per = tm // SUBLANES
    last = n_rows // SUBLANES - 1
    return [
        pl.BlockSpec((tm, width), lambda i: (i, col)),
        pl.BlockSpec((SUBLANES, width), lambda i: (jnp.maximum(i * per - 1, 0), col)),
        pl.BlockSpec((SUBLANES, width), lambda i: (jnp.minimum((i + 1) * per, last), col)),
    ]


def _rwkv_prep_kernel(rc, rp, rn, kc, kp, kn, vc, vp, vn, lc, lp, ln,
                      mu_r, mu_k, mu_v, mu_l, w0, w2, a0, a2, g2, kk_s, ka_s,
                      r_o, v_o, kk_o, lw_o, kd_o, bb_o, g_o, *, tm, n_lat_rows, lat_len, ctx_len):
    first, last = _tile_flags(pl.program_id(0), tm, n_lat_rows, lat_len, ctx_len)

    def shifted(c_ref, p_ref, n_ref, mu_ref):
        c = c_ref[...]
        prev_row = jnp.where(first, 0.0, p_ref[SUBLANES - 1:SUBLANES, :])
        next_row = jnp.where(last, 0.0, n_ref[0:1, :])
        xp, xn = _row_shift(c, prev_row, next_row)
        return c + mu_ref[...] * (0.5 * (xp + xn) - c)

    r = shifted(rc, rp, rn, mu_r)
    k = shifted(kc, kp, kn, mu_k)
    v = shifted(vc, vp, vn, mu_v)
    lo = shifted(lc, lp, ln, mu_l)
    r_o[...] = r
    v_o[...] = v

    ones = _head_ones()
    kk = k * kk_s[...]
    ss = _head_sum(kk * kk, ones)
    kk = kk * lax.rsqrt(jnp.maximum(ss, 1e-24))
    kk_o[...] = kk

    for d in range(2):
        wd = lo[:, d * LANES:(d + 1) * LANES]
        ad = lo[:, (2 + d) * LANES:(3 + d) * LANES]
        z = w0[d:d + 1, :] + _bdot(jnp.tanh(wd), w2[d])
        log_w = -_softplus(-z) - 0.5
        lw_o[d] = -jnp.exp(log_w)
        icl = jax.nn.sigmoid(a0[d:d + 1, :] + _bdot(ad, a2[d]))
        kd_o[d] = k * (1.0 + (icl - 1.0) * ka_s[...])
        bb_o[d] = kk * icl
    gd = lo[:, 4 * LANES:4 * LANES + GATE_LORA]
    g_o[...] = _bdot(jax.nn.sigmoid(gd), g2[...])


def rwkv_prep_call(p, prm, *, n_lat_rows, lat_len, ctx_len, col_r, col_k, col_v, col_l):
    r_rows = p.shape[0]
    tm = PREP_ROWS
    aw = prm["mu_r"].shape[1]
    full = lambda shape: pl.BlockSpec(shape, lambda i: (0,) * len(shape))
    in_specs = (_halo_specs(tm, aw, col_r, r_rows) + _halo_specs(tm, aw, col_k, r_rows)
                + _halo_specs(tm, aw, col_v, r_rows) + _halo_specs(tm, LORA_PAD, col_l, r_rows)
                + [full((1, aw)), full((1, aw)), full((1, aw)), full((1, LORA_PAD)),
                   full((2, aw)), full((2, LANES, aw)), full((2, aw)), full((2, LANES, aw)),
                   full((GATE_LORA, aw)), full((1, aw)), full((1, aw))])
    row_spec = pl.BlockSpec((tm, aw), lambda i: (i, 0))
    dir_spec = pl.BlockSpec((2, tm, aw), lambda i: (0, i, 0))
    one = jax.ShapeDtypeStruct((r_rows, aw), F32)
    two = jax.ShapeDtypeStruct((2, r_rows, aw), F32)
    kern = functools.partial(_rwkv_prep_kernel, tm=tm, n_lat_rows=n_lat_rows, lat_len=lat_len, ctx_len=ctx_len)
    return pl.pallas_call(
        kern,
        out_shape=(one, one, one, two, two, two, one),
        grid=(r_rows // tm,),
        in_specs=in_specs,
        out_specs=(row_spec, row_spec, row_spec, dir_spec, dir_spec, dir_spec, row_spec),
        compiler_params=_cparams(("parallel",)),
        name="rwkv_prep",
    )(p, p, p, p, p, p, p, p, p, p, p, p,
      prm["mu_r"], prm["mu_k"], prm["mu_v"], prm["mu_l"], prm["w0"], prm["w2"], prm["a0"], prm["a2"],
      prm["g2"], prm["k_k"], prm["k_a"])


def _scan_block(b, s, *, reverse, n_lat_blocks_total, lat_blocks):
    ctx_blk = n_lat_blocks_total + b
    lat_blk = b * lat_blocks + ((lat_blocks - s) if reverse else (s - 1))
    return jnp.where(s == 0, ctx_blk, lat_blk)


def _rwkv_scan_kernel(r_ref, lw_ref, k_ref, v_ref, kk_ref, bb_ref, y_ref, s_ref, *, reverse):
    @pl.when(pl.program_id(2) == 0)
    def _():
        s_ref[...] = jnp.zeros_like(s_ref)

    L = CHUNK
    ri = lax.broadcasted_iota(jnp.int32, (L, L), 0)
    ci = lax.broadcasted_iota(jnp.int32, (L, L), 1)
    tri = ((ci >= ri) if reverse else (ci <= ri)).astype(F32)
    r2 = lax.broadcasted_iota(jnp.int32, (2 * L, 2 * L), 0) % L
    c2 = lax.broadcasted_iota(jnp.int32, (2 * L, 2 * L), 1) % L
    strict = (c2 > r2) if reverse else (c2 < r2)
    incl = (c2 >= r2) if reverse else (c2 <= r2)
    rr = lax.broadcasted_iota(jnp.int32, (2 * L, 2 * L), 0)
    cc = lax.broadcasted_iota(jnp.int32, (2 * L, 2 * L), 1)
    eye = (rr == cc).astype(F32)
    off_masks = []
    sz = 2
    while sz <= L:
        late, early = (cc, rr) if reverse else (rr, cc)
        off_masks.append((rr // sz == cc // sz) & (late % sz >= sz // 2) & (early % sz < sz // 2))
        sz *= 2
    lane = lax.broadcasted_iota(jnp.int32, (L, LANES), 1)
    head0 = lane < HEAD_DIM

    def stack2(x):
        return jnp.concatenate([jnp.where(head0, x, 0.0), jnp.where(head0, 0.0, x)], axis=0)

    n_sub = SCAN_ROWS // L
    n_pair = r_ref.shape[1] // LANES
    order = [(n_sub - 1 - j) if reverse else j for j in range(n_sub)]

    keys = [(c, p) for c in order for p in range(n_pair)]
    a2, b2, k2, q2, v2, bt2, kt2, decay = ({} for _ in range(8))
    for c in order:
        rows = pl.ds(c * L, L)
        lw_all = lw_ref[rows, :]
        cum_all = jnp.dot(tri, lw_all, preferred_element_type=F32, precision=lax.Precision.HIGHEST)
        for p in range(n_pair):
            cols = slice(p * LANES, (p + 1) * LANES)
            lw = lw_all[:, cols]
            cum = cum_all[:, cols]
            tot = cum[0:1, :] if reverse else cum[L - 1:L, :]
            e_pos = jnp.exp(cum)
            e_neg = jnp.exp(-cum)
            e_rem = jnp.exp(tot - cum)
            kk = kk_ref[rows, cols]
            kd = k_ref[rows, cols]
            bb = bb_ref[rows, cols]
            key = (c, p)
            a2[key] = stack2(-kk * jnp.exp(cum - lw))
            b2[key] = stack2(bb * e_neg)
            k2[key] = stack2(kd * e_neg)
            q2[key] = stack2(r_ref[rows, cols] * e_pos)
            v2[key] = stack2(v_ref[rows, cols])
            bt2[key] = stack2(bb * e_rem)
            kt2[key] = stack2(kd * e_rem)
            decay[key] = jnp.exp(tot)

    hh = 2 * L
    sc = {k: _bdot_nt(jnp.concatenate([a2[k], q2[k]], axis=0), jnp.concatenate([b2[k], k2[k]], axis=0))
          for k in keys}
    mm = {k: jnp.where(strict, sc[k][:hh, :hh], 0.0) for k in keys}
    nn = {k: jnp.where(strict, sc[k][:hh, hh:], 0.0) for k in keys}
    qq = {k: jnp.where(incl, sc[k][hh:, :hh], 0.0) for k in keys}
    zz = {k: jnp.where(incl, sc[k][hh:, hh:], 0.0) for k in keys}
    tinv = {k: eye + jnp.where(off_masks[0], mm[k], 0.0) for k in keys}
    for off in off_masks[1:]:
        half_step = {k: _bdot(tinv[k], jnp.where(off, mm[k], 0.0)) for k in keys}
        tinv = {k: tinv[k] + _bdot(half_step[k], tinv[k]) for k in keys}
    nzv = {k: _bdot(jnp.concatenate([nn[k], zz[k]], axis=0), v2[k]) for k in keys}
    px = {k: _bdot(tinv[k], jnp.concatenate([a2[k], nzv[k][:hh]], axis=1)) for k in keys}
    qpx = {k: _bdot(qq[k], px[k]) for k in keys}
    g_mat = {k: q2[k] + qpx[k][:, :LANES] for k in keys}
    y_loc = {k: qpx[k][:, LANES:] + nzv[k][hh:] for k in keys}
    pb = {k: _bdot_tn(px[k], bt2[k]) for k in keys}
    phi = {k: pb[k][:LANES] for k in keys}
    psi = {k: pb[k][LANES:] + _bdot_tn(v2[k], kt2[k]) for k in keys}
    items = {k: (g_mat[k], y_loc[k], decay[k], phi[k], psi[k]) for k in keys}

    st = [s_ref[p] for p in range(n_pair)]
    for c in order:
        for p in range(n_pair):
            g, yl, dec, ph, ps = items[(c, p)]
            y2 = _bdot_nt(g, st[p]) + yl
            y_ref[pl.ds(c * L, L), p * LANES:(p + 1) * LANES] = y2[:L, :] + y2[L:, :]
            st[p] = st[p] * dec + _bdot(st[p], ph) + ps
    for p in range(n_pair):
        s_ref[p] = st[p]


def rwkv_scan_call(r, lw, kd, v, kk, bb, *, reverse, batch, lat_len, ctx_len):
    r_rows, aw = r.shape
    d = 1 if reverse else 0
    lat_blocks = lat_len // SCAN_ROWS
    width = SCAN_PAIRS * LANES
    blk = functools.partial(_scan_block, reverse=reverse, n_lat_blocks_total=batch * lat_blocks,
                            lat_blocks=lat_blocks)
    row_spec = pl.BlockSpec((SCAN_ROWS, width), lambda b, h, s: (blk(b, s), h))
    dir_spec = pl.BlockSpec((None, SCAN_ROWS, width), lambda b, h, s: (d, blk(b, s), h))
    return pl.pallas_call(
        functools.partial(_rwkv_scan_kernel, reverse=reverse),
        out_shape=jax.ShapeDtypeStruct((r_rows, aw), F32),
        grid=(batch, aw // width, 1 + lat_blocks),
        in_specs=[row_spec, dir_spec, dir_spec, row_spec, row_spec, dir_spec],
        out_specs=row_spec,
        scratch_shapes=[pltpu.VMEM((SCAN_PAIRS, LANES, LANES), F32)],
        compiler_params=_cparams(("parallel", "parallel", "arbitrary")),
        name="rwkv_scan_bwd" if reverse else "rwkv_scan_fwd",
    )(r, lw, kd, v, kk, bb)


def _lru_kernel(xcf, xpf, xnf, xcb, xpb, xnb, cw, cb, wcat, bcat, lsl, hf_ref, hb_ref,
                af_ref, uf_ref, ab_ref, ub_ref, carry_f, carry_b, *, lat_blocks):
    s = pl.program_id(1)

    @pl.when(s == 0)
    def _():
        carry_f[...] = jnp.zeros_like(carry_f)
        carry_b[...] = jnp.zeros_like(carry_b)

    for d, (xc_ref, xp_ref, xn_ref, a_ref, u_ref) in enumerate(((xcf, xpf, xnf, af_ref, uf_ref),
                                                                (xcb, xpb, xnb, ab_ref, ub_ref))):
        lat_idx = (lat_blocks - s) if d == 1 else (s - 1)
        first = (s == 0) | (lat_idx == 0)
        last = (s == 0) | (lat_idx == lat_blocks - 1)
        _lru_gates(xc_ref, xp_ref, xn_ref, cw, cb, wcat.at[d], bcat.at[d], lsl.at[d], a_ref, u_ref, first, last)

    w = af_ref.shape[1]
    rows = lax.broadcasted_iota(jnp.int32, (SUBLANES, w), 0)
    n_groups = SCAN_ROWS // SUBLANES

    def group_scan(a_ref, u_ref, h_ref, g, carry, reverse):
        sl = pl.ds(pl.multiple_of(g * SUBLANES, SUBLANES), SUBLANES)
        a = a_ref[sl, :]
        u = u_ref[sl, :]
        for sh in (1, 2, 4):
            if reverse:
                ok = rows < SUBLANES - sh
                a_s = pltpu.roll(a, SUBLANES - sh, axis=0)
                u_s = pltpu.roll(u, SUBLANES - sh, axis=0)
            else:
                ok = rows >= sh
                a_s = pltpu.roll(a, sh, axis=0)
                u_s = pltpu.roll(u, sh, axis=0)
            u = jnp.where(ok, a * u_s + u, u)
            a = jnp.where(ok, a * a_s, a)
        h = a * carry + u
        h_ref[sl, :] = h
        return h[0:1, :] if reverse else h[SUBLANES - 1:SUBLANES, :]

    def body(gi, carry):
        cf, cb_ = carry
        return (group_scan(af_ref, uf_ref, hf_ref, gi, cf, False),
                group_scan(ab_ref, ub_ref, hb_ref, n_groups - 1 - gi, cb_, True))

    cf, cb_ = lax.fori_loop(0, n_groups, body, (carry_f[...], carry_b[...]))
    carry_f[...] = cf
    carry_b[...] = cb_


def _lru_gates(xc_ref, xp_ref, xn_ref, cw, cb, wcat, bcat, lsl, a_ref, u_ref, first, last):
    tm = SCAN_ROWS
    x = xc_ref[...]
    prev_row = jnp.where(first, 0.0, xp_ref[SUBLANES - 1:SUBLANES, :])
    next1 = jnp.where(last, 0.0, xn_ref[0:1, :])
    next2 = jnp.where(last, 0.0, xn_ref[1:2, :])
    row_id = lax.broadcasted_iota(jnp.int32, x.shape, 0)
    xm1 = jnp.where(row_id == 0, prev_row, pltpu.roll(x, 1, axis=0))
    xp1 = jnp.where(row_id == tm - 1, next1, pltpu.roll(x, tm - 1, axis=0))
    xp2 = jnp.where(row_id == tm - 1, next2, jnp.where(row_id == tm - 2, next1, pltpu.roll(x, tm - 2, axis=0)))
    xc = cb[...] + xm1 * cw[0:1, :] + x * cw[1:2, :] + xp1 * cw[2:3, :] + xp2 * cw[3:4, :]
    bwid = LRU_BLOCK_W
    for n in range(x.shape[1] // bwid):
        cols = slice(n * bwid, (n + 1) * bwid)
        xn = xc[:, cols]
        gates = _bdot(xn, wcat[n]) + bcat[:, 2 * n * bwid:2 * (n + 1) * bwid]
        log_a = jax.nn.sigmoid(gates[:, :bwid]) * lsl[:, cols]
        a = jnp.exp(log_a)
        om = -jnp.tanh(log_a) * (a * a + 1.0)
        a_ref[:, cols] = a
        u_ref[:, cols] = xn * jax.nn.sigmoid(gates[:, bwid:]) * jnp.sqrt(om)


def lru_call(p, prm, *, batch, lat_len, ctx_len, col_x):
    r_rows = p.shape[0]
    bw = prm["conv_b"].shape[1]
    nb = bw // LRU_BLOCK_W
    lat_blocks = lat_len // SCAN_ROWS
    per = SCAN_ROWS // SUBLANES
    last8 = r_rows // SUBLANES - 1
    full = lambda shape: pl.BlockSpec(shape, lambda b, s: (0,) * len(shape))
    x_specs, out_specs = [], []
    for reverse in (False, True):
        blk = functools.partial(_scan_block, reverse=reverse, n_lat_blocks_total=batch * lat_blocks,
                                lat_blocks=lat_blocks)
        x_specs += [
            pl.BlockSpec((SCAN_ROWS, bw), lambda b, s, blk=blk: (blk(b, s), col_x)),
            pl.BlockSpec((SUBLANES, bw), lambda b, s, blk=blk: (jnp.maximum(blk(b, s) * per - 1, 0), col_x)),
            pl.BlockSpec((SUBLANES, bw), lambda b, s, blk=blk: (jnp.minimum((blk(b, s) + 1) * per, last8), col_x)),
        ]
        out_specs.append(pl.BlockSpec((SCAN_ROWS, bw), lambda b, s, blk=blk: (blk(b, s), 0)))
    out = jax.ShapeDtypeStruct((r_rows, bw), F32)
    tile = pltpu.VMEM((SCAN_ROWS, bw), F32)
    return pl.pallas_call(
        functools.partial(_lru_kernel, lat_blocks=lat_blocks),
        out_shape=(out, out),
        grid=(batch, 1 + lat_blocks),
        in_specs=x_specs + [full((CONV_W, bw)), full((1, bw)), full((2, nb, LRU_BLOCK_W, 2 * LRU_BLOCK_W)),
                            full((2, 1, 2 * bw)), full((2, 1, bw))],
        out_specs=tuple(out_specs),
        scratch_shapes=[tile, tile, tile, tile, pltpu.VMEM((1, bw), F32), pltpu.VMEM((1, bw), F32)],
        compiler_params=_cparams(("parallel", "arbitrary")),
        name="lru",
    )(p, p, p, p, p, p, prm["conv_w"], prm["conv_b"], prm["wcat"], prm["bcat"], prm["lsl"])


def _even_post_kernel(yf, yb, r_ref, v_ref, kdf, kdb, g_ref, hf, hb, gb_ref, rk, lnw, lnb, o_ref):
    ones = _head_ones()
    y = yf[...] + yb[...]
    inv_n = 1.0 / HEAD_DIM
    yc = y - _head_sum(y, ones) * inv_n
    var = _head_sum(yc * yc, ones) * inv_n
    hn = yc * lax.rsqrt(var + LNX_EPS) * lnw[...] + lnb[...]
    r = r_ref[...]
    v = v_ref[...]
    bonus = _head_sum(r * kdf[...] * rk[...], ones) * v + _head_sum(r * kdb[...] * rk[...], ones) * v
    ya = (hn + bonus) * g_ref[...]
    yl = (hf[...] + hb[...]) * jax.nn.gelu(gb_ref[...])
    aw = ya.shape[1]
    o_ref[:, :aw] = ya.astype(BF16)
    o_ref[:, aw:] = yl.astype(BF16)


def even_post_call(yf, yb, r, v, kd, g, hf, hb, p, prm, *, col_gb):
    r_rows, aw = yf.shape
    bw = hf.shape[1]
    tm = PREP_ROWS
    row = lambda w: pl.BlockSpec((tm, w), lambda i: (i, 0))
    full = lambda shape: pl.BlockSpec(shape, lambda i: (0,) * len(shape))
    return pl.pallas_call(
        _even_post_kernel,
        out_shape=jax.ShapeDtypeStruct((r_rows, aw + bw), BF16),
        grid=(r_rows // tm,),
        in_specs=[row(aw), row(aw), row(aw), row(aw),
                  pl.BlockSpec((None, tm, aw), lambda i: (0, i, 0)),
                  pl.BlockSpec((None, tm, aw), lambda i: (1, i, 0)),
                  row(aw), row(bw), row(bw),
                  pl.BlockSpec((tm, bw), lambda i: (i, col_gb)),
                  full((1, aw)), full((1, aw)), full((1, aw))],
        out_specs=row(aw + bw),
        compiler_params=_cparams(("parallel",)),
        name="even_post",
    )(yf, yb, r, v, kd, kd, g, hf, hb, p, prm["r_k"], prm["lnx_w"], prm["lnx_b"])


def _mlstm_kernel(qf, kf, vf, brf, bcf, irf, icf, qb, kb, vb, brb, bcb, irb, icb, hf_ref, hb_ref,
                  cf, nf, mf, cb, nb, mb):
    @pl.when(pl.program_id(2) == 0)
    def _():
        for ref in (cf, nf, mf, cb, nb, mb):
            ref[...] = jnp.zeros_like(ref)

    _mlstm_block(qf, kf, vf, brf, bcf, irf, icf, hf_ref, cf, nf, mf, reverse=False)
    _mlstm_block(qb, kb, vb, brb, bcb, irb, icb, hb_ref, cb, nb, mb, reverse=True)


def _mlstm_block(q_ref, k_ref, v_ref, br_ref, bc_ref, ir_ref, ic_ref, h_ref, c_ref, n_ref, m_ref, *, reverse):
    L = MLSTM_CHUNK
    ri = lax.broadcasted_iota(jnp.int32, (L, L), 0)
    ci = lax.broadcasted_iota(jnp.int32, (L, L), 1)
    causal = (ci >= ri) if reverse else (ci <= ri)
    n_sub = SCAN_ROWS // L
    order = [(n_sub - 1 - j) if reverse else j for j in range(n_sub)]
    scale = C_QK_HEAD ** -0.5

    m_cur = m_ref[...]
    m_in, ew, ec = {}, {}, {}
    for c in order:
        b_r = br_ref[c:c + 1, :]
        b_tot = b_r[:, 0:1] if reverse else b_r[:, L - 1:L]
        w_in = b_tot - bc_ref[:, c:c + 1] + ic_ref[:, c:c + 1]
        carry_log = b_tot + m_cur
        m_new = jnp.maximum(carry_log, jnp.max(w_in, axis=0, keepdims=True))
        m_in[c] = m_cur
        ew[c] = jnp.exp(w_in - m_new)
        ec[c] = jnp.exp(carry_log - m_new)
        m_cur = m_new
    m_ref[...] = m_cur

    q = {c: (q_ref[pl.ds(c * L, L), :] * scale).astype(BF16) for c in order}
    k = {c: k_ref[pl.ds(c * L, L), :] for c in order}
    v = {c: v_ref[pl.ds(c * L, L), :].astype(BF16) for c in order}
    qk = {c: _bdot_nt(q[c], k[c]) for c in order}
    m_t, e_inter, scores = {}, {}, {}
    for c in order:
        b_c = bc_ref[:, c:c + 1]
        dmat = jnp.where(causal, b_c - br_ref[c:c + 1, :] + ir_ref[c:c + 1, :], -jnp.inf)
        inter = b_c + m_in[c]
        m_t[c] = jnp.maximum(inter, jnp.max(dmat, axis=1, keepdims=True))
        scores[c] = qk[c] * jnp.exp(dmat - m_t[c])
        e_inter[c] = jnp.exp(inter - m_t[c])
    intra = {c: _bdot(scores[c], v[c]) for c in order}
    kw = {c: k[c] * ew[c] for c in order}
    kv = {c: _bdot_tn(kw[c], v[c]) for c in order}

    c_cur = c_ref[...]
    n_cur = n_ref[...]
    c_in, n_in = {}, {}
    for c in order:
        c_in[c] = c_cur
        n_in[c] = n_cur
        c_cur = ec[c] * c_cur + kv[c]
        n_cur = ec[c] * n_cur + jnp.sum(kw[c], axis=0, keepdims=True)
    c_ref[...] = c_cur
    n_ref[...] = n_cur

    qc = {c: _bdot(q[c], c_in[c]) for c in order}
    for c in order:
        qf = q[c].astype(F32)
        num = e_inter[c] * qc[c] + intra[c]
        den = (e_inter[c] * jnp.sum(qf * n_in[c], axis=1, keepdims=True)
               + jnp.sum(scores[c], axis=1, keepdims=True))
        h_ref[pl.ds(c * L, L), :] = num / jnp.maximum(jnp.abs(den), jnp.exp(-m_t[c]))


def mlstm_call(p, g_rows, g_cols, *, batch, lat_len, ctx_len, col_q, col_k, col_v):
    r_rows = p.shape[0]
    n_sub = SCAN_ROWS // MLSTM_CHUNK
    lat_blocks = lat_len // SCAN_ROWS
    in_specs, out_specs = [], []
    for d, reverse in enumerate((False, True)):
        blk = functools.partial(_scan_block, reverse=reverse, n_lat_blocks_total=batch * lat_blocks,
                                lat_blocks=lat_blocks)
        grow = lambda which, d=d, blk=blk: pl.BlockSpec(
            (None, None, None, None, n_sub, MLSTM_CHUNK), lambda b, h, s: (d, which, h, blk(b, s), 0, 0))
        gcol = lambda which, d=d, blk=blk: pl.BlockSpec(
            (None, None, None, None, MLSTM_CHUNK, n_sub), lambda b, h, s: (d, which, h, blk(b, s), 0, 0))
        in_specs += [
            pl.BlockSpec((SCAN_ROWS, C_QK_HEAD), lambda b, h, s, blk=blk: (blk(b, s), col_q + h)),
            pl.BlockSpec((SCAN_ROWS, C_QK_HEAD), lambda b, h, s, blk=blk: (blk(b, s), col_k + h)),
            pl.BlockSpec((SCAN_ROWS, C_V_HEAD), lambda b, h, s, blk=blk: (blk(b, s), col_v + h)),
            grow(0), gcol(0), grow(1), gcol(1),
        ]
        out_specs.append(pl.BlockSpec((SCAN_ROWS, C_V_HEAD), lambda b, h, s, blk=blk: (blk(b, s), h)))
    out = jax.ShapeDtypeStruct((r_rows, C_HEADS * C_V_HEAD), F32)
    state = [pltpu.VMEM((C_QK_HEAD, C_V_HEAD), F32), pltpu.VMEM((1, C_QK_HEAD), F32), pltpu.VMEM((1, 1), F32)]
    one_dir = (p, p, p, g_rows, g_cols, g_rows, g_cols)
    return pl.pallas_call(
        _mlstm_kernel,
        out_shape=(out, out),
        grid=(batch, C_HEADS, 1 + lat_blocks),
        in_specs=in_specs,
        out_specs=tuple(out_specs),
        scratch_shapes=state + state,
        compiler_params=_cparams(("parallel", "parallel", "arbitrary")),
        name="mlstm",
    )(*one_dir, *one_dir)


def _odd_post_kernel(hf, hb, o_lo, o_hi, nw, nb, y_ref):
    half_heads = C_HEADS // 2
    for hd in range(C_HEADS):
        cols = slice(hd * C_V_HEAD, (hd + 1) * C_V_HEAD)
        x = hf[:, cols] + hb[:, cols]
        xc = x - jnp.mean(x, axis=1, keepdims=True)
        var = jnp.mean(xc * xc, axis=1, keepdims=True)
        hn = xc * lax.rsqrt(var + MLSTM_NORM_EPS) * nw[:, cols] + nb[:, cols]
        o_ref = o_lo if hd < half_heads else o_hi
        oc = slice((hd % half_heads) * C_V_HEAD, (hd % half_heads + 1) * C_V_HEAD)
        y_ref[:, cols] = (hn * jax.nn.sigmoid(o_ref[:, oc])).astype(BF16)


def odd_post_call(hf, hb, p, nw, nb, *, n_rows, col_o):
    vw = hf.shape[1]
    tm = LN_ROWS
    row = pl.BlockSpec((tm, vw), lambda i: (i, 0))
    par = pl.BlockSpec((1, vw), lambda i: (0, 0))
    return pl.pallas_call(
        _odd_post_kernel,
        out_shape=jax.ShapeDtypeStruct((n_rows, vw), BF16),
        grid=(n_rows // tm,),
        in_specs=[row, row, pl.BlockSpec((tm, vw // 2), lambda i: (i, col_o)),
                  pl.BlockSpec((tm, vw // 2), lambda i: (i, col_o + 1)), par, par],
        out_specs=row,
        compiler_params=_cparams(("parallel",)),
        name="odd_post",
    )(hf, hb, p, p, nw, nb)


def _layer_norm(z, g, b):
    zc = z - jnp.mean(z, axis=1, keepdims=True)
    var = jnp.mean(zc * zc, axis=1, keepdims=True)
    return zc * lax.rsqrt(var + LN_EPS) * g + b


def _route(logits_t, bias):
    aff = [jax.nn.sigmoid(logits_t[e:e + 1, :]) for e in range(N_EXPERTS)]
    biased = [aff[e] + bias[e:e + 1, :] for e in range(N_EXPERTS)]
    best_g = best_v = None
    for g in range(N_GROUPS):
        m = biased[g * EXPERTS_PER_GROUP:(g + 1) * EXPERTS_PER_GROUP]
        pair = None
        for i in range(EXPERTS_PER_GROUP):
            for j in range(i + 1, EXPERTS_PER_GROUP):
                hi = jnp.maximum(m[i], m[j])
                lo_ = jnp.minimum(m[i], m[j])
                s = hi + lo_
                pair = s if pair is None else jnp.maximum(pair, s)
        if best_v is None:
            best_v, best_g = pair, jnp.zeros(pair.shape, jnp.int32)
        else:
            upd = pair > best_v
            best_g = jnp.where(upd, g, best_g)
            best_v = jnp.where(upd, pair, best_v)
    ids, sels = [], []
    taken = None
    for _ in range(TOP_K):
        cur_v = cur_i = cur_a = None
        for e in range(N_EXPERTS):
            ok = best_g == (e // EXPERTS_PER_GROUP)
            if taken is not None:
                ok = jnp.logical_and(ok, taken != e)
            val = jnp.where(ok, biased[e], -jnp.inf)
            if cur_v is None:
                cur_v, cur_i, cur_a = val, jnp.zeros(val.shape, jnp.int32), aff[e]
            else:
                upd = val > cur_v
                cur_i = jnp.where(upd, e, cur_i)
                cur_a = jnp.where(upd, aff[e], cur_a)
                cur_v = jnp.where(upd, val, cur_v)
        ids.append(cur_i)
        sels.append(cur_a)
        taken = cur_i
    tot = sels[0] + sels[1]
    return ids, [sels[0] / tot, sels[1] / tot]


def _ln_route_kernel(lat_ref, y_ref, gate_ref, sh_ref, sc_ref, g_ref, b_ref, rw_ref, rb_ref,
                     lat_o, h_o, id_o, gt_o):
    z = ALPHA * lat_ref[...] + gate_ref[...] * y_ref[...]
    ln = _layer_norm(z, g_ref[...], b_ref[...])
    lat_o[...] = ln
    h = ln * (1.0 + sc_ref[...]) + sh_ref[...]
    half = h.shape[1] // 2
    bits = lax.bitcast_convert_type(h.astype(BF16).astype(F32), jnp.uint32)
    h_o[...] = (bits[:, :half] >> 16) | bits[:, half:]
    logits_t = lax.dot_general(rw_ref[...], h, (((1,), (1,)), ((), ())), preferred_element_type=F32,
                               precision=lax.Precision.HIGHEST)
    ids, gts = _route(logits_t, rb_ref[...])
    id_o[...] = jnp.concatenate(ids, axis=0)
    gt_o[...] = jnp.concatenate(gts, axis=0)


def ln_route_call(lat, y, mods, norm_g, norm_b, layer, router_wt, router_b, *, n_rows, rows_per_mod, n_mod_rows):
    d = lat.shape[1]
    tm = LN_ROWS

    def midx(i):
        return jnp.minimum(i * tm // rows_per_mod, n_mod_rows - 1)

    row = pl.BlockSpec((tm, d), lambda i: (i, 0))
    packed = pl.BlockSpec((tm, d // 2), lambda i: (i, 0))
    mod = lambda k: pl.BlockSpec((None, 1, d), lambda i: (midx(i), 0, k))
    nrm = pl.BlockSpec((None, None, 1, d), lambda i: (layer, 0, 0, 0))
    full = lambda shape: pl.BlockSpec(shape, lambda i: (0,) * len(shape))
    sel = pl.BlockSpec((TOP_K, tm), lambda i: (0, i))
    return pl.pallas_call(
        _ln_route_kernel,
        out_shape=(jax.ShapeDtypeStruct((n_rows, d), F32), jax.ShapeDtypeStruct((n_rows, d // 2), jnp.uint32),
                   jax.ShapeDtypeStruct((TOP_K, n_rows), jnp.int32), jax.ShapeDtypeStruct((TOP_K, n_rows), F32)),
        grid=(n_rows // tm,),
        in_specs=[row, row, mod(2), mod(3), mod(4), nrm, nrm, full((N_EXPERTS, d)), full((N_EXPERTS, 1))],
        out_specs=(row, packed, sel, sel),
        compiler_params=_cparams(("parallel",)),
        name="ln_route",
    )(lat, y, mods, mods, mods, norm_g, norm_b, router_wt, router_b)


def _ln_combine_kernel(lat_ref, y0_ref, y1_ref, gt_ref, gate_ref, g_ref, b_ref, lat_o):
    gt = gt_ref[...]
    f = y0_ref[...] * gt[:, 0:1] + y1_ref[...] * gt[:, 1:2]
    z = ALPHA * lat_ref[...] + gate_ref[...] * f
    lat_o[...] = _layer_norm(z, g_ref[...], b_ref[...])


def ln_combine_call(lat, y_slots, gates_t, mods, norm_g, norm_b, layer, *, n_rows, rows_per_mod, n_mod_rows):
    d = lat.shape[1]
    tm = LN_ROWS
    per_k = n_rows // tm

    def midx(i):
        return jnp.minimum(i * tm // rows_per_mod, n_mod_rows - 1)

    row = pl.BlockSpec((tm, d), lambda i: (i, 0))
    nrm = pl.BlockSpec((None, None, 1, d), lambda i: (layer, 1, 0, 0))
    return pl.pallas_call(
        _ln_combine_kernel,
        out_shape=jax.ShapeDtypeStruct((n_rows, d), F32),
        grid=(per_k,),
        in_specs=[row, row, pl.BlockSpec((tm, d), lambda i: (per_k + i, 0)),
                  pl.BlockSpec((tm, TOP_K), lambda i: (i, 0)),
                  pl.BlockSpec((None, 1, d), lambda i: (midx(i), 0, 5)), nrm, nrm],
        out_specs=row,
        compiler_params=_cparams(("parallel",)),
        name="ln_combine",
    )(lat, y_slots, y_slots, gates_t, mods, norm_g, norm_b)


def _moe_kernel(nu_ref, be_ref, jw_ref, src_cur, src_nxt, dst_prv, h_hbm, wg_ref, wu_ref, wd_ref, y_hbm,
                xbuf, acc, gsem, ssem, *, n_j, n_slots):
    del be_ref, jw_ref
    i = pl.program_id(0)
    j = pl.program_id(1)
    buf = i % 2
    other = 1 - buf
    n_used = nu_ref[0]
    per_step, half = xbuf.shape[2], xbuf.shape[3]
    tm = n_j * per_step
    d = acc.shape[3]

    def row_in(tok, q, rr, b):
        return pltpu.make_async_copy(h_hbm.at[pl.ds(tok, 1)], xbuf.at[b, q, pl.ds(rr, 1)], gsem.at[b])

    def row_out(q, rr, slot, b):
        return pltpu.make_async_copy(acc.at[b, q, pl.ds(rr, 1)], y_hbm.at[pl.ds(slot, 1)], ssem.at[b])

    def send_previous(q, rr):
        r = q * per_step + rr
        slot = jnp.where(i > 0, dst_prv[0, r], n_slots + r)
        row_out(q, rr, slot, other).start()

    @pl.when((i == 0) & (j == 0))
    def _():
        acc[1] = jnp.zeros((n_j, per_step, d), F32)
        for q in range(n_j):
            def body(rr, carry, q=q):
                row_in(src_cur[0, q * per_step + rr], q, rr, 0).start()
                return carry
            lax.fori_loop(0, per_step, body, 0, unroll=8)

    @pl.when((j == 0) & (i <= n_used))
    def _():
        pltpu.make_async_copy(xbuf.at[buf], xbuf.at[buf], gsem.at[buf]).wait()

    @pl.when((j == 0) & (i < n_used))
    def _():
        acc[buf] = jnp.zeros((n_j, per_step, d), F32)

    @pl.when(i < n_used)
    def _():
        tf = wg_ref.shape[1]
        g = jnp.zeros((tm, tf), F32)
        u = jnp.zeros((tm, tf), F32)
        for c0 in range(0, half, MOE_KCHUNK):
            xp = xbuf[buf, :, :, c0:c0 + MOE_KCHUNK].reshape(tm, MOE_KCHUNK)
            lo = lax.bitcast_convert_type(xp << 16, F32)
            hi = lax.bitcast_convert_type(xp & jnp.uint32(0xFFFF0000), F32)
            g = g + _bdot(lo, wg_ref[c0:c0 + MOE_KCHUNK, :]) + _bdot(hi, wg_ref[half + c0:half + c0 + MOE_KCHUNK, :])
            u = u + _bdot(lo, wu_ref[c0:c0 + MOE_KCHUNK, :]) + _bdot(hi, wu_ref[half + c0:half + c0 + MOE_KCHUNK, :])
        hdn = (g * jax.nn.sigmoid(g) * u).astype(BF16)
        for c0 in range(0, d, MOE_NCHUNK):
            part = _bdot(hdn, wd_ref[:, c0:c0 + MOE_NCHUNK])
            acc[buf, :, :, c0:c0 + MOE_NCHUNK] += part.reshape(n_j, per_step, MOE_NCHUNK)
        for rr in range(per_step):
            row_in(src_nxt[0, j * per_step + rr], j, rr, other).start()
            send_previous(j, rr)

    @pl.when(i == n_used)
    def _():
        for rr in range(per_step):
            send_previous(j, rr)

    @pl.when((j == n_j - 1) & (i <= n_used))
    def _():
        pltpu.make_async_copy(acc.at[other], acc.at[other], ssem.at[other]).wait()


def moe_call(h_packed, n_used, block_e, j_of, src_tok, dst_slot, w_gate, w_up, w_down, layer, *, n_slots):
    d = 2 * h_packed.shape[1]
    f = w_gate.shape[3]
    tm, tf = MOE_ROWS, MOE_FTILE
    n_blocks = block_e.shape[0]
    n_j = f // tf
    assert tm % n_j == 0
    idx = lambda fn: pl.BlockSpec((None, 1, tm), fn, memory_space=pltpu.SMEM)
    grid_spec = pltpu.PrefetchScalarGridSpec(
        num_scalar_prefetch=3,
        grid=(n_blocks, n_j),
        in_specs=[
            idx(lambda i, j, nu, be, jw: (i, 0, 0)),
            idx(lambda i, j, nu, be, jw: (jnp.minimum(i + 1, n_blocks - 1), 0, 0)),
            idx(lambda i, j, nu, be, jw: (jnp.maximum(i - 1, 0), 0, 0)),
            pl.BlockSpec(memory_space=pl.ANY),
            pl.BlockSpec((None, None, d, tf), lambda i, j, nu, be, jw: (layer, be[i], 0, jw[i, j])),
            pl.BlockSpec((None, None, d, tf), lambda i, j, nu, be, jw: (layer, be[i], 0, jw[i, j])),
            pl.BlockSpec((None, None, tf, d), lambda i, j, nu, be, jw: (layer, be[i], jw[i, j], 0)),
        ],
        out_specs=pl.BlockSpec(memory_space=pl.ANY),
        scratch_shapes=[pltpu.VMEM((2, n_j, tm // n_j, d // 2), jnp.uint32),
                        pltpu.VMEM((2, n_j, tm // n_j, d), F32),
                        pltpu.SemaphoreType.DMA((2,)), pltpu.SemaphoreType.DMA((2,))],
    )
    return pl.pallas_call(
        functools.partial(_moe_kernel, n_j=n_j, n_slots=n_slots),
        out_shape=jax.ShapeDtypeStruct((n_slots + tm, d), F32),
        grid_spec=grid_spec,
        compiler_params=_cparams(("arbitrary", "arbitrary")),
        name="moe_experts",
    )(n_used, block_e, j_of, src_tok, src_tok, dst_slot, h_packed, w_gate, w_up, w_down)


def moe_ffn(h_packed, ids, w_gate, w_up, w_down, layer):
    n_tok = h_packed.shape[0]
    n_slots = n_tok * TOP_K
    tm = MOE_ROWS
    i32 = jnp.int32
    flat_e = ids.reshape(n_slots)
    order = jnp.argsort(flat_e, stable=True).astype(i32)
    counts = jnp.sum((flat_e[:, None] == jnp.arange(N_EXPERTS, dtype=i32)[None, :]).astype(i32), axis=0)
    start = jnp.cumsum(counts) - counts
    padded = (counts + tm - 1) // tm * tm
    pad_end = jnp.cumsum(padded)
    pad_start = pad_end - padded
    n_blocks = -(-n_slots // tm) + N_EXPERTS + 1
    pos = jnp.arange(n_blocks * tm, dtype=i32)
    e_pos = jnp.minimum(jnp.sum((pos[:, None] >= pad_end[None, :]).astype(i32), axis=1), N_EXPERTS - 1)
    r_pos = pos - pad_start[e_pos]
    valid = r_pos < counts[e_pos]
    slot = order[jnp.clip(start[e_pos] + r_pos, 0, n_slots - 1)]
    src_tok = jnp.where(valid, slot % n_tok, 0).reshape(n_blocks, 1, tm)
    dst_slot = jnp.where(valid, slot, n_slots + pos % tm).reshape(n_blocks, 1, tm)
    n_used = (pad_end[-1] // tm).astype(i32)
    blk = jnp.arange(n_blocks, dtype=i32)
    block_e = e_pos.reshape(n_blocks, tm)[:, 0]
    block_e = jnp.where(blk < n_used, block_e, block_e[jnp.maximum(n_used - 1, 0)]).astype(i32)
    n_j = w_gate.shape[3] // MOE_FTILE
    tiles = jnp.arange(n_j, dtype=i32)[None, :]
    j_of = jnp.where((blk % 2 == 0)[:, None], tiles, n_j - 1 - tiles)
    j_of = jnp.where((blk < n_used)[:, None], j_of, j_of[jnp.maximum(n_used - 1, 0), n_j - 1]).astype(i32)
    return moe_call(h_packed, n_used.reshape(1), block_e, j_of, src_tok.astype(i32), dst_slot.astype(i32),
                    w_gate, w_up, w_down, layer, n_slots=n_slots)


def _pad_cols(w, width):
    return jnp.pad(w, [(0, 0)] * (w.ndim - 1) + [(0, width - w.shape[-1])])


def _even_weights(ev_w_in, rwkv_mu, aw, bw):
    rk = 3 * aw
    lo0 = rk
    pieces = [DECAY_LORA, DECAY_LORA, ICL_LORA, ICL_LORA]

    def regroup(t):
        out = [t[..., rk + 2 * (DECAY_LORA + ICL_LORA) + GATE_LORA:],
               t[..., :rk]]
        off = lo0
        for wdt in pieces:
            out.append(_pad_cols(t[..., off:off + wdt], LANES))
            off += wdt
        out.append(t[..., off:off + GATE_LORA])
        used = 4 * LANES + GATE_LORA
        out.append(jnp.zeros(t.shape[:-1] + (LORA_PAD - used,), t.dtype))
        return jnp.concatenate(out, axis=-1)

    w = regroup(ev_w_in).astype(BF16)
    rwkv_cols = rk + 2 * (DECAY_LORA + ICL_LORA) + GATE_LORA
    mu_full = jnp.concatenate([rwkv_mu, jnp.zeros(rwkv_mu.shape[:-1] + (2 * bw,), rwkv_mu.dtype)], axis=-1)
    mu = regroup(mu_full)
    del rwkv_cols
    return w, mu


def kernel(x, c, ctx, c_ctx, ada_w, ada_b, norm_g, norm_b,
           ev_w_in, ev_w_out, rwkv_mu, rwkv_w0, rwkv_w2, rwkv_a0, rwkv_a2, rwkv_g2,
           rwkv_k_k, rwkv_k_a, rwkv_r_k, rwkv_lnx_w, rwkv_lnx_b,
           lru_conv_w, lru_conv_b, lru_wa, lru_ba, lru_wx, lru_bx, lru_lam,
           od_w_in, od_w_out, mlstm_ig_b, mlstm_fg_b, mlstm_norm_w, mlstm_norm_b,
           router_w, router_b, moe_w_gate, moe_w_up, moe_w_down):
    bsz, s_len, dim = x.shape
    n_ctx = ctx.shape[1]
    n_lat_rows = bsz * s_len
    n_all_rows = n_lat_rows + bsz * n_ctx
    aw = rwkv_k_k.shape[1]
    bw = lru_conv_b.shape[1]
    assert s_len % SCAN_ROWS == 0 and n_ctx == SCAN_ROWS and n_all_rows % MM_ROWS == 0
    assert (bsz * n_ctx) % MM_ROWS == 0 and s_len % MM_ROWS == 0 and bsz + 1 <= SUBLANES
    seq = dict(n_lat_rows=n_lat_rows, lat_len=s_len, ctx_len=n_ctx)
    scan = dict(batch=bsz, lat_len=s_len, ctx_len=n_ctx)
    modk = dict(rows_per_mod=s_len, n_mod_rows=bsz + 1)

    stream = embed_call(x.reshape(n_lat_rows, dim), ctx.reshape(bsz * n_ctx, dim), s_len)
    cvec = jnp.concatenate([c, c_ctx[None, :], jnp.zeros((SUBLANES - bsz - 1, dim), c.dtype)], axis=0)
    ada_b3 = ada_b[:, None, :]
    norm_g4 = norm_g[:, :, None, :]
    norm_b4 = norm_b[:, :, None, :]
    router_wt = router_w.T
    router_b2 = router_b[:, None]

    for layer in range(DEPTH):
        last = layer == DEPTH - 1
        i = layer // 2
        mods = adaln_call(cvec, ada_w, ada_b3, layer).reshape(SUBLANES, 1, N_MOD * dim)
        if layer % 2 == 0:
            w_in, mu = _even_weights(ev_w_in, rwkv_mu, aw, bw)
            n_in = w_in.shape[2]
            p = matmul_mod_call(stream, mods, w_in, i, tn=1024, **modk)
            col_x, col_gb = 0, bw // bw
            col_r, col_k, col_v = 2 * bw // aw, 2 * bw // aw + 1, 2 * bw // aw + 2
            col_l = (2 * bw + 3 * aw) // LORA_PAD
            assert (2 * bw + 3 * aw) % LORA_PAD == 0 and n_in == 2 * bw + 3 * aw + LORA_PAD
            mu_i = mu[i]
            o_r = 2 * bw
            prm = dict(
                mu_r=mu_i[None, o_r:o_r + aw], mu_k=mu_i[None, o_r + aw:o_r + 2 * aw],
                mu_v=mu_i[None, o_r + 2 * aw:o_r + 3 * aw], mu_l=mu_i[None, o_r + 3 * aw:],
                w0=rwkv_w0[i], w2=_pad_rows(rwkv_w2[i], LANES).astype(BF16),
                a0=rwkv_a0[i], a2=_pad_rows(rwkv_a2[i], LANES).astype(BF16),
                g2=rwkv_g2[i].astype(BF16), k_k=rwkv_k_k[i][None, :], k_a=rwkv_k_a[i][None, :])
            r_s, v_s, kk, lw, kd, bb, g = rwkv_prep_call(p, prm, col_r=col_r, col_k=col_k, col_v=col_v,
                                                          col_l=col_l, **seq)
            nb = bw // LRU_BLOCK_W
            wcat = jnp.concatenate([lru_wa[i], lru_wx[i]], axis=-1).astype(BF16)
            bcat = jnp.stack([lru_ba[i].reshape(2, nb, LRU_BLOCK_W), lru_bx[i].reshape(2, nb, LRU_BLOCK_W)],
                             axis=2).reshape(2, 1, 2 * bw)
            lprm = dict(conv_w=lru_conv_w[i], conv_b=lru_conv_b[i][None, :], wcat=wcat, bcat=bcat,
                        lsl=(RGLRU_C * jax.nn.log_sigmoid(lru_lam[i]))[:, None, :])
            yf = rwkv_scan_call(r_s, lw, kd, v_s, kk, bb, reverse=False, **scan)
            yb = rwkv_scan_call(r_s, lw, kd, v_s, kk, bb, reverse=True, **scan)
            hf, hb = lru_call(p, lprm, col_x=col_x, **scan)
            pprm = dict(r_k=rwkv_r_k[i].reshape(1, aw), lnx_w=rwkv_lnx_w[i][None, :], lnx_b=rwkv_lnx_b[i][None, :])
            ymix = even_post_call(yf, yb, r_s, v_s, kd, g, hf, hb, p, pprm, col_gb=col_gb)
            n_rows = n_all_rows if not last else n_lat_rows
            y = matmul_call(ymix, ev_w_out.astype(BF16), i, tn=1024, n_rows=n_rows)
        else:
            qk_w = C_HEADS * C_QK_HEAD
            v_w = C_HEADS * C_V_HEAD
            main = 2 * qk_w + 2 * v_w
            w_gate = _pad_cols(od_w_in[..., main:], LANES).astype(BF16)
            p = matmul_mod_call(stream, mods, od_w_in.astype(BF16), i, tn=1024, n_cols=main, **modk)
            gp = matmul_mod_call(stream, mods, w_gate, i, tn=LANES, **modk)[:, :4 * C_HEADS]
            gp = gp.reshape(n_all_rows, 2, 2, C_HEADS)
            g_rows, g_cols = _mlstm_gates(gp, mlstm_ig_b[i], mlstm_fg_b[i])
            cols = dict(col_q=0, col_k=(qk_w + v_w) // C_QK_HEAD, col_v=(2 * qk_w + v_w) // C_V_HEAD)
            hf, hb = mlstm_call(p, g_rows, g_cols, **scan, **cols)
            n_rows = n_all_rows if not last else n_lat_rows
            ymix = odd_post_call(hf, hb, p, mlstm_norm_w[i][None, :], mlstm_norm_b[i][None, :],
                                 n_rows=n_rows, col_o=qk_w // (v_w // 2))
            y = matmul_call(ymix, od_w_out.astype(BF16), i, tn=1024, n_rows=n_rows)
        n_rows = n_all_rows if not last else n_lat_rows
        stream1, h_moe, ids, gates = ln_route_call(stream, y, mods, norm_g4, norm_b4, layer, router_wt, router_b2,
                                                   n_rows=n_rows, **modk)
        y_slots = moe_ffn(h_moe, ids, moe_w_gate, moe_w_up, moe_w_down, layer)
        stream = ln_combine_call(stream1, y_slots, gates.T, mods, norm_g4, norm_b4, layer, n_rows=n_rows, **modk)
    return stream[:n_lat_rows].reshape(bsz, s_len, dim)


def _pad_rows(w, rows):
    return jnp.pad(w, ((0, 0), (0, rows - w.shape[1]), (0, 0)))


def _mlstm_gates(gp, ig_b, fg_b):
    n_rows = gp.shape[0]
    chunk = MLSTM_CHUNK
    n_sub = SCAN_ROWS // chunk
    outs = []
    for d in range(2):
        ig = GATE_CAP * jnp.tanh((gp[:, d, 0] + ig_b[d]) / GATE_CAP)
        lf = jax.nn.log_sigmoid(GATE_CAP * jnp.tanh((gp[:, d, 1] + fg_b[d]) / GATE_CAP))
        lf = lf.reshape(n_rows // chunk, chunk, C_HEADS)
        b = jnp.cumsum(lf[:, ::-1], axis=1)[:, ::-1] if d == 1 else jnp.cumsum(lf, axis=1)
        both = jnp.stack([b, ig.reshape(n_rows // chunk, chunk, C_HEADS)], axis=0)
        outs.append(both)
    g = jnp.stack(outs, axis=0)
    g = g.reshape(2, 2, n_rows // SCAN_ROWS, n_sub, chunk, C_HEADS)
    g_rows = jnp.transpose(g, (0, 1, 5, 2, 3, 4))
    g_cols = jnp.transpose(g, (0, 1, 5, 2, 4, 3))
    return g_rows, g_cols
```

```python
import functools
import math

import jax
import jax.numpy as jnp
from jax import lax
from jax.experimental import pallas as pl
from jax.experimental.pallas import tpu as pltpu

F32 = jnp.float32
BF16 = jnp.bfloat16

DEPTH = 2
N_MOD = 6
ALPHA = (2 * DEPTH) ** 0.25
LN_EPS = 1e-5
HEAD_DIM = 64
DECAY_LORA = 96
ICL_LORA = 96
GATE_LORA = 256
LNX_EPS = 64e-5
LRU_BLOCK_W = 128
CONV_W = 4
RGLRU_C = 8.0
C_HEADS = 8
C_QK_HEAD = 256
C_V_HEAD = 512
GATE_CAP = 15.0
MLSTM_NORM_EPS = 1e-6
N_EXPERTS = 16
N_GROUPS = 4
EXPERTS_PER_GROUP = N_EXPERTS // N_GROUPS
TOP_K = 2
POS_BASE = 10000.0
GRID_W = 64

LANES = 128
SUBLANES = 8
SCAN_ROWS = 256
CHUNK = 64
MLSTM_CHUNK = 128
MLSTM_HEADS = 2
SCAN_PAIRS = 8
PREP_ROWS = 128
LN_ROWS = 256
MM_ROWS = 512
MOE_ROWS = 512
MOE_FTILE = 256
MOE_NCHUNK = 1024
MOE_KCHUNK = 512
LORA_PAD = 1024
VMEM_LIMIT = 60 * 1024 * 1024


def _cparams(sem):
    return pltpu.CompilerParams(dimension_semantics=sem, vmem_limit_bytes=VMEM_LIMIT)


def _bdot(a, b):
    return jnp.dot(a.astype(BF16), b.astype(BF16), preferred_element_type=F32)


def _bdot_nt(a, b):
    return lax.dot_general(a.astype(BF16), b.astype(BF16), (((1,), (1,)), ((), ())),
                           preferred_element_type=F32)


def _bdot_tn(a, b):
    return lax.dot_general(a.astype(BF16), b.astype(BF16), (((0,), (0,)), ((), ())),
                           preferred_element_type=F32)


def _softplus(z):
    return jnp.maximum(z, 0.0) + jnp.log1p(jnp.exp(-jnp.abs(z)))


def _adaln_kernel(c_ref, w_ref, b_ref, o_ref):
    c = c_ref[...]
    s = c * jax.nn.sigmoid(c)
    o_ref[...] = _bdot(s, w_ref[...]) + b_ref[...]


def adaln_call(cvec, ada_w, ada_b, layer, tn=512):
    rows, d = cvec.shape
    n = ada_w.shape[2]
    return pl.pallas_call(
        _adaln_kernel,
        out_shape=jax.ShapeDtypeStruct((rows, n), F32),
        grid=(n // tn,),
        in_specs=[
            pl.BlockSpec((rows, d), lambda j: (0, 0)),
            pl.BlockSpec((None, d, tn), lambda j: (layer, 0, j)),
            pl.BlockSpec((None, 1, tn), lambda j: (layer, 0, j)),
        ],
        out_specs=pl.BlockSpec((rows, tn), lambda j: (0, j)),
        compiler_params=_cparams(("parallel",)),
        name="adaln",
    )(cvec, ada_w, ada_b)


def _embed_kernel(x_ref, ctx_ref, sr, cr, sc, cc, o_ref, *, n_lat_tiles, tiles_per_tab):
    i = pl.program_id(0)
    tm = o_ref.shape[0]
    g_rows = tm // GRID_W

    @pl.when(i < n_lat_tiles)
    def _():
        base = (i % tiles_per_tab) * g_rows

        def by_row(t_ref):
            return jnp.concatenate([jnp.broadcast_to(t_ref[pl.ds(base + g, 1), :], (GRID_W, t_ref.shape[1]))
                                    for g in range(g_rows)], axis=0)

        def by_col(t_ref):
            return jnp.concatenate([t_ref[...]] * g_rows, axis=0)

        o_ref[...] = x_ref[...] + jnp.concatenate([by_row(sr), by_row(cr), by_col(sc), by_col(cc)], axis=1)

    @pl.when(i >= n_lat_tiles)
    def _():
        o_ref[...] = ctx_ref[...]


def embed_call(x2d, ctx2d, seq_len):
    n_lat, d = x2d.shape
    n_ctx = ctx2d.shape[0]
    tm = LN_ROWS
    quarter = d // 4
    g_rows = tm // GRID_W
    assert tm % GRID_W == 0 and SUBLANES % g_rows == 0 and seq_len % (SUBLANES * GRID_W) == 0
    omega = POS_BASE ** (-jnp.arange(quarter, dtype=F32) / quarter)
    ang_r = jnp.arange(seq_len // GRID_W, dtype=F32)[:, None] * omega[None, :]
    ang_c = jnp.arange(GRID_W, dtype=F32)[:, None] * omega[None, :]
    n_lat_tiles = n_lat // tm
    tiles_per_tab = SUBLANES // g_rows
    tabs_per_seq = seq_len // GRID_W // SUBLANES
    tab = pl.BlockSpec((SUBLANES, quarter), lambda i: ((i // tiles_per_tab) % tabs_per_seq, 0))
    col = pl.BlockSpec((GRID_W, quarter), lambda i: (0, 0))
    return pl.pallas_call(
        functools.partial(_embed_kernel, n_lat_tiles=n_lat_tiles, tiles_per_tab=tiles_per_tab),
        out_shape=jax.ShapeDtypeStruct((n_lat + n_ctx, d), F32),
        grid=((n_lat + n_ctx) // tm,),
        in_specs=[pl.BlockSpec((tm, d), lambda i: (jnp.minimum(i, n_lat_tiles - 1), 0)),
                  pl.BlockSpec((tm, d), lambda i: (jnp.maximum(i - n_lat_tiles, 0), 0)),
                  tab, tab, col, col],
        out_specs=pl.BlockSpec((tm, d), lambda i: (i, 0)),
        compiler_params=_cparams(("parallel",)),
        name="embed",
    )(x2d, ctx2d, jnp.sin(ang_r), jnp.cos(ang_r), jnp.sin(ang_c), jnp.cos(ang_c))


def _mm_mod_kernel(x_ref, sh_ref, sc_ref, w_ref, o_ref, xb_ref):
    @pl.when(pl.program_id(1) == 0)
    def _():
        xb_ref[...] = (x_ref[...] * (1.0 + sc_ref[...]) + sh_ref[...]).astype(BF16)

    o_ref[...] = jnp.dot(xb_ref[...], w_ref[...], preferred_element_type=F32)


def matmul_mod_call(x, mods, w, layer_w, *, rows_per_mod, n_mod_rows, tn, n_rows=None, n_cols=None):
    r, d = x.shape
    n = w.shape[2] if n_cols is None else n_cols
    r = r if n_rows is None else n_rows
    tm = MM_ROWS

    def midx(i):
        return jnp.minimum(i * tm // rows_per_mod, n_mod_rows - 1)

    return pl.pallas_call(
        _mm_mod_kernel,
        out_shape=jax.ShapeDtypeStruct((r, n), F32),
        grid=(r // tm, n // tn),
        in_specs=[
            pl.BlockSpec((tm, d), lambda i, j: (i, 0)),
            pl.BlockSpec((None, 1, d), lambda i, j: (midx(i), 0, 0)),
            pl.BlockSpec((None, 1, d), lambda i, j: (midx(i), 0, 1)),
            pl.BlockSpec((None, d, tn), lambda i, j: (layer_w, 0, j)),
        ],
        out_specs=pl.BlockSpec((tm, tn), lambda i, j: (i, j)),
        scratch_shapes=[pltpu.VMEM((tm, d), BF16)],
        compiler_params=_cparams(("parallel", "arbitrary")),
        name="proj_in",
    )(x, mods, mods, w)


def _mm_kernel(x_ref, w_ref, o_ref):
    o_ref[...] = jnp.dot(x_ref[...], w_ref[...], preferred_element_type=F32)


def matmul_call(x, w, layer_w, *, tn, n_rows=None):
    r, k = x.shape
    n = w.shape[2]
    r = r if n_rows is None else n_rows
    tm = MM_ROWS
    return pl.pallas_call(
        _mm_kernel,
        out_shape=jax.ShapeDtypeStruct((r, n), F32),
        grid=(r // tm, n // tn),
        in_specs=[
            pl.BlockSpec((tm, k), lambda i, j: (i, 0)),
            pl.BlockSpec((None, k, tn), lambda i, j: (layer_w, 0, j)),
        ],
        out_specs=pl.BlockSpec((tm, tn), lambda i, j: (i, j)),
        compiler_params=_cparams(("parallel", "arbitrary")),
        name="proj_out",
    )(x, w)


def _tile_flags(i, tm, n_lat_rows, lat_len, ctx_len):
    row = i * tm
    in_lat = row < n_lat_rows
    pos = jnp.where(in_lat, row % lat_len, (row - n_lat_rows) % ctx_len)
    seq = jnp.where(in_lat, lat_len, ctx_len)
    return pos == 0, pos + tm == seq


def _row_shift(c, prev_row, next_row):
    tm = c.shape[0]
    rows = lax.broadcasted_iota(jnp.int32, c.shape, 0)
    xp = jnp.where(rows == 0, prev_row, pltpu.roll(c, 1, axis=0))
    xn = jnp.where(rows == tm - 1, next_row, pltpu.roll(c, tm - 1, axis=0))
    return xp, xn


def _head_ones():
    r = lax.broadcasted_iota(jnp.int32, (LANES, LANES), 0) // HEAD_DIM
    c = lax.broadcasted_iota(jnp.int32, (LANES, LANES), 1) // HEAD_DIM
    return (r == c).astype(BF16)


def _head_sum(x, ones):
    w = x.shape[1]
    parts = [jnp.dot(x[:, j:j + LANES].astype(BF16), ones, preferred_element_type=F32)
             for j in range(0, w, LANES)]
    return jnp.concatenate(parts, axis=1)


def _halo_specs(tm, width, col, n_rows):
    per = tm // SUBLANES
    last = n_rows // SUBLANES - 1
    return [
        pl.BlockSpec((tm, width), lambda i: (i, col)),
        pl.BlockSpec((SUBLANES, width), lambda i: (jnp.maximum(i * per - 1, 0), col)),
        pl.BlockSpec((SUBLANES, width), lambda i: (jnp.minimum((i + 1) * per, last), col)),
    ]


def _rwkv_prep_kernel(rc, rp, rn, kc, kp, kn, vc, vp, vn, lc, lp, ln,
                      mu_r, mu_k, mu_v, mu_l, w0, w2, a0, a2, g2, kk_s, ka_s,
                      r_o, v_o, kk_o, lw_o, kd_o, bb_o, g_o, *, tm, n_lat_rows, lat_len, ctx_len):
    first, last = _tile_flags(pl.program_id(0), tm, n_lat_rows, lat_len, ctx_len)

    def shifted(c_ref, p_ref, n_ref, mu_ref):
        c = c_ref[...]
        prev_row = jnp.where(first, 0.0, p_ref[SUBLANES - 1:SUBLANES, :])
        next_row = jnp.where(last, 0.0, n_ref[0:1, :])
        xp, xn = _row_shift(c, prev_row, next_row)
        return c + mu_ref[...] * (0.5 * (xp + xn) - c)

    r = shifted(rc, rp, rn, mu_r)
    k = shifted(kc, kp, kn, mu_k)
    v = shifted(vc, vp, vn, mu_v)
    lo = shifted(lc, lp, ln, mu_l)
    r_o[...] = r
    v_o[...] = v

    ones = _head_ones()
    kk = k * kk_s[...]
    ss = _head_sum(kk * kk, ones)
    kk = kk * lax.rsqrt(jnp.maximum(ss, 1e-24))
    kk_o[...] = kk

    for d in range(2):
        wd = lo[:, d * LANES:(d + 1) * LANES]
        ad = lo[:, (2 + d) * LANES:(3 + d) * LANES]
        z = w0[d:d + 1, :] + _bdot(jnp.tanh(wd), w2[d])
        log_w = -_softplus(-z) - 0.5
        lw_o[d] = -jnp.exp(log_w)
        icl = jax.nn.sigmoid(a0[d:d + 1, :] + _bdot(ad, a2[d]))
        kd_o[d] = k * (1.0 + (icl - 1.0) * ka_s[...])
        bb_o[d] = kk * icl
    gd = lo[:, 4 * LANES:4 * LANES + GATE_LORA]
    g_o[...] = _bdot(jax.nn.sigmoid(gd), g2[...])


def rwkv_prep_call(p, prm, *, n_lat_rows, lat_len, ctx_len, col_r, col_k, col_v, col_l):
    r_rows = p.shape[0]
    tm = PREP_ROWS
    aw = prm["mu_r"].shape[1]
    full = lambda shape: pl.BlockSpec(shape, lambda i: (0,) * len(shape))
    in_specs = (_halo_specs(tm, aw, col_r, r_rows) + _halo_specs(tm, aw, col_k, r_rows)
                + _halo_specs(tm, aw, col_v, r_rows) + _halo_specs(tm, LORA_PAD, col_l, r_rows)
                + [full((1, aw)), full((1, aw)), full((1, aw)), full((1, LORA_PAD)),
                   full((2, aw)), full((2, LANES, aw)), full((2, aw)), full((2, LANES, aw)),
                   full((GATE_LORA, aw)), full((1, aw)), full((1, aw))])
    row_spec = pl.BlockSpec((tm, aw), lambda i: (i, 0))
    dir_spec = pl.BlockSpec((2, tm, aw), lambda i: (0, i, 0))
    one = jax.ShapeDtypeStruct((r_rows, aw), F32)
    two = jax.ShapeDtypeStruct((2, r_rows, aw), F32)
    kern = functools.partial(_rwkv_prep_kernel, tm=tm, n_lat_rows=n_lat_rows, lat_len=lat_len, ctx_len=ctx_len)
    return pl.pallas_call(
        kern,
        out_shape=(one, one, one, two, two, two, one),
        grid=(r_rows // tm,),
        in_specs=in_specs,
        out_specs=(row_spec, row_spec, row_spec, dir_spec, dir_spec, dir_spec, row_spec),
        compiler_params=_cparams(("parallel",)),
        name="rwkv_prep",
    )(p, p, p, p, p, p, p, p, p, p, p, p,
      prm["mu_r"], prm["mu_k"], prm["mu_v"], prm["mu_l"], prm["w0"], prm["w2"], prm["a0"], prm["a2"],
      prm["g2"], prm["k_k"], prm["k_a"])


def _scan_block(b, s, *, reverse, n_lat_blocks_total, lat_blocks):
    ctx_blk = n_lat_blocks_total + b
    lat_blk = b * lat_blocks + ((lat_blocks - s) if reverse else (s - 1))
    return jnp.where(s == 0, ctx_blk, lat_blk)


def _rwkv_scan_kernel(r_ref, lw_ref, k_ref, v_ref, kk_ref, bb_ref, y_ref, s_ref, *, reverse):
    @pl.when(pl.program_id(2) == 0)
    def _():
        s_ref[...] = jnp.zeros_like(s_ref)

    L = CHUNK
    ri = lax.broadcasted_iota(jnp.int32, (L, L), 0)
    ci = lax.broadcasted_iota(jnp.int32, (L, L), 1)
    tri = ((ci >= ri) if reverse else (ci <= ri)).astype(F32)
    r2 = lax.broadcasted_iota(jnp.int32, (2 * L, 2 * L), 0) % L
    c2 = lax.broadcasted_iota(jnp.int32, (2 * L, 2 * L), 1) % L
    strict = (c2 > r2) if reverse else (c2 < r2)
    incl = (c2 >= r2) if reverse else (c2 <= r2)
    rr = lax.broadcasted_iota(jnp.int32, (2 * L, 2 * L), 0)
    cc = lax.broadcasted_iota(jnp.int32, (2 * L, 2 * L), 1)
    eye = (rr == cc).astype(F32)
    off_masks = []
    sz = 2
    while sz <= L:
        late, early = (cc, rr) if reverse else (rr, cc)
        off_masks.append((rr // sz == cc // sz) & (late % sz >= sz // 2) & (early % sz < sz // 2))
        sz *= 2
    lane = lax.broadcasted_iota(jnp.int32, (L, LANES), 1)
    head0 = lane < HEAD_DIM

    def stack2(x):
        return jnp.concatenate([jnp.where(head0, x, 0.0), jnp.where(head0, 0.0, x)], axis=0)

    n_sub = SCAN_ROWS // L
    n_pair = r_ref.shape[1] // LANES
    order = [(n_sub - 1 - j) if reverse else j for j in range(n_sub)]

    keys = [(c, p) for c in order for p in range(n_pair)]
    a2, b2, k2, q2, v2, bt2, kt2, decay = ({} for _ in range(8))
    for c in order:
        rows = pl.ds(c * L, L)
        lw_all = lw_ref[rows, :]
        cum_all = jnp.dot(tri, lw_all, preferred_element_type=F32, precision=lax.Precision.HIGHEST)
        for p in range(n_pair):
            cols = slice(p * LANES, (p + 1) * LANES)
            lw = lw_all[:, cols]
            cum = cum_all[:, cols]
            tot = cum[0:1, :] if reverse else cum[L - 1:L, :]
            e_pos = jnp.exp(cum)
            e_neg = jnp.exp(-cum)
            e_rem = jnp.exp(tot - cum)
            kk = kk_ref[rows, cols]
            kd = k_ref[rows, cols]
            bb = bb_ref[rows, cols]
            key = (c, p)
            a2[key] = stack2(-kk * jnp.exp(cum - lw))
            b2[key] = stack2(bb * e_neg)
            k2[key] = stack2(kd * e_neg)
            q2[key] = stack2(r_ref[rows, cols] * e_pos)
            v2[key] = stack2(v_ref[rows, cols])
            bt2[key] = stack2(bb * e_rem)
            kt2[key] = stack2(kd * e_rem)
            decay[key] = jnp.exp(tot)

    hh = 2 * L
    sc = {k: _bdot_nt(jnp.concatenate([a2[k], q2[k]], axis=0), jnp.concatenate([b2[k], k2[k]], axis=0))
          for k in keys}
    mm = {k: jnp.where(strict, sc[k][:hh, :hh], 0.0) for k in keys}
    nn = {k: jnp.where(strict, sc[k][:hh, hh:], 0.0) for k in keys}
    qq = {k: jnp.where(incl, sc[k][hh:, :hh], 0.0) for k in keys}
    zz = {k: jnp.where(incl, sc[k][hh:, hh:], 0.0) for k in keys}
    tinv = {k: eye + jnp.where(off_masks[0], mm[k], 0.0) for k in keys}
    for off in off_masks[1:]:
        half_step = {k: _bdot(tinv[k], jnp.where(off, mm[k], 0.0)) for k in keys}
        tinv = {k: tinv[k] + _bdot(half_step[k], tinv[k]) for k in keys}
    nzv = {k: _bdot(jnp.concatenate([nn[k], zz[k]], axis=0), v2[k]) for k in keys}
    px = {k: _bdot(tinv[k], jnp.concatenate([a2[k], nzv[k][:hh]], axis=1)) for k in keys}
    qpx = {k: _bdot(qq[k], px[k]) for k in keys}
    g_mat = {k: q2[k] + qpx[k][:, :LANES] for k in keys}
    y_loc = {k: qpx[k][:, LANES:] + nzv[k][hh:] for k in keys}
    pb = {k: _bdot_tn(px[k], bt2[k]) for k in keys}
    phi = {k: pb[k][:LANES] for k in keys}
    psi = {k: pb[k][LANES:] + _bdot_tn(v2[k], kt2[k]) for k in keys}
    items = {k: (g_mat[k], y_loc[k], decay[k], phi[k], psi[k]) for k in keys}

    st = [s_ref[p] for p in range(n_pair)]
    for c in order:
        for p in range(n_pair):
            g, yl, dec, ph, ps = items[(c, p)]
            y2 = _bdot_nt(g, st[p]) + yl
            y_ref[pl.ds(c * L, L), p * LANES:(p + 1) * LANES] = y2[:L, :] + y2[L:, :]
            st[p] = st[p] * dec + _bdot(st[p], ph) + ps
    for p in range(n_pair):
        s_ref[p] = st[p]


def rwkv_scan_call(r, lw, kd, v, kk, bb, *, reverse, batch, lat_len, ctx_len):
    r_rows, aw = r.shape
    d = 1 if reverse else 0
    lat_blocks = lat_len // SCAN_ROWS
    width = SCAN_PAIRS * LANES
    blk = functools.partial(_scan_block, reverse=reverse, n_lat_blocks_total=batch * lat_blocks,
                            lat_blocks=lat_blocks)
    row_spec = pl.BlockSpec((SCAN_ROWS, width), lambda b, h, s: (blk(b, s), h))
    dir_spec = pl.BlockSpec((None, SCAN_ROWS, width), lambda b, h, s: (d, blk(b, s), h))
    return pl.pallas_call(
        functools.partial(_rwkv_scan_kernel, reverse=reverse),
        out_shape=jax.ShapeDtypeStruct((r_rows, aw), F32),
        grid=(batch, aw // width, 1 + lat_blocks),
        in_specs=[row_spec, dir_spec, dir_spec, row_spec, row_spec, dir_spec],
        out_specs=row_spec,
        scratch_shapes=[pltpu.VMEM((SCAN_PAIRS, LANES, LANES), F32)],
        compiler_params=_cparams(("parallel", "parallel", "arbitrary")),
        name="rwkv_scan_bwd" if reverse else "rwkv_scan_fwd",
    )(r, lw, kd, v, kk, bb)


def _lru_kernel(xcf, xpf, xnf, xcb, xpb, xnb, cw, cb, wcat, bcat, lsl, hf_ref, hb_ref,
                af_ref, uf_ref, ab_ref, ub_ref, carry_f, carry_b, *, lat_blocks):
    s = pl.program_id(1)

    @pl.when(s == 0)
    def _():
        carry_f[...] = jnp.zeros_like(carry_f)
        carry_b[...] = jnp.zeros_like(carry_b)

    for d, (xc_ref, xp_ref, xn_ref, a_ref, u_ref) in enumerate(((xcf, xpf, xnf, af_ref, uf_ref),
                                                                (xcb, xpb, xnb, ab_ref, ub_ref))):
        lat_idx = (lat_blocks - s) if d == 1 else (s - 1)
        first = (s == 0) | (lat_idx == 0)
        last = (s == 0) | (lat_idx == lat_blocks - 1)
        _lru_gates(xc_ref, xp_ref, xn_ref, cw, cb, wcat.at[d], bcat.at[d], lsl.at[d], a_ref, u_ref, first, last)

    w = af_ref.shape[1]
    rows = lax.broadcasted_iota(jnp.int32, (SUBLANES, w), 0)
    n_groups = SCAN_ROWS // SUBLANES

    def group_scan(a_ref, u_ref, h_ref, g, carry, reverse):
        sl = pl.ds(pl.multiple_of(g * SUBLANES, SUBLANES), SUBLANES)
        a = a_ref[sl, :]
        u = u_ref[sl, :]
        for sh in (1, 2, 4):
            if reverse:
                ok = rows < SUBLANES - sh
                a_s = pltpu.roll(a, SUBLANES - sh, axis=0)
                u_s = pltpu.roll(u, SUBLANES - sh, axis=0)
            else:
                ok = rows >= sh
                a_s = pltpu.roll(a, sh, axis=0)
                u_s = pltpu.roll(u, sh, axis=0)
            u = jnp.where(ok, a * u_s + u, u)
            a = jnp.where(ok, a * a_s, a)
        h = a * carry + u
        h_ref[sl, :] = h
        return h[0:1, :] if reverse else h[SUBLANES - 1:SUBLANES, :]

    def body(gi, carry):
        cf, cb_ = carry
        return (group_scan(af_ref, uf_ref, hf_ref, gi, cf, False),
                group_scan(ab_ref, ub_ref, hb_ref, n_groups - 1 - gi, cb_, True))

    cf, cb_ = lax.fori_loop(0, n_groups, body, (carry_f[...], carry_b[...]))
    carry_f[...] = cf
    carry_b[...] = cb_


def _lru_gates(xc_ref, xp_ref, xn_ref, cw, cb, wcat, bcat, lsl, a_ref, u_ref, first, last):
    tm = SCAN_ROWS
    x = xc_ref[...]
    prev_row = jnp.where(first, 0.0, xp_ref[SUBLANES - 1:SUBLANES, :])
    next1 = jnp.where(last, 0.0, xn_ref[0:1, :])
    next2 = jnp.where(last, 0.0, xn_ref[1:2, :])
    row_id = lax.broadcasted_iota(jnp.int32, x.shape, 0)
    xm1 = jnp.where(row_id == 0, prev_row, pltpu.roll(x, 1, axis=0))
    xp1 = jnp.where(row_id == tm - 1, next1, pltpu.roll(x, tm - 1, axis=0))
    xp2 = jnp.where(row_id == tm - 1, next2, jnp.where(row_id == tm - 2, next1, pltpu.roll(x, tm - 2, axis=0)))
    xc = cb[...] + xm1 * cw[0:1, :] + x * cw[1:2, :] + xp1 * cw[2:3, :] + xp2 * cw[3:4, :]
    bwid = LRU_BLOCK_W
    for n in range(x.shape[1] // bwid):
        cols = slice(n * bwid, (n + 1) * bwid)
        xn = xc[:, cols]
        gates = _bdot(xn, wcat[n]) + bcat[:, 2 * n * bwid:2 * (n + 1) * bwid]
        log_a = jax.nn.sigmoid(gates[:, :bwid]) * lsl[:, cols]
        a = jnp.exp(log_a)
        om = -jnp.tanh(log_a) * (a * a + 1.0)
        a_ref[:, cols] = a
        u_ref[:, cols] = xn * jax.nn.sigmoid(gates[:, bwid:]) * jnp.sqrt(om)


def lru_call(p, prm, *, batch, lat_len, ctx_len, col_x):
    r_rows = p.shape[0]
    bw = prm["conv_b"].shape[1]
    nb = bw // LRU_BLOCK_W
    lat_blocks = lat_len // SCAN_ROWS
    per = SCAN_ROWS // SUBLANES
    last8 = r_rows // SUBLANES - 1
    full = lambda shape: pl.BlockSpec(shape, lambda b, s: (0,) * len(shape))
    x_specs, out_specs = [], []
    for reverse in (False, True):
        blk = functools.partial(_scan_block, reverse=reverse, n_lat_blocks_total=batch * lat_blocks,
                                lat_blocks=lat_blocks)
        x_specs += [
            pl.BlockSpec((SCAN_ROWS, bw), lambda b, s, blk=blk: (blk(b, s), col_x)),
            pl.BlockSpec((SUBLANES, bw), lambda b, s, blk=blk: (jnp.maximum(blk(b, s) * per - 1, 0), col_x)),
            pl.BlockSpec((SUBLANES, bw), lambda b, s, blk=blk: (jnp.minimum((blk(b, s) + 1) * per, last8), col_x)),
        ]
        out_specs.append(pl.BlockSpec((SCAN_ROWS, bw), lambda b, s, blk=blk: (blk(b, s), 0)))
    out = jax.ShapeDtypeStruct((r_rows, bw), F32)
    tile = pltpu.VMEM((SCAN_ROWS, bw), F32)
    return pl.pallas_call(
        functools.partial(_lru_kernel, lat_blocks=lat_blocks),
        out_shape=(out, out),
        grid=(batch, 1 + lat_blocks),
        in_specs=x_specs + [full((CONV_W, bw)), full((1, bw)), full((2, nb, LRU_BLOCK_W, 2 * LRU_BLOCK_W)),
                            full((2, 1, 2 * bw)), full((2, 1, bw))],
        out_specs=tuple(out_specs),
        scratch_shapes=[tile, tile, tile, tile, pltpu.VMEM((1, bw), F32), pltpu.VMEM((1, bw), F32)],
        compiler_params=_cparams(("parallel", "arbitrary")),
        name="lru",
    )(p, p, p, p, p, p, prm["conv_w"], prm["conv_b"], prm["wcat"], prm["bcat"], prm["lsl"])


def _even_post_kernel(yf, yb, r_ref, v_ref, kdf, kdb, g_ref, hf, hb, gb_ref, rk, lnw, lnb, o_ref):
    ones = _head_ones()
    y = yf[...] + yb[...]
    inv_n = 1.0 / HEAD_DIM
    yc = y - _head_sum(y, ones) * inv_n
    var = _head_sum(yc * yc, ones) * inv_n
    hn = yc * lax.rsqrt(var + LNX_EPS) * lnw[...] + lnb[...]
    r = r_ref[...]
    v = v_ref[...]
    bonus = _head_sum(r * kdf[...] * rk[...], ones) * v + _head_sum(r * kdb[...] * rk[...], ones) * v
    ya = (hn + bonus) * g_ref[...]
    yl = (hf[...] + hb[...]) * jax.nn.gelu(gb_ref[...])
    aw = ya.shape[1]
    o_ref[:, :aw] = ya.astype(BF16)
    o_ref[:, aw:] = yl.astype(BF16)


def even_post_call(yf, yb, r, v, kd, g, hf, hb, p, prm, *, col_gb):
    r_rows, aw = yf.shape
    bw = hf.shape[1]
    tm = PREP_ROWS
    row = lambda w: pl.BlockSpec((tm, w), lambda i: (i, 0))
    full = lambda shape: pl.BlockSpec(shape, lambda i: (0,) * len(shape))
    return pl.pallas_call(
        _even_post_kernel,
        out_shape=jax.ShapeDtypeStruct((r_rows, aw + bw), BF16),
        grid=(r_rows // tm,),
        in_specs=[row(aw), row(aw), row(aw), row(aw),
                  pl.BlockSpec((None, tm, aw), lambda i: (0, i, 0)),
                  pl.BlockSpec((None, tm, aw), lambda i: (1, i, 0)),
                  row(aw), row(bw), row(bw),
                  pl.BlockSpec((tm, bw), lambda i: (i, col_gb)),
                  full((1, aw)), full((1, aw)), full((1, aw))],
        out_specs=row(aw + bw),
        compiler_params=_cparams(("parallel",)),
        name="even_post",
    )(yf, yb, r, v, kd, kd, g, hf, hb, p, prm["r_k"], prm["lnx_w"], prm["lnx_b"])


def _mlstm_kernel(qf, kf, vf, brf, bcf, irf, icf, qb, kb, vb, brb, bcb, irb, icb, hf_ref, hb_ref, *states):
    @pl.when(pl.program_id(2) == 0)
    def _():
        for ref in states:
            ref[...] = jnp.zeros_like(ref)

    dirs = ((qf, kf, vf, brf, bcf, irf, icf, hf_ref, False), (qb, kb, vb, brb, bcb, irb, icb, hb_ref, True))
    for d, (q, k, v, br, bc, ir, ic, h_ref, reverse) in enumerate(dirs):
        for hh in range(MLSTM_HEADS):
            qk = slice(hh * C_QK_HEAD, (hh + 1) * C_QK_HEAD)
            vv = slice(hh * C_V_HEAD, (hh + 1) * C_V_HEAD)
            c_ref, n_ref, m_ref = states[3 * (d * MLSTM_HEADS + hh):3 * (d * MLSTM_HEADS + hh) + 3]
            _mlstm_block(q.at[:, qk], k.at[:, qk], v.at[:, vv], br.at[hh], bc.at[hh], ir.at[hh], ic.at[hh],
                         h_ref.at[:, vv], c_ref, n_ref, m_ref, reverse=reverse)


def _mlstm_block(q_ref, k_ref, v_ref, br_ref, bc_ref, ir_ref, ic_ref, h_ref, c_ref, n_ref, m_ref, *, reverse):
    L = MLSTM_CHUNK
    ri = lax.broadcasted_iota(jnp.int32, (L, L), 0)
    ci = lax.broadcasted_iota(jnp.int32, (L, L), 1)
    causal = (ci >= ri) if reverse else (ci <= ri)
    n_sub = SCAN_ROWS // L
    order = [(n_sub - 1 - j) if reverse else j for j in range(n_sub)]
    scale = C_QK_HEAD ** -0.5

    m_cur = m_ref[...]
    m_in, ew, ec = {}, {}, {}
    for c in order:
        b_r = br_ref[c:c + 1, :]
        b_tot = b_r[:, 0:1] if reverse else b_r[:, L - 1:L]
        w_in = b_tot - bc_ref[:, c:c + 1] + ic_ref[:, c:c + 1]
        carry_log = b_tot + m_cur
        m_new = jnp.maximum(carry_log, jnp.max(w_in, axis=0, keepdims=True))
        m_in[c] = m_cur
        ew[c] = jnp.exp(w_in - m_new)
        ec[c] = jnp.exp(carry_log - m_new)
        m_cur = m_new
    m_ref[...] = m_cur

    q = {c: (q_ref[pl.ds(c * L, L), :] * scale).astype(BF16) for c in order}
    k = {c: k_ref[pl.ds(c * L, L), :] for c in order}
    v = {c: v_ref[pl.ds(c * L, L), :].astype(BF16) for c in order}
    qk = {c: _bdot_nt(q[c], k[c]) for c in order}
    m_t, e_inter, scores = {}, {}, {}
    for c in order:
        b_c = bc_ref[:, c:c + 1]
        dmat = jnp.where(causal, b_c - br_ref[c:c + 1, :] + ir_ref[c:c + 1, :], -jnp.inf)
        inter = b_c + m_in[c]
        m_t[c] = jnp.maximum(inter, jnp.max(dmat, axis=1, keepdims=True))
        scores[c] = qk[c] * jnp.exp(dmat - m_t[c])
        e_inter[c] = jnp.exp(inter - m_t[c])
    intra = {c: _bdot(scores[c], v[c]) for c in order}
    kw = {c: k[c] * ew[c] for c in order}
    kv = {c: _bdot_tn(kw[c], v[c]) for c in order}

    c_cur = c_ref[...]
    n_cur = n_ref[...]
    c_in, n_in = {}, {}
    for c in order:
        c_in[c] = c_cur
        n_in[c] = n_cur
        c_cur = ec[c] * c_cur + kv[c]
        n_cur = ec[c] * n_cur + jnp.sum(kw[c], axis=0, keepdims=True)
    c_ref[...] = c_cur
    n_ref[...] = n_cur

    qc = {c: _bdot(q[c], c_in[c]) for c in order}
    for c in order:
        qf = q[c].astype(F32)
        num = e_inter[c] * qc[c] + intra[c]
        den = (e_inter[c] * jnp.sum(qf * n_in[c], axis=1, keepdims=True)
               + jnp.sum(scores[c], axis=1, keepdims=True))
        h_ref[pl.ds(c * L, L), :] = num / jnp.maximum(jnp.abs(den), jnp.exp(-m_t[c]))


def mlstm_call(p, g_rows, g_cols, *, batch, lat_len, ctx_len, col_q, col_k, col_v):
    r_rows = p.shape[0]
    n_sub = SCAN_ROWS // MLSTM_CHUNK
    lat_blocks = lat_len // SCAN_ROWS
    hp = MLSTM_HEADS
    assert C_HEADS % hp == 0 and col_q % hp == 0 and col_k % hp == 0 and col_v % hp == 0
    in_specs, out_specs = [], []
    for d, reverse in enumerate((False, True)):
        blk = functools.partial(_scan_block, reverse=reverse, n_lat_blocks_total=batch * lat_blocks,
                                lat_blocks=lat_blocks)
        grow = lambda which, d=d, blk=blk: pl.BlockSpec(
            (None, None, hp, None, n_sub, MLSTM_CHUNK), lambda b, h, s: (d, which, h, blk(b, s), 0, 0))
        gcol = lambda which, d=d, blk=blk: pl.BlockSpec(
            (None, None, hp, None, MLSTM_CHUNK, n_sub), lambda b, h, s: (d, which, h, blk(b, s), 0, 0))
        in_specs += [
            pl.BlockSpec((SCAN_ROWS, hp * C_QK_HEAD), lambda b, h, s, blk=blk: (blk(b, s), col_q // hp + h)),
            pl.BlockSpec((SCAN_ROWS, hp * C_QK_HEAD), lambda b, h, s, blk=blk: (blk(b, s), col_k // hp + h)),
            pl.BlockSpec((SCAN_ROWS, hp * C_V_HEAD), lambda b, h, s, blk=blk: (blk(b, s), col_v // hp + h)),
            grow(0), gcol(0), grow(1), gcol(1),
        ]
        out_specs.append(pl.BlockSpec((SCAN_ROWS, hp * C_V_HEAD), lambda b, h, s, blk=blk: (blk(b, s), h)))
    out = jax.ShapeDtypeStruct((r_rows, C_HEADS * C_V_HEAD), F32)
    state = [pltpu.VMEM((C_QK_HEAD, C_V_HEAD), F32), pltpu.VMEM((1, C_QK_HEAD), F32), pltpu.VMEM((1, 1), F32)]
    one_dir = (p, p, p, g_rows, g_cols, g_rows, g_cols)
    return pl.pallas_call(
        _mlstm_kernel,
        out_shape=(out, out),
        grid=(batch, C_HEADS // hp, 1 + lat_blocks),
        in_specs=in_specs,
        out_specs=tuple(out_specs),
        scratch_shapes=state * (2 * hp),
        compiler_params=_cparams(("parallel", "parallel", "arbitrary")),
        name="mlstm",
    )(*one_dir, *one_dir)


def _odd_post_kernel(hf, hb, o_lo, o_hi, nw, nb, y_ref):
    half_heads = C_HEADS // 2
    for hd in range(C_HEADS):
        cols = slice(hd * C_V_HEAD, (hd + 1) * C_V_HEAD)
        x = hf[:, cols] + hb[:, cols]
        xc = x - jnp.mean(x, axis=1, keepdims=True)
        var = jnp.mean(xc * xc, axis=1, keepdims=True)
        hn = xc * lax.rsqrt(var + MLSTM_NORM_EPS) * nw[:, cols] + nb[:, cols]
        o_ref = o_lo if hd < half_heads else o_hi
        oc = slice((hd % half_heads) * C_V_HEAD, (hd % half_heads + 1) * C_V_HEAD)
        y_ref[:, cols] = (hn * jax.nn.sigmoid(o_ref[:, oc])).astype(BF16)


def odd_post_call(hf, hb, p, nw, nb, *, n_rows, col_o):
    vw = hf.shape[1]
    tm = LN_ROWS
    row = pl.BlockSpec((tm, vw), lambda i: (i, 0))
    par = pl.BlockSpec((1, vw), lambda i: (0, 0))
    return pl.pallas_call(
        _odd_post_kernel,
        out_shape=jax.ShapeDtypeStruct((n_rows, vw), BF16),
        grid=(n_rows // tm,),
        in_specs=[row, row, pl.BlockSpec((tm, vw // 2), lambda i: (i, col_o)),
                  pl.BlockSpec((tm, vw // 2), lambda i: (i, col_o + 1)), par, par],
        out_specs=row,
        compiler_params=_cparams(("parallel",)),
        name="odd_post",
    )(hf, hb, p, p, nw, nb)


def _layer_norm(z, g, b):
    zc = z - jnp.mean(z, axis=1, keepdims=True)
    var = jnp.mean(zc * zc, axis=1, keepdims=True)
    return zc * lax.rsqrt(var + LN_EPS) * g + b


def _route(logits_t, bias):
    aff = [jax.nn.sigmoid(logits_t[e:e + 1, :]) for e in range(N_EXPERTS)]
    biased = [aff[e] + bias[e:e + 1, :] for e in range(N_EXPERTS)]
    best_g = best_v = None
    for g in range(N_GROUPS):
        m = biased[g * EXPERTS_PER_GROUP:(g + 1) * EXPERTS_PER_GROUP]
        pair = None
        for i in range(EXPERTS_PER_GROUP):
            for j in range(i + 1, EXPERTS_PER_GROUP):
                hi = jnp.maximum(m[i], m[j])
                lo_ = jnp.minimum(m[i], m[j])
                s = hi + lo_
                pair = s if pair is None else jnp.maximum(pair, s)
        if best_v is None:
            best_v, best_g = pair, jnp.zeros(pair.shape, jnp.int32)
        else:
            upd = pair > best_v
            best_g = jnp.where(upd, g, best_g)
            best_v = jnp.where(upd, pair, best_v)
    ids, sels = [], []
    taken = None
    for _ in range(TOP_K):
        cur_v = cur_i = cur_a = None
        for e in range(N_EXPERTS):
            ok = best_g == (e // EXPERTS_PER_GROUP)
            if taken is not None:
                ok = jnp.logical_and(ok, taken != e)
            val = jnp.where(ok, biased[e], -jnp.inf)
            if cur_v is None:
                cur_v, cur_i, cur_a = val, jnp.zeros(val.shape, jnp.int32), aff[e]
            else:
                upd = val > cur_v
                cur_i = jnp.where(upd, e, cur_i)
                cur_a = jnp.where(upd, aff[e], cur_a)
                cur_v = jnp.where(upd, val, cur_v)
        ids.append(cur_i)
        sels.append(cur_a)
        taken = cur_i
    tot = sels[0] + sels[1]
    return ids, [sels[0] / tot, sels[1] / tot]


def _ln_route_kernel(lat_ref, y_ref, gate_ref, sh_ref, sc_ref, g_ref, b_ref, rw_ref, rb_ref,
                     lat_o, h_o, id_o, gt_o):
    z = ALPHA * lat_ref[...] + gate_ref[...] * y_ref[...]
    ln = _layer_norm(z, g_ref[...], b_ref[...])
    lat_o[...] = ln
    h = ln * (1.0 + sc_ref[...]) + sh_ref[...]
    half = h.shape[1] // 2
    bits = lax.bitcast_convert_type(h.astype(BF16).astype(F32), jnp.uint32)
    h_o[...] = (bits[:, :half] >> 16) | bits[:, half:]
    logits_t = lax.dot_general(rw_ref[...], h, (((1,), (1,)), ((), ())), preferred_element_type=F32,
                               precision=lax.Precision.HIGHEST)
    ids, gts = _route(logits_t, rb_ref[...])
    id_o[...] = jnp.concatenate(ids, axis=0)
    gt_o[...] = jnp.concatenate(gts, axis=0)


def ln_route_call(lat, y, mods, norm_g, norm_b, layer, router_wt, router_b, *, n_rows, rows_per_mod, n_mod_rows):
    d = lat.shape[1]
    tm = LN_ROWS

    def midx(i):
        return jnp.minimum(i * tm // rows_per_mod, n_mod_rows - 1)

    row = pl.BlockSpec((tm, d), lambda i: (i, 0))
    packed = pl.BlockSpec((tm, d // 2), lambda i: (i, 0))
    mod = lambda k: pl.BlockSpec((None, 1, d), lambda i: (midx(i), 0, k))
    nrm = pl.BlockSpec((None, None, 1, d), lambda i: (layer, 0, 0, 0))
    full = lambda shape: pl.BlockSpec(shape, lambda i: (0,) * len(shape))
    sel = pl.BlockSpec((TOP_K, tm), lambda i: (0, i))
    return pl.pallas_call(
        _ln_route_kernel,
        out_shape=(jax.ShapeDtypeStruct((n_rows, d), F32), jax.ShapeDtypeStruct((n_rows, d // 2), jnp.uint32),
                   jax.ShapeDtypeStruct((TOP_K, n_rows), jnp.int32), jax.ShapeDtypeStruct((TOP_K, n_rows), F32)),
        grid=(n_rows // tm,),
        in_specs=[row, row, mod(2), mod(3), mod(4), nrm, nrm, full((N_EXPERTS, d)), full((N_EXPERTS, 1))],
        out_specs=(row, packed, sel, sel),
        compiler_params=_cparams(("parallel",)),
        name="ln_route",
    )(lat, y, mods, mods, mods, norm_g, norm_b, router_wt, router_b)


def _ln_combine_kernel(lat_ref, y0_ref, y1_ref, gt_ref, gate_ref, g_ref, b_ref, lat_o):
    gt = gt_ref[...]
    f = y0_ref[...] * gt[:, 0:1] + y1_ref[...] * gt[:, 1:2]
    z = ALPHA * lat_ref[...] + gate_ref[...] * f
    lat_o[...] = _layer_norm(z, g_ref[...], b_ref[...])


def ln_combine_call(lat, y_slots, gates_t, mods, norm_g, norm_b, layer, *, n_rows, rows_per_mod, n_mod_rows):
    d = lat.shape[1]
    tm = LN_ROWS
    per_k = n_rows // tm

    def midx(i):
        return jnp.minimum(i * tm // rows_per_mod, n_mod_rows - 1)

    row = pl.BlockSpec((tm, d), lambda i: (i, 0))
    nrm = pl.BlockSpec((None, None, 1, d), lambda i: (layer, 1, 0, 0))
    return pl.pallas_call(
        _ln_combine_kernel,
        out_shape=jax.ShapeDtypeStruct((n_rows, d), F32),
        grid=(per_k,),
        in_specs=[row, row, pl.BlockSpec((tm, d), lambda i: (per_k + i, 0)),
                  pl.BlockSpec((tm, TOP_K), lambda i: (i, 0)),
                  pl.BlockSpec((None, 1, d), lambda i: (midx(i), 0, 5)), nrm, nrm],
        out_specs=row,
        compiler_params=_cparams(("parallel",)),
        name="ln_combine",
    )(lat, y_slots, y_slots, gates_t, mods, norm_g, norm_b)


def _moe_kernel(nu_ref, be_ref, jw_ref, src_cur, src_nxt, dst_prv, h_hbm, wg_ref, wu_ref, wd_ref, y_hbm,
                xbuf, acc, gsem, ssem, *, n_j, n_slots):
    del be_ref, jw_ref
    i = pl.program_id(0)
    j = pl.program_id(1)
    buf = i % 2
    other = 1 - buf
    n_used = nu_ref[0]
    per_step, half = xbuf.shape[2], xbuf.shape[3]
    tm = n_j * per_step
    d = acc.shape[3]

    def row_in(tok, q, rr, b):
        return pltpu.make_async_copy(h_hbm.at[pl.ds(tok, 1)], xbuf.at[b, q, pl.ds(rr, 1)], gsem.at[b])

    def row_out(q, rr, slot, b):
        return pltpu.make_async_copy(acc.at[b, q, pl.ds(rr, 1)], y_hbm.at[pl.ds(slot, 1)], ssem.at[b])

    def send_previous(q, rr):
        r = q * per_step + rr
        slot = jnp.where(i > 0, dst_prv[0, r], n_slots + r)
        row_out(q, rr, slot, other).start()

    @pl.when((i == 0) & (j == 0))
    def _():
        acc[1] = jnp.zeros((n_j, per_step, d), F32)
        for q in range(n_j):
            def body(rr, carry, q=q):
                row_in(src_cur[0, q * per_step + rr], q, rr, 0).start()
                return carry
            lax.fori_loop(0, per_step, body, 0, unroll=8)

    @pl.when((j == 0) & (i <= n_used))
    def _():
        pltpu.make_async_copy(xbuf.at[buf], xbuf.at[buf], gsem.at[buf]).wait()

    @pl.when((j == 0) & (i < n_used))
    def _():
        acc[buf] = jnp.zeros((n_j, per_step, d), F32)

    @pl.when(i < n_used)
    def _():
        tf = wg_ref.shape[1]
        g = jnp.zeros((tm, tf), F32)
        u = jnp.zeros((tm, tf), F32)
        for c0 in range(0, half, MOE_KCHUNK):
            xp = xbuf[buf, :, :, c0:c0 + MOE_KCHUNK].reshape(tm, MOE_KCHUNK)
            lo = lax.bitcast_convert_type(xp << 16, F32)
            hi = lax.bitcast_convert_type(xp & jnp.uint32(0xFFFF0000), F32)
            g = g + _bdot(lo, wg_ref[c0:c0 + MOE_KCHUNK, :]) + _bdot(hi, wg_ref[half + c0:half + c0 + MOE_KCHUNK, :])
            u = u + _bdot(lo, wu_ref[c0:c0 + MOE_KCHUNK, :]) + _bdot(hi, wu_ref[half + c0:half + c0 + MOE_KCHUNK, :])
        hdn = (g * jax.nn.sigmoid(g) * u).astype(BF16)
        for c0 in range(0, d, MOE_NCHUNK):
            part = _bdot(hdn, wd_ref[:, c0:c0 + MOE_NCHUNK])
            acc[buf, :, :, c0:c0 + MOE_NCHUNK] += part.reshape(n_j, per_step, MOE_NCHUNK)
        for rr in range(per_step):
            row_in(src_nxt[0, j * per_step + rr], j, rr, other).start()
            send_previous(j, rr)

    @pl.when(i == n_used)
    def _():
        for rr in range(per_step):
            send_previous(j, rr)

    @pl.when((j == n_j - 1) & (i <= n_used))
    def _():
        pltpu.make_async_copy(acc.at[other], acc.at[other], ssem.at[other]).wait()


def moe_call(h_packed, n_used, block_e, j_of, src_tok, dst_slot, w_gate, w_up, w_down, layer, *, n_slots):
    d = 2 * h_packed.shape[1]
    f = w_gate.shape[3]
    tm, tf = MOE_ROWS, MOE_FTILE
    n_blocks = block_e.shape[0]
    n_j = f // tf
    assert tm % n_j == 0
    idx = lambda fn: pl.BlockSpec((None, 1, tm), fn, memory_space=pltpu.SMEM)
    grid_spec = pltpu.PrefetchScalarGridSpec(
        num_scalar_prefetch=3,
        grid=(n_blocks, n_j),
        in_specs=[
            idx(lambda i, j, nu, be, jw: (i, 0, 0)),
            idx(lambda i, j, nu, be, jw: (jnp.minimum(i + 1, n_blocks - 1), 0, 0)),
            idx(lambda i, j, nu, be, jw: (jnp.maximum(i - 1, 0), 0, 0)),
            pl.BlockSpec(memory_space=pl.ANY),
            pl.BlockSpec((None, None, d, tf), lambda i, j, nu, be, jw: (layer, be[i], 0, jw[i, j])),
            pl.BlockSpec((None, None, d, tf), lambda i, j, nu, be, jw: (layer, be[i], 0, jw[i, j])),
            pl.BlockSpec((None, None, tf, d), lambda i, j, nu, be, jw: (layer, be[i], jw[i, j], 0)),
        ],
        out_specs=pl.BlockSpec(memory_space=pl.ANY),
        scratch_shapes=[pltpu.VMEM((2, n_j, tm // n_j, d // 2), jnp.uint32),
                        pltpu.VMEM((2, n_j, tm // n_j, d), F32),
                        pltpu.SemaphoreType.DMA((2,)), pltpu.SemaphoreType.DMA((2,))],
    )
    return pl.pallas_call(
        functools.partial(_moe_kernel, n_j=n_j, n_slots=n_slots),
        out_shape=jax.ShapeDtypeStruct((n_slots + tm, d), F32),
        grid_spec=grid_spec,
        compiler_params=_cparams(("arbitrary", "arbitrary")),
        name="moe_experts",
    )(n_used, block_e, j_of, src_tok, src_tok, dst_slot, h_packed, w_gate, w_up, w_down)


def moe_ffn(h_packed, ids, w_gate, w_up, w_down, layer):
    n_tok = h_packed.shape[0]
    n_slots = n_tok * TOP_K
    tm = MOE_ROWS
    i32 = jnp.int32
    flat_e = ids.reshape(n_slots)
    order = jnp.argsort(flat_e, stable=True).astype(i32)
    counts = jnp.sum((flat_e[:, None] == jnp.arange(N_EXPERTS, dtype=i32)[None, :]).astype(i32), axis=0)
    start = jnp.cumsum(counts) - counts
    padded = (counts + tm - 1) // tm * tm
    pad_end = jnp.cumsum(padded)
    pad_start = pad_end - padded
    n_blocks = -(-n_slots // tm) + N_EXPERTS + 1
    pos = jnp.arange(n_blocks * tm, dtype=i32)
    e_pos = jnp.minimum(jnp.sum((pos[:, None] >= pad_end[None, :]).astype(i32), axis=1), N_EXPERTS - 1)
    r_pos = pos - pad_start[e_pos]
    valid = r_pos < counts[e_pos]
    slot = order[jnp.clip(start[e_pos] + r_pos, 0, n_slots - 1)]
    src_tok = jnp.where(valid, slot % n_tok, 0).reshape(n_blocks, 1, tm)
    dst_slot = jnp.where(valid, slot, n_slots + pos % tm).reshape(n_blocks, 1, tm)
    n_used = (pad_end[-1] // tm).astype(i32)
    blk = jnp.arange(n_blocks, dtype=i32)
    block_e = e_pos.reshape(n_blocks, tm)[:, 0]
    block_e = jnp.where(blk < n_used, block_e, block_e[jnp.maximum(n_used - 1, 0)]).astype(i32)
    n_j = w_gate.shape[3] // MOE_FTILE
    tiles = jnp.arange(n_j, dtype=i32)[None, :]
    j_of = jnp.where((blk % 2 == 0)[:, None], tiles, n_j - 1 - tiles)
    j_of = jnp.where((blk < n_used)[:, None], j_of, j_of[jnp.maximum(n_used - 1, 0), n_j - 1]).astype(i32)
    return moe_call(h_packed, n_used.reshape(1), block_e, j_of, src_tok.astype(i32), dst_slot.astype(i32),
                    w_gate, w_up, w_down, layer, n_slots=n_slots)


def _pad_cols(w, width):
    return jnp.pad(w, [(0, 0)] * (w.ndim - 1) + [(0, width - w.shape[-1])])


def _even_weights(ev_w_in, rwkv_mu, aw, bw):
    rk = 3 * aw
    lo0 = rk
    pieces = [DECAY_LORA, DECAY_LORA, ICL_LORA, ICL_LORA]

    def regroup(t):
        out = [t[..., rk + 2 * (DECAY_LORA + ICL_LORA) + GATE_LORA:],
               t[..., :rk]]
        off = lo0
        for wdt in pieces:
            out.append(_pad_cols(t[..., off:off + wdt], LANES))
            off += wdt
        out.append(t[..., off:off + GATE_LORA])
        used = 4 * LANES + GATE_LORA
        out.append(jnp.zeros(t.shape[:-1] + (LORA_PAD - used,), t.dtype))
        return jnp.concatenate(out, axis=-1)

    w = regroup(ev_w_in).astype(BF16)
    rwkv_cols = rk + 2 * (DECAY_LORA + ICL_LORA) + GATE_LORA
    mu_full = jnp.concatenate([rwkv_mu, jnp.zeros(rwkv_mu.shape[:-1] + (2 * bw,), rwkv_mu.dtype)], axis=-1)
    mu = regroup(mu_full)
    del rwkv_cols
    return w, mu


def kernel(x, c, ctx, c_ctx, ada_w, ada_b, norm_g, norm_b,
           ev_w_in, ev_w_out, rwkv_mu, rwkv_w0, rwkv_w2, rwkv_a0, rwkv_a2, rwkv_g2,
           rwkv_k_k, rwkv_k_a, rwkv_r_k, rwkv_lnx_w, rwkv_lnx_b,
           lru_conv_w, lru_conv_b, lru_wa, lru_ba, lru_wx, lru_bx, lru_lam,
           od_w_in, od_w_out, mlstm_ig_b, mlstm_fg_b, mlstm_norm_w, mlstm_norm_b,
           router_w, router_b, moe_w_gate, moe_w_up, moe_w_down):
    bsz, s_len, dim = x.shape
    n_ctx = ctx.shape[1]
    n_lat_rows = bsz * s_len
    n_all_rows = n_lat_rows + bsz * n_ctx
    aw = rwkv_k_k.shape[1]
    bw = lru_conv_b.shape[1]
    assert s_len % SCAN_ROWS == 0 and n_ctx == SCAN_ROWS and n_all_rows % MM_ROWS == 0
    assert (bsz * n_ctx) % MM_ROWS == 0 and s_len % MM_ROWS == 0 and bsz + 1 <= SUBLANES
    seq = dict(n_lat_rows=n_lat_rows, lat_len=s_len, ctx_len=n_ctx)
    scan = dict(batch=bsz, lat_len=s_len, ctx_len=n_ctx)
    modk = dict(rows_per_mod=s_len, n_mod_rows=bsz + 1)

    stream = embed_call(x.reshape(n_lat_rows, dim), ctx.reshape(bsz * n_ctx, dim), s_len)
    cvec = jnp.concatenate([c, c_ctx[None, :], jnp.zeros((SUBLANES - bsz - 1, dim), c.dtype)], axis=0)
    ada_b3 = ada_b[:, None, :]
    norm_g4 = norm_g[:, :, None, :]
    norm_b4 = norm_b[:, :, None, :]
    router_wt = router_w.T
    router_b2 = router_b[:, None]

    for layer in range(DEPTH):
        last = layer == DEPTH - 1
        i = layer // 2
        mods = adaln_call(cvec, ada_w, ada_b3, layer).reshape(SUBLANES, 1, N_MOD * dim)
        if layer % 2 == 0:
            w_in, mu = _even_weights(ev_w_in, rwkv_mu, aw, bw)
            n_in = w_in.shape[2]
            p = matmul_mod_call(stream, mods, w_in, i, tn=1024, **modk)
            col_x, col_gb = 0, bw // bw
            col_r, col_k, col_v = 2 * bw // aw, 2 * bw // aw + 1, 2 * bw // aw + 2
            col_l = (2 * bw + 3 * aw) // LORA_PAD
            assert (2 * bw + 3 * aw) % LORA_PAD == 0 and n_in == 2 * bw + 3 * aw + LORA_PAD
            mu_i = mu[i]
            o_r = 2 * bw
            prm = dict(
                mu_r=mu_i[None, o_r:o_r + aw], mu_k=mu_i[None, o_r + aw:o_r + 2 * aw],
                mu_v=mu_i[None, o_r + 2 * aw:o_r + 3 * aw], mu_l=mu_i[None, o_r + 3 * aw:],
                w0=rwkv_w0[i], w2=_pad_rows(rwkv_w2[i], LANES).astype(BF16),
                a0=rwkv_a0[i], a2=_pad_rows(rwkv_a2[i], LANES).astype(BF16),
                g2=rwkv_g2[i].astype(BF16), k_k=rwkv_k_k[i][None, :], k_a=rwkv_k_a[i][None, :])
            r_s, v_s, kk, lw, kd, bb, g = rwkv_prep_call(p, prm, col_r=col_r, col_k=col_k, col_v=col_v,
                                                          col_l=col_l, **seq)
            nb = bw // LRU_BLOCK_W
            wcat = jnp.concatenate([lru_wa[i], lru_wx[i]], axis=-1).astype(BF16)
            bcat = jnp.stack([lru_ba[i].reshape(2, nb, LRU_BLOCK_W), lru_bx[i].reshape(2, nb, LRU_BLOCK_W)],
                             axis=2).reshape(2, 1, 2 * bw)
            lprm = dict(conv_w=lru_conv_w[i], conv_b=lru_conv_b[i][None, :], wcat=wcat, bcat=bcat,
                        lsl=(RGLRU_C * jax.nn.log_sigmoid(lru_lam[i]))[:, None, :])
            yf = rwkv_scan_call(r_s, lw, kd, v_s, kk, bb, reverse=False, **scan)
            yb = rwkv_scan_call(r_s, lw, kd, v_s, kk, bb, reverse=True, **scan)
            hf, hb = lru_call(p, lprm, col_x=col_x, **scan)
            pprm = dict(r_k=rwkv_r_k[i].reshape(1, aw), lnx_w=rwkv_lnx_w[i][None, :], lnx_b=rwkv_lnx_b[i][None, :])
            ymix = even_post_call(yf, yb, r_s, v_s, kd, g, hf, hb, p, pprm, col_gb=col_gb)
            n_rows = n_all_rows if not last else n_lat_rows
            y = matmul_call(ymix, ev_w_out.astype(BF16), i, tn=1024, n_rows=n_rows)
        else:
            qk_w = C_HEADS * C_QK_HEAD
            v_w = C_HEADS * C_V_HEAD
            main = 2 * qk_w + 2 * v_w
            w_gate = _pad_cols(od_w_in[..., main:], LANES).astype(BF16)
            p = matmul_mod_call(stream, mods, od_w_in.astype(BF16), i, tn=1024, n_cols=main, **modk)
            gp = matmul_mod_call(stream, mods, w_gate, i, tn=LANES, **modk)[:, :4 * C_HEADS]
            gp = gp.reshape(n_all_rows, 2, 2, C_HEADS)
            g_rows, g_cols = _mlstm_gates(gp, mlstm_ig_b[i], mlstm_fg_b[i])
            cols = dict(col_q=0, col_k=(qk_w + v_w) // C_QK_HEAD, col_v=(2 * qk_w + v_w) // C_V_HEAD)
            hf, hb = mlstm_call(p, g_rows, g_cols, **scan, **cols)
            n_rows = n_all_rows if not last else n_lat_rows
            ymix = odd_post_call(hf, hb, p, mlstm_norm_w[i][None, :], mlstm_norm_b[i][None, :],
                                 n_rows=n_rows, col_o=qk_w // (v_w // 2))
            y = matmul_call(ymix, od_w_out.astype(BF16), i, tn=1024, n_rows=n_rows)
        n_rows = n_all_rows if not last else n_lat_rows
        stream1, h_moe, ids, gates = ln_route_call(stream, y, mods, norm_g4, norm_b4, layer, router_wt, router_b2,
                                                   n_rows=n_rows, **modk)
        y_slots = moe_ffn(h_moe, ids, moe_w_gate, moe_w_up, moe_w_down, layer)
        stream = ln_combine_call(stream1, y_slots, gates.T, mods, norm_g4, norm_b4, layer, n_rows=n_rows, **modk)
    return stream[:n_lat_rows].reshape(bsz, s_len, dim)


def _pad_rows(w, rows):
    return jnp.pad(w, ((0, 0), (0, rows - w.shape[1]), (0, 0)))


def _mlstm_gates(gp, ig_b, fg_b):
    n_rows = gp.shape[0]
    chunk = MLSTM_CHUNK
    n_sub = SCAN_ROWS // chunk
    outs = []
    for d in range(2):
        ig = GATE_CAP * jnp.tanh((gp[:, d, 0] + ig_b[d]) / GATE_CAP)
        lf = jax.nn.log_sigmoid(GATE_CAP * jnp.tanh((gp[:, d, 1] + fg_b[d]) / GATE_CAP))
        lf = lf.reshape(n_rows // chunk, chunk, C_HEADS)
        b = jnp.cumsum(lf[:, ::-1], axis=1)[:, ::-1] if d == 1 else jnp.cumsum(lf, axis=1)
        both = jnp.stack([b, ig.reshape(n_rows // chunk, chunk, C_HEADS)], axis=0)
        outs.append(both)
    g = jnp.stack(outs, axis=0)
    g = g.reshape(2, 2, n_rows // SCAN_ROWS, n_sub, chunk, C_HEADS)
    g_rows = jnp.transpose(g, (0, 1, 5, 2, 3, 4))
    g_cols = jnp.transpose(g, (0, 1, 5, 2, 4, 3))
    return g_rows, g_cols
```
